```python
import math
import jax, jax.numpy as jnp
from jax import lax
import numpy as np

D_MODEL = 1024
BATCH = 8
SEQ = 4096
DEPTH = 2
DEC_BATCH = 8
DEC_SEQ = 2048
PAST_LEN = 128

GRID_W = 64
HEAD_DIM = 64
MIX_WIDTH = D_MODEL
Q_BLOCK = 128
NORM_EPS = 1e-6
A_HEADS = 4
A_KV_HEADS = 2
A_THETA = 10000.0
B_HEADS = 4
B_SUB = HEAD_DIM // 2
B_ROT = B_SUB // 4
PARTIAL_THETA = 500000.0
C_HEADS = 4
RET_THETA = 10000.0
RET_CHUNK = 128
D_HEADS = 4
W_LORA = 64
A_LORA = 64
G_LORA = 128
WKV_GN_EPS = 64e-5
D_FF = ((-(-8 * D_MODEL // 3) + 255) // 256) * 256

A_W = A_HEADS * HEAD_DIM
A_KV_W = A_KV_HEADS * HEAD_DIM
B_W = B_HEADS * HEAD_DIM
C_W = C_HEADS * HEAD_DIM
D_W = D_HEADS * HEAD_DIM
A_COLS = A_W + 2 * A_KV_W
B_COLS = 3 * B_W
C_COLS = 4 * C_W
D_SPLITS = (D_W, D_W, D_W, W_LORA, W_LORA, A_LORA, G_LORA)
D_COLS = 3 * D_W + 2 * W_LORA + A_LORA + G_LORA
IN_COLS = A_COLS + B_COLS + C_COLS + D_COLS
GROUP_OFFS = (A_COLS, A_COLS + B_COLS, A_COLS + B_COLS + C_COLS)

kernel_name = 'hybrid_bidir_parallel_heads_encoder'


def _rms(x, gain, eps=NORM_EPS):
    xf = x.astype(jnp.float32)
    y = xf * lax.rsqrt(jnp.mean(xf * xf, axis=-1, keepdims=True) + eps)
    return (y * gain.astype(jnp.float32)).astype(x.dtype)


def _split(z, sizes):
    offs = [int(o) for o in np.cumsum(sizes)[:-1]]
    return jnp.split(z, offs, axis=-1)


def _angles(pos, rot_dim, theta):
    inv = theta ** (-jnp.arange(0, rot_dim, 2, dtype=jnp.float32) / rot_dim)
    return pos.astype(jnp.float32)[:, None] * inv[None, :]


def _rotate(x, ang):
    half = x.shape[-1] // 2
    x1, x2 = x[..., :half], x[..., half:]
    c = jnp.cos(ang)[None, :, None, :].astype(x.dtype)
    s = jnp.sin(ang)[None, :, None, :].astype(x.dtype)
    return jnp.concatenate([x1 * c - x2 * s, x2 * c + x1 * s], axis=-1)


def _mixer_gqa(z, q_gain, k_gain, row_idx, col_idx):
    bn, t, _ = z.shape
    q, k, v = _split(z, (A_W, A_KV_W, A_KV_W))
    q = _rms(q.reshape(bn, t, A_HEADS, HEAD_DIM), q_gain)
    k = _rms(k.reshape(bn, t, A_KV_HEADS, HEAD_DIM), k_gain)
    v = v.reshape(bn, t, A_KV_HEADS, HEAD_DIM)
    half = HEAD_DIM // 2
    ang_r = _angles(row_idx, half, A_THETA)
    ang_c = _angles(col_idx, half, A_THETA)

    def axial(x):
        return jnp.concatenate([_rotate(x[..., :half], ang_r), _rotate(x[..., half:], ang_c)], axis=-1)

    q, k = axial(q), axial(k)
    g = A_HEADS // A_KV_HEADS
    nb = t // Q_BLOCK
    qb = q.reshape(bn, nb, Q_BLOCK, A_KV_HEADS, g, HEAD_DIM).transpose(1, 0, 3, 4, 2, 5)
    kt = k.transpose(0, 2, 1, 3)
    vt = v.transpose(0, 2, 1, 3)
    scale = HEAD_DIM ** -0.5

    def block(qblk):
        s = jnp.einsum('bhgqd,bhkd->bhgqk', qblk, kt).astype(jnp.float32) * scale
        p = jax.nn.softmax(s, axis=-1).astype(vt.dtype)
        return jnp.einsum('bhgqk,bhkd->bhgqd', p, vt)

    o = lax.map(block, qb)
    return o.transpose(1, 0, 4, 2, 3, 5).reshape(bn, t, A_W)


def _mixer_diff(z, lam_params, subln_gain, pos, lam_init):
    bn, t, _ = z.shape
    q, k, v = _split(z, (B_W, B_W, B_W))
    q = q.reshape(bn, t, 2 * B_HEADS, B_SUB)
    k = k.reshape(bn, t, 2 * B_HEADS, B_SUB)
    v = v.reshape(bn, t, B_HEADS, HEAD_DIM)
    ang = _angles(pos, B_ROT, PARTIAL_THETA)

    def partial_rope(x):
        return jnp.concatenate([_rotate(x[..., :B_ROT], ang), x[..., B_ROT:]], axis=-1)

    q, k = partial_rope(q), partial_rope(k)
    lp = lam_params.astype(jnp.float32)
    lam = jnp.exp(jnp.sum(lp[0] * lp[1])) - jnp.exp(jnp.sum(lp[2] * lp[3])) + lam_init
    nb = t // Q_BLOCK
    qb = q.reshape(bn, nb, Q_BLOCK, 2 * B_HEADS, B_SUB).transpose(1, 0, 3, 2, 4)
    kt = k.transpose(0, 2, 1, 3)
    vt = v.transpose(0, 2, 1, 3)
    scale = B_SUB ** -0.5

    def block(qblk):
        s = jnp.einsum('bhqd,bhkd->bhqk', qblk, kt).astype(jnp.float32) * scale
        p = jax.nn.softmax(s, axis=-1)
        p = p.reshape(p.shape[0], B_HEADS, 2, p.shape[2], p.shape[3])
        a = p[:, :, 0] - lam * p[:, :, 1]
        return jnp.einsum('bhqk,bhkd->bhqd', a.astype(vt.dtype), vt)

    o = lax.map(block, qb)
    o = o.transpose(1, 0, 3, 2, 4).reshape(bn, t, B_HEADS, HEAD_DIM)
    o = _rms(o, subln_gain) * (1.0 - lam_init)
    return o.reshape(bn, t, B_W)


def _retention_dir(q, k, v, log_gamma):
    bn, h, t, d = q.shape
    c = RET_CHUNK
    n = t // c
    qc = q.reshape(bn, h, n, c, d)
    kc = k.reshape(bn, h, n, c, d)
    vc = v.reshape(bn, h, n, c, d)
    i = jnp.arange(c, dtype=jnp.float32)
    lg = log_gamma.astype(jnp.float32)[:, None]
    diff = i[:, None] - i[None, :]
    dmask = jnp.where(diff >= 0, jnp.exp(lg[:, :, None] * jnp.maximum(diff, 0.0)), 0.0)
    inner = jnp.einsum('bhnid,bhnjd->bhnij', qc, kc) * dmask[None, :, None].astype(q.dtype)
    o_inner = jnp.einsum('bhnij,bhnje->bhnie', inner, vc)
    k_dec = jnp.exp(lg * (c - 1 - i))
    kv = jnp.einsum('bhnjd,bhnje->bhnde', kc * k_dec[None, :, None, :, None].astype(q.dtype), vc)
    g_chunk = jnp.exp(log_gamma.astype(jnp.float32) * c)[None, :, None, None]

    def step(state, kv_c):
        return g_chunk * state + kv_c, state

    s0 = jnp.zeros((bn, h, d, d), jnp.float32)
    _, states = lax.scan(step, s0, kv.transpose(2, 0, 1, 3, 4).astype(jnp.float32))
    states = states.transpose(1, 2, 0, 3, 4).astype(q.dtype)
    q_dec = jnp.exp(lg * (i + 1.0))
    cross = jnp.einsum('bhnid,bhnde->bhnie', qc * q_dec[None, :, None, :, None].astype(q.dtype), states)
    return (o_inner + cross).reshape(bn, h, t, d)


def _mixer_retention(z, gn_gain, pos, lg_fwd, lg_bwd):
    bn, t, _ = z.shape
    q, k, v, g = _split(z, (C_W, C_W, C_W, C_W))
    ang = _angles(pos, HEAD_DIM, RET_THETA)
    q = _rotate(q.reshape(bn, t, C_HEADS, HEAD_DIM), ang).transpose(0, 2, 1, 3)
    k = (_rotate(k.reshape(bn, t, C_HEADS, HEAD_DIM), ang) * (HEAD_DIM ** -0.5)).transpose(0, 2, 1, 3)
    v = v.reshape(bn, t, C_HEADS, HEAD_DIM).transpose(0, 2, 1, 3)
    fwd = _retention_dir(q, k, v, lg_fwd)
    bwd = jnp.flip(_retention_dir(jnp.flip(q, 2), jnp.flip(k, 2), jnp.flip(v, 2), lg_bwd), 2)
    o = (fwd + bwd).transpose(0, 2, 1, 3)
    o = _rms(o, gn_gain.reshape(C_HEADS, HEAD_DIM))
    return o.reshape(bn, t, C_W) * jax.nn.silu(g)


def _wkv7(r, w, k, v, kk, b, reverse):
    bn, t, h, n = r.shape
    xs = (r.transpose(1, 0, 2, 3), w.transpose(1, 0, 2, 3), k.transpose(1, 0, 2, 3),
          v.transpose(1, 0, 2, 3), kk.transpose(1, 0, 2, 3), b.transpose(1, 0, 2, 3))

    def step(state, xs_t):
        r_t, w_t, k_t, v_t, kk_t, b_t = xs_t
        sa = jnp.einsum('bhvk,bhk->bhv', state, -kk_t)
        state = state * w_t[:, :, None, :] + sa[..., None] * b_t[:, :, None, :] + v_t[..., None] * k_t[:, :, None, :]
        return state, jnp.einsum('bhvk,bhk->bhv', state, r_t)

    _, y = lax.scan(step, jnp.zeros((bn, h, n, n), jnp.float32), xs, reverse=reverse)
    return y.transpose(1, 0, 2, 3)


def _mixer_rwkv7(z, mu_prev, mu_next, w0, w_up, a0, a_up, g_up, k_k, k_a, r_k, gn_w, gn_b):
    bn, t, _ = z.shape
    z_prev = jnp.pad(z[:, :-1], ((0, 0), (1, 0), (0, 0)))
    z_next = jnp.pad(z[:, 1:], ((0, 0), (0, 1), (0, 0)))
    u = z + mu_prev * (z_prev - z) + mu_next * (z_next - z)
    r, k, v, wdf, wdb, ad, gd = _split(u, D_SPLITS)
    f32 = jnp.float32

    def decay(wd, w0_d, w_up_d):
        w = -jax.nn.softplus(-(w0_d + jnp.tanh(wd) @ w_up_d).astype(f32)) - 0.5
        return jnp.exp(-jnp.exp(w)).reshape(bn, t, D_HEADS, HEAD_DIM)

    dec_f = decay(wdf, w0[0], w_up[0])
    dec_b = decay(wdb, w0[1], w_up[1])
    a = jax.nn.sigmoid((a0 + ad @ a_up).astype(f32)).reshape(bn, t, D_HEADS, HEAD_DIM)
    g = (jax.nn.sigmoid(gd) @ g_up).astype(f32)
    rh = r.astype(f32).reshape(bn, t, D_HEADS, HEAD_DIM)
    kh = k.astype(f32).reshape(bn, t, D_HEADS, HEAD_DIM)
    vh = v.astype(f32).reshape(bn, t, D_HEADS, HEAD_DIM)
    kk = kh * k_k.astype(f32).reshape(D_HEADS, HEAD_DIM)
    kk = kk / jnp.maximum(jnp.sqrt(jnp.sum(kk * kk, axis=-1, keepdims=True)), 1e-12)
    kh = kh * (1.0 + (a - 1.0) * k_a.astype(f32).reshape(D_HEADS, HEAD_DIM))
    b = kk * a
    y = _wkv7(rh, dec_f, kh, vh, kk, b, False) + _wkv7(rh, dec_b, kh, vh, kk, b, True)
    mean = jnp.mean(y, axis=-1, keepdims=True)
    var = jnp.mean((y - mean) ** 2, axis=-1, keepdims=True)
    yn = (y - mean) * lax.rsqrt(var + WKV_GN_EPS) * gn_w.astype(f32).reshape(D_HEADS, HEAD_DIM) \
        + gn_b.astype(f32).reshape(D_HEADS, HEAD_DIM)
    yn = yn + jnp.sum(rh * kh * r_k.astype(f32), axis=-1, keepdims=True) * vh
    return (yn.reshape(bn, t, D_W) * g).astype(z.dtype)


def _trunk(x, p):
    bn, t, _ = x.shape
    rows = t // GRID_W
    row_idx = jnp.repeat(jnp.arange(rows), GRID_W)
    col_idx = jnp.tile(jnp.arange(GRID_W), rows)
    pos = jnp.arange(t)
    lg_fwd = jnp.log1p(-jnp.exp2(-5.0 - jnp.arange(C_HEADS, dtype=jnp.float32)))
    lg_bwd = lg_fwd[::-1]
    for l in range(DEPTH):
        h = _rms(x, p['norm_mix_pre'][l])
        z = h @ p['w_in'][l]
        z_a, z_b, z_c, z_d = jnp.split(z, list(GROUP_OFFS), axis=-1)
        lam_init = 0.8 - 0.6 * math.exp(-0.3 * l)
        o_a = _mixer_gqa(z_a, p['a_q_gain'][l], p['a_k_gain'][l], row_idx, col_idx)
        o_b = _mixer_diff(z_b, p['b_lambda'][l], p['b_subln_gain'][l], pos, lam_init)
        o_c = _mixer_retention(z_c, p['c_gn_gain'][l], pos, lg_fwd, lg_bwd)
        o_d = _mixer_rwkv7(z_d, p['d_mu_prev'][l], p['d_mu_next'][l], p['d_w0'][l], p['d_w_up'][l],
                           p['d_a0'][l], p['d_a_up'][l], p['d_g_up'][l], p['d_k_k'][l], p['d_k_a'][l],
                           p['d_r_k'][l], p['d_gn_w'][l], p['d_gn_b'][l])
        mix = jnp.concatenate([o_a, o_b, o_c, o_d], axis=-1) @ p['w_out'][l]
        x = x + _rms(mix, p['norm_mix_post'][l])
        h = _rms(x, p['norm_ffn_pre'][l])
        f = (jax.nn.silu(h @ p['ffn_w_gate'][l]) * (h @ p['ffn_w_up'][l])) @ p['ffn_w_down'][l]
        x = x + _rms(f, p['norm_ffn_post'][l])
    return x


def setup_inputs(seed: int = 0) -> dict:
    key = jax.random.key(seed)
    ks = iter(jax.random.split(key, 40))
    L = DEPTH

    def nrm(shape, scale):
        return jax.random.normal(next(ks), shape, jnp.float32) * scale

    return {
        'x_prompt': nrm((BATCH, SEQ, D_MODEL), 1.0),
        'x_sample': nrm((DEC_BATCH, DEC_SEQ, D_MODEL), 1.0),
        'norm_mix_pre': 1.0 + nrm((L, D_MODEL), 0.02),
        'norm_mix_post': 1.0 + nrm((L, D_MODEL), 0.02),
        'norm_ffn_pre': 1.0 + nrm((L, D_MODEL), 0.02),
        'norm_ffn_post': 1.0 + nrm((L, D_MODEL), 0.02),
        'w_in': nrm((L, D_MODEL, IN_COLS), D_MODEL ** -0.5),
        'w_out': nrm((L, MIX_WIDTH, D_MODEL), MIX_WIDTH ** -0.5),
        'a_q_gain': 1.0 + nrm((L, HEAD_DIM), 0.02),
        'a_k_gain': 1.0 + nrm((L, HEAD_DIM), 0.02),
        'b_lambda': nrm((L, 4, B_SUB), 0.1),
        'b_subln_gain': 1.0 + nrm((L, HEAD_DIM), 0.02),
        'c_gn_gain': 1.0 + nrm((L, C_W), 0.02),
        'd_mu_prev': 0.3 + nrm((L, D_COLS), 0.1),
        'd_mu_next': 0.3 + nrm((L, D_COLS), 0.1),
        'd_w0': -1.0 + nrm((L, 2, D_W), 0.5),
        'd_w_up': nrm((L, 2, W_LORA, D_W), 0.1 * W_LORA ** -0.5),
        'd_a0': nrm((L, D_W), 0.1),
        'd_a_up': nrm((L, A_LORA, D_W), 0.5 * A_LORA ** -0.5),
        'd_g_up': nrm((L, G_LORA, D_W), G_LORA ** -0.5),
        'd_k_k': 0.85 + nrm((L, D_W), 0.05),
        'd_k_a': 1.0 + nrm((L, D_W), 0.05),
        'd_r_k': nrm((L, D_HEADS, HEAD_DIM), 0.1),
        'd_gn_w': 1.0 + nrm((L, D_W), 0.02),
        'd_gn_b': nrm((L, D_W), 0.02),
        'ffn_w_gate': nrm((L, D_MODEL, D_FF), D_MODEL ** -0.5),
        'ffn_w_up': nrm((L, D_MODEL, D_FF), D_MODEL ** -0.5),
        'ffn_w_down': nrm((L, D_FF, D_MODEL), D_FF ** -0.5),
    }


def reference(x_prompt, x_sample, norm_mix_pre, norm_mix_post, norm_ffn_pre, norm_ffn_post,
              w_in, w_out, a_q_gain, a_k_gain, b_lambda, b_subln_gain, c_gn_gain,
              d_mu_prev, d_mu_next, d_w0, d_w_up, d_a0, d_a_up, d_g_up, d_k_k, d_k_a, d_r_k,
              d_gn_w, d_gn_b, ffn_w_gate, ffn_w_up, ffn_w_down):
    params = {
        'norm_mix_pre': norm_mix_pre, 'norm_mix_post': norm_mix_post,
        'norm_ffn_pre': norm_ffn_pre, 'norm_ffn_post': norm_ffn_post,
        'w_in': w_in, 'w_out': w_out,
        'a_q_gain': a_q_gain, 'a_k_gain': a_k_gain,
        'b_lambda': b_lambda, 'b_subln_gain': b_subln_gain,
        'c_gn_gain': c_gn_gain,
        'd_mu_prev': d_mu_prev, 'd_mu_next': d_mu_next, 'd_w0': d_w0, 'd_w_up': d_w_up,
        'd_a0': d_a0, 'd_a_up': d_a_up, 'd_g_up': d_g_up, 'd_k_k': d_k_k, 'd_k_a': d_k_a,
        'd_r_k': d_r_k, 'd_gn_w': d_gn_w, 'd_gn_b': d_gn_b,
        'ffn_w_gate': ffn_w_gate, 'ffn_w_up': ffn_w_up, 'ffn_w_down': ffn_w_down,
    }
    y_prompt = _trunk(x_prompt, params)
    y_sample = _trunk(x_sample, params)
    return (y_prompt, y_sample)
```

```python
import functools
import math

import jax
import jax.numpy as jnp
from jax import lax
from jax.experimental import pallas as pl
from jax.experimental.pallas import tpu as pltpu

F32 = jnp.float32
BF16 = jnp.bfloat16

HEAD_DIM = 64
GRID_W = 64
NORM_EPS = 1e-6
A_HEADS, A_KV_HEADS, A_THETA = 4, 2, 10000.0
B_HEADS, B_SUB, B_ROT, B_THETA = 4, 32, 8, 500000.0
C_HEADS, C_THETA = 4, 10000.0
D_HEADS = 4
WKV_GN_EPS = 64e-5
MIX_W = 256
ABC_COLS = 2304
D_COLS = 1088
LANES = 128
VMEM_LIMIT = 56 * 1024 * 1024


def _params(*sem):
    return pltpu.CompilerParams(dimension_semantics=sem, vmem_limit_bytes=VMEM_LIMIT)


def _rms_rows(x, gain):
    return x * lax.rsqrt(jnp.mean(x * x, axis=-1, keepdims=True) + NORM_EPS) * gain


def _split2(x):
    hi = x.astype(BF16)
    lo = (x - hi.astype(F32)).astype(BF16)
    return hi, lo


def _split3(x):
    h1 = x.astype(BF16)
    r1 = x - h1.astype(F32)
    h2 = r1.astype(BF16)
    h3 = (r1 - h2.astype(F32)).astype(BF16)
    return h1, h2, h3


def _head_sum(x, seg):
    w = x.shape[-1]
    r = lax.broadcasted_iota(jnp.int32, (w, w), 0) // seg
    c = lax.broadcasted_iota(jnp.int32, (w, w), 1) // seg
    bd = jnp.where(r == c, 1.0, 0.0).astype(BF16)
    hi, lo = _split2(x)
    return (jnp.dot(hi, bd, preferred_element_type=F32)
            + jnp.dot(lo, bd, preferred_element_type=F32))


def _rope(x, cos, sin, half):
    w = x.shape[-1]
    lane = lax.broadcasted_iota(jnp.int32, x.shape, 1)
    nxt = pltpu.roll(x, w - half, 1)
    prv = pltpu.roll(x, half, 1)
    return x * cos + jnp.where((lane % (2 * half)) < half, nxt, prv) * sin


def _sigmoid(x):
    return 1.0 / (1.0 + jnp.exp(-x))


def _dot(a, b):
    return jnp.dot(a, b, preferred_element_type=F32)


def _dot_nt(a, b):
    return lax.dot_general(a, b, (((1,), (1,)), ((), ())), preferred_element_type=F32)


def _in_proj_kernel(x_ref, g_ref, wabc_ref, wd_ref, zabc_ref, zd_ref):
    h = _rms_rows(x_ref[...], g_ref[...]).astype(BF16)
    zabc_ref[...] = _dot(h, wabc_ref[...]).astype(BF16)
    zd_ref[...] = _dot(h, wd_ref[...])


def _in_proj(x2, gain, w_abc, w_d, tm):
    n, d = x2.shape
    return pl.pallas_call(
        _in_proj_kernel,
        grid=(n // tm,),
        in_specs=[
            pl.BlockSpec((tm, d), lambda i: (i, 0)),
            pl.BlockSpec((1, d), lambda i: (0, 0)),
            pl.BlockSpec((d, ABC_COLS), lambda i: (0, 0)),
            pl.BlockSpec((d, D_COLS), lambda i: (0, 0)),
        ],
        out_specs=[
            pl.BlockSpec((tm, ABC_COLS), lambda i: (i, 0)),
            pl.BlockSpec((tm, D_COLS), lambda i: (i, 0)),
        ],
        out_shape=[
            jax.ShapeDtypeStruct((n, ABC_COLS), BF16),
            jax.ShapeDtypeStruct((n, D_COLS), F32),
        ],
        compiler_params=_params("parallel"),
        name="in_proj",
    )(x2, gain, w_abc, w_d)


def _value_ext(v, head):
    lane = lax.broadcasted_iota(jnp.int32, v.shape, 1)
    vh = v if head == 0 else pltpu.roll(v, HEAD_DIM, 1)
    return jnp.where(lane < HEAD_DIM, vh, jnp.where(lane == HEAD_DIM, 1.0, 0.0)).astype(BF16)


def _gqa_kernel(q_ref, k_ref, v_ref, cq_ref, sq_ref, ck_ref, sk_ref, gq_ref, gk_ref, o_ref,
                kt_s, vx_s):
    @pl.when(pl.program_id(1) == 0)
    def _():
        k = k_ref[0].astype(F32)
        k = k * lax.rsqrt(_head_sum(k * k, HEAD_DIM) * (1.0 / HEAD_DIM) + NORM_EPS) * gk_ref[...]
        k = _rope(k, ck_ref[...], sk_ref[...], HEAD_DIM // 4)
        kt_s[...] = k.T.astype(BF16)
        v = v_ref[0].astype(F32)
        for g in range(A_KV_HEADS):
            vx_s[g] = _value_ext(v, g)

    q = q_ref[0].astype(F32)
    q = q * lax.rsqrt(_head_sum(q * q, HEAD_DIM) * (1.0 / HEAD_DIM) + NORM_EPS) * gq_ref[...]
    q = _rope(q, cq_ref[...], sq_ref[...], HEAD_DIM // 4) * (HEAD_DIM ** -0.5)
    outs = []
    for h in range(A_HEADS):
        g = h // (A_HEADS // A_KV_HEADS)
        qh = q[:, h * HEAD_DIM:(h + 1) * HEAD_DIM].astype(BF16)
        s = _dot(qh, kt_s[g * HEAD_DIM:(g + 1) * HEAD_DIM, :])
        p = jnp.exp(s - jnp.max(s, axis=1, keepdims=True)).astype(BF16)
        oe = _dot(p, vx_s[g])
        outs.append(oe[:, :HEAD_DIM] / oe[:, HEAD_DIM:HEAD_DIM + 1])
    o_ref[0] = jnp.concatenate(outs, axis=1).astype(BF16)


def _mixer_gqa(zabc, tabs, gq, gk, tq):
    b, t, _ = zabc.shape
    cq, sq, ck, sk = tabs
    return pl.pallas_call(
        _gqa_kernel,
        grid=(b, t // tq),
        in_specs=[
            pl.BlockSpec((1, tq, 256), lambda bi, i: (bi, i, 0)),
            pl.BlockSpec((1, t, 128), lambda bi, i: (bi, 0, 2)),
            pl.BlockSpec((1, t, 128), lambda bi, i: (bi, 0, 3)),
            pl.BlockSpec((tq, 256), lambda bi, i: (i, 0)),
            pl.BlockSpec((tq, 256), lambda bi, i: (i, 0)),
            pl.BlockSpec((t, 128), lambda bi, i: (0, 0)),
            pl.BlockSpec((t, 128), lambda bi, i: (0, 0)),
            pl.BlockSpec((1, 256), lambda bi, i: (0, 0)),
            pl.BlockSpec((1, 128), lambda bi, i: (0, 0)),
        ],
        out_specs=pl.BlockSpec((1, tq, MIX_W), lambda bi, i: (bi, i, 0)),
        out_shape=jax.ShapeDtypeStruct((b, t, MIX_W), BF16),
        scratch_shapes=[
            pltpu.VMEM((A_KV_HEADS * HEAD_DIM, t), BF16),
            pltpu.VMEM((A_KV_HEADS, t, LANES), BF16),
        ],
        compiler_params=_params("parallel", "arbitrary"),
        name="mixer_gqa",
    )(zabc, zabc, zabc, cq, sq, ck, sk, gq, gk)


def _diff_kernel(lam_init, q_ref, k_ref, v_ref, cos_ref, sin_ref, cosk_ref, sink_ref, lam_ref, gain_ref,
                 o_ref, kt_s, vx_s):
    @pl.when(pl.program_id(1) == 0)
    def _():
        k = _rope(k_ref[0].astype(F32), cosk_ref[...], sink_ref[...], B_ROT // 2)
        kt_s[...] = k.T.astype(BF16)
        v = v_ref[0].astype(F32)
        lane = lax.broadcasted_iota(jnp.int32, (v.shape[0], LANES), 1)
        for h in range(B_HEADS):
            pair = v[:, (h // 2) * LANES:(h // 2 + 1) * LANES]
            vh = pair if h % 2 == 0 else pltpu.roll(pair, HEAD_DIM, 1)
            vx_s[h] = jnp.where(lane < HEAD_DIM, vh, jnp.where(lane == HEAD_DIM, 1.0, 0.0)).astype(BF16)

    lp = lam_ref[...]
    lam = (jnp.exp(jnp.sum(lp[0:1] * lp[1:2], axis=1, keepdims=True))
           - jnp.exp(jnp.sum(lp[2:3] * lp[3:4], axis=1, keepdims=True)) + lam_init)
    q = _rope(q_ref[0].astype(F32), cos_ref[...], sin_ref[...], B_ROT // 2) * (B_SUB ** -0.5)
    outs = []
    for h in range(B_HEADS):
        parts = []
        for j in (2 * h, 2 * h + 1):
            qj = q[:, j * B_SUB:(j + 1) * B_SUB].astype(BF16)
            s = _dot(qj, kt_s[j * B_SUB:(j + 1) * B_SUB, :])
            p = jnp.exp(s - jnp.max(s, axis=1, keepdims=True)).astype(BF16)
            oe = _dot(p, vx_s[h])
            parts.append(oe[:, :HEAD_DIM] / oe[:, HEAD_DIM:HEAD_DIM + 1])
        o = parts[0] - lam * parts[1]
        o = o * lax.rsqrt(jnp.mean(o * o, axis=-1, keepdims=True) + NORM_EPS) * gain_ref[...]
        outs.append(o * (1.0 - lam_init))
    o_ref[0] = jnp.concatenate(outs, axis=1).astype(BF16)


def _mixer_diff(zabc, tabs, lam_params, gain, lam_init, tq):
    b, t, _ = zabc.shape
    cos, sin = tabs
    return pl.pallas_call(
        functools.partial(_diff_kernel, lam_init),
        grid=(b, t // tq),
        in_specs=[
            pl.BlockSpec((1, tq, 256), lambda bi, i: (bi, i, 2)),
            pl.BlockSpec((1, t, 256), lambda bi, i: (bi, 0, 3)),
            pl.BlockSpec((1, t, 256), lambda bi, i: (bi, 0, 4)),
            pl.BlockSpec((tq, 256), lambda bi, i: (i, 0)),
            pl.BlockSpec((tq, 256), lambda bi, i: (i, 0)),
            pl.BlockSpec((t, 256), lambda bi, i: (0, 0)),
            pl.BlockSpec((t, 256), lambda bi, i: (0, 0)),
            pl.BlockSpec((4, B_SUB), lambda bi, i: (0, 0)),
            pl.BlockSpec((1, HEAD_DIM), lambda bi, i: (0, 0)),
        ],
        out_specs=pl.BlockSpec((1, tq, MIX_W), lambda bi, i: (bi, i, 0)),
        out_shape=jax.ShapeDtypeStruct((b, t, MIX_W), BF16),
        scratch_shapes=[
            pltpu.VMEM((2 * B_HEADS * B_SUB, t), BF16),
            pltpu.VMEM((B_HEADS, t, LANES), BF16),
        ],
        compiler_params=_params("parallel", "arbitrary"),
        name="mixer_diff",
    )(zabc, zabc, zabc, cos, sin, cos, sin, lam_params, gain)


def _ret_log_gammas():
    lg = [math.log1p(-(2.0 ** (-5.0 - h))) for h in range(C_HEADS)]
    return lg, lg[::-1]


def _lane_consts(vals, shape):
    head = lax.broadcasted_iota(jnp.int32, shape, len(shape) - 1) // HEAD_DIM
    out = jnp.full(shape, vals[-1], F32)
    for h in range(len(vals) - 2, -1, -1):
        out = jnp.where(head == h, vals[h], out)
    return out


def _ret_kernel(chunk, q_ref, k_ref, v_ref, g_ref, cos_ref, sin_ref, gain_ref, o_ref,
                qr_s, kr_s, acc_s, sf_s, sb_s):
    t = q_ref.shape[1]
    c = chunk
    n = t // c
    lgf, lgb = _ret_log_gammas()
    qr_s[...] = _rope(q_ref[0].astype(F32), cos_ref[...], sin_ref[...], HEAD_DIM // 2).astype(BF16)
    kr_s[...] = (_rope(k_ref[0].astype(F32), cos_ref[...], sin_ref[...], HEAD_DIM // 2)
                 * (HEAD_DIM ** -0.5)).astype(BF16)

    ti = lax.broadcasted_iota(jnp.int32, (c, c), 0)
    si = lax.broadcasted_iota(jnp.int32, (c, c), 1)
    dist = (ti - si).astype(F32)
    lane_head = lax.broadcasted_iota(jnp.int32, (1, MIX_W), 1) // HEAD_DIM
    row = lax.broadcasted_iota(jnp.int32, (c, MIX_W), 0).astype(F32)
    lgf_l = _lane_consts(lgf, (c, MIX_W))
    lgb_l = _lane_consts(lgb, (c, MIX_W))
    qdec_f = jnp.exp(lgf_l * (row + 1.0))
    kdec_f = jnp.exp(lgf_l * (c - 1.0 - row))
    qdec_b = jnp.exp(lgb_l * (c - row))
    kdec_b = jnp.exp(lgb_l * row)
    r2 = lax.broadcasted_iota(jnp.int32, (MIX_W, MIX_W), 0) // HEAD_DIM
    c2 = lax.broadcasted_iota(jnp.int32, (MIX_W, MIX_W), 1) // HEAD_DIM
    same_head = r2 == c2
    gf_blk = jnp.where(same_head, jnp.exp(_lane_consts(lgf, (MIX_W, MIX_W)) * c), 0.0)
    gb_blk = jnp.where(same_head, jnp.exp(_lane_consts(lgb, (MIX_W, MIX_W)) * c), 0.0)

    def intra(qc, kc, vc):
        out = jnp.zeros((c, MIX_W), F32)
        for h in range(C_HEADS):
            mh = (lane_head == h).astype(F32)
            dm = jnp.where(dist > 0, jnp.exp(lgf[h] * dist),
                           jnp.where(dist < 0, jnp.exp(-lgb[h] * dist), 2.0))
            s = _dot_nt((qc * mh).astype(BF16), kc) * dm
            out = out + _dot(s.astype(BF16), (vc * mh).astype(BF16))
        return out

    sf_s[...] = jnp.zeros_like(sf_s)
    sb_s[...] = jnp.zeros_like(sb_s)

    def fwd(i, carry):
        rows = pl.ds(pl.multiple_of(i * c, c), c)
        qc = qr_s[rows, :].astype(F32)
        kc = kr_s[rows, :]
        vc = v_ref[0, rows, :].astype(F32)
        o = intra(qc, kc, vc) + _dot((qc * qdec_f).astype(BF16), sf_s[...].astype(BF16))
        acc_s[rows, :] = o
        kd = (kc.astype(F32) * kdec_f).T.astype(BF16)
        sf_s[...] = gf_blk * sf_s[...] + jnp.where(same_head, _dot(kd, vc.astype(BF16)), 0.0)
        return carry

    lax.fori_loop(0, n, fwd, 0)

    def bwd(i, carry):
        rows = pl.ds(pl.multiple_of((n - 1 - i) * c, c), c)
        qc = qr_s[rows, :].astype(F32)
        kc = kr_s[rows, :].astype(F32)
        vc = v_ref[0, rows, :]
        acc_s[rows, :] = acc_s[rows, :] + _dot((qc * qdec_b).astype(BF16), sb_s[...].astype(BF16))
        kd = (kc * kdec_b).T.astype(BF16)
        sb_s[...] = gb_blk * sb_s[...] + jnp.where(same_head, _dot(kd, vc), 0.0)
        return carry

    lax.fori_loop(0, n, bwd, 0)

    o = acc_s[...]
    o = o * lax.rsqrt(_head_sum(o * o, HEAD_DIM) * (1.0 / HEAD_DIM) + NORM_EPS) * gain_ref[...]
    g = g_ref[0].astype(F32)
    o_ref[0] = (o * (g * _sigmoid(g))).astype(BF16)


def _mixer_ret(zabc, tabs, gain, chunk):
    b, t, _ = zabc.shape
    cos, sin = tabs
    blk = lambda j: pl.BlockSpec((1, t, 256), lambda bi: (bi, 0, j))
    return pl.pallas_call(
        functools.partial(_ret_kernel, chunk),
        grid=(b,),
        in_specs=[blk(5), blk(6), blk(7), blk(8),
                  pl.BlockSpec((t, 256), lambda bi: (0, 0)),
                  pl.BlockSpec((t, 256), lambda bi: (0, 0)),
                  pl.BlockSpec((1, 256), lambda bi: (0, 0))],
        out_specs=pl.BlockSpec((1, t, MIX_W), lambda bi: (bi, 0, 0)),
        out_shape=jax.ShapeDtypeStruct((b, t, MIX_W), BF16),
        scratch_shapes=[
            pltpu.VMEM((t, MIX_W), BF16),
            pltpu.VMEM((t, MIX_W), BF16),
            pltpu.VMEM((t, MIX_W), F32),
            pltpu.VMEM((MIX_W, MIX_W), F32),
            pltpu.VMEM((MIX_W, MIX_W), F32),
        ],
        compiler_params=_params("parallel"),
        name="mixer_ret",
    )(zabc, zabc, zabc, zabc, cos, sin, gain)


def _wkv_prep_kernel(z_ref, zp_ref, zn_ref, mup_ref, mun_ref, w0_ref, wupf_ref, wupb_ref, a0_ref, aup_ref,
                     gup1_ref, gup2_ref, kk_ref, ka_ref, rk_ref,
                     r_o, k_o, v_o, kk_o, b_o, ldf_o, ldb_o, g_o, bonus_o):
    i = pl.program_id(1)
    last = pl.num_programs(1) - 1
    z = z_ref[0]
    tm = z.shape[0]
    row = lax.broadcasted_iota(jnp.int32, z.shape, 0)
    prev_row = zp_ref[0, 0, 7:8, :] * jnp.where(i > 0, 1.0, 0.0)
    next_row = zn_ref[0, 0, 0:1, :] * jnp.where(i < last, 1.0, 0.0)
    z_prev = jnp.where(row == 0, prev_row, pltpu.roll(z, 1, 0))
    z_next = jnp.where(row == tm - 1, next_row, pltpu.roll(z, tm - 1, 0))
    u = z + mup_ref[...] * (z_prev - z) + mun_ref[...] * (z_next - z)
    r = u[:, 0:256]
    k = u[:, 256:512]
    v = u[:, 512:768]
    wd = jnp.tanh(u[:, 768:896]).astype(BF16)
    ag = u[:, 896:1024]
    g2 = u[:, 1024:1088]

    def log_decay(w0, wup):
        x = w0 + _dot(wd, wup)
        w = -(jnp.maximum(-x, 0.0) + jnp.log(1.0 + jnp.exp(-jnp.abs(x)))) - 0.5
        return -jnp.exp(w)

    ldf_o[0] = log_decay(w0_ref[0:1, :], wupf_ref[...])
    ldb_o[0] = log_decay(w0_ref[1:2, :], wupb_ref[...])
    a = _sigmoid(a0_ref[...] + _dot(ag.astype(BF16), aup_ref[...]))
    g_o[0] = (_dot(_sigmoid(ag).astype(BF16), gup1_ref[...])
              + _dot(_sigmoid(g2).astype(BF16), gup2_ref[...]))
    kk = k * kk_ref[...]
    kk = kk / jnp.maximum(jnp.sqrt(_head_sum(kk * kk, HEAD_DIM)), 1e-12)
    kh = k * (1.0 + (a - 1.0) * ka_ref[...])
    r_o[0] = r
    k_o[0] = kh
    v_o[0] = v
    kk_o[0] = kk
    b_o[0] = kk * a
    bonus_o[0] = _head_sum(r * kh * rk_ref[...], HEAD_DIM) * v


def _wkv_prep(zd, p, tm):
    b, t, _ = zd.shape
    zd8 = zd.reshape(b, t // 8, 8, D_COLS)
    r8 = tm // 8
    nb8 = t // 8
    row = lambda w: pl.BlockSpec((1, w), lambda bi, i: (0, 0))
    full = lambda a: pl.BlockSpec(a.shape, lambda bi, i: (0,) * a.ndim)
    out_spec = pl.BlockSpec((1, tm, MIX_W), lambda bi, i: (bi, i, 0))
    out_shape = jax.ShapeDtypeStruct((b, t, MIX_W), F32)
    return pl.pallas_call(
        _wkv_prep_kernel,
        grid=(b, t // tm),
        in_specs=[
            pl.BlockSpec((1, tm, D_COLS), lambda bi, i: (bi, i, 0)),
            pl.BlockSpec((1, 1, 8, D_COLS), lambda bi, i: (bi, jnp.maximum(i * r8 - 1, 0), 0, 0)),
            pl.BlockSpec((1, 1, 8, D_COLS), lambda bi, i: (bi, jnp.minimum((i + 1) * r8, nb8 - 1), 0, 0)),
            row(D_COLS), row(D_COLS),
            full(p["w0"]), full(p["wup_f"]), full(p["wup_b"]), row(256), full(p["aup"]),
            full(p["gup1"]), full(p["gup2"]), row(256), row(256), row(256),
        ],
        out_specs=[out_spec] * 9,
        out_shape=[out_shape] * 9,
        compiler_params=_params("parallel", "parallel"),
        name="wkv_prep",
    )(zd, zd8, zd8, p["mu_prev"], p["mu_next"], p["w0"], p["wup_f"], p["wup_b"], p["a0"], p["aup"],
      p["gup1"], p["gup2"], p["k_k"], p["k_a"], p["r_k"])


def _wkv_chunk(chunk, reverse, r, kh, v, kk, bb, lw, x):
    c = chunk
    sc = D_HEADS * c
    ti = lax.broadcasted_iota(jnp.int32, (c, c), 0)
    si = lax.broadcasted_iota(jnp.int32, (c, c), 1)
    tri = (jnp.where(si >= ti, 1.0, 0.0) if reverse else jnp.where(si <= ti, 1.0, 0.0)).astype(BF16)
    l1, l2, l3 = _split3(lw)
    cl = _dot(tri, l1) + _dot(tri, l2) + _dot(tri, l3)
    tot = jnp.sum(lw, axis=0, keepdims=True)
    w_in = jnp.exp(cl)
    w_ex = jnp.exp(cl - lw)
    w_inv = jnp.exp(-cl)
    w_end = jnp.exp(tot - cl)

    srow = lax.broadcasted_iota(jnp.int32, (sc, MIX_W), 0) // c
    slane = lax.broadcasted_iota(jnp.int32, (sc, MIX_W), 1) // HEAD_DIM
    hmask = srow == slane

    def stack(a):
        return jnp.where(hmask, jnp.concatenate([a] * D_HEADS, axis=0), 0.0).astype(BF16)

    kk_s = stack(kk * w_ex)
    r_s = stack(r * w_in)
    b_s = stack(bb * w_inv)
    k_s = stack(kh * w_inv)
    v_s = stack(v)
    be_s = stack(bb * w_end)
    ke_s = stack(kh * w_end)

    st = lax.broadcasted_iota(jnp.int32, (sc, sc), 0)
    ss = lax.broadcasted_iota(jnp.int32, (sc, sc), 1)
    same = (st // c) == (ss // c)
    if reverse:
        strict = same & (ss > st)
        incl = same & (ss >= st)
    else:
        strict = same & (ss < st)
        incl = same & (ss <= st)
    l_ab = jnp.where(strict, _dot_nt(kk_s, b_s), 0.0)
    l_ak = jnp.where(strict, _dot_nt(kk_s, k_s), 0.0)
    m_rb = jnp.where(incl, _dot_nt(r_s, b_s), 0.0).astype(BF16)
    m_rk = jnp.where(incl, _dot_nt(r_s, k_s), 0.0).astype(BF16)

    eye = jnp.where(st == ss, 1.0, 0.0)
    inv = eye - l_ab
    pw = l_ab
    steps = int(math.log2(c)) - 1
    for _ in range(steps):
        pwb = pw.astype(BF16)
        pw = _dot(pwb, pwb)
        inv = inv + _dot(inv.astype(BF16), pw.astype(BF16))
    inv = inv.astype(BF16)

    g_hat = _dot(inv, kk_s)
    u_hat = _dot(inv, _dot(l_ak.astype(BF16), v_s).astype(BF16))
    g_hb = g_hat.astype(BF16)
    u_hb = u_hat.astype(BF16)
    q_hat = r_s.astype(F32) - _dot(m_rb, g_hb)
    y_hat = _dot(m_rk, v_s) - _dot(m_rb, u_hb)
    xb = x.astype(BF16)
    ys = _dot(q_hat.astype(BF16), xb) + y_hat
    y = ys[0:c]
    for h in range(1, D_HEADS):
        y = y + ys[h * c:(h + 1) * c]

    r2 = lax.broadcasted_iota(jnp.int32, (MIX_W, MIX_W), 0)
    c2 = lax.broadcasted_iota(jnp.int32, (MIX_W, MIX_W), 1)
    be_t = be_s.astype(F32).T.astype(BF16)
    ke_t = ke_s.astype(F32).T.astype(BF16)
    t_mat = jnp.where(r2 == c2, jnp.exp(tot), 0.0) - _dot(be_t, g_hb)
    d_mat = _dot(ke_t, v_s) - _dot(be_t, u_hb)
    x_new = _dot(t_mat.astype(BF16), xb) + d_mat
    return y, x_new


def _wkv_scan_kernel(chunk, rf, kf, vf, kkf, bf, lf, rb, kb, vb, kkb, bb, lb, yf_o, yb_o, xf_s, xb_s):
    @pl.when(pl.program_id(1) == 0)
    def _():
        xf_s[...] = jnp.zeros_like(xf_s)
        xb_s[...] = jnp.zeros_like(xb_s)

    tm = rf.shape[1]
    n = tm // chunk

    def body(j, carry):
        fr = pl.ds(pl.multiple_of(j * chunk, chunk), chunk)
        y, xn = _wkv_chunk(chunk, False, rf[0, fr, :], kf[0, fr, :], vf[0, fr, :], kkf[0, fr, :],
                           bf[0, fr, :], lf[0, fr, :], xf_s[...])
        yf_o[0, fr, :] = y
        xf_s[...] = xn
        br = pl.ds(pl.multiple_of((n - 1 - j) * chunk, chunk), chunk)
        y, xn = _wkv_chunk(chunk, True, rb[0, br, :], kb[0, br, :], vb[0, br, :], kkb[0, br, :],
                           bb[0, br, :], lb[0, br, :], xb_s[...])
        yb_o[0, br, :] = y
        xb_s[...] = xn
        return carry

    lax.fori_loop(0, n, body, 0)


def _wkv_scan(r, kh, v, kk, bb, ldf, ldb, tm, chunk):
    b, t, _ = r.shape
    nt = t // tm
    fspec = pl.BlockSpec((1, tm, MIX_W), lambda bi, i: (bi, i, 0))
    bspec = pl.BlockSpec((1, tm, MIX_W), lambda bi, i: (bi, nt - 1 - i, 0))
    shape = jax.ShapeDtypeStruct((b, t, MIX_W), F32)
    return pl.pallas_call(
        functools.partial(_wkv_scan_kernel, chunk),
        grid=(b, nt),
        in_specs=[fspec] * 6 + [bspec] * 6,
        out_specs=[fspec, bspec],
        out_shape=[shape, shape],
        scratch_shapes=[pltpu.VMEM((MIX_W, MIX_W), F32), pltpu.VMEM((MIX_W, MIX_W), F32)],
        compiler_params=_params("parallel", "arbitrary"),
        name="wkv_scan",
    )(r, kh, v, kk, bb, ldf, r, kh, v, kk, bb, ldb)


def _wkv_fin_kernel(yf_ref, yb_ref, bonus_ref, g_ref, gnw_ref, gnb_ref, o_ref):
    y = yf_ref[...] + yb_ref[...]
    mean = _head_sum(y, HEAD_DIM) * (1.0 / HEAD_DIM)
    yc = y - mean
    var = _head_sum(yc * yc, HEAD_DIM) * (1.0 / HEAD_DIM)
    yn = yc * lax.rsqrt(var + WKV_GN_EPS) * gnw_ref[...] + gnb_ref[...]
    o_ref[...] = ((yn + bonus_ref[...]) * g_ref[...]).astype(BF16)


def _wkv_fin(yf, yb, bonus, g, gn_w, gn_b, tm):
    n = yf.shape[0]
    tok = pl.BlockSpec((tm, MIX_W), lambda i: (i, 0))
    row = pl.BlockSpec((1, MIX_W), lambda i: (0, 0))
    return pl.pallas_call(
        _wkv_fin_kernel,
        grid=(n // tm,),
        in_specs=[tok, tok, tok, tok, row, row],
        out_specs=tok,
        out_shape=jax.ShapeDtypeStruct((n, MIX_W), BF16),
        compiler_params=_params("parallel"),
        name="wkv_fin",
    )(yf, yb, bonus, g, gn_w, gn_b)


def _post_kernel(x_ref, oa_ref, ob_ref, oc_ref, od_ref, wo_ref, gpost_ref, gpre_ref, wg_ref, wu_ref, wd_ref,
                 gfpost_ref, y_ref):
    mix = (_dot(oa_ref[...], wo_ref[0]) + _dot(ob_ref[...], wo_ref[1])
           + _dot(oc_ref[...], wo_ref[2]) + _dot(od_ref[...], wo_ref[3]))
    x = x_ref[...] + _rms_rows(mix, gpost_ref[...])
    h = _rms_rows(x, gpre_ref[...]).astype(BF16)
    gate = _dot(h, wg_ref[...])
    up = _dot(h, wu_ref[...])
    act = (gate * _sigmoid(gate) * up).astype(BF16)
    f = _dot(act, wd_ref[...])
    y_ref[...] = x + _rms_rows(f, gfpost_ref[...])


def _post(x2, oa, ob, oc, od, w_out4, g_post, g_pre, wg, wu, wd, gf_post, tm):
    n, d = x2.shape
    dff = wg.shape[1]
    tok = lambda w: pl.BlockSpec((tm, w), lambda i: (i, 0))
    row = pl.BlockSpec((1, d), lambda i: (0, 0))
    return pl.pallas_call(
        _post_kernel,
        grid=(n // tm,),
        in_specs=[tok(d), tok(MIX_W), tok(MIX_W), tok(MIX_W), tok(MIX_W),
                  pl.BlockSpec((4, MIX_W, d), lambda i: (0, 0, 0)), row, row,
                  pl.BlockSpec((d, dff), lambda i: (0, 0)),
                  pl.BlockSpec((d, dff), lambda i: (0, 0)),
                  pl.BlockSpec((dff, d), lambda i: (0, 0)), row],
        out_specs=tok(d),
        out_shape=jax.ShapeDtypeStruct((n, d), F32),
        compiler_params=_params("parallel"),
        name="out_proj_ffn",
    )(x2, oa, ob, oc, od, w_out4, g_post, g_pre, wg, wu, wd, gf_post)


def _angles(pos, rot_dim, theta):
    inv = theta ** (-jnp.arange(0, rot_dim, 2, dtype=F32) / rot_dim)
    return pos.astype(F32)[:, None] * inv[None, :]


def _rope_tables(t):
    rows = t // GRID_W
    row_idx = jnp.repeat(jnp.arange(rows), GRID_W)
    col_idx = jnp.tile(jnp.arange(GRID_W), rows)
    pos = jnp.arange(t)
    ar = _angles(row_idx, HEAD_DIM // 2, A_THETA)
    ac = _angles(col_idx, HEAD_DIM // 2, A_THETA)
    cos_a = jnp.concatenate([jnp.cos(ar), jnp.cos(ar), jnp.cos(ac), jnp.cos(ac)], axis=1)
    sin_a = jnp.concatenate([-jnp.sin(ar), jnp.sin(ar), -jnp.sin(ac), jnp.sin(ac)], axis=1)
    tabs_a = (jnp.tile(cos_a, (1, 4)), jnp.tile(sin_a, (1, 4)), jnp.tile(cos_a, (1, 2)), jnp.tile(sin_a, (1, 2)))
    ab = _angles(pos, B_ROT, B_THETA)
    pad1 = jnp.ones((t, B_SUB - B_ROT), F32)
    pad0 = jnp.zeros((t, B_SUB - B_ROT), F32)
    cos_b = jnp.concatenate([jnp.cos(ab), jnp.cos(ab), pad1], axis=1)
    sin_b = jnp.concatenate([-jnp.sin(ab), jnp.sin(ab), pad0], axis=1)
    tabs_b = (jnp.tile(cos_b, (1, 8)), jnp.tile(sin_b, (1, 8)))
    ang_c = _angles(pos, HEAD_DIM, C_THETA)
    cos_c = jnp.concatenate([jnp.cos(ang_c), jnp.cos(ang_c)], axis=1)
    sin_c = jnp.concatenate([-jnp.sin(ang_c), jnp.sin(ang_c)], axis=1)
    tabs_c = (jnp.tile(cos_c, (1, 4)), jnp.tile(sin_c, (1, 4)))
    return tabs_a, tabs_b, tabs_c


def _tile(n, pref):
    while n % pref:
        pref //= 2
    return pref


def _layer_weights(l, w):
    d_model = w["w_in"].shape[1]
    row = lambda a: a.reshape(1, -1).astype(F32)
    zeros = lambda r: jnp.zeros((r, MIX_W), F32)
    w_lora = w["d_w_up"].shape[2]
    return {
        "g_mix_pre": row(w["norm_mix_pre"][l]), "g_mix_post": row(w["norm_mix_post"][l]),
        "g_ffn_pre": row(w["norm_ffn_pre"][l]), "g_ffn_post": row(w["norm_ffn_post"][l]),
        "w_abc": w["w_in"][l][:, :ABC_COLS].astype(BF16), "w_d": w["w_in"][l][:, ABC_COLS:].astype(BF16),
        "w_out4": w["w_out"][l].reshape(4, MIX_W, d_model).astype(BF16),
        "a_gq": jnp.tile(row(w["a_q_gain"][l]), (1, A_HEADS)),
        "a_gk": jnp.tile(row(w["a_k_gain"][l]), (1, A_KV_HEADS)),
        "b_lambda": w["b_lambda"][l].astype(F32), "b_gain": row(w["b_subln_gain"][l]),
        "c_gain": row(w["c_gn_gain"][l]),
        "wkv": {
            "mu_prev": row(w["d_mu_prev"][l]), "mu_next": row(w["d_mu_next"][l]),
            "w0": w["d_w0"][l].astype(F32),
            "wup_f": jnp.concatenate([w["d_w_up"][l, 0], zeros(w_lora)], axis=0).astype(BF16),
            "wup_b": jnp.concatenate([zeros(w_lora), w["d_w_up"][l, 1]], axis=0).astype(BF16),
            "a0": row(w["d_a0"][l]),
            "aup": jnp.concatenate([w["d_a_up"][l], zeros(64)], axis=0).astype(BF16),
            "gup1": jnp.concatenate([zeros(64), w["d_g_up"][l][:64]], axis=0).astype(BF16),
            "gup2": w["d_g_up"][l][64:].astype(BF16),
            "k_k": row(w["d_k_k"][l]), "k_a": row(w["d_k_a"][l]), "r_k": row(w["d_r_k"][l]),
        },
        "gn_w": row(w["d_gn_w"][l]), "gn_b": row(w["d_gn_b"][l]),
        "wg": w["ffn_w_gate"][l].astype(BF16), "wu": w["ffn_w_up"][l].astype(BF16),
        "wd": w["ffn_w_down"][l].astype(BF16),
    }


def _trunk(x, layers):
    b, t, d = x.shape
    n = b * t
    tabs_a, tabs_b, tabs_c = _rope_tables(t)
    tm_proj = _tile(n, 512)
    tm_post = _tile(n, 256)
    tq = _tile(t, 256)
    tm_wkv = _tile(t, 256)
    chunk = 32
    x2 = x.reshape(n, d)
    for l, p in enumerate(layers):
        lam_init = 0.8 - 0.6 * math.exp(-0.3 * l)
        zabc, zd = _in_proj(x2, p["g_mix_pre"], p["w_abc"], p["w_d"], tm_proj)
        zabc = zabc.reshape(b, t, ABC_COLS)
        zd = zd.reshape(b, t, D_COLS)
        o_a = _mixer_gqa(zabc, tabs_a, p["a_gq"], p["a_gk"], tq)
        o_b = _mixer_diff(zabc, tabs_b, p["b_lambda"], p["b_gain"], lam_init, tq)
        o_c = _mixer_ret(zabc, tabs_c, p["c_gain"], _tile(t, 256))
        r, kh, v, kk, bb, ldf, ldb, g, bonus = _wkv_prep(zd, p["wkv"], tm_wkv)
        yf, yb = _wkv_scan(r, kh, v, kk, bb, ldf, ldb, tm_wkv, chunk)
        o_d = _wkv_fin(yf.reshape(n, MIX_W), yb.reshape(n, MIX_W), bonus.reshape(n, MIX_W),
                       g.reshape(n, MIX_W), p["gn_w"], p["gn_b"], tm_proj)
        x2 = _post(x2, o_a.reshape(n, MIX_W), o_b.reshape(n, MIX_W), o_c.reshape(n, MIX_W), o_d,
                   p["w_out4"], p["g_mix_post"], p["g_ffn_pre"], p["wg"], p["wu"], p["wd"],
                   p["g_ffn_post"], tm_post)
    return x2.reshape(b, t, d)


def kernel(x_prompt, x_sample, norm_mix_pre, norm_mix_post, norm_ffn_pre, norm_ffn_post, w_in, w_out,
           a_q_gain, a_k_gain, b_lambda, b_subln_gain, c_gn_gain, d_mu_prev, d_mu_next, d_w0, d_w_up,
           d_a0, d_a_up, d_g_up, d_k_k, d_k_a, d_r_k, d_gn_w, d_gn_b, ffn_w_gate, ffn_w_up, ffn_w_down):
    w = {
        "norm_mix_pre": norm_mix_pre, "norm_mix_post": norm_mix_post,
        "norm_ffn_pre": norm_ffn_pre, "norm_ffn_post": norm_ffn_post,
        "w_in": w_in, "w_out": w_out, "a_q_gain": a_q_gain, "a_k_gain": a_k_gain,
        "b_lambda": b_lambda, "b_subln_gain": b_subln_gain, "c_gn_gain": c_gn_gain,
        "d_mu_prev": d_mu_prev, "d_mu_next": d_mu_next, "d_w0": d_w0, "d_w_up": d_w_up,
        "d_a0": d_a0, "d_a_up": d_a_up, "d_g_up": d_g_up, "d_k_k": d_k_k, "d_k_a": d_k_a,
        "d_r_k": d_r_k, "d_gn_w": d_gn_w, "d_gn_b": d_gn_b,
        "ffn_w_gate": ffn_w_gate, "ffn_w_up": ffn_w_up, "ffn_w_down": ffn_w_down,
    }
    layers = [_layer_weights(l, w) for l in range(w_in.shape[0])]
    return (_trunk(x_prompt, layers), _trunk(x_sample, layers))
```

```python
import functools
import math

import jax
import jax.numpy as jnp
from jax import lax
from jax.experimental import pallas as pl
from jax.experimental.pallas import tpu as pltpu

F32 = jnp.float32
BF16 = jnp.bfloat16

HEAD_DIM = 64
GRID_W = 64
NORM_EPS = 1e-6
A_HEADS, A_KV_HEADS, A_THETA = 4, 2, 10000.0
B_HEADS, B_SUB, B_ROT, B_THETA = 4, 32, 8, 500000.0
C_HEADS, C_THETA = 4, 10000.0
D_HEADS = 4
WKV_GN_EPS = 64e-5
MIX_W = 256
ABC_COLS = 2304
D_COLS = 1088
BF16_SUBLANES = 16
VEXT_ROWS = HEAD_DIM + BF16_SUBLANES
LOG2E = math.log2(math.e)
KEY_CHUNK = 512
MAX_ROWS = 32
VMEM_LIMIT = 56 * 1024 * 1024


def _params(*sem):
    return pltpu.CompilerParams(dimension_semantics=sem, vmem_limit_bytes=VMEM_LIMIT)


def _rms_rows(x, gain):
    return x * lax.rsqrt(jnp.mean(x * x, axis=-1, keepdims=True) + NORM_EPS) * gain


def _split2(x):
    hi = x.astype(BF16)
    lo = (x - hi.astype(F32)).astype(BF16)
    return hi, lo


def _split3(x):
    h1 = x.astype(BF16)
    r1 = x - h1.astype(F32)
    h2 = r1.astype(BF16)
    h3 = (r1 - h2.astype(F32)).astype(BF16)
    return h1, h2, h3


def _head_sum(x, seg):
    w = x.shape[-1]
    r = lax.broadcasted_iota(jnp.int32, (w, w), 0) // seg
    c = lax.broadcasted_iota(jnp.int32, (w, w), 1) // seg
    bd = jnp.where(r == c, 1.0, 0.0).astype(BF16)
    hi, lo = _split2(x)
    return (jnp.dot(hi, bd, preferred_element_type=F32)
            + jnp.dot(lo, bd, preferred_element_type=F32))


def _rope(x, cos, sin, half):
    w = x.shape[-1]
    lane = lax.broadcasted_iota(jnp.int32, x.shape, 1)
    nxt = pltpu.roll(x, w - half, 1)
    prv = pltpu.roll(x, half, 1)
    return x * cos + jnp.where((lane % (2 * half)) < half, nxt, prv) * sin


def _sigmoid(x):
    return 1.0 / (1.0 + jnp.exp(-x))


def _dot(a, b):
    return jnp.dot(a, b, preferred_element_type=F32)


def _dot_nt(a, b):
    return lax.dot_general(a, b, (((1,), (1,)), ((), ())), preferred_element_type=F32)


def _in_proj_kernel(x_ref, g_ref, wabc_ref, wd_ref, zabc_ref, zd_ref):
    h = _rms_rows(x_ref[...], g_ref[...]).astype(BF16)
    zabc_ref[...] = _dot(h, wabc_ref[...]).astype(BF16)
    zd_ref[...] = _dot(h, wd_ref[...])


def _in_proj(x2, gain, w_abc, w_d, tm):
    n, d = x2.shape
    return pl.pallas_call(
        _in_proj_kernel,
        grid=(n // tm,),
        in_specs=[
            pl.BlockSpec((tm, d), lambda i: (i, 0)),
            pl.BlockSpec((1, d), lambda i: (0, 0)),
            pl.BlockSpec((d, ABC_COLS), lambda i: (0, 0)),
            pl.BlockSpec((d, D_COLS), lambda i: (0, 0)),
        ],
        out_specs=[
            pl.BlockSpec((tm, ABC_COLS), lambda i: (i, 0)),
            pl.BlockSpec((tm, D_COLS), lambda i: (i, 0)),
        ],
        out_shape=[
            jax.ShapeDtypeStruct((n, ABC_COLS), BF16),
            jax.ShapeDtypeStruct((n, D_COLS), F32),
        ],
        compiler_params=_params("parallel"),
        name="in_proj",
    )(x2, gain, w_abc, w_d)


def _value_ext_t(v_t):
    row = lax.broadcasted_iota(jnp.int32, (VEXT_ROWS - HEAD_DIM, v_t.shape[1]), 0)
    return jnp.concatenate([v_t, jnp.where(row == 0, 1.0, 0.0)], axis=0).astype(BF16)


def _attend_t(k_ref, q_ts, vx_refs):
    t = k_ref.shape[0]
    ck = min(t, KEY_CHUNK)
    n_chunks = t // ck
    nh = len(q_ts)
    tq = q_ts[0].shape[1]
    s_cur = m_cur = None
    outs = []
    keys = [slice(c * ck, (c + 1) * ck) for c in range(n_chunks)]
    for j in range(nh + 1):
        s_next, m_acc, o_acc, p_prev = [], None, None, None
        for c in range(n_chunks + 1):
            if j < nh and c < n_chunks:
                s = _dot(k_ref[keys[c], :], q_ts[j])
                s_next.append(s)
                mc = jnp.max(s.reshape(ck // MAX_ROWS, MAX_ROWS, tq), axis=0)
                m_acc = mc if m_acc is None else jnp.maximum(m_acc, mc)
            if j > 0:
                p = jnp.exp2(s_cur[c] - m_cur).astype(BF16) if c < n_chunks else None
                if p_prev is not None:
                    o = _dot(vx_refs[j - 1][:, keys[c - 1]], p_prev)
                    o_acc = o if o_acc is None else o_acc + o
                p_prev = p
        if j > 0:
            outs.append(o_acc[:HEAD_DIM] / o_acc[HEAD_DIM:HEAD_DIM + 1])
        if j < nh:
            m_cur = jnp.max(m_acc, axis=0, keepdims=True)
            s_cur = s_next
    return outs


def _gqa_kernel(q_ref, k_ref, v_ref, cq_ref, sq_ref, ck_ref, sk_ref, gq_ref, gk_ref, o_ref,
                k_s, vx_s):
    @pl.when(pl.program_id(1) == 0)
    def _():
        k = k_ref[0].astype(F32)
        k = k * lax.rsqrt(_head_sum(k * k, HEAD_DIM) * (1.0 / HEAD_DIM) + NORM_EPS) * gk_ref[...]
        k_s[...] = _rope(k, ck_ref[...], sk_ref[...], HEAD_DIM // 4).astype(BF16)
        v_t = v_ref[0].astype(F32).T
        for g in range(A_KV_HEADS):
            vx_s[g] = _value_ext_t(v_t[g * HEAD_DIM:(g + 1) * HEAD_DIM])

    q = q_ref[0].astype(F32)
    q = q * lax.rsqrt(_head_sum(q * q, HEAD_DIM) * (1.0 / HEAD_DIM) + NORM_EPS) * gq_ref[...]
    q_t = (_rope(q, cq_ref[...], sq_ref[...], HEAD_DIM // 4) * (HEAD_DIM ** -0.5 * LOG2E)).T
    zeros = jnp.zeros((HEAD_DIM, q_t.shape[1]), F32)
    q_ts, vx_refs = [], []
    for h in range(A_HEADS):
        g = h // (A_HEADS // A_KV_HEADS)
        rows = [zeros] * A_KV_HEADS
        rows[g] = q_t[h * HEAD_DIM:(h + 1) * HEAD_DIM]
        q_ts.append(jnp.concatenate(rows, axis=0).astype(BF16))
        vx_refs.append(vx_s.at[g])
    o_ref[0] = jnp.concatenate(_attend_t(k_s, q_ts, vx_refs), axis=0).T.astype(BF16)


def _mixer_gqa(zabc, tabs, gq, gk, tq):
    b, t, _ = zabc.shape
    cq, sq, ck, sk = tabs
    return pl.pallas_call(
        _gqa_kernel,
        grid=(b, t // tq),
        in_specs=[
            pl.BlockSpec((1, tq, 256), lambda bi, i: (bi, i, 0)),
            pl.BlockSpec((1, t, 128), lambda bi, i: (bi, 0, 2)),
            pl.BlockSpec((1, t, 128), lambda bi, i: (bi, 0, 3)),
            pl.BlockSpec((tq, 256), lambda bi, i: (i, 0)),
            pl.BlockSpec((tq, 256), lambda bi, i: (i, 0)),
            pl.BlockSpec((t, 128), lambda bi, i: (0, 0)),
            pl.BlockSpec((t, 128), lambda bi, i: (0, 0)),
            pl.BlockSpec((1, 256), lambda bi, i: (0, 0)),
            pl.BlockSpec((1, 128), lambda bi, i: (0, 0)),
        ],
        out_specs=pl.BlockSpec((1, tq, MIX_W), lambda bi, i: (bi, i, 0)),
        out_shape=jax.ShapeDtypeStruct((b, t, MIX_W), BF16),
        scratch_shapes=[
            pltpu.VMEM((t, A_KV_HEADS * HEAD_DIM), BF16),
            pltpu.VMEM((A_KV_HEADS, VEXT_ROWS, t), BF16),
        ],
        compiler_params=_params("parallel", "arbitrary"),
        name="mixer_gqa",
    )(zabc, zabc, zabc, cq, sq, ck, sk, gq, gk)


def _diff_kernel(lam_init, q_ref, k_ref, v_ref, cos_ref, sin_ref, cosk_ref, sink_ref, lam_ref, gain_ref,
                 o_ref, k_s, vx_s):
    @pl.when(pl.program_id(1) == 0)
    def _():
        k_s[...] = _rope(k_ref[0].astype(F32), cosk_ref[...], sink_ref[...], B_ROT // 2).astype(BF16)
        v_t = v_ref[0].astype(F32).T
        for h in range(B_HEADS):
            vx_s[h] = _value_ext_t(v_t[h * HEAD_DIM:(h + 1) * HEAD_DIM])

    lp = lam_ref[...]
    lam = (jnp.exp(jnp.sum(lp[0:1] * lp[1:2], axis=1, keepdims=True))
           - jnp.exp(jnp.sum(lp[2:3] * lp[3:4], axis=1, keepdims=True)) + lam_init)
    q = _rope(q_ref[0].astype(F32), cos_ref[...], sin_ref[...], B_ROT // 2) * (B_SUB ** -0.5 * LOG2E)
    q_t = q.T
    sub = lax.broadcasted_iota(jnp.int32, q_t.shape, 0) // B_SUB
    parts = _attend_t(k_s, [jnp.where(sub == j, q_t, 0.0).astype(BF16) for j in range(2 * B_HEADS)],
                      [vx_s.at[j // 2] for j in range(2 * B_HEADS)])
    outs = []
    for h in range(B_HEADS):
        o = parts[2 * h] - lam * parts[2 * h + 1]
        o = o * lax.rsqrt(jnp.mean(o * o, axis=0, keepdims=True) + NORM_EPS) * gain_ref[...]
        outs.append(o * (1.0 - lam_init))
    o_ref[0] = jnp.concatenate(outs, axis=0).T.astype(BF16)


def _mixer_diff(zabc, tabs, lam_params, gain, lam_init, tq):
    b, t, _ = zabc.shape
    cos, sin = tabs
    return pl.pallas_call(
        functools.partial(_diff_kernel, lam_init),
        grid=(b, t // tq),
        in_specs=[
            pl.BlockSpec((1, tq, 256), lambda bi, i: (bi, i, 2)),
            pl.BlockSpec((1, t, 256), lambda bi, i: (bi, 0, 3)),
            pl.BlockSpec((1, t, 256), lambda bi, i: (bi, 0, 4)),
            pl.BlockSpec((tq, 256), lambda bi, i: (i, 0)),
            pl.BlockSpec((tq, 256), lambda bi, i: (i, 0)),
            pl.BlockSpec((t, 256), lambda bi, i: (0, 0)),
            pl.BlockSpec((t, 256), lambda bi, i: (0, 0)),
            pl.BlockSpec((4, B_SUB), lambda bi, i: (0, 0)),
            pl.BlockSpec((HEAD_DIM, 1), lambda bi, i: (0, 0)),
        ],
        out_specs=pl.BlockSpec((1, tq, MIX_W), lambda bi, i: (bi, i, 0)),
        out_shape=jax.ShapeDtypeStruct((b, t, MIX_W), BF16),
        scratch_shapes=[
            pltpu.VMEM((t, 2 * B_HEADS * B_SUB), BF16),
            pltpu.VMEM((B_HEADS, VEXT_ROWS, t), BF16),
        ],
        compiler_params=_params("parallel", "arbitrary"),
        name="mixer_diff",
    )(zabc, zabc, zabc, cos, sin, cos, sin, lam_params, gain)


def _ret_log_gammas():
    lg = [math.log1p(-(2.0 ** (-5.0 - h))) for h in range(C_HEADS)]
    return lg, lg[::-1]


def _lane_consts(vals, shape):
    head = lax.broadcasted_iota(jnp.int32, shape, len(shape) - 1) // HEAD_DIM
    out = jnp.full(shape, vals[-1], F32)
    for h in range(len(vals) - 2, -1, -1):
        out = jnp.where(head == h, vals[h], out)
    return out


def _ret_kernel(chunk, q_ref, k_ref, v_ref, g_ref, cos_ref, sin_ref, gain_ref, o_ref,
                qr_s, kr_s, acc_s, sf_s, sb_s):
    t = q_ref.shape[1]
    c = chunk
    n = t // c
    lgf, lgb = _ret_log_gammas()
    qr_s[...] = _rope(q_ref[0].astype(F32), cos_ref[...], sin_ref[...], HEAD_DIM // 2).astype(BF16)
    kr_s[...] = (_rope(k_ref[0].astype(F32), cos_ref[...], sin_ref[...], HEAD_DIM // 2)
                 * (HEAD_DIM ** -0.5)).astype(BF16)

    ti = lax.broadcasted_iota(jnp.int32, (c, c), 0)
    si = lax.broadcasted_iota(jnp.int32, (c, c), 1)
    dist = (ti - si).astype(F32)
    lane_head = lax.broadcasted_iota(jnp.int32, (1, MIX_W), 1) // HEAD_DIM
    row = lax.broadcasted_iota(jnp.int32, (c, MIX_W), 0).astype(F32)
    lgf_l = _lane_consts(lgf, (c, MIX_W))
    lgb_l = _lane_consts(lgb, (c, MIX_W))
    qdec_f = jnp.exp(lgf_l * (row + 1.0))
    kdec_f = jnp.exp(lgf_l * (c - 1.0 - row))
    qdec_b = jnp.exp(lgb_l * (c - row))
    kdec_b = jnp.exp(lgb_l * row)
    r2 = lax.broadcasted_iota(jnp.int32, (MIX_W, MIX_W), 0) // HEAD_DIM
    c2 = lax.broadcasted_iota(jnp.int32, (MIX_W, MIX_W), 1) // HEAD_DIM
    same_head = r2 == c2
    gf_blk = jnp.where(same_head, jnp.exp(_lane_consts(lgf, (MIX_W, MIX_W)) * c), 0.0)
    gb_blk = jnp.where(same_head, jnp.exp(_lane_consts(lgb, (MIX_W, MIX_W)) * c), 0.0)

    def intra(qc, kc, vc):
        out = jnp.zeros((c, MIX_W), F32)
        for h in range(C_HEADS):
            mh = (lane_head == h).astype(F32)
            dm = jnp.where(dist > 0, jnp.exp(lgf[h] * dist),
                           jnp.where(dist < 0, jnp.exp(-lgb[h] * dist), 2.0))
            s = _dot_nt((qc * mh).astype(BF16), kc) * dm
            out = out + _dot(s.astype(BF16), (vc * mh).astype(BF16))
        return out

    sf_s[...] = jnp.zeros_like(sf_s)
    sb_s[...] = jnp.zeros_like(sb_s)

    def fwd(i, carry):
        rows = pl.ds(pl.multiple_of(i * c, c), c)
        qc = qr_s[rows, :].astype(F32)
        kc = kr_s[rows, :]
        vc = v_ref[0, rows, :].astype(F32)
        o = intra(qc, kc, vc) + _dot((qc * qdec_f).astype(BF16), sf_s[...].astype(BF16))
        acc_s[rows, :] = o
        kd = (kc.astype(F32) * kdec_f).T.astype(BF16)
        sf_s[...] = gf_blk * sf_s[...] + jnp.where(same_head, _dot(kd, vc.astype(BF16)), 0.0)
        return carry

    lax.fori_loop(0, n, fwd, 0)

    def bwd(i, carry):
        rows = pl.ds(pl.multiple_of((n - 1 - i) * c, c), c)
        qc = qr_s[rows, :].astype(F32)
        kc = kr_s[rows, :].astype(F32)
        vc = v_ref[0, rows, :]
        acc_s[rows, :] = acc_s[rows, :] + _dot((qc * qdec_b).astype(BF16), sb_s[...].astype(BF16))
        kd = (kc * kdec_b).T.astype(BF16)
        sb_s[...] = gb_blk * sb_s[...] + jnp.where(same_head, _dot(kd, vc), 0.0)
        return carry

    lax.fori_loop(0, n, bwd, 0)

    o = acc_s[...]
    o = o * lax.rsqrt(_head_sum(o * o, HEAD_DIM) * (1.0 / HEAD_DIM) + NORM_EPS) * gain_ref[...]
    g = g_ref[0].astype(F32)
    o_ref[0] = (o * (g * _sigmoid(g))).astype(BF16)


def _mixer_ret(zabc, tabs, gain, chunk):
    b, t, _ = zabc.shape
    cos, sin = tabs
    blk = lambda j: pl.BlockSpec((1, t, 256), lambda bi: (bi, 0, j))
    return pl.pallas_call(
        functools.partial(_ret_kernel, chunk),
        grid=(b,),
        in_specs=[blk(5), blk(6), blk(7), blk(8),
                  pl.BlockSpec((t, 256), lambda bi: (0, 0)),
                  pl.BlockSpec((t, 256), lambda bi: (0, 0)),
                  pl.BlockSpec((1, 256), lambda bi: (0, 0))],
        out_specs=pl.BlockSpec((1, t, MIX_W), lambda bi: (bi, 0, 0)),
        out_shape=jax.ShapeDtypeStruct((b, t, MIX_W), BF16),
        scratch_shapes=[
            pltpu.VMEM((t, MIX_W), BF16),
            pltpu.VMEM((t, MIX_W), BF16),
            pltpu.VMEM((t, MIX_W), F32),
            pltpu.VMEM((MIX_W, MIX_W), F32),
            pltpu.VMEM((MIX_W, MIX_W), F32),
        ],
        compiler_params=_params("parallel"),
        name="mixer_ret",
    )(zabc, zabc, zabc, zabc, cos, sin, gain)


def _wkv_prep_kernel(z_ref, zp_ref, zn_ref, mup_ref, mun_ref, w0_ref, wupf_ref, wupb_ref, a0_ref, aup_ref,
                     gup1_ref, gup2_ref, kk_ref, ka_ref, rk_ref,
                     r_o, k_o, v_o, kk_o, b_o, ldf_o, ldb_o, g_o, bonus_o):
    i = pl.program_id(1)
    last = pl.num_programs(1) - 1
    z = z_ref[0]
    tm = z.shape[0]
    row = lax.broadcasted_iota(jnp.int32, z.shape, 0)
    prev_row = zp_ref[0, 0, 7:8, :] * jnp.where(i > 0, 1.0, 0.0)
    next_row = zn_ref[0, 0, 0:1, :] * jnp.where(i < last, 1.0, 0.0)
    z_prev = jnp.where(row == 0, prev_row, pltpu.roll(z, 1, 0))
    z_next = jnp.where(row == tm - 1, next_row, pltpu.roll(z, tm - 1, 0))
    u = z + mup_ref[...] * (z_prev - z) + mun_ref[...] * (z_next - z)
    r = u[:, 0:256]
    k = u[:, 256:512]
    v = u[:, 512:768]
    wd = jnp.tanh(u[:, 768:896]).astype(BF16)
    ag = u[:, 896:1024]
    g2 = u[:, 1024:1088]

    def log_decay(w0, wup):
        x = w0 + _dot(wd, wup)
        w = -(jnp.maximum(-x, 0.0) + jnp.log(1.0 + jnp.exp(-jnp.abs(x)))) - 0.5
        return -jnp.exp(w)

    ldf_o[0] = log_decay(w0_ref[0:1, :], wupf_ref[...])
    ldb_o[0] = log_decay(w0_ref[1:2, :], wupb_ref[...])
    a = _sigmoid(a0_ref[...] + _dot(ag.astype(BF16), aup_ref[...]))
    g_o[0] = (_dot(_sigmoid(ag).astype(BF16), gup1_ref[...])
              + _dot(_sigmoid(g2).astype(BF16), gup2_ref[...]))
    kk = k * kk_ref[...]
    kk = kk / jnp.maximum(jnp.sqrt(_head_sum(kk * kk, HEAD_DIM)), 1e-12)
    kh = k * (1.0 + (a - 1.0) * ka_ref[...])
    r_o[0] = r
    k_o[0] = kh
    v_o[0] = v
    kk_o[0] = kk
    b_o[0] = kk * a
    bonus_o[0] = _head_sum(r * kh * rk_ref[...], HEAD_DIM) * v


def _wkv_prep(zd, p, tm):
    b, t, _ = zd.shape
    zd8 = zd.reshape(b, t // 8, 8, D_COLS)
    r8 = tm // 8
    nb8 = t // 8
    row = lambda w: pl.BlockSpec((1, w), lambda bi, i: (0, 0))
    full = lambda a: pl.BlockSpec(a.shape, lambda bi, i: (0,) * a.ndim)
    out_spec = pl.BlockSpec((1, tm, MIX_W), lambda bi, i: (bi, i, 0))
    out_shape = jax.ShapeDtypeStruct((b, t, MIX_W), F32)
    return pl.pallas_call(
        _wkv_prep_kernel,
        grid=(b, t // tm),
        in_specs=[
            pl.BlockSpec((1, tm, D_COLS), lambda bi, i: (bi, i, 0)),
            pl.BlockSpec((1, 1, 8, D_COLS), lambda bi, i: (bi, jnp.maximum(i * r8 - 1, 0), 0, 0)),
            pl.BlockSpec((1, 1, 8, D_COLS), lambda bi, i: (bi, jnp.minimum((i + 1) * r8, nb8 - 1), 0, 0)),
            row(D_COLS), row(D_COLS),
            full(p["w0"]), full(p["wup_f"]), full(p["wup_b"]), row(256), full(p["aup"]),
            full(p["gup1"]), full(p["gup2"]), row(256), row(256), row(256),
        ],
        out_specs=[out_spec] * 9,
        out_shape=[out_shape] * 9,
        compiler_params=_params("parallel", "parallel"),
        name="wkv_prep",
    )(zd, zd8, zd8, p["mu_prev"], p["mu_next"], p["w0"], p["wup_f"], p["wup_b"], p["a0"], p["aup"],
      p["gup1"], p["gup2"], p["k_k"], p["k_a"], p["r_k"])


def _wkv_chunk_consts(chunk):
    c = chunk
    sc = D_HEADS * c
    ti = lax.broadcasted_iota(jnp.int32, (c, c), 0)
    si = lax.broadcasted_iota(jnp.int32, (c, c), 1)
    srow = lax.broadcasted_iota(jnp.int32, (sc, MIX_W), 0) // c
    slane = lax.broadcasted_iota(jnp.int32, (sc, MIX_W), 1) // HEAD_DIM
    gt = lax.broadcasted_iota(jnp.int32, (2 * sc, 2 * sc), 0)
    gs = lax.broadcasted_iota(jnp.int32, (2 * sc, 2 * sc), 1)
    t_in, s_in = gt % sc, gs % sc
    same = (t_in // c) == (s_in // c)
    upper = gt < sc
    st = lax.broadcasted_iota(jnp.int32, (sc, sc), 0)
    ss = lax.broadcasted_iota(jnp.int32, (sc, sc), 1)
    r2 = lax.broadcasted_iota(jnp.int32, (MIX_W, MIX_W), 0)
    c2 = lax.broadcasted_iota(jnp.int32, (MIX_W, MIX_W), 1)
    return {
        "tri": {False: jnp.where(si <= ti, 1.0, 0.0).astype(BF16), True: jnp.where(si >= ti, 1.0, 0.0).astype(BF16)},
        "hmask": srow == slane,
        "gmask": {False: same & ((s_in < t_in) | ((s_in == t_in) & ~upper)),
                  True: same & ((s_in > t_in) | ((s_in == t_in) & ~upper))},
        "eye": jnp.where(st == ss, 1.0, 0.0),
        "diag": r2 == c2,
    }


def _wkv_phase1(chunk, consts, probs):
    c = chunk
    sc = D_HEADS * c
    hmask = consts["hmask"]

    def stack(a):
        return jnp.where(hmask, jnp.concatenate([a] * D_HEADS, axis=0), 0.0).astype(BF16)

    pre = []
    for rev, r, kh, v, kk, bb, lw in probs:
        tri = consts["tri"][rev]
        l1, l2, l3 = _split3(lw)
        cl = _dot(tri, l1) + _dot(tri, l2) + _dot(tri, l3)
        tot = jnp.sum(lw, axis=0, keepdims=True)
        pre.append((cl, tot))
    ops = []
    for (rev, r, kh, v, kk, bb, lw), (cl, tot) in zip(probs, pre):
        w_inv = jnp.exp(-cl)
        w_end = jnp.exp(tot - cl)
        lhs = jnp.concatenate([stack(kk * jnp.exp(cl - lw)), stack(r * jnp.exp(cl))], axis=0)
        rhs = jnp.concatenate([stack(bb * w_inv), stack(kh * w_inv)], axis=0)
        ops.append((lhs, rhs, stack(v), stack(bb * w_end), stack(kh * w_end), jnp.exp(tot)))
    gram = [jnp.where(consts["gmask"][p[0]], _dot_nt(o[0], o[1]), 0.0) for p, o in zip(probs, ops)]
    l_ab = [g[:sc, :sc] for g in gram]
    l_ak = [g[:sc, sc:].astype(BF16) for g in gram]
    m_r = [g[sc:, :].astype(BF16) for g in gram]

    steps = int(math.log2(c))
    pw = [(-l).astype(BF16) for l in l_ab]
    inv = [consts["eye"] - l for l in l_ab]
    w = [_dot(a, o[2]).astype(BF16) for a, o in zip(l_ak, ops)]
    for k in range(1, steps):
        if k == 1:
            pw = [_dot(p, p).astype(BF16) for p in pw]
        if k < steps - 1:
            nxt = [_dot(p, jnp.concatenate([p, i.astype(BF16)], axis=1)) for p, i in zip(pw, inv)]
            inv = [i + n[:, sc:] for i, n in zip(inv, nxt)]
            pw = [n[:, :sc].astype(BF16) for n in nxt]
        else:
            inv = [i + _dot(p, i.astype(BF16)) for p, i in zip(pw, inv)]
    inv = [i.astype(BF16) for i in inv]

    gu = [_dot(i, jnp.concatenate([o[0][:sc], wv], axis=1)).astype(BF16)
          for i, o, wv in zip(inv, ops, w)]
    mgu = [_dot(m[:, :sc], x) for m, x in zip(m_r, gu)]
    mv = [_dot(m[:, sc:], o[2]) for m, o in zip(m_r, ops)]
    q_hat = [(o[0][sc:].astype(F32) - x[:, :MIX_W]).astype(BF16) for o, x in zip(ops, mgu)]
    y_hat = [a - x[:, MIX_W:] for a, x in zip(mv, mgu)]
    be_t = [o[3].astype(F32).T.astype(BF16) for o in ops]
    ke_t = [o[4].astype(F32).T.astype(BF16) for o in ops]
    bgu = [_dot(b_, x) for b_, x in zip(be_t, gu)]
    kv = [_dot(k_, o[2]) for k_, o in zip(ke_t, ops)]
    t_mat = [(jnp.where(consts["diag"], o[5], 0.0) - x[:, :MIX_W]).astype(BF16) for o, x in zip(ops, bgu)]
    d_mat = [a - x[:, MIX_W:] for a, x in zip(kv, bgu)]
    return list(zip(q_hat, y_hat, t_mat, d_mat))


def _wkv_scan_kernel(chunk, group, rf, kf, vf, kkf, bf, lf, rb, kb, vb, kkb, bb, lb, yf_o, yb_o, xf_s, xb_s):
    @pl.when(pl.program_id(1) == 0)
    def _():
        xf_s[...] = jnp.zeros_like(xf_s)
        xb_s[...] = jnp.zeros_like(xb_s)

    tm = rf.shape[1]
    n = tm // chunk
    consts = _wkv_chunk_consts(chunk)

    def advance(x_s, y_o, rows, sol):
        q_hat, y_hat, t_mat, d_mat = sol
        xb = x_s[...].astype(BF16)
        ys = _dot(q_hat, xb) + y_hat
        y = ys[0:chunk]
        for h in range(1, D_HEADS):
            y = y + ys[h * chunk:(h + 1) * chunk]
        y_o[0, rows, :] = y
        x_s[...] = _dot(t_mat, xb) + d_mat

    def body(j, carry):
        frows, brows, probs = [], [], []
        for u in range(group):
            fr = pl.ds(pl.multiple_of((j * group + u) * chunk, chunk), chunk)
            br = pl.ds(pl.multiple_of((n - 1 - j * group - u) * chunk, chunk), chunk)
            frows.append(fr)
            brows.append(br)
            probs.append((False, rf[0, fr, :], kf[0, fr, :], vf[0, fr, :], kkf[0, fr, :], bf[0, fr, :],
                          lf[0, fr, :]))
            probs.append((True, rb[0, br, :], kb[0, br, :], vb[0, br, :], kkb[0, br, :], bb[0, br, :],
                          lb[0, br, :]))
        sols = _wkv_phase1(chunk, consts, probs)
        for u in range(group):
            advance(xf_s, yf_o, frows[u], sols[2 * u])
            advance(xb_s, yb_o, brows[u], sols[2 * u + 1])
        return carry

    lax.fori_loop(0, n // group, body, 0)


def _wkv_scan(r, kh, v, kk, bb, ldf, ldb, tm, chunk, group):
    b, t, _ = r.shape
    nt = t // tm
    fspec = pl.BlockSpec((1, tm, MIX_W), lambda bi, i: (bi, i, 0))
    bspec = pl.BlockSpec((1, tm, MIX_W), lambda bi, i: (bi, nt - 1 - i, 0))
    shape = jax.ShapeDtypeStruct((b, t, MIX_W), F32)
    return pl.pallas_call(
        functools.partial(_wkv_scan_kernel, chunk, group),
        grid=(b, nt),
        in_specs=[fspec] * 6 + [bspec] * 6,
        out_specs=[fspec, bspec],
        out_shape=[shape, shape],
        scratch_shapes=[pltpu.VMEM((MIX_W, MIX_W), F32), pltpu.VMEM((MIX_W, MIX_W), F32)],
        compiler_params=_params("parallel", "arbitrary"),
        name="wkv_scan",
    )(r, kh, v, kk, bb, ldf, r, kh, v, kk, bb, ldb)


def _wkv_fin_kernel(yf_ref, yb_ref, bonus_ref, g_ref, gnw_ref, gnb_ref, o_ref):
    y = yf_ref[...] + yb_ref[...]
    mean = _head_sum(y, HEAD_DIM) * (1.0 / HEAD_DIM)
    yc = y - mean
    var = _head_sum(yc * yc, HEAD_DIM) * (1.0 / HEAD_DIM)
    yn = yc * lax.rsqrt(var + WKV_GN_EPS) * gnw_ref[...] + gnb_ref[...]
    o_ref[...] = ((yn + bonus_ref[...]) * g_ref[...]).astype(BF16)


def _wkv_fin(yf, yb, bonus, g, gn_w, gn_b, tm):
    n = yf.shape[0]
    tok = pl.BlockSpec((tm, MIX_W), lambda i: (i, 0))
    row = pl.BlockSpec((1, MIX_W), lambda i: (0, 0))
    return pl.pallas_call(
        _wkv_fin_kernel,
        grid=(n // tm,),
        in_specs=[tok, tok, tok, tok, row, row],
        out_specs=tok,
        out_shape=jax.ShapeDtypeStruct((n, MIX_W), BF16),
        compiler_params=_params("parallel"),
        name="wkv_fin",
    )(yf, yb, bonus, g, gn_w, gn_b)


def _post_kernel(x_ref, oa_ref, ob_ref, oc_ref, od_ref, wo_ref, gpost_ref, gpre_ref, wg_ref, wu_ref, wd_ref,
                 gfpost_ref, y_ref):
    mix = (_dot(oa_ref[...], wo_ref[0]) + _dot(ob_ref[...], wo_ref[1])
           + _dot(oc_ref[...], wo_ref[2]) + _dot(od_ref[...], wo_ref[3]))
    x = x_ref[...] + _rms_rows(mix, gpost_ref[...])
    h = _rms_rows(x, gpre_ref[...]).astype(BF16)
    gate = _dot(h, wg_ref[...])
    up = _dot(h, wu_ref[...])
    act = (gate * _sigmoid(gate) * up).astype(BF16)
    f = _dot(act, wd_ref[...])
    y_ref[...] = x + _rms_rows(f, gfpost_ref[...])


def _post(x2, oa, ob, oc, od, w_out4, g_post, g_pre, wg, wu, wd, gf_post, tm):
    n, d = x2.shape
    dff = wg.shape[1]
    tok = lambda w: pl.BlockSpec((tm, w), lambda i: (i, 0))
    row = pl.BlockSpec((1, d), lambda i: (0, 0))
    return pl.pallas_call(
        _post_kernel,
        grid=(n // tm,),
        in_specs=[tok(d), tok(MIX_W), tok(MIX_W), tok(MIX_W), tok(MIX_W),
                  pl.BlockSpec((4, MIX_W, d), lambda i: (0, 0, 0)), row, row,
                  pl.BlockSpec((d, dff), lambda i: (0, 0)),
                  pl.BlockSpec((d, dff), lambda i: (0, 0)),
                  pl.BlockSpec((dff, d), lambda i: (0, 0)), row],
        out_specs=tok(d),
        out_shape=jax.ShapeDtypeStruct((n, d), F32),
        compiler_params=_params("parallel"),
        name="out_proj_ffn",
    )(x2, oa, ob, oc, od, w_out4, g_post, g_pre, wg, wu, wd, gf_post)


def _angles(pos, rot_dim, theta):
    inv = theta ** (-jnp.arange(0, rot_dim, 2, dtype=F32) / rot_dim)
    return pos.astype(F32)[:, None] * inv[None, :]


def _rope_tables(t):
    rows = t // GRID_W
    row_idx = jnp.repeat(jnp.arange(rows), GRID_W)
    col_idx = jnp.tile(jnp.arange(GRID_W), rows)
    pos = jnp.arange(t)
    ar = _angles(row_idx, HEAD_DIM // 2, A_THETA)
    ac = _angles(col_idx, HEAD_DIM // 2, A_THETA)
    cos_a = jnp.concatenate([jnp.cos(ar), jnp.cos(ar), jnp.cos(ac), jnp.cos(ac)], axis=1)
    sin_a = jnp.concatenate([-jnp.sin(ar), jnp.sin(ar), -jnp.sin(ac), jnp.sin(ac)], axis=1)
    tabs_a = (jnp.tile(cos_a, (1, 4)), jnp.tile(sin_a, (1, 4)), jnp.tile(cos_a, (1, 2)), jnp.tile(sin_a, (1, 2)))
    ab = _angles(pos, B_ROT, B_THETA)
    pad1 = jnp.ones((t, B_SUB - B_ROT), F32)
    pad0 = jnp.zeros((t, B_SUB - B_ROT), F32)
    cos_b = jnp.concatenate([jnp.cos(ab), jnp.cos(ab), pad1], axis=1)
    sin_b = jnp.concatenate([-jnp.sin(ab), jnp.sin(ab), pad0], axis=1)
    tabs_b = (jnp.tile(cos_b, (1, 8)), jnp.tile(sin_b, (1, 8)))
    ang_c = _angles(pos, HEAD_DIM, C_THETA)
    cos_c = jnp.concatenate([jnp.cos(ang_c), jnp.cos(ang_c)], axis=1)
    sin_c = jnp.concatenate([-jnp.sin(ang_c), jnp.sin(ang_c)], axis=1)
    tabs_c = (jnp.tile(cos_c, (1, 4)), jnp.tile(sin_c, (1, 4)))
    return tabs_a, tabs_b, tabs_c


def _tile(n, pref):
    while n % pref:
        pref //= 2
    return pref


def _layer_weights(l, w):
    d_model = w["w_in"].shape[1]
    row = lambda a: a.reshape(1, -1).astype(F32)
    zeros = lambda r: jnp.zeros((r, MIX_W), F32)
    w_lora = w["d_w_up"].shape[2]
    return {
        "g_mix_pre": row(w["norm_mix_pre"][l]), "g_mix_post": row(w["norm_mix_post"][l]),
        "g_ffn_pre": row(w["norm_ffn_pre"][l]), "g_ffn_post": row(w["norm_ffn_post"][l]),
        "w_abc": w["w_in"][l][:, :ABC_COLS].astype(BF16), "w_d": w["w_in"][l][:, ABC_COLS:].astype(BF16),
        "w_out4": w["w_out"][l].reshape(4, MIX_W, d_model).astype(BF16),
        "a_gq": jnp.tile(row(w["a_q_gain"][l]), (1, A_HEADS)),
        "a_gk": jnp.tile(row(w["a_k_gain"][l]), (1, A_KV_HEADS)),
        "b_lambda": w["b_lambda"][l].astype(F32), "b_gain": w["b_subln_gain"][l].reshape(-1, 1).astype(F32),
        "c_gain": row(w["c_gn_gain"][l]),
        "wkv": {
            "mu_prev": row(w["d_mu_prev"][l]), "mu_next": row(w["d_mu_next"][l]),
            "w0": w["d_w0"][l].astype(F32),
            "wup_f": jnp.concatenate([w["d_w_up"][l, 0], zeros(w_lora)], axis=0).astype(BF16),
            "wup_b": jnp.concatenate([zeros(w_lora), w["d_w_up"][l, 1]], axis=0).astype(BF16),
            "a0": row(w["d_a0"][l]),
            "aup": jnp.concatenate([w["d_a_up"][l], zeros(64)], axis=0).astype(BF16),
            "gup1": jnp.concatenate([zeros(64), w["d_g_up"][l][:64]], axis=0).astype(BF16),
            "gup2": w["d_g_up"][l][64:].astype(BF16),
            "k_k": row(w["d_k_k"][l]), "k_a": row(w["d_k_a"][l]), "r_k": row(w["d_r_k"][l]),
        },
        "gn_w": row(w["d_gn_w"][l]), "gn_b": row(w["d_gn_b"][l]),
        "wg": w["ffn_w_gate"][l].astype(BF16), "wu": w["ffn_w_up"][l].astype(BF16),
        "wd": w["ffn_w_down"][l].astype(BF16),
    }


def _trunk(x, layers):
    b, t, d = x.shape
    n = b * t
    tabs_a, tabs_b, tabs_c = _rope_tables(t)
    tm_proj = _tile(n, 512)
    tm_post = _tile(n, 256)
    tq = _tile(t, 256)
    tm_wkv = _tile(t, 256)
    chunk = 32
    x2 = x.reshape(n, d)
    for l, p in enumerate(layers):
        lam_init = 0.8 - 0.6 * math.exp(-0.3 * l)
        zabc, zd = _in_proj(x2, p["g_mix_pre"], p["w_abc"], p["w_d"], tm_proj)
        zabc = zabc.reshape(b, t, ABC_COLS)
        zd = zd.reshape(b, t, D_COLS)
        o_a = _mixer_gqa(zabc, tabs_a, p["a_gq"], p["a_gk"], tq)
        o_b = _mixer_diff(zabc, tabs_b, p["b_lambda"], p["b_gain"], lam_init, tq)
        o_c = _mixer_ret(zabc, tabs_c, p["c_gain"], _tile(t, 256))
        r, kh, v, kk, bb, ldf, ldb, g, bonus = _wkv_prep(zd, p["wkv"], tm_wkv)
        yf, yb = _wkv_scan(r, kh, v, kk, bb, ldf, ldb, tm_wkv, chunk, min(4, tm_wkv // chunk))
        o_d = _wkv_fin(yf.reshape(n, MIX_W), yb.reshape(n, MIX_W), bonus.reshape(n, MIX_W),
                       g.reshape(n, MIX_W), p["gn_w"], p["gn_b"], tm_proj)
        x2 = _post(x2, o_a.reshape(n, MIX_W), o_b.reshape(n, MIX_W), o_c.reshape(n, MIX_W), o_d,
                   p["w_out4"], p["g_mix_post"], p["g_ffn_pre"], p["wg"], p["wu"], p["wd"],
                   p["g_ffn_post"], tm_post)
    return x2.reshape(b, t, d)


def kernel(x_prompt, x_sample, norm_mix_pre, norm_mix_post, norm_ffn_pre, norm_ffn_post, w_in, w_out,
           a_q_gain, a_k_gain, b_lambda, b_subln_gain, c_gn_gain, d_mu_prev, d_mu_next, d_w0, d_w_up,
           d_a0, d_a_up, d_g_up, d_k_k, d_k_a, d_r_k, d_gn_w, d_gn_b, ffn_w_gate, ffn_w_up, ffn_w_down):
    w = {
        "norm_mix_pre": norm_mix_pre, "norm_mix_post": norm_mix_post,
        "norm_ffn_pre": norm_ffn_pre, "norm_ffn_post": norm_ffn_post,
        "w_in": w_in, "w_out": w_out, "a_q_gain": a_q_gain, "a_k_gain": a_k_gain,
        "b_lambda": b_lambda, "b_subln_gain": b_subln_gain, "c_gn_gain": c_gn_gain,
        "d_mu_prev": d_mu_prev, "d_mu_next": d_mu_next, "d_w0": d_w0, "d_w_up": d_w_up,
        "d_a0": d_a0, "d_a_up": d_a_up, "d_g_up": d_g_up, "d_k_k": d_k_k, "d_k_a": d_k_a,
        "d_r_k": d_r_k, "d_gn_w": d_gn_w, "d_gn_b": d_gn_b,
        "ffn_w_gate": ffn_w_gate, "ffn_w_up": ffn_w_up, "ffn_w_down": ffn_w_down,
    }
    layers = [_layer_weights(l, w) for l in range(w_in.shape[0])]
    return (_trunk(x_prompt, layers), _trunk(x_sample, layers))
```

```python
import functools
import math

import jax
import jax.numpy as jnp
from jax import lax
from jax.experimental import pallas as pl
from jax.experimental.pallas import tpu as pltpu

F32 = jnp.float32
BF16 = jnp.bfloat16

HEAD_DIM = 64
GRID_W = 64
NORM_EPS = 1e-6
A_HEADS, A_KV_HEADS, A_THETA = 4, 2, 10000.0
B_HEADS, B_SUB, B_ROT, B_THETA = 4, 32, 8, 500000.0
C_HEADS, C_THETA = 4, 10000.0
D_HEADS = 4
WKV_GN_EPS = 64e-5
MIX_W = 256
ABC_COLS = 2304
D_COLS = 1088
BF16_SUBLANES = 16
VEXT_ROWS = HEAD_DIM + BF16_SUBLANES
LOG2E = math.log2(math.e)
KEY_CHUNK = 256
SHIFT_LIMIT = 100.0
VMEM_LIMIT = 56 * 1024 * 1024


def _params(*sem):
    return pltpu.CompilerParams(dimension_semantics=sem, vmem_limit_bytes=VMEM_LIMIT)


def _rms_rows(x, gain):
    return x * lax.rsqrt(jnp.mean(x * x, axis=-1, keepdims=True) + NORM_EPS) * gain


def _split2(x):
    hi = x.astype(BF16)
    lo = (x - hi.astype(F32)).astype(BF16)
    return hi, lo


def _split3(x):
    h1 = x.astype(BF16)
    r1 = x - h1.astype(F32)
    h2 = r1.astype(BF16)
    h3 = (r1 - h2.astype(F32)).astype(BF16)
    return h1, h2, h3


def _head_sum(x, seg):
    w = x.shape[-1]
    r = lax.broadcasted_iota(jnp.int32, (w, w), 0) // seg
    c = lax.broadcasted_iota(jnp.int32, (w, w), 1) // seg
    bd = jnp.where(r == c, 1.0, 0.0).astype(BF16)
    hi, lo = _split2(x)
    return (jnp.dot(hi, bd, preferred_element_type=F32)
            + jnp.dot(lo, bd, preferred_element_type=F32))


def _rope(x, cos, sin, half):
    w = x.shape[-1]
    lane = lax.broadcasted_iota(jnp.int32, x.shape, 1)
    nxt = pltpu.roll(x, w - half, 1)
    prv = pltpu.roll(x, half, 1)
    return x * cos + jnp.where((lane % (2 * half)) < half, nxt, prv) * sin


def _sigmoid(x):
    return 1.0 / (1.0 + jnp.exp(-x))


def _dot(a, b):
    return jnp.dot(a, b, preferred_element_type=F32)


def _dot_nt(a, b):
    return lax.dot_general(a, b, (((1,), (1,)), ((), ())), preferred_element_type=F32)


def _in_proj_kernel(x_ref, g_ref, wabc_ref, wd_ref, zabc_ref, zd_ref):
    h = _rms_rows(x_ref[...], g_ref[...]).astype(BF16)
    zabc_ref[...] = _dot(h, wabc_ref[...]).astype(BF16)
    zd_ref[...] = _dot(h, wd_ref[...])


def _in_proj(x2, gain, w_abc, w_d, tm):
    n, d = x2.shape
    return pl.pallas_call(
        _in_proj_kernel,
        grid=(n // tm,),
        in_specs=[
            pl.BlockSpec((tm, d), lambda i: (i, 0)),
            pl.BlockSpec((1, d), lambda i: (0, 0)),
            pl.BlockSpec((d, ABC_COLS), lambda i: (0, 0)),
            pl.BlockSpec((d, D_COLS), lambda i: (0, 0)),
        ],
        out_specs=[
            pl.BlockSpec((tm, ABC_COLS), lambda i: (i, 0)),
            pl.BlockSpec((tm, D_COLS), lambda i: (i, 0)),
        ],
        out_shape=[
            jax.ShapeDtypeStruct((n, ABC_COLS), BF16),
            jax.ShapeDtypeStruct((n, D_COLS), F32),
        ],
        compiler_params=_params("parallel"),
        name="in_proj",
    )(x2, gain, w_abc, w_d)


def _value_ext_t(v_t):
    row = lax.broadcasted_iota(jnp.int32, (VEXT_ROWS - HEAD_DIM, v_t.shape[1]), 0)
    return jnp.concatenate([v_t, jnp.where(row == 0, 1.0, 0.0)], axis=0).astype(BF16)


def _attend_exact(k_ref, q_ts, vx_refs):
    outs = []
    for q_t, vx in zip(q_ts, vx_refs):
        s_t = _dot(k_ref[...], q_t)
        p_t = jnp.exp2(s_t - jnp.max(s_t, axis=0, keepdims=True)).astype(BF16)
        o = _dot(vx[...], p_t)
        outs.append(o[:HEAD_DIM] / o[HEAD_DIM:HEAD_DIM + 1])
    return outs


def _attend_shifted(k_ref, q_ts, shifts, vx_refs):
    t = k_ref.shape[0]
    ck = min(t, KEY_CHUNK)
    steps = [(j, slice(c * ck, (c + 1) * ck)) for j in range(len(q_ts)) for c in range(t // ck)]
    acc = [None] * len(q_ts)
    s_prev = p_prev = None
    for i in range(len(steps) + 2):
        s_new = p_new = None
        if i < len(steps):
            j, keys = steps[i]
            s_new = _dot(k_ref[keys, :], q_ts[j])
        if 1 <= i <= len(steps):
            p_new = jnp.exp2(s_prev - shifts[steps[i - 1][0]]).astype(BF16)
        if i >= 2:
            j, keys = steps[i - 2]
            o = _dot(vx_refs[j][:, keys], p_prev)
            acc[j] = o if acc[j] is None else acc[j] + o
        s_prev, p_prev = s_new, p_new
    return [a[:HEAD_DIM] / a[HEAD_DIM:HEAD_DIM + 1] for a in acc]


def _attend_t(k_ref, kn_ref, q_t, head_rows, head_keys, vx_refs, finish):
    c = k_ref.shape[1]
    tq = q_t.shape[1]
    qsq = q_t * q_t
    q_ts, shifts = [], []
    for rows, keys in zip(head_rows, head_keys):
        pieces = [jnp.zeros((keys.start, tq), F32), q_t[rows], jnp.zeros((c - keys.stop, tq), F32)]
        q_ts.append(jnp.concatenate([x for x in pieces if x.shape[0]], axis=0).astype(BF16))
        qn = jnp.sqrt(jnp.sum(qsq[rows], axis=0, keepdims=True))
        shifts.append(qn * kn_ref[0:1, keys.start:keys.start + 1])
    worst = jnp.max(jnp.concatenate(shifts, axis=0))

    @pl.when(2.0 * worst < SHIFT_LIMIT)
    def _():
        finish(_attend_shifted(k_ref, q_ts, shifts, vx_refs))

    @pl.when(jnp.logical_not(2.0 * worst < SHIFT_LIMIT))
    def _():
        finish(_attend_exact(k_ref, q_ts, vx_refs))


def _gqa_kernel(q_ref, k_ref, v_ref, cq_ref, sq_ref, ck_ref, sk_ref, gq_ref, gk_ref, o_ref,
                k_s, kn_s, vx_s):
    @pl.when(pl.program_id(1) == 0)
    def _():
        k = k_ref[0].astype(F32)
        k = k * lax.rsqrt(_head_sum(k * k, HEAD_DIM) * (1.0 / HEAD_DIM) + NORM_EPS) * gk_ref[...]
        k = _rope(k, ck_ref[...], sk_ref[...], HEAD_DIM // 4)
        k_s[...] = k.astype(BF16)
        kn_s[...] = jnp.sqrt(jnp.max(_head_sum(k * k, HEAD_DIM), axis=0, keepdims=True))
        v_t = v_ref[0].astype(F32).T
        for g in range(A_KV_HEADS):
            vx_s[g] = _value_ext_t(v_t[g * HEAD_DIM:(g + 1) * HEAD_DIM])

    q = q_ref[0].astype(F32)
    q = q * lax.rsqrt(_head_sum(q * q, HEAD_DIM) * (1.0 / HEAD_DIM) + NORM_EPS) * gq_ref[...]
    q_t = (_rope(q, cq_ref[...], sq_ref[...], HEAD_DIM // 4) * (HEAD_DIM ** -0.5 * LOG2E)).T
    group = A_HEADS // A_KV_HEADS
    head_rows = [slice(h * HEAD_DIM, (h + 1) * HEAD_DIM) for h in range(A_HEADS)]
    head_keys = [slice((h // group) * HEAD_DIM, (h // group + 1) * HEAD_DIM) for h in range(A_HEADS)]

    def finish(outs):
        o_ref[0] = jnp.concatenate(outs, axis=0).T.astype(BF16)

    _attend_t(k_s, kn_s, q_t, head_rows, head_keys, [vx_s.at[h // group] for h in range(A_HEADS)], finish)


def _mixer_gqa(zabc, tabs, gq, gk, tq):
    b, t, _ = zabc.shape
    cq, sq, ck, sk = tabs
    return pl.pallas_call(
        _gqa_kernel,
        grid=(b, t // tq),
        in_specs=[
            pl.BlockSpec((1, tq, 256), lambda bi, i: (bi, i, 0)),
            pl.BlockSpec((1, t, 128), lambda bi, i: (bi, 0, 2)),
            pl.BlockSpec((1, t, 128), lambda bi, i: (bi, 0, 3)),
            pl.BlockSpec((tq, 256), lambda bi, i: (i, 0)),
            pl.BlockSpec((tq, 256), lambda bi, i: (i, 0)),
            pl.BlockSpec((t, 128), lambda bi, i: (0, 0)),
            pl.BlockSpec((t, 128), lambda bi, i: (0, 0)),
            pl.BlockSpec((1, 256), lambda bi, i: (0, 0)),
            pl.BlockSpec((1, 128), lambda bi, i: (0, 0)),
        ],
        out_specs=pl.BlockSpec((1, tq, MIX_W), lambda bi, i: (bi, i, 0)),
        out_shape=jax.ShapeDtypeStruct((b, t, MIX_W), BF16),
        scratch_shapes=[
            pltpu.VMEM((t, A_KV_HEADS * HEAD_DIM), BF16),
            pltpu.VMEM((1, A_KV_HEADS * HEAD_DIM), F32),
            pltpu.VMEM((A_KV_HEADS, VEXT_ROWS, t), BF16),
        ],
        compiler_params=_params("parallel", "arbitrary"),
        name="mixer_gqa",
    )(zabc, zabc, zabc, cq, sq, ck, sk, gq, gk)


def _diff_kernel(lam_init, q_ref, k_ref, v_ref, cos_ref, sin_ref, cosk_ref, sink_ref, lam_ref, gain_ref,
                 o_ref, k_s, kn_s, vx_s):
    @pl.when(pl.program_id(1) == 0)
    def _():
        k = _rope(k_ref[0].astype(F32), cosk_ref[...], sink_ref[...], B_ROT // 2)
        k_s[...] = k.astype(BF16)
        kn_s[...] = jnp.sqrt(jnp.max(_head_sum(k * k, B_SUB), axis=0, keepdims=True))
        v_t = v_ref[0].astype(F32).T
        for h in range(B_HEADS):
            vx_s[h] = _value_ext_t(v_t[h * HEAD_DIM:(h + 1) * HEAD_DIM])

    lp = lam_ref[...]
    lam = (jnp.exp(jnp.sum(lp[0:1] * lp[1:2], axis=1, keepdims=True))
           - jnp.exp(jnp.sum(lp[2:3] * lp[3:4], axis=1, keepdims=True)) + lam_init)
    q = _rope(q_ref[0].astype(F32), cos_ref[...], sin_ref[...], B_ROT // 2) * (B_SUB ** -0.5 * LOG2E)
    q_t = q.T
    sub_rows = [slice(j * B_SUB, (j + 1) * B_SUB) for j in range(2 * B_HEADS)]

    def finish(parts):
        outs = []
        for h in range(B_HEADS):
            o = parts[2 * h] - lam * parts[2 * h + 1]
            o = o * lax.rsqrt(jnp.mean(o * o, axis=0, keepdims=True) + NORM_EPS) * gain_ref[...]
            outs.append(o * (1.0 - lam_init))
        o_ref[0] = jnp.concatenate(outs, axis=0).T.astype(BF16)

    _attend_t(k_s, kn_s, q_t, sub_rows, sub_rows, [vx_s.at[j // 2] for j in range(2 * B_HEADS)], finish)


def _mixer_diff(zabc, tabs, lam_params, gain, lam_init, tq):
    b, t, _ = zabc.shape
    cos, sin = tabs
    return pl.pallas_call(
        functools.partial(_diff_kernel, lam_init),
        grid=(b, t // tq),
        in_specs=[
            pl.BlockSpec((1, tq, 256), lambda bi, i: (bi, i, 2)),
            pl.BlockSpec((1, t, 256), lambda bi, i: (bi, 0, 3)),
            pl.BlockSpec((1, t, 256), lambda bi, i: (bi, 0, 4)),
            pl.BlockSpec((tq, 256), lambda bi, i: (i, 0)),
            pl.BlockSpec((tq, 256), lambda bi, i: (i, 0)),
            pl.BlockSpec((t, 256), lambda bi, i: (0, 0)),
            pl.BlockSpec((t, 256), lambda bi, i: (0, 0)),
            pl.BlockSpec((4, B_SUB), lambda bi, i: (0, 0)),
            pl.BlockSpec((HEAD_DIM, 1), lambda bi, i: (0, 0)),
        ],
        out_specs=pl.BlockSpec((1, tq, MIX_W), lambda bi, i: (bi, i, 0)),
        out_shape=jax.ShapeDtypeStruct((b, t, MIX_W), BF16),
        scratch_shapes=[
            pltpu.VMEM((t, 2 * B_HEADS * B_SUB), BF16),
            pltpu.VMEM((1, 2 * B_HEADS * B_SUB), F32),
            pltpu.VMEM((B_HEADS, VEXT_ROWS, t), BF16),
        ],
        compiler_params=_params("parallel", "arbitrary"),
        name="mixer_diff",
    )(zabc, zabc, zabc, cos, sin, cos, sin, lam_params, gain)


def _ret_log_gammas():
    lg = [math.log1p(-(2.0 ** (-5.0 - h))) for h in range(C_HEADS)]
    return lg, lg[::-1]


def _lane_consts(vals, shape):
    head = lax.broadcasted_iota(jnp.int32, shape, len(shape) - 1) // HEAD_DIM
    out = jnp.full(shape, vals[-1], F32)
    for h in range(len(vals) - 2, -1, -1):
        out = jnp.where(head == h, vals[h], out)
    return out


def _ret_kernel(chunk, q_ref, k_ref, v_ref, g_ref, cos_ref, sin_ref, gain_ref, o_ref,
                qr_s, kr_s, acc_s, sf_s, sb_s):
    t = q_ref.shape[1]
    c = chunk
    n = t // c
    lgf, lgb = _ret_log_gammas()
    qr_s[...] = _rope(q_ref[0].astype(F32), cos_ref[...], sin_ref[...], HEAD_DIM // 2).astype(BF16)
    kr_s[...] = (_rope(k_ref[0].astype(F32), cos_ref[...], sin_ref[...], HEAD_DIM // 2)
                 * (HEAD_DIM ** -0.5)).astype(BF16)

    ti = lax.broadcasted_iota(jnp.int32, (c, c), 0)
    si = lax.broadcasted_iota(jnp.int32, (c, c), 1)
    dist = (ti - si).astype(F32)
    lane_head = lax.broadcasted_iota(jnp.int32, (1, MIX_W), 1) // HEAD_DIM
    row = lax.broadcasted_iota(jnp.int32, (c, MIX_W), 0).astype(F32)
    lgf_l = _lane_consts(lgf, (c, MIX_W))
    lgb_l = _lane_consts(lgb, (c, MIX_W))
    qdec_f = jnp.exp(lgf_l * (row + 1.0))
    kdec_f = jnp.exp(lgf_l * (c - 1.0 - row))
    qdec_b = jnp.exp(lgb_l * (c - row))
    kdec_b = jnp.exp(lgb_l * row)
    r2 = lax.broadcasted_iota(jnp.int32, (MIX_W, MIX_W), 0) // HEAD_DIM
    c2 = lax.broadcasted_iota(jnp.int32, (MIX_W, MIX_W), 1) // HEAD_DIM
    same_head = r2 == c2
    gf_blk = jnp.where(same_head, jnp.exp(_lane_consts(lgf, (MIX_W, MIX_W)) * c), 0.0)
    gb_blk = jnp.where(same_head, jnp.exp(_lane_consts(lgb, (MIX_W, MIX_W)) * c), 0.0)

    def intra(qc, kc, vc):
        out = jnp.zeros((c, MIX_W), F32)
        for h in range(C_HEADS):
            mh = (lane_head == h).astype(F32)
            dm = jnp.where(dist > 0, jnp.exp(lgf[h] * dist),
                           jnp.where(dist < 0, jnp.exp(-lgb[h] * dist), 2.0))
            s = _dot_nt((qc * mh).astype(BF16), kc) * dm
            out = out + _dot(s.astype(BF16), (vc * mh).astype(BF16))
        return out

    sf_s[...] = jnp.zeros_like(sf_s)
    sb_s[...] = jnp.zeros_like(sb_s)

    def fwd(i, carry):
        rows = pl.ds(pl.multiple_of(i * c, c), c)
        qc = qr_s[rows, :].astype(F32)
        kc = kr_s[rows, :]
        vc = v_ref[0, rows, :].astype(F32)
        o = intra(qc, kc, vc) + _dot((qc * qdec_f).astype(BF16), sf_s[...].astype(BF16))
        acc_s[rows, :] = o
        kd = (kc.astype(F32) * kdec_f).T.astype(BF16)
        sf_s[...] = gf_blk * sf_s[...] + jnp.where(same_head, _dot(kd, vc.astype(BF16)), 0.0)
        return carry

    lax.fori_loop(0, n, fwd, 0)

    def bwd(i, carry):
        rows = pl.ds(pl.multiple_of((n - 1 - i) * c, c), c)
        qc = qr_s[rows, :].astype(F32)
        kc = kr_s[rows, :].astype(F32)
        vc = v_ref[0, rows, :]
        acc_s[rows, :] = acc_s[rows, :] + _dot((qc * qdec_b).astype(BF16), sb_s[...].astype(BF16))
        kd = (kc * kdec_b).T.astype(BF16)
        sb_s[...] = gb_blk * sb_s[...] + jnp.where(same_head, _dot(kd, vc), 0.0)
        return carry

    lax.fori_loop(0, n, bwd, 0)

    o = acc_s[...]
    o = o * lax.rsqrt(_head_sum(o * o, HEAD_DIM) * (1.0 / HEAD_DIM) + NORM_EPS) * gain_ref[...]
    g = g_ref[0].astype(F32)
    o_ref[0] = (o * (g * _sigmoid(g))).astype(BF16)


def _mixer_ret(zabc, tabs, gain, chunk):
    b, t, _ = zabc.shape
    cos, sin = tabs
    blk = lambda j: pl.BlockSpec((1, t, 256), lambda bi: (bi, 0, j))
    return pl.pallas_call(
        functools.partial(_ret_kernel, chunk),
        grid=(b,),
        in_specs=[blk(5), blk(6), blk(7), blk(8),
                  pl.BlockSpec((t, 256), lambda bi: (0, 0)),
                  pl.BlockSpec((t, 256), lambda bi: (0, 0)),
                  pl.BlockSpec((1, 256), lambda bi: (0, 0))],
        out_specs=pl.BlockSpec((1, t, MIX_W), lambda bi: (bi, 0, 0)),
        out_shape=jax.ShapeDtypeStruct((b, t, MIX_W), BF16),
        scratch_shapes=[
            pltpu.VMEM((t, MIX_W), BF16),
            pltpu.VMEM((t, MIX_W), BF16),
            pltpu.VMEM((t, MIX_W), F32),
            pltpu.VMEM((MIX_W, MIX_W), F32),
            pltpu.VMEM((MIX_W, MIX_W), F32),
        ],
        compiler_params=_params("parallel"),
        name="mixer_ret",
    )(zabc, zabc, zabc, zabc, cos, sin, gain)


def _wkv_prep_kernel(z_ref, zp_ref, zn_ref, mup_ref, mun_ref, w0_ref, wupf_ref, wupb_ref, a0_ref, aup_ref,
                     gup1_ref, gup2_ref, kk_ref, ka_ref, rk_ref,
                     r_o, k_o, v_o, kk_o, b_o, ldf_o, ldb_o, g_o, bonus_o):
    i = pl.program_id(1)
    last = pl.num_programs(1) - 1
    z = z_ref[0]
    tm = z.shape[0]
    row = lax.broadcasted_iota(jnp.int32, z.shape, 0)
    prev_row = zp_ref[0, 0, 7:8, :] * jnp.where(i > 0, 1.0, 0.0)
    next_row = zn_ref[0, 0, 0:1, :] * jnp.where(i < last, 1.0, 0.0)
    z_prev = jnp.where(row == 0, prev_row, pltpu.roll(z, 1, 0))
    z_next = jnp.where(row == tm - 1, next_row, pltpu.roll(z, tm - 1, 0))
    u = z + mup_ref[...] * (z_prev - z) + mun_ref[...] * (z_next - z)
    r = u[:, 0:256]
    k = u[:, 256:512]
    v = u[:, 512:768]
    wd = jnp.tanh(u[:, 768:896]).astype(BF16)
    ag = u[:, 896:1024]
    g2 = u[:, 1024:1088]

    def log_decay(w0, wup):
        x = w0 + _dot(wd, wup)
        w = -(jnp.maximum(-x, 0.0) + jnp.log(1.0 + jnp.exp(-jnp.abs(x)))) - 0.5
        return -jnp.exp(w)

    ldf_o[0] = log_decay(w0_ref[0:1, :], wupf_ref[...])
    ldb_o[0] = log_decay(w0_ref[1:2, :], wupb_ref[...])
    a = _sigmoid(a0_ref[...] + _dot(ag.astype(BF16), aup_ref[...]))
    g_o[0] = (_dot(_sigmoid(ag).astype(BF16), gup1_ref[...])
              + _dot(_sigmoid(g2).astype(BF16), gup2_ref[...]))
    kk = k * kk_ref[...]
    kk = kk / jnp.maximum(jnp.sqrt(_head_sum(kk * kk, HEAD_DIM)), 1e-12)
    kh = k * (1.0 + (a - 1.0) * ka_ref[...])
    r_o[0] = r
    k_o[0] = kh
    v_o[0] = v
    kk_o[0] = kk
    b_o[0] = kk * a
    bonus_o[0] = _head_sum(r * kh * rk_ref[...], HEAD_DIM) * v


def _wkv_prep(zd, p, tm):
    b, t, _ = zd.shape
    zd8 = zd.reshape(b, t // 8, 8, D_COLS)
    r8 = tm // 8
    nb8 = t // 8
    row = lambda w: pl.BlockSpec((1, w), lambda bi, i: (0, 0))
    full = lambda a: pl.BlockSpec(a.shape, lambda bi, i: (0,) * a.ndim)
    out_spec = pl.BlockSpec((1, tm, MIX_W), lambda bi, i: (bi, i, 0))
    out_shape = jax.ShapeDtypeStruct((b, t, MIX_W), F32)
    return pl.pallas_call(
        _wkv_prep_kernel,
        grid=(b, t // tm),
        in_specs=[
            pl.BlockSpec((1, tm, D_COLS), lambda bi, i: (bi, i, 0)),
            pl.BlockSpec((1, 1, 8, D_COLS), lambda bi, i: (bi, jnp.maximum(i * r8 - 1, 0), 0, 0)),
            pl.BlockSpec((1, 1, 8, D_COLS), lambda bi, i: (bi, jnp.minimum((i + 1) * r8, nb8 - 1), 0, 0)),
            row(D_COLS), row(D_COLS),
            full(p["w0"]), full(p["wup_f"]), full(p["wup_b"]), row(256), full(p["aup"]),
            full(p["gup1"]), full(p["gup2"]), row(256), row(256), row(256),
        ],
        out_specs=[out_spec] * 9,
        out_shape=[out_shape] * 9,
        compiler_params=_params("parallel", "parallel"),
        name="wkv_prep",
    )(zd, zd8, zd8, p["mu_prev"], p["mu_next"], p["w0"], p["wup_f"], p["wup_b"], p["a0"], p["aup"],
      p["gup1"], p["gup2"], p["k_k"], p["k_a"], p["r_k"])


def _wkv_chunk_consts(chunk):
    c = chunk
    sc = D_HEADS * c
    ti = lax.broadcasted_iota(jnp.int32, (c, c), 0)
    si = lax.broadcasted_iota(jnp.int32, (c, c), 1)
    srow = lax.broadcasted_iota(jnp.int32, (sc, MIX_W), 0) // c
    slane = lax.broadcasted_iota(jnp.int32, (sc, MIX_W), 1) // HEAD_DIM
    gt = lax.broadcasted_iota(jnp.int32, (2 * sc, 2 * sc), 0)
    gs = lax.broadcasted_iota(jnp.int32, (2 * sc, 2 * sc), 1)
    t_in, s_in = gt % sc, gs % sc
    same = (t_in // c) == (s_in // c)
    upper = gt < sc
    st = lax.broadcasted_iota(jnp.int32, (sc, sc), 0)
    ss = lax.broadcasted_iota(jnp.int32, (sc, sc), 1)
    r2 = lax.broadcasted_iota(jnp.int32, (MIX_W, MIX_W), 0)
    c2 = lax.broadcasted_iota(jnp.int32, (MIX_W, MIX_W), 1)
    return {
        "tri": {False: jnp.where(si <= ti, 1.0, 0.0).astype(BF16), True: jnp.where(si >= ti, 1.0, 0.0).astype(BF16)},
        "hmask": srow == slane,
        "gmask": {False: same & ((s_in < t_in) | ((s_in == t_in) & ~upper)),
                  True: same & ((s_in > t_in) | ((s_in == t_in) & ~upper))},
        "eye": jnp.where(st == ss, 1.0, 0.0),
        "diag": r2 == c2,
    }


def _wkv_phase1(chunk, consts, probs):
    c = chunk
    sc = D_HEADS * c
    hmask = consts["hmask"]

    def stack(a):
        return jnp.where(hmask, jnp.concatenate([a] * D_HEADS, axis=0), 0.0).astype(BF16)

    pre = []
    for rev, r, kh, v, kk, bb, lw in probs:
        tri = consts["tri"][rev]
        l1, l2, l3 = _split3(lw)
        cl = _dot(tri, l1) + _dot(tri, l2) + _dot(tri, l3)
        tot = jnp.sum(lw, axis=0, keepdims=True)
        pre.append((cl, tot))
    ops = []
    for (rev, r, kh, v, kk, bb, lw), (cl, tot) in zip(probs, pre):
        w_inv = jnp.exp(-cl)
        w_end = jnp.exp(tot - cl)
        lhs = jnp.concatenate([stack(kk * jnp.exp(cl - lw)), stack(r * jnp.exp(cl))], axis=0)
        rhs = jnp.concatenate([stack(bb * w_inv), stack(kh * w_inv)], axis=0)
        ops.append((lhs, rhs, stack(v), stack(bb * w_end), stack(kh * w_end), jnp.exp(tot)))
    gram = [jnp.where(consts["gmask"][p[0]], _dot_nt(o[0], o[1]), 0.0) for p, o in zip(probs, ops)]
    l_ab = [g[:sc, :sc] for g in gram]
    l_ak = [g[:sc, sc:].astype(BF16) for g in gram]
    m_r = [g[sc:, :].astype(BF16) for g in gram]

    steps = int(math.log2(c))
    pw = [(-l).astype(BF16) for l in l_ab]
    inv = [consts["eye"] - l for l in l_ab]
    w = [_dot(a, o[2]).astype(BF16) for a, o in zip(l_ak, ops)]
    for k in range(1, steps):
        if k == 1:
            pw = [_dot(p, p).astype(BF16) for p in pw]
        if k < steps - 1:
            nxt = [_dot(p, jnp.concatenate([p, i.astype(BF16)], axis=1)) for p, i in zip(pw, inv)]
            inv = [i + n[:, sc:] for i, n in zip(inv, nxt)]
            pw = [n[:, :sc].astype(BF16) for n in nxt]
        else:
            inv = [i + _dot(p, i.astype(BF16)) for p, i in zip(pw, inv)]
    inv = [i.astype(BF16) for i in inv]

    gu = [_dot(i, jnp.concatenate([o[0][:sc], wv], axis=1)).astype(BF16)
          for i, o, wv in zip(inv, ops, w)]
    mgu = [_dot(m[:, :sc], x) for m, x in zip(m_r, gu)]
    mv = [_dot(m[:, sc:], o[2]) for m, o in zip(m_r, ops)]
    q_hat = [(o[0][sc:].astype(F32) - x[:, :MIX_W]).astype(BF16) for o, x in zip(ops, mgu)]
    y_hat = [a - x[:, MIX_W:] for a, x in zip(mv, mgu)]
    be_t = [o[3].astype(F32).T.astype(BF16) for o in ops]
    ke_t = [o[4].astype(F32).T.astype(BF16) for o in ops]
    bgu = [_dot(b_, x) for b_, x in zip(be_t, gu)]
    kv = [_dot(k_, o[2]) for k_, o in zip(ke_t, ops)]
    t_mat = [(jnp.where(consts["diag"], o[5], 0.0) - x[:, :MIX_W]).astype(BF16) for o, x in zip(ops, bgu)]
    d_mat = [a - x[:, MIX_W:] for a, x in zip(kv, bgu)]
    return list(zip(q_hat, y_hat, t_mat, d_mat))


def _wkv_scan_kernel(chunk, group, rf, kf, vf, kkf, bf, lf, rb, kb, vb, kkb, bb, lb, yf_o, yb_o, xf_s, xb_s):
    @pl.when(pl.program_id(1) == 0)
    def _():
        xf_s[...] = jnp.zeros_like(xf_s)
        xb_s[...] = jnp.zeros_like(xb_s)

    tm = rf.shape[1]
    n = tm // chunk
    consts = _wkv_chunk_consts(chunk)

    def advance(x_s, y_o, rows, sol):
        q_hat, y_hat, t_mat, d_mat = sol
        xb = x_s[...].astype(BF16)
        ys = _dot(q_hat, xb) + y_hat
        y = ys[0:chunk]
        for h in range(1, D_HEADS):
            y = y + ys[h * chunk:(h + 1) * chunk]
        y_o[0, rows, :] = y
        x_s[...] = _dot(t_mat, xb) + d_mat

    def body(j, carry):
        frows, brows, probs = [], [], []
        for u in range(group):
            fr = pl.ds(pl.multiple_of((j * group + u) * chunk, chunk), chunk)
            br = pl.ds(pl.multiple_of((n - 1 - j * group - u) * chunk, chunk), chunk)
            frows.append(fr)
            brows.append(br)
            probs.append((False, rf[0, fr, :], kf[0, fr, :], vf[0, fr, :], kkf[0, fr, :], bf[0, fr, :],
                          lf[0, fr, :]))
            probs.append((True, rb[0, br, :], kb[0, br, :], vb[0, br, :], kkb[0, br, :], bb[0, br, :],
                          lb[0, br, :]))
        sols = _wkv_phase1(chunk, consts, probs)
        for u in range(group):
            advance(xf_s, yf_o, frows[u], sols[2 * u])
            advance(xb_s, yb_o, brows[u], sols[2 * u + 1])
        return carry

    lax.fori_loop(0, n // group, body, 0)


def _wkv_scan(r, kh, v, kk, bb, ldf, ldb, tm, chunk, group):
    b, t, _ = r.shape
    nt = t // tm
    fspec = pl.BlockSpec((1, tm, MIX_W), lambda bi, i: (bi, i, 0))
    bspec = pl.BlockSpec((1, tm, MIX_W), lambda bi, i: (bi, nt - 1 - i, 0))
    shape = jax.ShapeDtypeStruct((b, t, MIX_W), F32)
    return pl.pallas_call(
        functools.partial(_wkv_scan_kernel, chunk, group),
        grid=(b, nt),
        in_specs=[fspec] * 6 + [bspec] * 6,
        out_specs=[fspec, bspec],
        out_shape=[shape, shape],
        scratch_shapes=[pltpu.VMEM((MIX_W, MIX_W), F32), pltpu.VMEM((MIX_W, MIX_W), F32)],
        compiler_params=_params("parallel", "arbitrary"),
        name="wkv_scan",
    )(r, kh, v, kk, bb, ldf, r, kh, v, kk, bb, ldb)


def _wkv_fin_kernel(yf_ref, yb_ref, bonus_ref, g_ref, gnw_ref, gnb_ref, o_ref):
    y = yf_ref[...] + yb_ref[...]
    mean = _head_sum(y, HEAD_DIM) * (1.0 / HEAD_DIM)
    yc = y - mean
    var = _head_sum(yc * yc, HEAD_DIM) * (1.0 / HEAD_DIM)
    yn = yc * lax.rsqrt(var + WKV_GN_EPS) * gnw_ref[...] + gnb_ref[...]
    o_ref[...] = ((yn + bonus_ref[...]) * g_ref[...]).astype(BF16)


def _wkv_fin(yf, yb, bonus, g, gn_w, gn_b, tm):
    n = yf.shape[0]
    tok = pl.BlockSpec((tm, MIX_W), lambda i: (i, 0))
    row = pl.BlockSpec((1, MIX_W), lambda i: (0, 0))
    return pl.pallas_call(
        _wkv_fin_kernel,
        grid=(n // tm,),
        in_specs=[tok, tok, tok, tok, row, row],
        out_specs=tok,
        out_shape=jax.ShapeDtypeStruct((n, MIX_W), BF16),
        compiler_params=_params("parallel"),
        name="wkv_fin",
    )(yf, yb, bonus, g, gn_w, gn_b)


def _post_kernel(x_ref, oa_ref, ob_ref, oc_ref, od_ref, wo_ref, gpost_ref, gpre_ref, wg_ref, wu_ref, wd_ref,
                 gfpost_ref, y_ref):
    mix = (_dot(oa_ref[...], wo_ref[0]) + _dot(ob_ref[...], wo_ref[1])
           + _dot(oc_ref[...], wo_ref[2]) + _dot(od_ref[...], wo_ref[3]))
    x = x_ref[...] + _rms_rows(mix, gpost_ref[...])
    h = _rms_rows(x, gpre_ref[...]).astype(BF16)
    gate = _dot(h, wg_ref[...])
    up = _dot(h, wu_ref[...])
    act = (gate * _sigmoid(gate) * up).astype(BF16)
    f = _dot(act, wd_ref[...])
    y_ref[...] = x + _rms_rows(f, gfpost_ref[...])


def _post(x2, oa, ob, oc, od, w_out4, g_post, g_pre, wg, wu, wd, gf_post, tm):
    n, d = x2.shape
    dff = wg.shape[1]
    tok = lambda w: pl.BlockSpec((tm, w), lambda i: (i, 0))
    row = pl.BlockSpec((1, d), lambda i: (0, 0))
    return pl.pallas_call(
        _post_kernel,
        grid=(n // tm,),
        in_specs=[tok(d), tok(MIX_W), tok(MIX_W), tok(MIX_W), tok(MIX_W),
                  pl.BlockSpec((4, MIX_W, d), lambda i: (0, 0, 0)), row, row,
                  pl.BlockSpec((d, dff), lambda i: (0, 0)),
                  pl.BlockSpec((d, dff), lambda i: (0, 0)),
                  pl.BlockSpec((dff, d), lambda i: (0, 0)), row],
        out_specs=tok(d),
        out_shape=jax.ShapeDtypeStruct((n, d), F32),
        compiler_params=_params("parallel"),
        name="out_proj_ffn",
    )(x2, oa, ob, oc, od, w_out4, g_post, g_pre, wg, wu, wd, gf_post)


def _angles(pos, rot_dim, theta):
    inv = theta ** (-jnp.arange(0, rot_dim, 2, dtype=F32) / rot_dim)
    return pos.astype(F32)[:, None] * inv[None, :]


def _rope_tables(t):
    rows = t // GRID_W
    row_idx = jnp.repeat(jnp.arange(rows), GRID_W)
    col_idx = jnp.tile(jnp.arange(GRID_W), rows)
    pos = jnp.arange(t)
    ar = _angles(row_idx, HEAD_DIM // 2, A_THETA)
    ac = _angles(col_idx, HEAD_DIM // 2, A_THETA)
    cos_a = jnp.concatenate([jnp.cos(ar), jnp.cos(ar), jnp.cos(ac), jnp.cos(ac)], axis=1)
    sin_a = jnp.concatenate([-jnp.sin(ar), jnp.sin(ar), -jnp.sin(ac), jnp.sin(ac)], axis=1)
    tabs_a = (jnp.tile(cos_a, (1, 4)), jnp.tile(sin_a, (1, 4)), jnp.tile(cos_a, (1, 2)), jnp.tile(sin_a, (1, 2)))
    ab = _angles(pos, B_ROT, B_THETA)
    pad1 = jnp.ones((t, B_SUB - B_ROT), F32)
    pad0 = jnp.zeros((t, B_SUB - B_ROT), F32)
    cos_b = jnp.concatenate([jnp.cos(ab), jnp.cos(ab), pad1], axis=1)
    sin_b = jnp.concatenate([-jnp.sin(ab), jnp.sin(ab), pad0], axis=1)
    tabs_b = (jnp.tile(cos_b, (1, 8)), jnp.tile(sin_b, (1, 8)))
    ang_c = _angles(pos, HEAD_DIM, C_THETA)
    cos_c = jnp.concatenate([jnp.cos(ang_c), jnp.cos(ang_c)], axis=1)
    sin_c = jnp.concatenate([-jnp.sin(ang_c), jnp.sin(ang_c)], axis=1)
    tabs_c = (jnp.tile(cos_c, (1, 4)), jnp.tile(sin_c, (1, 4)))
    return tabs_a, tabs_b, tabs_c


def _tile(n, pref):
    while n % pref:
        pref //= 2
    return pref


def _layer_weights(l, w):
    d_model = w["w_in"].shape[1]
    row = lambda a: a.reshape(1, -1).astype(F32)
    zeros = lambda r: jnp.zeros((r, MIX_W), F32)
    w_lora = w["d_w_up"].shape[2]
    return {
        "g_mix_pre": row(w["norm_mix_pre"][l]), "g_mix_post": row(w["norm_mix_post"][l]),
        "g_ffn_pre": row(w["norm_ffn_pre"][l]), "g_ffn_post": row(w["norm_ffn_post"][l]),
        "w_abc": w["w_in"][l][:, :ABC_COLS].astype(BF16), "w_d": w["w_in"][l][:, ABC_COLS:].astype(BF16),
        "w_out4": w["w_out"][l].reshape(4, MIX_W, d_model).astype(BF16),
        "a_gq": jnp.tile(row(w["a_q_gain"][l]), (1, A_HEADS)),
        "a_gk": jnp.tile(row(w["a_k_gain"][l]), (1, A_KV_HEADS)),
        "b_lambda": w["b_lambda"][l].astype(F32), "b_gain": w["b_subln_gain"][l].reshape(-1, 1).astype(F32),
        "c_gain": row(w["c_gn_gain"][l]),
        "wkv": {
            "mu_prev": row(w["d_mu_prev"][l]), "mu_next": row(w["d_mu_next"][l]),
            "w0": w["d_w0"][l].astype(F32),
            "wup_f": jnp.concatenate([w["d_w_up"][l, 0], zeros(w_lora)], axis=0).astype(BF16),
            "wup_b": jnp.concatenate([zeros(w_lora), w["d_w_up"][l, 1]], axis=0).astype(BF16),
            "a0": row(w["d_a0"][l]),
            "aup": jnp.concatenate([w["d_a_up"][l], zeros(64)], axis=0).astype(BF16),
            "gup1": jnp.concatenate([zeros(64), w["d_g_up"][l][:64]], axis=0).astype(BF16),
            "gup2": w["d_g_up"][l][64:].astype(BF16),
            "k_k": row(w["d_k_k"][l]), "k_a": row(w["d_k_a"][l]), "r_k": row(w["d_r_k"][l]),
        },
        "gn_w": row(w["d_gn_w"][l]), "gn_b": row(w["d_gn_b"][l]),
        "wg": w["ffn_w_gate"][l].astype(BF16), "wu": w["ffn_w_up"][l].astype(BF16),
        "wd": w["ffn_w_down"][l].astype(BF16),
    }


def _trunk(x, layers):
    b, t, d = x.shape
    n = b * t
    tabs_a, tabs_b, tabs_c = _rope_tables(t)
    tm_proj = _tile(n, 512)
    tm_post = _tile(n, 256)
    tq = _tile(t, 256)
    tm_wkv = _tile(t, 256)
    chunk = 32
    x2 = x.reshape(n, d)
    for l, p in enumerate(layers):
        lam_init = 0.8 - 0.6 * math.exp(-0.3 * l)
        zabc, zd = _in_proj(x2, p["g_mix_pre"], p["w_abc"], p["w_d"], tm_proj)
        zabc = zabc.reshape(b, t, ABC_COLS)
        zd = zd.reshape(b, t, D_COLS)
        o_a = _mixer_gqa(zabc, tabs_a, p["a_gq"], p["a_gk"], tq)
        o_b = _mixer_diff(zabc, tabs_b, p["b_lambda"], p["b_gain"], lam_init, tq)
        o_c = _mixer_ret(zabc, tabs_c, p["c_gain"], _tile(t, 256))
        r, kh, v, kk, bb, ldf, ldb, g, bonus = _wkv_prep(zd, p["wkv"], tm_wkv)
        yf, yb = _wkv_scan(r, kh, v, kk, bb, ldf, ldb, tm_wkv, chunk, min(4, tm_wkv // chunk))
        o_d = _wkv_fin(yf.reshape(n, MIX_W), yb.reshape(n, MIX_W), bonus.reshape(n, MIX_W),
                       g.reshape(n, MIX_W), p["gn_w"], p["gn_b"], tm_proj)
        x2 = _post(x2, o_a.reshape(n, MIX_W), o_b.reshape(n, MIX_W), o_c.reshape(n, MIX_W), o_d,
                   p["w_out4"], p["g_mix_post"], p["g_ffn_pre"], p["wg"], p["wu"], p["wd"],
                   p["g_ffn_post"], tm_post)
    return x2.reshape(b, t, d)


def kernel(x_prompt, x_sample, norm_mix_pre, norm_mix_post, norm_ffn_pre, norm_ffn_post, w_in, w_out,
           a_q_gain, a_k_gain, b_lambda, b_subln_gain, c_gn_gain, d_mu_prev, d_mu_next, d_w0, d_w_up,
           d_a0, d_a_up, d_g_up, d_k_k, d_k_a, d_r_k, d_gn_w, d_gn_b, ffn_w_gate, ffn_w_up, ffn_w_down):
    w = {
        "norm_mix_pre": norm_mix_pre, "norm_mix_post": norm_mix_post,
        "norm_ffn_pre": norm_ffn_pre, "norm_ffn_post": norm_ffn_post,
        "w_in": w_in, "w_out": w_out, "a_q_gain": a_q_gain, "a_k_gain": a_k_gain,
        "b_lambda": b_lambda, "b_subln_gain": b_subln_gain, "c_gn_gain": c_gn_gain,
        "d_mu_prev": d_mu_prev, "d_mu_next": d_mu_next, "d_w0": d_w0, "d_w_up": d_w_up,
        "d_a0": d_a0, "d_a_up": d_a_up, "d_g_up": d_g_up, "d_k_k": d_k_k, "d_k_a": d_k_a,
        "d_r_k": d_r_k, "d_gn_w": d_gn_w, "d_gn_b": d_gn_b,
        "ffn_w_gate": ffn_w_gate, "ffn_w_up": ffn_w_up, "ffn_w_down": ffn_w_down,
    }
    layers = [_layer_weights(l, w) for l in range(w_in.shape[0])]
    return (_trunk(x_prompt, layers), _trunk(x_sample, layers))
```

```python
import functools
import math

import jax
import jax.numpy as jnp
from jax import lax
from jax.experimental import pallas as pl
from jax.experimental.pallas import tpu as pltpu

F32 = jnp.float32
BF16 = jnp.bfloat16

HEAD_DIM = 64
GRID_W = 64
NORM_EPS = 1e-6
A_HEADS, A_KV_HEADS, A_THETA = 4, 2, 10000.0
B_HEADS, B_SUB, B_ROT, B_THETA = 4, 32, 8, 500000.0
C_HEADS, C_THETA = 4, 10000.0
D_HEADS = 4
WKV_GN_EPS = 64e-5
MIX_W = 256
ABC_COLS = 2304
D_COLS = 1088
BF16_SUBLANES = 16
VEXT_ROWS = HEAD_DIM + BF16_SUBLANES
LOG2E = math.log2(math.e)
KEY_CHUNK = 256
SHIFT_LIMIT = 100.0
VMEM_LIMIT = 56 * 1024 * 1024


def _params(*sem):
    return pltpu.CompilerParams(dimension_semantics=sem, vmem_limit_bytes=VMEM_LIMIT)


def _rms_rows(x, gain):
    return x * lax.rsqrt(jnp.mean(x * x, axis=-1, keepdims=True) + NORM_EPS) * gain


def _split2(x):
    hi = x.astype(BF16)
    lo = (x - hi.astype(F32)).astype(BF16)
    return hi, lo


def _split3(x):
    h1 = x.astype(BF16)
    r1 = x - h1.astype(F32)
    h2 = r1.astype(BF16)
    h3 = (r1 - h2.astype(F32)).astype(BF16)
    return h1, h2, h3


def _head_sum(x, seg):
    w = x.shape[-1]
    r = lax.broadcasted_iota(jnp.int32, (w, w), 0) // seg
    c = lax.broadcasted_iota(jnp.int32, (w, w), 1) // seg
    bd = jnp.where(r == c, 1.0, 0.0).astype(BF16)
    hi, lo = _split2(x)
    return (jnp.dot(hi, bd, preferred_element_type=F32)
            + jnp.dot(lo, bd, preferred_element_type=F32))


def _rope(x, cos, sin, half):
    w = x.shape[-1]
    lane = lax.broadcasted_iota(jnp.int32, x.shape, 1)
    nxt = pltpu.roll(x, w - half, 1)
    prv = pltpu.roll(x, half, 1)
    return x * cos + jnp.where((lane % (2 * half)) < half, nxt, prv) * sin


def _sigmoid(x):
    return 1.0 / (1.0 + jnp.exp(-x))


def _dot(a, b):
    return jnp.dot(a, b, preferred_element_type=F32)


def _dot_nt(a, b):
    return lax.dot_general(a, b, (((1,), (1,)), ((), ())), preferred_element_type=F32)


def _in_proj_kernel(x_ref, g_ref, wabc_ref, wd_ref, zabc_ref, zd_ref):
    h = _rms_rows(x_ref[...], g_ref[...]).astype(BF16)
    zabc_ref[...] = _dot(h, wabc_ref[...]).astype(BF16)
    zd_ref[...] = _dot(h, wd_ref[...])


def _in_proj(x2, gain, w_abc, w_d, tm):
    n, d = x2.shape
    return pl.pallas_call(
        _in_proj_kernel,
        grid=(n // tm,),
        in_specs=[
            pl.BlockSpec((tm, d), lambda i: (i, 0)),
            pl.BlockSpec((1, d), lambda i: (0, 0)),
            pl.BlockSpec((d, ABC_COLS), lambda i: (0, 0)),
            pl.BlockSpec((d, D_COLS), lambda i: (0, 0)),
        ],
        out_specs=[
            pl.BlockSpec((tm, ABC_COLS), lambda i: (i, 0)),
            pl.BlockSpec((tm, D_COLS), lambda i: (i, 0)),
        ],
        out_shape=[
            jax.ShapeDtypeStruct((n, ABC_COLS), BF16),
            jax.ShapeDtypeStruct((n, D_COLS), F32),
        ],
        compiler_params=_params("parallel"),
        name="in_proj",
    )(x2, gain, w_abc, w_d)


def _value_ext_t(v_t):
    row = lax.broadcasted_iota(jnp.int32, (VEXT_ROWS - HEAD_DIM, v_t.shape[1]), 0)
    return jnp.concatenate([v_t, jnp.where(row == 0, 1.0, 0.0)], axis=0).astype(BF16)


def _attend_exact(k_ref, q_ts, vx_refs):
    outs = []
    for q_t, vx in zip(q_ts, vx_refs):
        s_t = _dot(k_ref[...], q_t)
        p_t = jnp.exp2(s_t - jnp.max(s_t, axis=0, keepdims=True)).astype(BF16)
        o = _dot(vx[...], p_t)
        outs.append(o[:HEAD_DIM] / o[HEAD_DIM:HEAD_DIM + 1])
    return outs


def _attend_shifted(k_ref, q_ts, shifts, vx_refs):
    t = k_ref.shape[0]
    ck = min(t, KEY_CHUNK)
    steps = [(j, slice(c * ck, (c + 1) * ck)) for j in range(len(q_ts)) for c in range(t // ck)]
    acc = [None] * len(q_ts)
    s_prev = p_prev = None
    for i in range(len(steps) + 2):
        s_new = p_new = None
        if i < len(steps):
            j, keys = steps[i]
            s_new = _dot(k_ref[keys, :], q_ts[j])
        if 1 <= i <= len(steps):
            p_new = jnp.exp2(s_prev - shifts[steps[i - 1][0]]).astype(BF16)
        if i >= 2:
            j, keys = steps[i - 2]
            o = _dot(vx_refs[j][:, keys], p_prev)
            acc[j] = o if acc[j] is None else acc[j] + o
        s_prev, p_prev = s_new, p_new
    return [a[:HEAD_DIM] / a[HEAD_DIM:HEAD_DIM + 1] for a in acc]


def _attend_t(k_ref, kn_ref, q_t, head_rows, head_keys, vx_refs, finish):
    c = k_ref.shape[1]
    tq = q_t.shape[1]
    qsq = q_t * q_t
    q_ts, shifts = [], []
    for rows, keys in zip(head_rows, head_keys):
        pieces = [jnp.zeros((keys.start, tq), F32), q_t[rows], jnp.zeros((c - keys.stop, tq), F32)]
        q_ts.append(jnp.concatenate([x for x in pieces if x.shape[0]], axis=0).astype(BF16))
        qn = jnp.sqrt(jnp.sum(qsq[rows], axis=0, keepdims=True))
        shifts.append(qn * kn_ref[0:1, keys.start:keys.start + 1])
    worst = jnp.max(jnp.concatenate(shifts, axis=0))

    @pl.when(2.0 * worst < SHIFT_LIMIT)
    def _():
        finish(_attend_shifted(k_ref, q_ts, shifts, vx_refs))

    @pl.when(jnp.logical_not(2.0 * worst < SHIFT_LIMIT))
    def _():
        finish(_attend_exact(k_ref, q_ts, vx_refs))


def _gqa_kernel(q_ref, k_ref, v_ref, cq_ref, sq_ref, ck_ref, sk_ref, gq_ref, gk_ref, o_ref,
                k_s, kn_s, vx_s):
    @pl.when(pl.program_id(1) == 0)
    def _():
        k = k_ref[0].astype(F32)
        k = k * lax.rsqrt(_head_sum(k * k, HEAD_DIM) * (1.0 / HEAD_DIM) + NORM_EPS) * gk_ref[...]
        k = _rope(k, ck_ref[...], sk_ref[...], HEAD_DIM // 4)
        k_s[...] = k.astype(BF16)
        kn_s[...] = jnp.sqrt(jnp.max(_head_sum(k * k, HEAD_DIM), axis=0, keepdims=True))
        v_t = v_ref[0].astype(F32).T
        for g in range(A_KV_HEADS):
            vx_s[g] = _value_ext_t(v_t[g * HEAD_DIM:(g + 1) * HEAD_DIM])

    q = q_ref[0].astype(F32)
    q = q * lax.rsqrt(_head_sum(q * q, HEAD_DIM) * (1.0 / HEAD_DIM) + NORM_EPS) * gq_ref[...]
    q_t = (_rope(q, cq_ref[...], sq_ref[...], HEAD_DIM // 4) * (HEAD_DIM ** -0.5 * LOG2E)).T
    group = A_HEADS // A_KV_HEADS
    head_rows = [slice(h * HEAD_DIM, (h + 1) * HEAD_DIM) for h in range(A_HEADS)]
    head_keys = [slice((h // group) * HEAD_DIM, (h // group + 1) * HEAD_DIM) for h in range(A_HEADS)]

    def finish(outs):
        o_ref[0] = jnp.concatenate(outs, axis=0).T.astype(BF16)

    _attend_t(k_s, kn_s, q_t, head_rows, head_keys, [vx_s.at[h // group] for h in range(A_HEADS)], finish)


def _mixer_gqa(zabc, tabs, gq, gk, tq):
    b, t, _ = zabc.shape
    cq, sq, ck, sk = tabs
    return pl.pallas_call(
        _gqa_kernel,
        grid=(b, t // tq),
        in_specs=[
            pl.BlockSpec((1, tq, 256), lambda bi, i: (bi, i, 0)),
            pl.BlockSpec((1, t, 128), lambda bi, i: (bi, 0, 2)),
            pl.BlockSpec((1, t, 128), lambda bi, i: (bi, 0, 3)),
            pl.BlockSpec((tq, 256), lambda bi, i: (i, 0)),
            pl.BlockSpec((tq, 256), lambda bi, i: (i, 0)),
            pl.BlockSpec((t, 128), lambda bi, i: (0, 0)),
            pl.BlockSpec((t, 128), lambda bi, i: (0, 0)),
            pl.BlockSpec((1, 256), lambda bi, i: (0, 0)),
            pl.BlockSpec((1, 128), lambda bi, i: (0, 0)),
        ],
        out_specs=pl.BlockSpec((1, tq, MIX_W), lambda bi, i: (bi, i, 0)),
        out_shape=jax.ShapeDtypeStruct((b, t, MIX_W), BF16),
        scratch_shapes=[
            pltpu.VMEM((t, A_KV_HEADS * HEAD_DIM), BF16),
            pltpu.VMEM((1, A_KV_HEADS * HEAD_DIM), F32),
            pltpu.VMEM((A_KV_HEADS, VEXT_ROWS, t), BF16),
        ],
        compiler_params=_params("parallel", "arbitrary"),
        name="mixer_gqa",
    )(zabc, zabc, zabc, cq, sq, ck, sk, gq, gk)


def _diff_kernel(lam_init, q_ref, k_ref, v_ref, cos_ref, sin_ref, cosk_ref, sink_ref, lam_ref, gain_ref,
                 o_ref, k_s, kn_s, vx_s):
    @pl.when(pl.program_id(1) == 0)
    def _():
        k = _rope(k_ref[0].astype(F32), cosk_ref[...], sink_ref[...], B_ROT // 2)
        k_s[...] = k.astype(BF16)
        kn_s[...] = jnp.sqrt(jnp.max(_head_sum(k * k, B_SUB), axis=0, keepdims=True))
        v_t = v_ref[0].astype(F32).T
        for h in range(B_HEADS):
            vx_s[h] = _value_ext_t(v_t[h * HEAD_DIM:(h + 1) * HEAD_DIM])

    lp = lam_ref[...]
    lam = (jnp.exp(jnp.sum(lp[0:1] * lp[1:2], axis=1, keepdims=True))
           - jnp.exp(jnp.sum(lp[2:3] * lp[3:4], axis=1, keepdims=True)) + lam_init)
    q = _rope(q_ref[0].astype(F32), cos_ref[...], sin_ref[...], B_ROT // 2) * (B_SUB ** -0.5 * LOG2E)
    q_t = q.T
    sub_rows = [slice(j * B_SUB, (j + 1) * B_SUB) for j in range(2 * B_HEADS)]

    def finish(parts):
        outs = []
        for h in range(B_HEADS):
            o = parts[2 * h] - lam * parts[2 * h + 1]
            o = o * lax.rsqrt(jnp.mean(o * o, axis=0, keepdims=True) + NORM_EPS) * gain_ref[...]
            outs.append(o * (1.0 - lam_init))
        o_ref[0] = jnp.concatenate(outs, axis=0).T.astype(BF16)

    _attend_t(k_s, kn_s, q_t, sub_rows, sub_rows, [vx_s.at[j // 2] for j in range(2 * B_HEADS)], finish)


def _mixer_diff(zabc, tabs, lam_params, gain, lam_init, tq):
    b, t, _ = zabc.shape
    cos, sin = tabs
    return pl.pallas_call(
        functools.partial(_diff_kernel, lam_init),
        grid=(b, t // tq),
        in_specs=[
            pl.BlockSpec((1, tq, 256), lambda bi, i: (bi, i, 2)),
            pl.BlockSpec((1, t, 256), lambda bi, i: (bi, 0, 3)),
            pl.BlockSpec((1, t, 256), lambda bi, i: (bi, 0, 4)),
            pl.BlockSpec((tq, 256), lambda bi, i: (i, 0)),
            pl.BlockSpec((tq, 256), lambda bi, i: (i, 0)),
            pl.BlockSpec((t, 256), lambda bi, i: (0, 0)),
            pl.BlockSpec((t, 256), lambda bi, i: (0, 0)),
            pl.BlockSpec((4, B_SUB), lambda bi, i: (0, 0)),
            pl.BlockSpec((HEAD_DIM, 1), lambda bi, i: (0, 0)),
        ],
        out_specs=pl.BlockSpec((1, tq, MIX_W), lambda bi, i: (bi, i, 0)),
        out_shape=jax.ShapeDtypeStruct((b, t, MIX_W), BF16),
        scratch_shapes=[
            pltpu.VMEM((t, 2 * B_HEADS * B_SUB), BF16),
            pltpu.VMEM((1, 2 * B_HEADS * B_SUB), F32),
            pltpu.VMEM((B_HEADS, VEXT_ROWS, t), BF16),
        ],
        compiler_params=_params("parallel", "arbitrary"),
        name="mixer_diff",
    )(zabc, zabc, zabc, cos, sin, cos, sin, lam_params, gain)


def _ret_log_gammas():
    lg = [math.log1p(-(2.0 ** (-5.0 - h))) for h in range(C_HEADS)]
    return lg, lg[::-1]


def _lane_consts(vals, shape):
    head = lax.broadcasted_iota(jnp.int32, shape, len(shape) - 1) // HEAD_DIM
    out = jnp.full(shape, vals[-1], F32)
    for h in range(len(vals) - 2, -1, -1):
        out = jnp.where(head == h, vals[h], out)
    return out


def _ret_kernel(chunk, q_ref, k_ref, v_ref, g_ref, cos_ref, sin_ref, gain_ref, o_ref,
                qr_s, kr_s, acc_s, sf_s, sb_s):
    t = q_ref.shape[1]
    c = chunk
    n = t // c
    lgf, lgb = _ret_log_gammas()
    qr_s[...] = _rope(q_ref[0].astype(F32), cos_ref[...], sin_ref[...], HEAD_DIM // 2).astype(BF16)
    kr_s[...] = (_rope(k_ref[0].astype(F32), cos_ref[...], sin_ref[...], HEAD_DIM // 2)
                 * (HEAD_DIM ** -0.5)).astype(BF16)

    ti = lax.broadcasted_iota(jnp.int32, (c, c), 0)
    si = lax.broadcasted_iota(jnp.int32, (c, c), 1)
    dist = (ti - si).astype(F32)
    lane_head = lax.broadcasted_iota(jnp.int32, (1, MIX_W), 1) // HEAD_DIM
    row = lax.broadcasted_iota(jnp.int32, (c, MIX_W), 0).astype(F32)
    lgf_l = _lane_consts(lgf, (c, MIX_W))
    lgb_l = _lane_consts(lgb, (c, MIX_W))
    qdec_f = jnp.exp(lgf_l * (row + 1.0))
    kdec_f = jnp.exp(lgf_l * (c - 1.0 - row))
    qdec_b = jnp.exp(lgb_l * (c - row))
    kdec_b = jnp.exp(lgb_l * row)
    r2 = lax.broadcasted_iota(jnp.int32, (MIX_W, MIX_W), 0) // HEAD_DIM
    c2 = lax.broadcasted_iota(jnp.int32, (MIX_W, MIX_W), 1) // HEAD_DIM
    same_head = r2 == c2
    gf_blk = jnp.where(same_head, jnp.exp(_lane_consts(lgf, (MIX_W, MIX_W)) * c), 0.0)
    gb_blk = jnp.where(same_head, jnp.exp(_lane_consts(lgb, (MIX_W, MIX_W)) * c), 0.0)

    def intra(qc, kc, vc):
        out = jnp.zeros((c, MIX_W), F32)
        for h in range(C_HEADS):
            mh = (lane_head == h).astype(F32)
            dm = jnp.where(dist > 0, jnp.exp(lgf[h] * dist),
                           jnp.where(dist < 0, jnp.exp(-lgb[h] * dist), 2.0))
            s = _dot_nt((qc * mh).astype(BF16), kc) * dm
            out = out + _dot(s.astype(BF16), (vc * mh).astype(BF16))
        return out

    sf_s[...] = jnp.zeros_like(sf_s)
    sb_s[...] = jnp.zeros_like(sb_s)

    def fwd(i, carry):
        rows = pl.ds(pl.multiple_of(i * c, c), c)
        qc = qr_s[rows, :].astype(F32)
        kc = kr_s[rows, :]
        vc = v_ref[0, rows, :].astype(F32)
        o = intra(qc, kc, vc) + _dot((qc * qdec_f).astype(BF16), sf_s[...].astype(BF16))
        acc_s[rows, :] = o
        kd = (kc.astype(F32) * kdec_f).T.astype(BF16)
        sf_s[...] = gf_blk * sf_s[...] + jnp.where(same_head, _dot(kd, vc.astype(BF16)), 0.0)
        return carry

    lax.fori_loop(0, n, fwd, 0)

    def bwd(i, carry):
        rows = pl.ds(pl.multiple_of((n - 1 - i) * c, c), c)
        qc = qr_s[rows, :].astype(F32)
        kc = kr_s[rows, :].astype(F32)
        vc = v_ref[0, rows, :]
        acc_s[rows, :] = acc_s[rows, :] + _dot((qc * qdec_b).astype(BF16), sb_s[...].astype(BF16))
        kd = (kc * kdec_b).T.astype(BF16)
        sb_s[...] = gb_blk * sb_s[...] + jnp.where(same_head, _dot(kd, vc), 0.0)
        return carry

    lax.fori_loop(0, n, bwd, 0)

    o = acc_s[...]
    o = o * lax.rsqrt(_head_sum(o * o, HEAD_DIM) * (1.0 / HEAD_DIM) + NORM_EPS) * gain_ref[...]
    g = g_ref[0].astype(F32)
    o_ref[0] = (o * (g * _sigmoid(g))).astype(BF16)


def _mixer_ret(zabc, tabs, gain, chunk):
    b, t, _ = zabc.shape
    cos, sin = tabs
    blk = lambda j: pl.BlockSpec((1, t, 256), lambda bi: (bi, 0, j))
    return pl.pallas_call(
        functools.partial(_ret_kernel, chunk),
        grid=(b,),
        in_specs=[blk(5), blk(6), blk(7), blk(8),
                  pl.BlockSpec((t, 256), lambda bi: (0, 0)),
                  pl.BlockSpec((t, 256), lambda bi: (0, 0)),
                  pl.BlockSpec((1, 256), lambda bi: (0, 0))],
        out_specs=pl.BlockSpec((1, t, MIX_W), lambda bi: (bi, 0, 0)),
        out_shape=jax.ShapeDtypeStruct((b, t, MIX_W), BF16),
        scratch_shapes=[
            pltpu.VMEM((t, MIX_W), BF16),
            pltpu.VMEM((t, MIX_W), BF16),
            pltpu.VMEM((t, MIX_W), F32),
            pltpu.VMEM((MIX_W, MIX_W), F32),
            pltpu.VMEM((MIX_W, MIX_W), F32),
        ],
        compiler_params=_params("parallel"),
        name="mixer_ret",
    )(zabc, zabc, zabc, zabc, cos, sin, gain)


def _wkv_prep_kernel(z_ref, zp_ref, zn_ref, mup_ref, mun_ref, w0_ref, wupf_ref, wupb_ref, a0_ref, aup_ref,
                     gup1_ref, gup2_ref, kk_ref, ka_ref, rk_ref,
                     r_o, k_o, v_o, kk_o, b_o, ldf_o, ldb_o, g_o, bonus_o):
    i = pl.program_id(1)
    last = pl.num_programs(1) - 1
    z = z_ref[0]
    tm = z.shape[0]
    row = lax.broadcasted_iota(jnp.int32, z.shape, 0)
    prev_row = zp_ref[0, 0, 7:8, :] * jnp.where(i > 0, 1.0, 0.0)
    next_row = zn_ref[0, 0, 0:1, :] * jnp.where(i < last, 1.0, 0.0)
    z_prev = jnp.where(row == 0, prev_row, pltpu.roll(z, 1, 0))
    z_next = jnp.where(row == tm - 1, next_row, pltpu.roll(z, tm - 1, 0))
    u = z + mup_ref[...] * (z_prev - z) + mun_ref[...] * (z_next - z)
    r = u[:, 0:256]
    k = u[:, 256:512]
    v = u[:, 512:768]
    wd = jnp.tanh(u[:, 768:896]).astype(BF16)
    ag = u[:, 896:1024]
    g2 = u[:, 1024:1088]

    def log_decay(w0, wup):
        x = w0 + _dot(wd, wup)
        w = -(jnp.maximum(-x, 0.0) + jnp.log(1.0 + jnp.exp(-jnp.abs(x)))) - 0.5
        return -jnp.exp(w)

    ldf_o[0] = log_decay(w0_ref[0:1, :], wupf_ref[...])
    ldb_o[0] = log_decay(w0_ref[1:2, :], wupb_ref[...])
    a = _sigmoid(a0_ref[...] + _dot(ag.astype(BF16), aup_ref[...]))
    g_o[0] = (_dot(_sigmoid(ag).astype(BF16), gup1_ref[...])
              + _dot(_sigmoid(g2).astype(BF16), gup2_ref[...]))
    kk = k * kk_ref[...]
    kk = kk / jnp.maximum(jnp.sqrt(_head_sum(kk * kk, HEAD_DIM)), 1e-12)
    kh = k * (1.0 + (a - 1.0) * ka_ref[...])
    r_o[0] = r
    k_o[0] = kh
    v_o[0] = v
    kk_o[0] = kk
    b_o[0] = kk * a
    bonus_o[0] = _head_sum(r * kh * rk_ref[...], HEAD_DIM) * v


def _wkv_prep(zd, p, tm):
    b, t, _ = zd.shape
    zd8 = zd.reshape(b, t // 8, 8, D_COLS)
    r8 = tm // 8
    nb8 = t // 8
    row = lambda w: pl.BlockSpec((1, w), lambda bi, i: (0, 0))
    full = lambda a: pl.BlockSpec(a.shape, lambda bi, i: (0,) * a.ndim)
    out_spec = pl.BlockSpec((1, tm, MIX_W), lambda bi, i: (bi, i, 0))
    out_shape = jax.ShapeDtypeStruct((b, t, MIX_W), F32)
    return pl.pallas_call(
        _wkv_prep_kernel,
        grid=(b, t // tm),
        in_specs=[
            pl.BlockSpec((1, tm, D_COLS), lambda bi, i: (bi, i, 0)),
            pl.BlockSpec((1, 1, 8, D_COLS), lambda bi, i: (bi, jnp.maximum(i * r8 - 1, 0), 0, 0)),
            pl.BlockSpec((1, 1, 8, D_COLS), lambda bi, i: (bi, jnp.minimum((i + 1) * r8, nb8 - 1), 0, 0)),
            row(D_COLS), row(D_COLS),
            full(p["w0"]), full(p["wup_f"]), full(p["wup_b"]), row(256), full(p["aup"]),
            full(p["gup1"]), full(p["gup2"]), row(256), row(256), row(256),
        ],
        out_specs=[out_spec] * 9,
        out_shape=[out_shape] * 9,
        compiler_params=_params("parallel", "parallel"),
        name="wkv_prep",
    )(zd, zd8, zd8, p["mu_prev"], p["mu_next"], p["w0"], p["wup_f"], p["wup_b"], p["a0"], p["aup"],
      p["gup1"], p["gup2"], p["k_k"], p["k_a"], p["r_k"])


def _wkv_chunk_consts(chunk):
    c = chunk
    sc = D_HEADS * c
    ti = lax.broadcasted_iota(jnp.int32, (c, c), 0)
    si = lax.broadcasted_iota(jnp.int32, (c, c), 1)
    srow = lax.broadcasted_iota(jnp.int32, (sc, MIX_W), 0) // c
    slane = lax.broadcasted_iota(jnp.int32, (sc, MIX_W), 1) // HEAD_DIM
    gt = lax.broadcasted_iota(jnp.int32, (2 * sc, 2 * sc), 0)
    gs = lax.broadcasted_iota(jnp.int32, (2 * sc, 2 * sc), 1)
    t_in, s_in = gt % sc, gs % sc
    same = (t_in // c) == (s_in // c)
    upper = gt < sc
    st = lax.broadcasted_iota(jnp.int32, (sc, sc), 0)
    ss = lax.broadcasted_iota(jnp.int32, (sc, sc), 1)
    r2 = lax.broadcasted_iota(jnp.int32, (MIX_W, MIX_W), 0)
    c2 = lax.broadcasted_iota(jnp.int32, (MIX_W, MIX_W), 1)
    return {
        "tri": {False: jnp.where(si <= ti, 1.0, 0.0).astype(BF16), True: jnp.where(si >= ti, 1.0, 0.0).astype(BF16)},
        "hmask": srow == slane,
        "gmask": {False: same & ((s_in < t_in) | ((s_in == t_in) & ~upper)),
                  True: same & ((s_in > t_in) | ((s_in == t_in) & ~upper))},
        "eye": jnp.where(st == ss, 1.0, 0.0),
        "diag": r2 == c2,
    }


def _wkv_phase1(chunk, consts, probs):
    c = chunk
    sc = D_HEADS * c
    hmask = consts["hmask"]

    def stack(a):
        return jnp.where(hmask, jnp.concatenate([a] * D_HEADS, axis=0), 0.0).astype(BF16)

    pre = []
    for rev, r, kh, v, kk, bb, lw in probs:
        tri = consts["tri"][rev]
        l1, l2, l3 = _split3(lw)
        cl = _dot(tri, l1) + _dot(tri, l2) + _dot(tri, l3)
        tot = jnp.sum(lw, axis=0, keepdims=True)
        pre.append((cl, tot))
    yield
    ops = []
    for (rev, r, kh, v, kk, bb, lw), (cl, tot) in zip(probs, pre):
        w_inv = jnp.exp(-cl)
        w_end = jnp.exp(tot - cl)
        lhs = jnp.concatenate([stack(kk * jnp.exp(cl - lw)), stack(r * jnp.exp(cl))], axis=0)
        rhs = jnp.concatenate([stack(bb * w_inv), stack(kh * w_inv)], axis=0)
        kb = jnp.concatenate([stack(kh * w_end), stack(bb * w_end)], axis=0)
        wc = jnp.sum(jnp.where(consts["diag"], jnp.exp(tot), 0.0), axis=1, keepdims=True)
        wc = jnp.broadcast_to(wc, (MIX_W, MIX_W))
        ops.append((lhs, rhs, stack(v), kb, wc))
    yield
    gram = [jnp.where(consts["gmask"][p[0]], _dot_nt(o[0], o[1]), 0.0) for p, o in zip(probs, ops)]
    l_ab = [g[:sc, :sc] for g in gram]
    l_ak = [g[:sc, sc:].astype(BF16) for g in gram]
    m_r = [g[sc:, :].astype(BF16) for g in gram]
    yield

    steps = int(math.log2(c))
    pw = [(-l).astype(BF16) for l in l_ab]
    inv = [consts["eye"] - l for l in l_ab]
    w = [_dot(a, o[2]).astype(BF16) for a, o in zip(l_ak, ops)]
    for k in range(1, steps):
        if k == 1:
            pw = [_dot(p, p).astype(BF16) for p in pw]
            yield
        if k < steps - 1:
            nxt = [_dot(p, jnp.concatenate([p, i.astype(BF16)], axis=1)) for p, i in zip(pw, inv)]
            inv = [i + n[:, sc:] for i, n in zip(inv, nxt)]
            pw = [n[:, :sc].astype(BF16) for n in nxt]
        else:
            inv = [i + _dot(p, i.astype(BF16)) for p, i in zip(pw, inv)]
        yield
    inv = [i.astype(BF16) for i in inv]

    gu = [_dot(i, jnp.concatenate([o[0][:sc], wv], axis=1)) for i, o, wv in zip(inv, ops, w)]
    yield
    kb_t = [o[3].astype(F32).T.astype(BF16) for o in ops]
    mgu = [_dot(m[:, :sc], x.astype(BF16)) for m, x in zip(m_r, gu)]
    mv = [_dot(m[:, sc:], o[2]) for m, o in zip(m_r, ops)]
    yield
    return [{
        "qg": jnp.concatenate([(o[0][sc:].astype(F32) - mg[:, :MIX_W]).astype(BF16),
                               g[:, :MIX_W].astype(BF16)], axis=0),
        "y_hat": a - mg[:, MIX_W:],
        "u_hat": g[:, MIX_W:],
        "kb_t": kt,
        "v": o[2],
        "wc": o[4],
    } for o, g, mg, a, kt in zip(ops, gu, mgu, mv, kb_t)]


def _wkv_scan_kernel(chunk, group, rf, kf, vf, kkf, bf, lf, rb, kb, vb, kkb, bb, lb, yf_o, yb_o,
                     xf_s, xb_s, qg_s, yh_s, uh_s, kbt_s, v_s, wc_s):
    @pl.when(pl.program_id(1) == 0)
    def _():
        xf_s[...] = jnp.zeros_like(xf_s)
        xb_s[...] = jnp.zeros_like(xb_s)

    tm = rf.shape[1]
    n = tm // chunk
    sc = D_HEADS * chunk
    consts = _wkv_chunk_consts(chunk)
    sol_refs = {"qg": qg_s, "y_hat": yh_s, "u_hat": uh_s, "kb_t": kbt_s, "v": v_s, "wc": wc_s}

    def group_rows(g):
        rows = []
        for u in range(group):
            rows.append(pl.ds(pl.multiple_of((g * group + u) * chunk, chunk), chunk))
            rows.append(pl.ds(pl.multiple_of((n - 1 - g * group - u) * chunk, chunk), chunk))
        return rows

    def problems(g):
        probs = []
        for slot, rows in enumerate(group_rows(g)):
            if slot % 2:
                probs.append((True, rb[0, rows, :], kb[0, rows, :], vb[0, rows, :], kkb[0, rows, :],
                              bb[0, rows, :], lb[0, rows, :]))
            else:
                probs.append((False, rf[0, rows, :], kf[0, rows, :], vf[0, rows, :], kkf[0, rows, :],
                              bf[0, rows, :], lf[0, rows, :]))
        return probs

    def advance(slot, rows):
        x_s, y_o = (xb_s, yb_o) if slot % 2 else (xf_s, yf_o)
        x = x_s[...]
        res = _dot(qg_s[slot], x.astype(BF16))
        ys = res[:sc] + yh_s[slot]
        y = ys[0:chunk]
        for h in range(1, D_HEADS):
            y = y + ys[h * chunk:(h + 1) * chunk]
        y_o[0, rows, :] = y
        u = -(res[sc:] + uh_s[slot])
        x_s[...] = x * wc_s[slot] + _dot(kbt_s[slot], jnp.concatenate([v_s[slot], u.astype(BF16)], axis=0))

    def solve(g, pending):
        gen = _wkv_phase1(chunk, consts, problems(g))
        pending = list(pending)
        while True:
            try:
                next(gen)
            except StopIteration as done:
                sols = done.value
                break
            for step in pending[:2]:
                step()
            pending = pending[2:]
        for step in pending:
            step()
        for slot, sol in enumerate(sols):
            for name, ref in sol_refs.items():
                ref[slot] = sol[name]

    def steps_for(g):
        return [functools.partial(advance, slot, r) for slot, r in enumerate(group_rows(g))]

    n_groups = n // group
    solve(0, [])

    for g in range(1, n_groups):
        solve(g, steps_for(g - 1))
    for step in steps_for(n_groups - 1):
        step()


def _wkv_scan(r, kh, v, kk, bb, ldf, ldb, tm, chunk, group):
    b, t, _ = r.shape
    nt = t // tm
    sc = D_HEADS * chunk
    slots = 2 * group
    fspec = pl.BlockSpec((1, tm, MIX_W), lambda bi, i: (bi, i, 0))
    bspec = pl.BlockSpec((1, tm, MIX_W), lambda bi, i: (bi, nt - 1 - i, 0))
    shape = jax.ShapeDtypeStruct((b, t, MIX_W), F32)
    return pl.pallas_call(
        functools.partial(_wkv_scan_kernel, chunk, group),
        grid=(b, nt),
        in_specs=[fspec] * 6 + [bspec] * 6,
        out_specs=[fspec, bspec],
        out_shape=[shape, shape],
        scratch_shapes=[
            pltpu.VMEM((MIX_W, MIX_W), F32), pltpu.VMEM((MIX_W, MIX_W), F32),
            pltpu.VMEM((slots, 2 * sc, MIX_W), BF16),
            pltpu.VMEM((slots, sc, MIX_W), F32),
            pltpu.VMEM((slots, sc, MIX_W), F32),
            pltpu.VMEM((slots, MIX_W, 2 * sc), BF16),
            pltpu.VMEM((slots, sc, MIX_W), BF16),
            pltpu.VMEM((slots, MIX_W, MIX_W), F32),
        ],
        compiler_params=_params("parallel", "arbitrary"),
        name="wkv_scan",
    )(r, kh, v, kk, bb, ldf, r, kh, v, kk, bb, ldb)


def _wkv_fin_kernel(yf_ref, yb_ref, bonus_ref, g_ref, gnw_ref, gnb_ref, o_ref):
    y = yf_ref[...] + yb_ref[...]
    mean = _head_sum(y, HEAD_DIM) * (1.0 / HEAD_DIM)
    yc = y - mean
    var = _head_sum(yc * yc, HEAD_DIM) * (1.0 / HEAD_DIM)
    yn = yc * lax.rsqrt(var + WKV_GN_EPS) * gnw_ref[...] + gnb_ref[...]
    o_ref[...] = ((yn + bonus_ref[...]) * g_ref[...]).astype(BF16)


def _wkv_fin(yf, yb, bonus, g, gn_w, gn_b, tm):
    n = yf.shape[0]
    tok = pl.BlockSpec((tm, MIX_W), lambda i: (i, 0))
    row = pl.BlockSpec((1, MIX_W), lambda i: (0, 0))
    return pl.pallas_call(
        _wkv_fin_kernel,
        grid=(n // tm,),
        in_specs=[tok, tok, tok, tok, row, row],
        out_specs=tok,
        out_shape=jax.ShapeDtypeStruct((n, MIX_W), BF16),
        compiler_params=_params("parallel"),
        name="wkv_fin",
    )(yf, yb, bonus, g, gn_w, gn_b)


def _post_kernel(x_ref, oa_ref, ob_ref, oc_ref, od_ref, wo_ref, gpost_ref, gpre_ref, wg_ref, wu_ref, wd_ref,
                 gfpost_ref, y_ref):
    mix = (_dot(oa_ref[...], wo_ref[0]) + _dot(ob_ref[...], wo_ref[1])
           + _dot(oc_ref[...], wo_ref[2]) + _dot(od_ref[...], wo_ref[3]))
    x = x_ref[...] + _rms_rows(mix, gpost_ref[...])
    h = _rms_rows(x, gpre_ref[...]).astype(BF16)
    gate = _dot(h, wg_ref[...])
    up = _dot(h, wu_ref[...])
    act = (gate * _sigmoid(gate) * up).astype(BF16)
    f = _dot(act, wd_ref[...])
    y_ref[...] = x + _rms_rows(f, gfpost_ref[...])


def _post(x2, oa, ob, oc, od, w_out4, g_post, g_pre, wg, wu, wd, gf_post, tm):
    n, d = x2.shape
    dff = wg.shape[1]
    tok = lambda w: pl.BlockSpec((tm, w), lambda i: (i, 0))
    row = pl.BlockSpec((1, d), lambda i: (0, 0))
    return pl.pallas_call(
        _post_kernel,
        grid=(n // tm,),
        in_specs=[tok(d), tok(MIX_W), tok(MIX_W), tok(MIX_W), tok(MIX_W),
                  pl.BlockSpec((4, MIX_W, d), lambda i: (0, 0, 0)), row, row,
                  pl.BlockSpec((d, dff), lambda i: (0, 0)),
                  pl.BlockSpec((d, dff), lambda i: (0, 0)),
                  pl.BlockSpec((dff, d), lambda i: (0, 0)), row],
        out_specs=tok(d),
        out_shape=jax.ShapeDtypeStruct((n, d), F32),
        compiler_params=_params("parallel"),
        name="out_proj_ffn",
    )(x2, oa, ob, oc, od, w_out4, g_post, g_pre, wg, wu, wd, gf_post)


def _angles(pos, rot_dim, theta):
    inv = theta ** (-jnp.arange(0, rot_dim, 2, dtype=F32) / rot_dim)
    return pos.astype(F32)[:, None] * inv[None, :]


def _rope_tables(t):
    rows = t // GRID_W
    row_idx = jnp.repeat(jnp.arange(rows), GRID_W)
    col_idx = jnp.tile(jnp.arange(GRID_W), rows)
    pos = jnp.arange(t)
    ar = _angles(row_idx, HEAD_DIM // 2, A_THETA)
    ac = _angles(col_idx, HEAD_DIM // 2, A_THETA)
    cos_a = jnp.concatenate([jnp.cos(ar), jnp.cos(ar), jnp.cos(ac), jnp.cos(ac)], axis=1)
    sin_a = jnp.concatenate([-jnp.sin(ar), jnp.sin(ar), -jnp.sin(ac), jnp.sin(ac)], axis=1)
    tabs_a = (jnp.tile(cos_a, (1, 4)), jnp.tile(sin_a, (1, 4)), jnp.tile(cos_a, (1, 2)), jnp.tile(sin_a, (1, 2)))
    ab = _angles(pos, B_ROT, B_THETA)
    pad1 = jnp.ones((t, B_SUB - B_ROT), F32)
    pad0 = jnp.zeros((t, B_SUB - B_ROT), F32)
    cos_b = jnp.concatenate([jnp.cos(ab), jnp.cos(ab), pad1], axis=1)
    sin_b = jnp.concatenate([-jnp.sin(ab), jnp.sin(ab), pad0], axis=1)
    tabs_b = (jnp.tile(cos_b, (1, 8)), jnp.tile(sin_b, (1, 8)))
    ang_c = _angles(pos, HEAD_DIM, C_THETA)
    cos_c = jnp.concatenate([jnp.cos(ang_c), jnp.cos(ang_c)], axis=1)
    sin_c = jnp.concatenate([-jnp.sin(ang_c), jnp.sin(ang_c)], axis=1)
    tabs_c = (jnp.tile(cos_c, (1, 4)), jnp.tile(sin_c, (1, 4)))
    return tabs_a, tabs_b, tabs_c


def _tile(n, pref):
    while n % pref:
        pref //= 2
    return pref


def _layer_weights(l, w):
    d_model = w["w_in"].shape[1]
    row = lambda a: a.reshape(1, -1).astype(F32)
    zeros = lambda r: jnp.zeros((r, MIX_W), F32)
    w_lora = w["d_w_up"].shape[2]
    return {
        "g_mix_pre": row(w["norm_mix_pre"][l]), "g_mix_post": row(w["norm_mix_post"][l]),
        "g_ffn_pre": row(w["norm_ffn_pre"][l]), "g_ffn_post": row(w["norm_ffn_post"][l]),
        "w_abc": w["w_in"][l][:, :ABC_COLS].astype(BF16), "w_d": w["w_in"][l][:, ABC_COLS:].astype(BF16),
        "w_out4": w["w_out"][l].reshape(4, MIX_W, d_model).astype(BF16),
        "a_gq": jnp.tile(row(w["a_q_gain"][l]), (1, A_HEADS)),
        "a_gk": jnp.tile(row(w["a_k_gain"][l]), (1, A_KV_HEADS)),
        "b_lambda": w["b_lambda"][l].astype(F32), "b_gain": w["b_subln_gain"][l].reshape(-1, 1).astype(F32),
        "c_gain": row(w["c_gn_gain"][l]),
        "wkv": {
            "mu_prev": row(w["d_mu_prev"][l]), "mu_next": row(w["d_mu_next"][l]),
            "w0": w["d_w0"][l].astype(F32),
            "wup_f": jnp.concatenate([w["d_w_up"][l, 0], zeros(w_lora)], axis=0).astype(BF16),
            "wup_b": jnp.concatenate([zeros(w_lora), w["d_w_up"][l, 1]], axis=0).astype(BF16),
            "a0": row(w["d_a0"][l]),
            "aup": jnp.concatenate([w["d_a_up"][l], zeros(64)], axis=0).astype(BF16),
            "gup1": jnp.concatenate([zeros(64), w["d_g_up"][l][:64]], axis=0).astype(BF16),
            "gup2": w["d_g_up"][l][64:].astype(BF16),
            "k_k": row(w["d_k_k"][l]), "k_a": row(w["d_k_a"][l]), "r_k": row(w["d_r_k"][l]),
        },
        "gn_w": row(w["d_gn_w"][l]), "gn_b": row(w["d_gn_b"][l]),
        "wg": w["ffn_w_gate"][l].astype(BF16), "wu": w["ffn_w_up"][l].astype(BF16),
        "wd": w["ffn_w_down"][l].astype(BF16),
    }


def _trunk(x, layers):
    b, t, d = x.shape
    n = b * t
    tabs_a, tabs_b, tabs_c = _rope_tables(t)
    tm_proj = _tile(n, 512)
    tm_post = _tile(n, 256)
    tq = _tile(t, 256)
    tm_wkv = _tile(t, 256)
    tm_scan = _tile(t, 1024)
    chunk = 32
    x2 = x.reshape(n, d)
    for l, p in enumerate(layers):
        lam_init = 0.8 - 0.6 * math.exp(-0.3 * l)
        zabc, zd = _in_proj(x2, p["g_mix_pre"], p["w_abc"], p["w_d"], tm_proj)
        zabc = zabc.reshape(b, t, ABC_COLS)
        zd = zd.reshape(b, t, D_COLS)
        o_a = _mixer_gqa(zabc, tabs_a, p["a_gq"], p["a_gk"], tq)
        o_b = _mixer_diff(zabc, tabs_b, p["b_lambda"], p["b_gain"], lam_init, tq)
        o_c = _mixer_ret(zabc, tabs_c, p["c_gain"], _tile(t, 256))
        r, kh, v, kk, bb, ldf, ldb, g, bonus = _wkv_prep(zd, p["wkv"], tm_wkv)
        yf, yb = _wkv_scan(r, kh, v, kk, bb, ldf, ldb, tm_scan, chunk, min(4, tm_scan // chunk))
        o_d = _wkv_fin(yf.reshape(n, MIX_W), yb.reshape(n, MIX_W), bonus.reshape(n, MIX_W),
                       g.reshape(n, MIX_W), p["gn_w"], p["gn_b"], tm_proj)
        x2 = _post(x2, o_a.reshape(n, MIX_W), o_b.reshape(n, MIX_W), o_c.reshape(n, MIX_W), o_d,
                   p["w_out4"], p["g_mix_post"], p["g_ffn_pre"], p["wg"], p["wu"], p["wd"],
                   p["g_ffn_post"], tm_post)
    return x2.reshape(b, t, d)


def kernel(x_prompt, x_sample, norm_mix_pre, norm_mix_post, norm_ffn_pre, norm_ffn_post, w_in, w_out,
           a_q_gain, a_k_gain, b_lambda, b_subln_gain, c_gn_gain, d_mu_prev, d_mu_next, d_w0, d_w_up,
           d_a0, d_a_up, d_g_up, d_k_k, d_k_a, d_r_k, d_gn_w, d_gn_b, ffn_w_gate, ffn_w_up, ffn_w_down):
    w = {
        "norm_mix_pre": norm_mix_pre, "norm_mix_post": norm_mix_post,
        "norm_ffn_pre": norm_ffn_pre, "norm_ffn_post": norm_ffn_post,
        "w_in": w_in, "w_out": w_out, "a_q_gain": a_q_gain, "a_k_gain": a_k_gain,
        "b_lambda": b_lambda, "b_subln_gain": b_subln_gain, "c_gn_gain": c_gn_gain,
        "d_mu_prev": d_mu_prev, "d_mu_next": d_mu_next, "d_w0": d_w0, "d_w_up": d_w_up,
        "d_a0": d_a0, "d_a_up": d_a_up, "d_g_up": d_g_up, "d_k_k": d_k_k, "d_k_a": d_k_a,
        "d_r_k": d_r_k, "d_gn_w": d_gn_w, "d_gn_b": d_gn_b,
        "ffn_w_gate": ffn_w_gate, "ffn_w_up": ffn_w_up, "ffn_w_down": ffn_w_down,
    }
    layers = [_layer_weights(l, w) for l in range(w_in.shape[0])]
    return (_trunk(x_prompt, layers), _trunk(x_sample, layers))
```

```python
import functools
import math

import jax
import jax.numpy as jnp
from jax import lax
from jax.experimental import pallas as pl
from jax.experimental.pallas import tpu as pltpu

F32 = jnp.float32
BF16 = jnp.bfloat16

HEAD_DIM = 64
GRID_W = 64
NORM_EPS = 1e-6
A_HEADS, A_KV_HEADS, A_THETA = 4, 2, 10000.0
B_HEADS, B_SUB, B_ROT, B_THETA = 4, 32, 8, 500000.0
C_HEADS, C_THETA = 4, 10000.0
D_HEADS = 4
WKV_GN_EPS = 64e-5
MIX_W = 256
ABC_COLS = 2304
D_COLS = 1088
BF16_SUBLANES = 16
VEXT_ROWS = HEAD_DIM + BF16_SUBLANES
LOG2E = math.log2(math.e)
KEY_CHUNK = 512
SHIFT_LIMIT = 100.0
VMEM_LIMIT = 56 * 1024 * 1024


def _params(*sem):
    return pltpu.CompilerParams(dimension_semantics=sem, vmem_limit_bytes=VMEM_LIMIT)


def _rms_rows(x, gain):
    return x * lax.rsqrt(jnp.mean(x * x, axis=-1, keepdims=True) + NORM_EPS) * gain


def _split2(x):
    hi = x.astype(BF16)
    lo = (x - hi.astype(F32)).astype(BF16)
    return hi, lo


def _split3(x):
    h1 = x.astype(BF16)
    r1 = x - h1.astype(F32)
    h2 = r1.astype(BF16)
    h3 = (r1 - h2.astype(F32)).astype(BF16)
    return h1, h2, h3


def _head_sum(x, seg):
    w = x.shape[-1]
    r = lax.broadcasted_iota(jnp.int32, (w, w), 0) // seg
    c = lax.broadcasted_iota(jnp.int32, (w, w), 1) // seg
    bd = jnp.where(r == c, 1.0, 0.0).astype(BF16)
    hi, lo = _split2(x)
    return (jnp.dot(hi, bd, preferred_element_type=F32)
            + jnp.dot(lo, bd, preferred_element_type=F32))


def _rope(x, cos, sin, half):
    w = x.shape[-1]
    lane = lax.broadcasted_iota(jnp.int32, x.shape, 1)
    nxt = pltpu.roll(x, w - half, 1)
    prv = pltpu.roll(x, half, 1)
    return x * cos + jnp.where((lane % (2 * half)) < half, nxt, prv) * sin


def _sigmoid(x):
    return 1.0 / (1.0 + jnp.exp(-x))


def _dot(a, b):
    return jnp.dot(a, b, preferred_element_type=F32)


def _dot_nt(a, b):
    return lax.dot_general(a, b, (((1,), (1,)), ((), ())), preferred_element_type=F32)


def _in_proj_kernel(x_ref, g_ref, wabc_ref, wd_ref, zabc_ref, zd_ref):
    h = _rms_rows(x_ref[...], g_ref[...]).astype(BF16)
    zabc_ref[...] = _dot(h, wabc_ref[...]).astype(BF16)
    zd_ref[...] = _dot(h, wd_ref[...])


def _in_proj(x2, gain, w_abc, w_d, tm):
    n, d = x2.shape
    return pl.pallas_call(
        _in_proj_kernel,
        grid=(n // tm,),
        in_specs=[
            pl.BlockSpec((tm, d), lambda i: (i, 0)),
            pl.BlockSpec((1, d), lambda i: (0, 0)),
            pl.BlockSpec((d, ABC_COLS), lambda i: (0, 0)),
            pl.BlockSpec((d, D_COLS), lambda i: (0, 0)),
        ],
        out_specs=[
            pl.BlockSpec((tm, ABC_COLS), lambda i: (i, 0)),
            pl.BlockSpec((tm, D_COLS), lambda i: (i, 0)),
        ],
        out_shape=[
            jax.ShapeDtypeStruct((n, ABC_COLS), BF16),
            jax.ShapeDtypeStruct((n, D_COLS), F32),
        ],
        compiler_params=_params("parallel"),
        name="in_proj",
    )(x2, gain, w_abc, w_d)


def _value_ext_t(v_t):
    row = lax.broadcasted_iota(jnp.int32, (VEXT_ROWS - HEAD_DIM, v_t.shape[1]), 0)
    return jnp.concatenate([v_t, jnp.where(row == 0, 1.0, 0.0)], axis=0).astype(BF16)


def _attend_exact(k_ref, q_ts, vx_refs):
    outs = []
    for q_t, vx in zip(q_ts, vx_refs):
        s_t = _dot(k_ref[...], q_t)
        p_t = jnp.exp2(s_t - jnp.max(s_t, axis=0, keepdims=True)).astype(BF16)
        o = _dot(vx[...], p_t)
        outs.append(o[:HEAD_DIM] / o[HEAD_DIM:HEAD_DIM + 1])
    return outs


def _attend_shifted(k_ref, q_ts, shifts, vx_refs):
    t = k_ref.shape[0]
    ck = min(t, KEY_CHUNK)
    steps = [(j, slice(c * ck, (c + 1) * ck)) for j in range(len(q_ts)) for c in range(t // ck)]
    acc = [None] * len(q_ts)
    s_prev = p_prev = None
    for i in range(len(steps) + 2):
        s_new = p_new = None
        if i < len(steps):
            j, keys = steps[i]
            s_new = _dot(k_ref[keys, :], q_ts[j])
        if 1 <= i <= len(steps):
            p_new = jnp.exp2(s_prev - shifts[steps[i - 1][0]]).astype(BF16)
        if i >= 2:
            j, keys = steps[i - 2]
            o = _dot(vx_refs[j][:, keys], p_prev)
            acc[j] = o if acc[j] is None else acc[j] + o
        s_prev, p_prev = s_new, p_new
    return [a[:HEAD_DIM] / a[HEAD_DIM:HEAD_DIM + 1] for a in acc]


def _attend_t(k_ref, kn_ref, q_t, head_rows, head_keys, vx_refs, finish):
    c = k_ref.shape[1]
    tq = q_t.shape[1]
    qsq = q_t * q_t
    q_ts, shifts = [], []
    for rows, keys in zip(head_rows, head_keys):
        pieces = [jnp.zeros((keys.start, tq), F32), q_t[rows], jnp.zeros((c - keys.stop, tq), F32)]
        q_ts.append(jnp.concatenate([x for x in pieces if x.shape[0]], axis=0).astype(BF16))
        qn = jnp.sqrt(jnp.sum(qsq[rows], axis=0, keepdims=True))
        shifts.append(qn * kn_ref[0:1, keys.start:keys.start + 1])
    worst = jnp.max(jnp.concatenate(shifts, axis=0))

    @pl.when(2.0 * worst < SHIFT_LIMIT)
    def _():
        finish(_attend_shifted(k_ref, q_ts, shifts, vx_refs))

    @pl.when(jnp.logical_not(2.0 * worst < SHIFT_LIMIT))
    def _():
        finish(_attend_exact(k_ref, q_ts, vx_refs))


def _gqa_kernel(q_ref, k_ref, v_ref, cq_ref, sq_ref, ck_ref, sk_ref, gq_ref, gk_ref, o_ref,
                k_s, kn_s, vx_s):
    @pl.when(pl.program_id(1) == 0)
    def _():
        k = k_ref[0].astype(F32)
        k = k * lax.rsqrt(_head_sum(k * k, HEAD_DIM) * (1.0 / HEAD_DIM) + NORM_EPS) * gk_ref[...]
        k = _rope(k, ck_ref[...], sk_ref[...], HEAD_DIM // 4)
        k_s[...] = k.astype(BF16)
        kn_s[...] = jnp.sqrt(jnp.max(_head_sum(k * k, HEAD_DIM), axis=0, keepdims=True))
        v_t = v_ref[0].astype(F32).T
        for g in range(A_KV_HEADS):
            vx_s[g] = _value_ext_t(v_t[g * HEAD_DIM:(g + 1) * HEAD_DIM])

    q = q_ref[0].astype(F32)
    q = q * lax.rsqrt(_head_sum(q * q, HEAD_DIM) * (1.0 / HEAD_DIM) + NORM_EPS) * gq_ref[...]
    q_t = (_rope(q, cq_ref[...], sq_ref[...], HEAD_DIM // 4) * (HEAD_DIM ** -0.5 * LOG2E)).T
    group = A_HEADS // A_KV_HEADS
    head_rows = [slice(h * HEAD_DIM, (h + 1) * HEAD_DIM) for h in range(A_HEADS)]
    head_keys = [slice((h // group) * HEAD_DIM, (h // group + 1) * HEAD_DIM) for h in range(A_HEADS)]

    def finish(outs):
        o_ref[0] = jnp.concatenate(outs, axis=0).T.astype(BF16)

    _attend_t(k_s, kn_s, q_t, head_rows, head_keys, [vx_s.at[h // group] for h in range(A_HEADS)], finish)


def _mixer_gqa(zabc, tabs, gq, gk, tq):
    b, t, _ = zabc.shape
    cq, sq, ck, sk = tabs
    return pl.pallas_call(
        _gqa_kernel,
        grid=(b, t // tq),
        in_specs=[
            pl.BlockSpec((1, tq, 256), lambda bi, i: (bi, i, 0)),
            pl.BlockSpec((1, t, 128), lambda bi, i: (bi, 0, 2)),
            pl.BlockSpec((1, t, 128), lambda bi, i: (bi, 0, 3)),
            pl.BlockSpec((tq, 256), lambda bi, i: (i, 0)),
            pl.BlockSpec((tq, 256), lambda bi, i: (i, 0)),
            pl.BlockSpec((t, 128), lambda bi, i: (0, 0)),
            pl.BlockSpec((t, 128), lambda bi, i: (0, 0)),
            pl.BlockSpec((1, 256), lambda bi, i: (0, 0)),
            pl.BlockSpec((1, 128), lambda bi, i: (0, 0)),
        ],
        out_specs=pl.BlockSpec((1, tq, MIX_W), lambda bi, i: (bi, i, 0)),
        out_shape=jax.ShapeDtypeStruct((b, t, MIX_W), BF16),
        scratch_shapes=[
            pltpu.VMEM((t, A_KV_HEADS * HEAD_DIM), BF16),
            pltpu.VMEM((1, A_KV_HEADS * HEAD_DIM), F32),
            pltpu.VMEM((A_KV_HEADS, VEXT_ROWS, t), BF16),
        ],
        compiler_params=_params("parallel", "arbitrary"),
        name="mixer_gqa",
    )(zabc, zabc, zabc, cq, sq, ck, sk, gq, gk)


def _diff_kernel(lam_init, q_ref, k_ref, v_ref, cos_ref, sin_ref, cosk_ref, sink_ref, lam_ref, gain_ref,
                 o_ref, k_s, kn_s, vx_s):
    @pl.when(pl.program_id(1) == 0)
    def _():
        k = _rope(k_ref[0].astype(F32), cosk_ref[...], sink_ref[...], B_ROT // 2)
        k_s[...] = k.astype(BF16)
        kn_s[...] = jnp.sqrt(jnp.max(_head_sum(k * k, B_SUB), axis=0, keepdims=True))
        v_t = v_ref[0].astype(F32).T
        for h in range(B_HEADS):
            vx_s[h] = _value_ext_t(v_t[h * HEAD_DIM:(h + 1) * HEAD_DIM])

    lp = lam_ref[...]
    lam = (jnp.exp(jnp.sum(lp[0:1] * lp[1:2], axis=1, keepdims=True))
           - jnp.exp(jnp.sum(lp[2:3] * lp[3:4], axis=1, keepdims=True)) + lam_init)
    q = _rope(q_ref[0].astype(F32), cos_ref[...], sin_ref[...], B_ROT // 2) * (B_SUB ** -0.5 * LOG2E)
    q_t = q.T
    sub_rows = [slice(j * B_SUB, (j + 1) * B_SUB) for j in range(2 * B_HEADS)]

    def finish(parts):
        outs = []
        for h in range(B_HEADS):
            o = parts[2 * h] - lam * parts[2 * h + 1]
            o = o * lax.rsqrt(jnp.mean(o * o, axis=0, keepdims=True) + NORM_EPS) * gain_ref[...]
            outs.append(o * (1.0 - lam_init))
        o_ref[0] = jnp.concatenate(outs, axis=0).T.astype(BF16)

    _attend_t(k_s, kn_s, q_t, sub_rows, sub_rows, [vx_s.at[j // 2] for j in range(2 * B_HEADS)], finish)


def _mixer_diff(zabc, tabs, lam_params, gain, lam_init, tq):
    b, t, _ = zabc.shape
    cos, sin = tabs
    return pl.pallas_call(
        functools.partial(_diff_kernel, lam_init),
        grid=(b, t // tq),
        in_specs=[
            pl.BlockSpec((1, tq, 256), lambda bi, i: (bi, i, 2)),
            pl.BlockSpec((1, t, 256), lambda bi, i: (bi, 0, 3)),
            pl.BlockSpec((1, t, 256), lambda bi, i: (bi, 0, 4)),
            pl.BlockSpec((tq, 256), lambda bi, i: (i, 0)),
            pl.BlockSpec((tq, 256), lambda bi, i: (i, 0)),
            pl.BlockSpec((t, 256), lambda bi, i: (0, 0)),
            pl.BlockSpec((t, 256), lambda bi, i: (0, 0)),
            pl.BlockSpec((4, B_SUB), lambda bi, i: (0, 0)),
            pl.BlockSpec((HEAD_DIM, 1), lambda bi, i: (0, 0)),
        ],
        out_specs=pl.BlockSpec((1, tq, MIX_W), lambda bi, i: (bi, i, 0)),
        out_shape=jax.ShapeDtypeStruct((b, t, MIX_W), BF16),
        scratch_shapes=[
            pltpu.VMEM((t, 2 * B_HEADS * B_SUB), BF16),
            pltpu.VMEM((1, 2 * B_HEADS * B_SUB), F32),
            pltpu.VMEM((B_HEADS, VEXT_ROWS, t), BF16),
        ],
        compiler_params=_params("parallel", "arbitrary"),
        name="mixer_diff",
    )(zabc, zabc, zabc, cos, sin, cos, sin, lam_params, gain)


def _ret_log_gammas():
    lg = [math.log1p(-(2.0 ** (-5.0 - h))) for h in range(C_HEADS)]
    return lg, lg[::-1]


def _lane_consts(vals, shape):
    head = lax.broadcasted_iota(jnp.int32, shape, len(shape) - 1) // HEAD_DIM
    out = jnp.full(shape, vals[-1], F32)
    for h in range(len(vals) - 2, -1, -1):
        out = jnp.where(head == h, vals[h], out)
    return out


def _ret_kernel(chunk, q_ref, k_ref, v_ref, g_ref, cos_ref, sin_ref, gain_ref, o_ref,
                qr_s, kr_s, acc_s, sf_s, sb_s):
    t = q_ref.shape[1]
    c = chunk
    n = t // c
    lgf, lgb = _ret_log_gammas()
    qr_s[...] = _rope(q_ref[0].astype(F32), cos_ref[...], sin_ref[...], HEAD_DIM // 2).astype(BF16)
    kr_s[...] = (_rope(k_ref[0].astype(F32), cos_ref[...], sin_ref[...], HEAD_DIM // 2)
                 * (HEAD_DIM ** -0.5)).astype(BF16)

    ti = lax.broadcasted_iota(jnp.int32, (c, c), 0)
    si = lax.broadcasted_iota(jnp.int32, (c, c), 1)
    dist = (ti - si).astype(F32)
    lane_head = lax.broadcasted_iota(jnp.int32, (1, MIX_W), 1) // HEAD_DIM
    row = lax.broadcasted_iota(jnp.int32, (c, MIX_W), 0).astype(F32)
    lgf_l = _lane_consts(lgf, (c, MIX_W))
    lgb_l = _lane_consts(lgb, (c, MIX_W))
    qdec_f = jnp.exp(lgf_l * (row + 1.0))
    kdec_f = jnp.exp(lgf_l * (c - 1.0 - row))
    qdec_b = jnp.exp(lgb_l * (c - row))
    kdec_b = jnp.exp(lgb_l * row)
    r2 = lax.broadcasted_iota(jnp.int32, (MIX_W, MIX_W), 0) // HEAD_DIM
    c2 = lax.broadcasted_iota(jnp.int32, (MIX_W, MIX_W), 1) // HEAD_DIM
    same_head = r2 == c2
    gf_blk = jnp.where(same_head, jnp.exp(_lane_consts(lgf, (MIX_W, MIX_W)) * c), 0.0)
    gb_blk = jnp.where(same_head, jnp.exp(_lane_consts(lgb, (MIX_W, MIX_W)) * c), 0.0)

    def intra(qc, kc, vc):
        out = jnp.zeros((c, MIX_W), F32)
        for h in range(C_HEADS):
            mh = (lane_head == h).astype(F32)
            dm = jnp.where(dist > 0, jnp.exp(lgf[h] * dist),
                           jnp.where(dist < 0, jnp.exp(-lgb[h] * dist), 2.0))
            s = _dot_nt((qc * mh).astype(BF16), kc) * dm
            out = out + _dot(s.astype(BF16), (vc * mh).astype(BF16))
        return out

    sf_s[...] = jnp.zeros_like(sf_s)
    sb_s[...] = jnp.zeros_like(sb_s)

    def fwd(i, carry):
        rows = pl.ds(pl.multiple_of(i * c, c), c)
        qc = qr_s[rows, :].astype(F32)
        kc = kr_s[rows, :]
        vc = v_ref[0, rows, :].astype(F32)
        o = intra(qc, kc, vc) + _dot((qc * qdec_f).astype(BF16), sf_s[...].astype(BF16))
        acc_s[rows, :] = o
        kd = (kc.astype(F32) * kdec_f).T.astype(BF16)
        sf_s[...] = gf_blk * sf_s[...] + jnp.where(same_head, _dot(kd, vc.astype(BF16)), 0.0)
        return carry

    lax.fori_loop(0, n, fwd, 0)

    def bwd(i, carry):
        rows = pl.ds(pl.multiple_of((n - 1 - i) * c, c), c)
        qc = qr_s[rows, :].astype(F32)
        kc = kr_s[rows, :].astype(F32)
        vc = v_ref[0, rows, :]
        acc_s[rows, :] = acc_s[rows, :] + _dot((qc * qdec_b).astype(BF16), sb_s[...].astype(BF16))
        kd = (kc * kdec_b).T.astype(BF16)
        sb_s[...] = gb_blk * sb_s[...] + jnp.where(same_head, _dot(kd, vc), 0.0)
        return carry

    lax.fori_loop(0, n, bwd, 0)

    o = acc_s[...]
    o = o * lax.rsqrt(_head_sum(o * o, HEAD_DIM) * (1.0 / HEAD_DIM) + NORM_EPS) * gain_ref[...]
    g = g_ref[0].astype(F32)
    o_ref[0] = (o * (g * _sigmoid(g))).astype(BF16)


def _mixer_ret(zabc, tabs, gain, chunk):
    b, t, _ = zabc.shape
    cos, sin = tabs
    blk = lambda j: pl.BlockSpec((1, t, 256), lambda bi: (bi, 0, j))
    return pl.pallas_call(
        functools.partial(_ret_kernel, chunk),
        grid=(b,),
        in_specs=[blk(5), blk(6), blk(7), blk(8),
                  pl.BlockSpec((t, 256), lambda bi: (0, 0)),
                  pl.BlockSpec((t, 256), lambda bi: (0, 0)),
                  pl.BlockSpec((1, 256), lambda bi: (0, 0))],
        out_specs=pl.BlockSpec((1, t, MIX_W), lambda bi: (bi, 0, 0)),
        out_shape=jax.ShapeDtypeStruct((b, t, MIX_W), BF16),
        scratch_shapes=[
            pltpu.VMEM((t, MIX_W), BF16),
            pltpu.VMEM((t, MIX_W), BF16),
            pltpu.VMEM((t, MIX_W), F32),
            pltpu.VMEM((MIX_W, MIX_W), F32),
            pltpu.VMEM((MIX_W, MIX_W), F32),
        ],
        compiler_params=_params("parallel"),
        name="mixer_ret",
    )(zabc, zabc, zabc, zabc, cos, sin, gain)


def _wkv_prep_kernel(z_ref, zp_ref, zn_ref, mup_ref, mun_ref, w0_ref, wupf_ref, wupb_ref, a0_ref, aup_ref,
                     gup1_ref, gup2_ref, kk_ref, ka_ref, rk_ref,
                     r_o, k_o, v_o, kk_o, b_o, ldf_o, ldb_o, g_o, bonus_o):
    i = pl.program_id(1)
    last = pl.num_programs(1) - 1
    z = z_ref[0]
    tm = z.shape[0]
    row = lax.broadcasted_iota(jnp.int32, z.shape, 0)
    prev_row = zp_ref[0, 0, 7:8, :] * jnp.where(i > 0, 1.0, 0.0)
    next_row = zn_ref[0, 0, 0:1, :] * jnp.where(i < last, 1.0, 0.0)
    z_prev = jnp.where(row == 0, prev_row, pltpu.roll(z, 1, 0))
    z_next = jnp.where(row == tm - 1, next_row, pltpu.roll(z, tm - 1, 0))
    u = z + mup_ref[...] * (z_prev - z) + mun_ref[...] * (z_next - z)
    r = u[:, 0:256]
    k = u[:, 256:512]
    v = u[:, 512:768]
    wd = jnp.tanh(u[:, 768:896]).astype(BF16)
    ag = u[:, 896:1024]
    g2 = u[:, 1024:1088]

    def log_decay(w0, wup):
        x = w0 + _dot(wd, wup)
        w = -(jnp.maximum(-x, 0.0) + jnp.log(1.0 + jnp.exp(-jnp.abs(x)))) - 0.5
        return -jnp.exp(w)

    ldf_o[0] = log_decay(w0_ref[0:1, :], wupf_ref[...])
    ldb_o[0] = log_decay(w0_ref[1:2, :], wupb_ref[...])
    a = _sigmoid(a0_ref[...] + _dot(ag.astype(BF16), aup_ref[...]))
    g_o[0] = (_dot(_sigmoid(ag).astype(BF16), gup1_ref[...])
              + _dot(_sigmoid(g2).astype(BF16), gup2_ref[...]))
    kk = k * kk_ref[...]
    kk = kk / jnp.maximum(jnp.sqrt(_head_sum(kk * kk, HEAD_DIM)), 1e-12)
    kh = k * (1.0 + (a - 1.0) * ka_ref[...])
    r_o[0] = r
    k_o[0] = kh
    v_o[0] = v
    kk_o[0] = kk
    b_o[0] = kk * a
    bonus_o[0] = _head_sum(r * kh * rk_ref[...], HEAD_DIM) * v


def _wkv_prep(zd, p, tm):
    b, t, _ = zd.shape
    zd8 = zd.reshape(b, t // 8, 8, D_COLS)
    r8 = tm // 8
    nb8 = t // 8
    row = lambda w: pl.BlockSpec((1, w), lambda bi, i: (0, 0))
    full = lambda a: pl.BlockSpec(a.shape, lambda bi, i: (0,) * a.ndim)
    out_spec = pl.BlockSpec((1, tm, MIX_W), lambda bi, i: (bi, i, 0))
    out_shape = jax.ShapeDtypeStruct((b, t, MIX_W), F32)
    return pl.pallas_call(
        _wkv_prep_kernel,
        grid=(b, t // tm),
        in_specs=[
            pl.BlockSpec((1, tm, D_COLS), lambda bi, i: (bi, i, 0)),
            pl.BlockSpec((1, 1, 8, D_COLS), lambda bi, i: (bi, jnp.maximum(i * r8 - 1, 0), 0, 0)),
            pl.BlockSpec((1, 1, 8, D_COLS), lambda bi, i: (bi, jnp.minimum((i + 1) * r8, nb8 - 1), 0, 0)),
            row(D_COLS), row(D_COLS),
            full(p["w0"]), full(p["wup_f"]), full(p["wup_b"]), row(256), full(p["aup"]),
            full(p["gup1"]), full(p["gup2"]), row(256), row(256), row(256),
        ],
        out_specs=[out_spec] * 9,
        out_shape=[out_shape] * 9,
        compiler_params=_params("parallel", "parallel"),
        name="wkv_prep",
    )(zd, zd8, zd8, p["mu_prev"], p["mu_next"], p["w0"], p["wup_f"], p["wup_b"], p["a0"], p["aup"],
      p["gup1"], p["gup2"], p["k_k"], p["k_a"], p["r_k"])


def _wkv_chunk_consts(chunk):
    c = chunk
    sc = D_HEADS * c
    ti = lax.broadcasted_iota(jnp.int32, (c, c), 0)
    si = lax.broadcasted_iota(jnp.int32, (c, c), 1)
    srow = lax.broadcasted_iota(jnp.int32, (sc, MIX_W), 0) // c
    slane = lax.broadcasted_iota(jnp.int32, (sc, MIX_W), 1) // HEAD_DIM
    gt = lax.broadcasted_iota(jnp.int32, (2 * sc, 2 * sc), 0)
    gs = lax.broadcasted_iota(jnp.int32, (2 * sc, 2 * sc), 1)
    t_in, s_in = gt % sc, gs % sc
    same = (t_in // c) == (s_in // c)
    upper = gt < sc
    st = lax.broadcasted_iota(jnp.int32, (sc, sc), 0)
    ss = lax.broadcasted_iota(jnp.int32, (sc, sc), 1)
    r2 = lax.broadcasted_iota(jnp.int32, (MIX_W, MIX_W), 0)
    c2 = lax.broadcasted_iota(jnp.int32, (MIX_W, MIX_W), 1)
    return {
        "tri": {False: jnp.where(si <= ti, 1.0, 0.0).astype(BF16), True: jnp.where(si >= ti, 1.0, 0.0).astype(BF16)},
        "hmask": srow == slane,
        "gmask": {False: same & ((s_in < t_in) | ((s_in == t_in) & ~upper)),
                  True: same & ((s_in > t_in) | ((s_in == t_in) & ~upper))},
        "eye": jnp.where(st == ss, 1.0, 0.0),
        "diag": r2 == c2,
    }


def _wkv_phase1(chunk, consts, probs):
    c = chunk
    sc = D_HEADS * c
    hmask = consts["hmask"]

    def stack(a):
        return jnp.where(hmask, jnp.concatenate([a] * D_HEADS, axis=0), 0.0).astype(BF16)

    pre = []
    for rev, r, kh, v, kk, bb, lw in probs:
        tri = consts["tri"][rev]
        l1, l2, l3 = _split3(lw)
        cl = _dot(tri, l1) + _dot(tri, l2) + _dot(tri, l3)
        tot = jnp.sum(lw, axis=0, keepdims=True)
        pre.append((cl, tot))
    yield
    ops = []
    for (rev, r, kh, v, kk, bb, lw), (cl, tot) in zip(probs, pre):
        w_inv = jnp.exp(-cl)
        w_end = jnp.exp(tot - cl)
        lhs = jnp.concatenate([stack(kk * jnp.exp(cl - lw)), stack(r * jnp.exp(cl))], axis=0)
        rhs = jnp.concatenate([stack(bb * w_inv), stack(kh * w_inv)], axis=0)
        kb = jnp.concatenate([stack(kh * w_end), stack(bb * w_end)], axis=0)
        wc = jnp.sum(jnp.where(consts["diag"], jnp.exp(tot), 0.0), axis=1, keepdims=True)
        wc = jnp.broadcast_to(wc, (MIX_W, MIX_W))
        ops.append((lhs, rhs, stack(v), kb, wc))
    yield
    gram = [jnp.where(consts["gmask"][p[0]], _dot_nt(o[0], o[1]), 0.0) for p, o in zip(probs, ops)]
    l_ab = [g[:sc, :sc] for g in gram]
    l_ak = [g[:sc, sc:].astype(BF16) for g in gram]
    m_r = [g[sc:, :].astype(BF16) for g in gram]
    yield

    steps = int(math.log2(c))
    pw = [(-l).astype(BF16) for l in l_ab]
    inv = [consts["eye"] - l for l in l_ab]
    w = [_dot(a, o[2]).astype(BF16) for a, o in zip(l_ak, ops)]
    for k in range(1, steps):
        if k == 1:
            pw = [_dot(p, p).astype(BF16) for p in pw]
            yield
        if k < steps - 1:
            nxt = [_dot(p, jnp.concatenate([p, i.astype(BF16)], axis=1)) for p, i in zip(pw, inv)]
            inv = [i + n[:, sc:] for i, n in zip(inv, nxt)]
            pw = [n[:, :sc].astype(BF16) for n in nxt]
        else:
            inv = [i + _dot(p, i.astype(BF16)) for p, i in zip(pw, inv)]
        yield
    inv = [i.astype(BF16) for i in inv]

    gu = [_dot(i, jnp.concatenate([o[0][:sc], wv], axis=1)) for i, o, wv in zip(inv, ops, w)]
    yield
    kb_t = [o[3].astype(F32).T.astype(BF16) for o in ops]
    mgu = [_dot(m[:, :sc], x.astype(BF16)) for m, x in zip(m_r, gu)]
    mv = [_dot(m[:, sc:], o[2]) for m, o in zip(m_r, ops)]
    yield
    return [{
        "qg": jnp.concatenate([(o[0][sc:].astype(F32) - mg[:, :MIX_W]).astype(BF16),
                               g[:, :MIX_W].astype(BF16)], axis=0),
        "y_hat": a - mg[:, MIX_W:],
        "u_hat": g[:, MIX_W:],
        "kb_t": kt,
        "v": o[2],
        "wc": o[4],
    } for o, g, mg, a, kt in zip(ops, gu, mgu, mv, kb_t)]


def _wkv_scan_kernel(chunk, group, rf, kf, vf, kkf, bf, lf, rb, kb, vb, kkb, bb, lb, yf_o, yb_o,
                     xf_s, xb_s, qg_s, yh_s, uh_s, kbt_s, v_s, wc_s):
    @pl.when(pl.program_id(1) == 0)
    def _():
        xf_s[...] = jnp.zeros_like(xf_s)
        xb_s[...] = jnp.zeros_like(xb_s)

    tm = rf.shape[1]
    n = tm // chunk
    sc = D_HEADS * chunk
    consts = _wkv_chunk_consts(chunk)
    sol_refs = {"qg": qg_s, "y_hat": yh_s, "u_hat": uh_s, "kb_t": kbt_s, "v": v_s, "wc": wc_s}

    def group_rows(g):
        rows = []
        for u in range(group):
            rows.append(pl.ds(pl.multiple_of((g * group + u) * chunk, chunk), chunk))
            rows.append(pl.ds(pl.multiple_of((n - 1 - g * group - u) * chunk, chunk), chunk))
        return rows

    def problems(g):
        probs = []
        for slot, rows in enumerate(group_rows(g)):
            if slot % 2:
                probs.append((True, rb[0, rows, :], kb[0, rows, :], vb[0, rows, :], kkb[0, rows, :],
                              bb[0, rows, :], lb[0, rows, :]))
            else:
                probs.append((False, rf[0, rows, :], kf[0, rows, :], vf[0, rows, :], kkf[0, rows, :],
                              bf[0, rows, :], lf[0, rows, :]))
        return probs

    def advance(slot, rows):
        x_s, y_o = (xb_s, yb_o) if slot % 2 else (xf_s, yf_o)
        x = x_s[...]
        res = _dot(qg_s[slot], x.astype(BF16))
        ys = res[:sc] + yh_s[slot]
        y = ys[0:chunk]
        for h in range(1, D_HEADS):
            y = y + ys[h * chunk:(h + 1) * chunk]
        y_o[0, rows, :] = y
        u = -(res[sc:] + uh_s[slot])
        x_s[...] = x * wc_s[slot] + _dot(kbt_s[slot], jnp.concatenate([v_s[slot], u.astype(BF16)], axis=0))

    def solve(g, pending):
        gen = _wkv_phase1(chunk, consts, problems(g))
        pending = list(pending)
        while True:
            try:
                next(gen)
            except StopIteration as done:
                sols = done.value
                break
            for step in pending[:2]:
                step()
            pending = pending[2:]
        for step in pending:
            step()
        for slot, sol in enumerate(sols):
            for name, ref in sol_refs.items():
                ref[slot] = sol[name]

    def steps_for(g):
        return [functools.partial(advance, slot, r) for slot, r in enumerate(group_rows(g))]

    n_groups = n // group
    solve(0, [])

    for g in range(1, n_groups):
        solve(g, steps_for(g - 1))
    for step in steps_for(n_groups - 1):
        step()


def _wkv_scan(r, kh, v, kk, bb, ldf, ldb, tm, chunk, group):
    b, t, _ = r.shape
    nt = t // tm
    sc = D_HEADS * chunk
    slots = 2 * group
    fspec = pl.BlockSpec((1, tm, MIX_W), lambda bi, i: (bi, i, 0))
    bspec = pl.BlockSpec((1, tm, MIX_W), lambda bi, i: (bi, nt - 1 - i, 0))
    shape = jax.ShapeDtypeStruct((b, t, MIX_W), F32)
    return pl.pallas_call(
        functools.partial(_wkv_scan_kernel, chunk, group),
        grid=(b, nt),
        in_specs=[fspec] * 6 + [bspec] * 6,
        out_specs=[fspec, bspec],
        out_shape=[shape, shape],
        scratch_shapes=[
            pltpu.VMEM((MIX_W, MIX_W), F32), pltpu.VMEM((MIX_W, MIX_W), F32),
            pltpu.VMEM((slots, 2 * sc, MIX_W), BF16),
            pltpu.VMEM((slots, sc, MIX_W), F32),
            pltpu.VMEM((slots, sc, MIX_W), F32),
            pltpu.VMEM((slots, MIX_W, 2 * sc), BF16),
            pltpu.VMEM((slots, sc, MIX_W), BF16),
            pltpu.VMEM((slots, MIX_W, MIX_W), F32),
        ],
        compiler_params=_params("parallel", "arbitrary"),
        name="wkv_scan",
    )(r, kh, v, kk, bb, ldf, r, kh, v, kk, bb, ldb)


def _wkv_out(yf, yb, bonus, g, gn_w, gn_b):
    y = yf + yb
    mean = _head_sum(y, HEAD_DIM) * (1.0 / HEAD_DIM)
    yc = y - mean
    var = _head_sum(yc * yc, HEAD_DIM) * (1.0 / HEAD_DIM)
    yn = yc * lax.rsqrt(var + WKV_GN_EPS) * gn_w + gn_b
    return ((yn + bonus) * g).astype(BF16)


def _post_kernel(x_ref, oa_ref, ob_ref, oc_ref, yf_ref, yb_ref, bonus_ref, g_ref, gnw_ref, gnb_ref,
                 wo_ref, gpost_ref, gpre_ref, wg_ref, wu_ref, wd_ref, gfpost_ref, y_ref):
    o_d = _wkv_out(yf_ref[...], yb_ref[...], bonus_ref[...], g_ref[...], gnw_ref[...], gnb_ref[...])
    mix = (_dot(oa_ref[...], wo_ref[0]) + _dot(ob_ref[...], wo_ref[1])
           + _dot(oc_ref[...], wo_ref[2]) + _dot(o_d, wo_ref[3]))
    x = x_ref[...] + _rms_rows(mix, gpost_ref[...])
    h = _rms_rows(x, gpre_ref[...]).astype(BF16)
    gate = _dot(h, wg_ref[...])
    up = _dot(h, wu_ref[...])
    act = (gate * _sigmoid(gate) * up).astype(BF16)
    f = _dot(act, wd_ref[...])
    y_ref[...] = x + _rms_rows(f, gfpost_ref[...])


def _post(x2, oa, ob, oc, wkv_parts, gn_w, gn_b, w_out4, g_post, g_pre, wg, wu, wd, gf_post, tm):
    n, d = x2.shape
    dff = wg.shape[1]
    tok = lambda w: pl.BlockSpec((tm, w), lambda i: (i, 0))
    row = pl.BlockSpec((1, d), lambda i: (0, 0))
    mrow = pl.BlockSpec((1, MIX_W), lambda i: (0, 0))
    return pl.pallas_call(
        _post_kernel,
        grid=(n // tm,),
        in_specs=[tok(d)] + [tok(MIX_W)] * 7 + [mrow, mrow,
                  pl.BlockSpec((4, MIX_W, d), lambda i: (0, 0, 0)), row, row,
                  pl.BlockSpec((d, dff), lambda i: (0, 0)),
                  pl.BlockSpec((d, dff), lambda i: (0, 0)),
                  pl.BlockSpec((dff, d), lambda i: (0, 0)), row],
        out_specs=tok(d),
        out_shape=jax.ShapeDtypeStruct((n, d), F32),
        compiler_params=_params("parallel"),
        name="out_proj_ffn",
    )(x2, oa, ob, oc, *wkv_parts, gn_w, gn_b, w_out4, g_post, g_pre, wg, wu, wd, gf_post)


def _angles(pos, rot_dim, theta):
    inv = theta ** (-jnp.arange(0, rot_dim, 2, dtype=F32) / rot_dim)
    return pos.astype(F32)[:, None] * inv[None, :]


def _rope_tables(t):
    rows = t // GRID_W
    row_idx = jnp.repeat(jnp.arange(rows), GRID_W)
    col_idx = jnp.tile(jnp.arange(GRID_W), rows)
    pos = jnp.arange(t)
    ar = _angles(row_idx, HEAD_DIM // 2, A_THETA)
    ac = _angles(col_idx, HEAD_DIM // 2, A_THETA)
    cos_a = jnp.concatenate([jnp.cos(ar), jnp.cos(ar), jnp.cos(ac), jnp.cos(ac)], axis=1)
    sin_a = jnp.concatenate([-jnp.sin(ar), jnp.sin(ar), -jnp.sin(ac), jnp.sin(ac)], axis=1)
    tabs_a = (jnp.tile(cos_a, (1, 4)), jnp.tile(sin_a, (1, 4)), jnp.tile(cos_a, (1, 2)), jnp.tile(sin_a, (1, 2)))
    ab = _angles(pos, B_ROT, B_THETA)
    pad1 = jnp.ones((t, B_SUB - B_ROT), F32)
    pad0 = jnp.zeros((t, B_SUB - B_ROT), F32)
    cos_b = jnp.concatenate([jnp.cos(ab), jnp.cos(ab), pad1], axis=1)
    sin_b = jnp.concatenate([-jnp.sin(ab), jnp.sin(ab), pad0], axis=1)
    tabs_b = (jnp.tile(cos_b, (1, 8)), jnp.tile(sin_b, (1, 8)))
    ang_c = _angles(pos, HEAD_DIM, C_THETA)
    cos_c = jnp.concatenate([jnp.cos(ang_c), jnp.cos(ang_c)], axis=1)
    sin_c = jnp.concatenate([-jnp.sin(ang_c), jnp.sin(ang_c)], axis=1)
    tabs_c = (jnp.tile(cos_c, (1, 4)), jnp.tile(sin_c, (1, 4)))
    return tabs_a, tabs_b, tabs_c


def _tile(n, pref):
    while n % pref:
        pref //= 2
    return pref


def _layer_weights(l, w):
    d_model = w["w_in"].shape[1]
    row = lambda a: a.reshape(1, -1).astype(F32)
    zeros = lambda r: jnp.zeros((r, MIX_W), F32)
    w_lora = w["d_w_up"].shape[2]
    return {
        "g_mix_pre": row(w["norm_mix_pre"][l]), "g_mix_post": row(w["norm_mix_post"][l]),
        "g_ffn_pre": row(w["norm_ffn_pre"][l]), "g_ffn_post": row(w["norm_ffn_post"][l]),
        "w_abc": w["w_in"][l][:, :ABC_COLS].astype(BF16), "w_d": w["w_in"][l][:, ABC_COLS:].astype(BF16),
        "w_out4": w["w_out"][l].reshape(4, MIX_W, d_model).astype(BF16),
        "a_gq": jnp.tile(row(w["a_q_gain"][l]), (1, A_HEADS)),
        "a_gk": jnp.tile(row(w["a_k_gain"][l]), (1, A_KV_HEADS)),
        "b_lambda": w["b_lambda"][l].astype(F32), "b_gain": w["b_subln_gain"][l].reshape(-1, 1).astype(F32),
        "c_gain": row(w["c_gn_gain"][l]),
        "wkv": {
            "mu_prev": row(w["d_mu_prev"][l]), "mu_next": row(w["d_mu_next"][l]),
            "w0": w["d_w0"][l].astype(F32),
            "wup_f": jnp.concatenate([w["d_w_up"][l, 0], zeros(w_lora)], axis=0).astype(BF16),
            "wup_b": jnp.concatenate([zeros(w_lora), w["d_w_up"][l, 1]], axis=0).astype(BF16),
            "a0": row(w["d_a0"][l]),
            "aup": jnp.concatenate([w["d_a_up"][l], zeros(64)], axis=0).astype(BF16),
            "gup1": jnp.concatenate([zeros(64), w["d_g_up"][l][:64]], axis=0).astype(BF16),
            "gup2": w["d_g_up"][l][64:].astype(BF16),
            "k_k": row(w["d_k_k"][l]), "k_a": row(w["d_k_a"][l]), "r_k": row(w["d_r_k"][l]),
        },
        "gn_w": row(w["d_gn_w"][l]), "gn_b": row(w["d_gn_b"][l]),
        "wg": w["ffn_w_gate"][l].astype(BF16), "wu": w["ffn_w_up"][l].astype(BF16),
        "wd": w["ffn_w_down"][l].astype(BF16),
    }


def _trunk(x, layers):
    b, t, d = x.shape
    n = b * t
    tabs_a, tabs_b, tabs_c = _rope_tables(t)
    tm_proj = _tile(n, 512)
    tm_post = _tile(n, 256)
    tq = _tile(t, 256)
    tm_wkv = _tile(t, 256)
    tm_scan = _tile(t, 1024)
    chunk = 32
    x2 = x.reshape(n, d)
    for l, p in enumerate(layers):
        lam_init = 0.8 - 0.6 * math.exp(-0.3 * l)
        zabc, zd = _in_proj(x2, p["g_mix_pre"], p["w_abc"], p["w_d"], tm_proj)
        zabc = zabc.reshape(b, t, ABC_COLS)
        zd = zd.reshape(b, t, D_COLS)
        o_a = _mixer_gqa(zabc, tabs_a, p["a_gq"], p["a_gk"], tq)
        o_b = _mixer_diff(zabc, tabs_b, p["b_lambda"], p["b_gain"], lam_init, tq)
        o_c = _mixer_ret(zabc, tabs_c, p["c_gain"], _tile(t, 256))
        r, kh, v, kk, bb, ldf, ldb, g, bonus = _wkv_prep(zd, p["wkv"], tm_wkv)
        yf, yb = _wkv_scan(r, kh, v, kk, bb, ldf, ldb, tm_scan, chunk, min(4, tm_scan // chunk))
        wkv_parts = [a.reshape(n, MIX_W) for a in (yf, yb, bonus, g)]
        x2 = _post(x2, o_a.reshape(n, MIX_W), o_b.reshape(n, MIX_W), o_c.reshape(n, MIX_W), wkv_parts,
                   p["gn_w"], p["gn_b"], p["w_out4"], p["g_mix_post"], p["g_ffn_pre"], p["wg"], p["wu"], p["wd"],
                   p["g_ffn_post"], tm_post)
    return x2.reshape(b, t, d)


def kernel(x_prompt, x_sample, norm_mix_pre, norm_mix_post, norm_ffn_pre, norm_ffn_post, w_in, w_out,
           a_q_gain, a_k_gain, b_lambda, b_subln_gain, c_gn_gain, d_mu_prev, d_mu_next, d_w0, d_w_up,
           d_a0, d_a_up, d_g_up, d_k_k, d_k_a, d_r_k, d_gn_w, d_gn_b, ffn_w_gate, ffn_w_up, ffn_w_down):
    w = {
        "norm_mix_pre": norm_mix_pre, "norm_mix_post": norm_mix_post,
        "norm_ffn_pre": norm_ffn_pre, "norm_ffn_post": norm_ffn_post,
        "w_in": w_in, "w_out": w_out, "a_q_gain": a_q_gain, "a_k_gain": a_k_gain,
        "b_lambda": b_lambda, "b_subln_gain": b_subln_gain, "c_gn_gain": c_gn_gain,
        "d_mu_prev": d_mu_prev, "d_mu_next": d_mu_next, "d_w0": d_w0, "d_w_up": d_w_up,
        "d_a0": d_a0, "d_a_up": d_a_up, "d_g_up": d_g_up, "d_k_k": d_k_k, "d_k_a": d_k_a,
        "d_r_k": d_r_k, "d_gn_w": d_gn_w, "d_gn_b": d_gn_b,
        "ffn_w_gate": ffn_w_gate, "ffn_w_up": ffn_w_up, "ffn_w_down": ffn_w_down,
    }
    layers = [_layer_weights(l, w) for l in range(w_in.shape[0])]
    return (_trunk(x_prompt, layers), _trunk(x_sample, layers))
```

```python
import functools
import math

import jax
import jax.numpy as jnp
from jax import lax
from jax.experimental import pallas as pl
from jax.experimental.pallas import tpu as pltpu

F32 = jnp.float32
BF16 = jnp.bfloat16

HEAD_DIM = 64
GRID_W = 64
NORM_EPS = 1e-6
A_HEADS, A_KV_HEADS, A_THETA = 4, 2, 10000.0
B_HEADS, B_SUB, B_ROT, B_THETA = 4, 32, 8, 500000.0
C_HEADS, C_THETA = 4, 10000.0
D_HEADS = 4
WKV_GN_EPS = 64e-5
MIX_W = 256
ABC_COLS = 2304
D_COLS = 1088
BF16_SUBLANES = 16
VEXT_ROWS = HEAD_DIM + BF16_SUBLANES
LOG2E = math.log2(math.e)
MXU_COLS = 256
KEY_CHUNK = 1024
SHIFT_LIMIT = 100.0
VMEM_LIMIT = 56 * 1024 * 1024


def _params(*sem):
    return pltpu.CompilerParams(dimension_semantics=sem, vmem_limit_bytes=VMEM_LIMIT)


def _rms_rows(x, gain):
    return x * lax.rsqrt(jnp.mean(x * x, axis=-1, keepdims=True) + NORM_EPS) * gain


def _split2(x):
    hi = x.astype(BF16)
    lo = (x - hi.astype(F32)).astype(BF16)
    return hi, lo


def _split3(x):
    h1 = x.astype(BF16)
    r1 = x - h1.astype(F32)
    h2 = r1.astype(BF16)
    h3 = (r1 - h2.astype(F32)).astype(BF16)
    return h1, h2, h3


def _head_sum(x, seg):
    w = x.shape[-1]
    r = lax.broadcasted_iota(jnp.int32, (w, w), 0) // seg
    c = lax.broadcasted_iota(jnp.int32, (w, w), 1) // seg
    bd = jnp.where(r == c, 1.0, 0.0).astype(BF16)
    hi, lo = _split2(x)
    return (jnp.dot(hi, bd, preferred_element_type=F32)
            + jnp.dot(lo, bd, preferred_element_type=F32))


def _rope(x, cos, sin, half):
    w = x.shape[-1]
    lane = lax.broadcasted_iota(jnp.int32, x.shape, 1)
    nxt = pltpu.roll(x, w - half, 1)
    prv = pltpu.roll(x, half, 1)
    return x * cos + jnp.where((lane % (2 * half)) < half, nxt, prv) * sin


def _sigmoid(x):
    return 0.5 * jnp.tanh(0.5 * x) + 0.5


def _dot(a, b):
    return jnp.dot(a, b, preferred_element_type=F32)


def _dot_nt(a, b):
    return lax.dot_general(a, b, (((1,), (1,)), ((), ())), preferred_element_type=F32)


def _in_proj_kernel(x_ref, g_ref, wabc_ref, wd_ref, zabc_ref, zd_ref):
    h = _rms_rows(x_ref[...], g_ref[...]).astype(BF16)
    zabc_ref[...] = _dot(h, wabc_ref[...]).astype(BF16)
    zd_ref[...] = _dot(h, wd_ref[...])


def _in_proj(x2, gain, w_abc, w_d, tm):
    n, d = x2.shape
    return pl.pallas_call(
        _in_proj_kernel,
        grid=(n // tm,),
        in_specs=[
            pl.BlockSpec((tm, d), lambda i: (i, 0)),
            pl.BlockSpec((1, d), lambda i: (0, 0)),
            pl.BlockSpec((d, ABC_COLS), lambda i: (0, 0)),
            pl.BlockSpec((d, D_COLS), lambda i: (0, 0)),
        ],
        out_specs=[
            pl.BlockSpec((tm, ABC_COLS), lambda i: (i, 0)),
            pl.BlockSpec((tm, D_COLS), lambda i: (i, 0)),
        ],
        out_shape=[
            jax.ShapeDtypeStruct((n, ABC_COLS), BF16),
            jax.ShapeDtypeStruct((n, D_COLS), F32),
        ],
        compiler_params=_params("parallel"),
        name="in_proj",
    )(x2, gain, w_abc, w_d)


def _value_ext_t(v_t):
    row = lax.broadcasted_iota(jnp.int32, (VEXT_ROWS - HEAD_DIM, v_t.shape[1]), 0)
    return jnp.concatenate([v_t, jnp.where(row == 0, 1.0, 0.0)], axis=0).astype(BF16)


def _attend_exact(k_ref, q_ts, vx_refs):
    outs = []
    for q_t, vx in zip(q_ts, vx_refs):
        s_t = _dot(k_ref[...], q_t)
        p_t = jnp.exp2(s_t - jnp.max(s_t, axis=0, keepdims=True)).astype(BF16)
        o = _dot(vx[...], p_t)
        outs.append(o[:HEAD_DIM] / o[HEAD_DIM:HEAD_DIM + 1])
    return outs


def _attend_shifted(k_ref, q_ts, shifts, vx_refs):
    t = k_ref.shape[0]
    ck = min(t, KEY_CHUNK)
    steps = [(j, slice(c * ck, (c + 1) * ck)) for j in range(len(q_ts)) for c in range(t // ck)]
    acc = [None] * len(q_ts)
    s_prev = p_prev = None
    for i in range(len(steps) + 2):
        s_new = p_new = None
        if i < len(steps):
            j, keys = steps[i]
            s_new = _dot(k_ref[keys, :], q_ts[j])
        if 1 <= i <= len(steps):
            p_new = jnp.exp2(s_prev - shifts[steps[i - 1][0]]).astype(BF16)
        if i >= 2:
            j, keys = steps[i - 2]
            o = _dot(vx_refs[j][:, keys], p_prev)
            acc[j] = o if acc[j] is None else acc[j] + o
        s_prev, p_prev = s_new, p_new
    return [a[:HEAD_DIM] / a[HEAD_DIM:HEAD_DIM + 1] for a in acc]


def _attend_t(k_ref, kn_ref, q_t, head_rows, head_keys, vx_refs, finish):
    c = k_ref.shape[1]
    tq = q_t.shape[1]
    qsq = q_t * q_t
    q_ts, shifts = [], []
    for rows, keys in zip(head_rows, head_keys):
        pieces = [jnp.zeros((keys.start, tq), F32), q_t[rows], jnp.zeros((c - keys.stop, tq), F32)]
        q_ts.append(jnp.concatenate([x for x in pieces if x.shape[0]], axis=0).astype(BF16))
        qn = jnp.sqrt(jnp.sum(qsq[rows], axis=0, keepdims=True))
        shifts.append(qn * kn_ref[0:1, keys.start:keys.start + 1])
    worst = jnp.max(jnp.concatenate(shifts, axis=0))

    @pl.when(2.0 * worst < SHIFT_LIMIT)
    def _():
        finish(_attend_shifted(k_ref, q_ts, shifts, vx_refs))

    @pl.when(jnp.logical_not(2.0 * worst < SHIFT_LIMIT))
    def _():
        finish(_attend_exact(k_ref, q_ts, vx_refs))


def _gqa_kernel(q_ref, k_ref, v_ref, cq_ref, sq_ref, ck_ref, sk_ref, gq_ref, gk_ref, o_ref,
                k_s, kn_s, vx_s):
    @pl.when(pl.program_id(1) == 0)
    def _():
        k = k_ref[0].astype(F32)
        k = k * lax.rsqrt(_head_sum(k * k, HEAD_DIM) * (1.0 / HEAD_DIM) + NORM_EPS) * gk_ref[...]
        k = _rope(k, ck_ref[...], sk_ref[...], HEAD_DIM // 4)
        k_s[...] = k.astype(BF16)
        kn_s[...] = jnp.sqrt(jnp.max(_head_sum(k * k, HEAD_DIM), axis=0, keepdims=True))
        v_t = v_ref[0].astype(F32).T
        for g in range(A_KV_HEADS):
            vx_s[g] = _value_ext_t(v_t[g * HEAD_DIM:(g + 1) * HEAD_DIM])

    q = q_ref[0].astype(F32)
    q = q * lax.rsqrt(_head_sum(q * q, HEAD_DIM) * (1.0 / HEAD_DIM) + NORM_EPS) * gq_ref[...]
    q_t = (_rope(q, cq_ref[...], sq_ref[...], HEAD_DIM // 4) * (HEAD_DIM ** -0.5 * LOG2E)).T
    group = A_HEADS // A_KV_HEADS
    head_rows = [slice(h * HEAD_DIM, (h + 1) * HEAD_DIM) for h in range(A_HEADS)]
    head_keys = [slice((h // group) * HEAD_DIM, (h // group + 1) * HEAD_DIM) for h in range(A_HEADS)]

    def finish(outs):
        o_ref[0] = jnp.concatenate(outs, axis=0).T.astype(BF16)

    _attend_t(k_s, kn_s, q_t, head_rows, head_keys, [vx_s.at[h // group] for h in range(A_HEADS)], finish)


def _mixer_gqa(zabc, tabs, gq, gk, tq):
    b, t, _ = zabc.shape
    cq, sq, ck, sk = tabs
    return pl.pallas_call(
        _gqa_kernel,
        grid=(b, t // tq),
        in_specs=[
            pl.BlockSpec((1, tq, 256), lambda bi, i: (bi, i, 0)),
            pl.BlockSpec((1, t, 128), lambda bi, i: (bi, 0, 2)),
            pl.BlockSpec((1, t, 128), lambda bi, i: (bi, 0, 3)),
            pl.BlockSpec((tq, 256), lambda bi, i: (i, 0)),
            pl.BlockSpec((tq, 256), lambda bi, i: (i, 0)),
            pl.BlockSpec((t, 128), lambda bi, i: (0, 0)),
            pl.BlockSpec((t, 128), lambda bi, i: (0, 0)),
            pl.BlockSpec((1, 256), lambda bi, i: (0, 0)),
            pl.BlockSpec((1, 128), lambda bi, i: (0, 0)),
        ],
        out_specs=pl.BlockSpec((1, tq, MIX_W), lambda bi, i: (bi, i, 0)),
        out_shape=jax.ShapeDtypeStruct((b, t, MIX_W), BF16),
        scratch_shapes=[
            pltpu.VMEM((t, A_KV_HEADS * HEAD_DIM), BF16),
            pltpu.VMEM((1, A_KV_HEADS * HEAD_DIM), F32),
            pltpu.VMEM((A_KV_HEADS, VEXT_ROWS, t), BF16),
        ],
        compiler_params=_params("parallel", "arbitrary"),
        name="mixer_gqa",
    )(zabc, zabc, zabc, cq, sq, ck, sk, gq, gk)


def _diff_kernel(lam_init, q_ref, k_ref, v_ref, cos_ref, sin_ref, cosk_ref, sink_ref, lam_ref, gain_ref,
                 o_ref, k_s, kn_s, vx_s):
    @pl.when(pl.program_id(1) == 0)
    def _():
        k = _rope(k_ref[0].astype(F32), cosk_ref[...], sink_ref[...], B_ROT // 2)
        k_s[...] = k.astype(BF16)
        kn_s[...] = jnp.sqrt(jnp.max(_head_sum(k * k, B_SUB), axis=0, keepdims=True))
        v_t = v_ref[0].astype(F32).T
        for h in range(B_HEADS):
            vx_s[h] = _value_ext_t(v_t[h * HEAD_DIM:(h + 1) * HEAD_DIM])

    lp = lam_ref[...]
    lam = (jnp.exp(jnp.sum(lp[0:1] * lp[1:2], axis=1, keepdims=True))
           - jnp.exp(jnp.sum(lp[2:3] * lp[3:4], axis=1, keepdims=True)) + lam_init)
    q = _rope(q_ref[0].astype(F32), cos_ref[...], sin_ref[...], B_ROT // 2) * (B_SUB ** -0.5 * LOG2E)
    q_t = q.T
    sub_rows = [slice(j * B_SUB, (j + 1) * B_SUB) for j in range(2 * B_HEADS)]

    def finish(parts):
        outs = []
        for h in range(B_HEADS):
            o = parts[2 * h] - lam * parts[2 * h + 1]
            o = o * lax.rsqrt(jnp.mean(o * o, axis=0, keepdims=True) + NORM_EPS) * gain_ref[...]
            outs.append(o * (1.0 - lam_init))
        o_ref[0] = jnp.concatenate(outs, axis=0).T.astype(BF16)

    _attend_t(k_s, kn_s, q_t, sub_rows, sub_rows, [vx_s.at[j // 2] for j in range(2 * B_HEADS)], finish)


def _mixer_diff(zabc, tabs, lam_params, gain, lam_init, tq):
    b, t, _ = zabc.shape
    cos, sin = tabs
    return pl.pallas_call(
        functools.partial(_diff_kernel, lam_init),
        grid=(b, t // tq),
        in_specs=[
            pl.BlockSpec((1, tq, 256), lambda bi, i: (bi, i, 2)),
            pl.BlockSpec((1, t, 256), lambda bi, i: (bi, 0, 3)),
            pl.BlockSpec((1, t, 256), lambda bi, i: (bi, 0, 4)),
            pl.BlockSpec((tq, 256), lambda bi, i: (i, 0)),
            pl.BlockSpec((tq, 256), lambda bi, i: (i, 0)),
            pl.BlockSpec((t, 256), lambda bi, i: (0, 0)),
            pl.BlockSpec((t, 256), lambda bi, i: (0, 0)),
            pl.BlockSpec((4, B_SUB), lambda bi, i: (0, 0)),
            pl.BlockSpec((HEAD_DIM, 1), lambda bi, i: (0, 0)),
        ],
        out_specs=pl.BlockSpec((1, tq, MIX_W), lambda bi, i: (bi, i, 0)),
        out_shape=jax.ShapeDtypeStruct((b, t, MIX_W), BF16),
        scratch_shapes=[
            pltpu.VMEM((t, 2 * B_HEADS * B_SUB), BF16),
            pltpu.VMEM((1, 2 * B_HEADS * B_SUB), F32),
            pltpu.VMEM((B_HEADS, VEXT_ROWS, t), BF16),
        ],
        compiler_params=_params("parallel", "arbitrary"),
        name="mixer_diff",
    )(zabc, zabc, zabc, cos, sin, cos, sin, lam_params, gain)


def _ret_log_gammas():
    lg = [math.log1p(-(2.0 ** (-5.0 - h))) for h in range(C_HEADS)]
    return lg, lg[::-1]


def _lane_consts(vals, shape):
    head = lax.broadcasted_iota(jnp.int32, shape, len(shape) - 1) // HEAD_DIM
    out = jnp.full(shape, vals[-1], F32)
    for h in range(len(vals) - 2, -1, -1):
        out = jnp.where(head == h, vals[h], out)
    return out


def _ret_kernel(chunk, q_ref, k_ref, v_ref, g_ref, cos_ref, sin_ref, gain_ref, o_ref,
                qr_s, kr_s, acc_s, sf_s, sb_s):
    t = q_ref.shape[1]
    c = chunk
    n = t // c
    lgf, lgb = _ret_log_gammas()
    qr_s[...] = _rope(q_ref[0].astype(F32), cos_ref[...], sin_ref[...], HEAD_DIM // 2).astype(BF16)
    kr_s[...] = (_rope(k_ref[0].astype(F32), cos_ref[...], sin_ref[...], HEAD_DIM // 2)
                 * (HEAD_DIM ** -0.5)).astype(BF16)

    ti = lax.broadcasted_iota(jnp.int32, (c, c), 0)
    si = lax.broadcasted_iota(jnp.int32, (c, c), 1)
    dist = (ti - si).astype(F32)
    lane_head = lax.broadcasted_iota(jnp.int32, (1, MIX_W), 1) // HEAD_DIM
    row = lax.broadcasted_iota(jnp.int32, (c, MIX_W), 0).astype(F32)
    lgf_l = _lane_consts(lgf, (c, MIX_W))
    lgb_l = _lane_consts(lgb, (c, MIX_W))
    qdec_f = jnp.exp(lgf_l * (row + 1.0))
    kdec_f = jnp.exp(lgf_l * (c - 1.0 - row))
    qdec_b = jnp.exp(lgb_l * (c - row))
    kdec_b = jnp.exp(lgb_l * row)
    r2 = lax.broadcasted_iota(jnp.int32, (MIX_W, MIX_W), 0) // HEAD_DIM
    c2 = lax.broadcasted_iota(jnp.int32, (MIX_W, MIX_W), 1) // HEAD_DIM
    same_head = r2 == c2
    gf_blk = jnp.where(same_head, jnp.exp(_lane_consts(lgf, (MIX_W, MIX_W)) * c), 0.0)
    gb_blk = jnp.where(same_head, jnp.exp(_lane_consts(lgb, (MIX_W, MIX_W)) * c), 0.0)

    def intra(qc, kc, vc):
        out = jnp.zeros((c, MIX_W), F32)
        for h in range(C_HEADS):
            mh = (lane_head == h).astype(F32)
            dm = jnp.where(dist > 0, jnp.exp(lgf[h] * dist),
                           jnp.where(dist < 0, jnp.exp(-lgb[h] * dist), 2.0))
            s = _dot_nt((qc * mh).astype(BF16), kc) * dm
            out = out + _dot(s.astype(BF16), (vc * mh).astype(BF16))
        return out

    sf_s[...] = jnp.zeros_like(sf_s)
    sb_s[...] = jnp.zeros_like(sb_s)

    def fwd(i, carry):
        rows = pl.ds(pl.multiple_of(i * c, c), c)
        qc = qr_s[rows, :].astype(F32)
        kc = kr_s[rows, :]
        vc = v_ref[0, rows, :].astype(F32)
        o = intra(qc, kc, vc) + _dot((qc * qdec_f).astype(BF16), sf_s[...].astype(BF16))
        acc_s[rows, :] = o
        kd = (kc.astype(F32) * kdec_f).T.astype(BF16)
        sf_s[...] = gf_blk * sf_s[...] + jnp.where(same_head, _dot(kd, vc.astype(BF16)), 0.0)
        return carry

    lax.fori_loop(0, n, fwd, 0)

    def bwd(i, carry):
        rows = pl.ds(pl.multiple_of((n - 1 - i) * c, c), c)
        qc = qr_s[rows, :].astype(F32)
        kc = kr_s[rows, :].astype(F32)
        vc = v_ref[0, rows, :]
        acc_s[rows, :] = acc_s[rows, :] + _dot((qc * qdec_b).astype(BF16), sb_s[...].astype(BF16))
        kd = (kc * kdec_b).T.astype(BF16)
        sb_s[...] = gb_blk * sb_s[...] + jnp.where(same_head, _dot(kd, vc), 0.0)
        return carry

    lax.fori_loop(0, n, bwd, 0)

    o = acc_s[...]
    o = o * lax.rsqrt(_head_sum(o * o, HEAD_DIM) * (1.0 / HEAD_DIM) + NORM_EPS) * gain_ref[...]
    g = g_ref[0].astype(F32)
    o_ref[0] = (o * (g * _sigmoid(g))).astype(BF16)


def _mixer_ret(zabc, tabs, gain, chunk):
    b, t, _ = zabc.shape
    cos, sin = tabs
    blk = lambda j: pl.BlockSpec((1, t, 256), lambda bi: (bi, 0, j))
    return pl.pallas_call(
        functools.partial(_ret_kernel, chunk),
        grid=(b,),
        in_specs=[blk(5), blk(6), blk(7), blk(8),
                  pl.BlockSpec((t, 256), lambda bi: (0, 0)),
                  pl.BlockSpec((t, 256), lambda bi: (0, 0)),
                  pl.BlockSpec((1, 256), lambda bi: (0, 0))],
        out_specs=pl.BlockSpec((1, t, MIX_W), lambda bi: (bi, 0, 0)),
        out_shape=jax.ShapeDtypeStruct((b, t, MIX_W), BF16),
        scratch_shapes=[
            pltpu.VMEM((t, MIX_W), BF16),
            pltpu.VMEM((t, MIX_W), BF16),
            pltpu.VMEM((t, MIX_W), F32),
            pltpu.VMEM((MIX_W, MIX_W), F32),
            pltpu.VMEM((MIX_W, MIX_W), F32),
        ],
        compiler_params=_params("parallel"),
        name="mixer_ret",
    )(zabc, zabc, zabc, zabc, cos, sin, gain)


def _wkv_prep_kernel(z_ref, zp_ref, zn_ref, mup_ref, mun_ref, w0_ref, wupf_ref, wupb_ref, a0_ref, aup_ref,
                     gup1_ref, gup2_ref, kk_ref, ka_ref, rk_ref,
                     r_o, k_o, v_o, kk_o, b_o, ldf_o, ldb_o, g_o, bonus_o):
    i = pl.program_id(1)
    last = pl.num_programs(1) - 1
    z = z_ref[0]
    tm = z.shape[0]
    row = lax.broadcasted_iota(jnp.int32, z.shape, 0)
    prev_row = zp_ref[0, 0, 7:8, :] * jnp.where(i > 0, 1.0, 0.0)
    next_row = zn_ref[0, 0, 0:1, :] * jnp.where(i < last, 1.0, 0.0)
    z_prev = jnp.where(row == 0, prev_row, pltpu.roll(z, 1, 0))
    z_next = jnp.where(row == tm - 1, next_row, pltpu.roll(z, tm - 1, 0))
    u = z + mup_ref[...] * (z_prev - z) + mun_ref[...] * (z_next - z)
    r = u[:, 0:256]
    k = u[:, 256:512]
    v = u[:, 512:768]
    wd = jnp.tanh(u[:, 768:896]).astype(BF16)
    ag = u[:, 896:1024]
    g2 = u[:, 1024:1088]

    def log_decay(w0, wup):
        x = w0 + _dot(wd, wup)
        w = -(jnp.maximum(-x, 0.0) + jnp.log(1.0 + jnp.exp(-jnp.abs(x)))) - 0.5
        return -jnp.exp(w)

    ldf_o[0] = log_decay(w0_ref[0:1, :], wupf_ref[...])
    ldb_o[0] = log_decay(w0_ref[1:2, :], wupb_ref[...])
    a = _sigmoid(a0_ref[...] + _dot(ag.astype(BF16), aup_ref[...]))
    g_o[0] = (_dot(_sigmoid(ag).astype(BF16), gup1_ref[...])
              + _dot(_sigmoid(g2).astype(BF16), gup2_ref[...]))
    kk = k * kk_ref[...]
    kk = kk / jnp.maximum(jnp.sqrt(_head_sum(kk * kk, HEAD_DIM)), 1e-12)
    kh = k * (1.0 + (a - 1.0) * ka_ref[...])
    r_o[0] = r
    k_o[0] = kh
    v_o[0] = v
    kk_o[0] = kk
    b_o[0] = kk * a
    bonus_o[0] = _head_sum(r * kh * rk_ref[...], HEAD_DIM) * v


def _wkv_prep(zd, p, tm):
    b, t, _ = zd.shape
    zd8 = zd.reshape(b, t // 8, 8, D_COLS)
    r8 = tm // 8
    nb8 = t // 8
    row = lambda w: pl.BlockSpec((1, w), lambda bi, i: (0, 0))
    full = lambda a: pl.BlockSpec(a.shape, lambda bi, i: (0,) * a.ndim)
    out_spec = pl.BlockSpec((1, tm, MIX_W), lambda bi, i: (bi, i, 0))
    out_shape = jax.ShapeDtypeStruct((b, t, MIX_W), F32)
    return pl.pallas_call(
        _wkv_prep_kernel,
        grid=(b, t // tm),
        in_specs=[
            pl.BlockSpec((1, tm, D_COLS), lambda bi, i: (bi, i, 0)),
            pl.BlockSpec((1, 1, 8, D_COLS), lambda bi, i: (bi, jnp.maximum(i * r8 - 1, 0), 0, 0)),
            pl.BlockSpec((1, 1, 8, D_COLS), lambda bi, i: (bi, jnp.minimum((i + 1) * r8, nb8 - 1), 0, 0)),
            row(D_COLS), row(D_COLS),
            full(p["w0"]), full(p["wup_f"]), full(p["wup_b"]), row(256), full(p["aup"]),
            full(p["gup1"]), full(p["gup2"]), row(256), row(256), row(256),
        ],
        out_specs=[out_spec] * 9,
        out_shape=[out_shape] * 9,
        compiler_params=_params("parallel", "parallel"),
        name="wkv_prep",
    )(zd, zd8, zd8, p["mu_prev"], p["mu_next"], p["w0"], p["wup_f"], p["wup_b"], p["a0"], p["aup"],
      p["gup1"], p["gup2"], p["k_k"], p["k_a"], p["r_k"])


def _wkv_chunk_consts(chunk):
    c = chunk
    sc = D_HEADS * c
    ti = lax.broadcasted_iota(jnp.int32, (c, c), 0)
    si = lax.broadcasted_iota(jnp.int32, (c, c), 1)
    srow = lax.broadcasted_iota(jnp.int32, (sc, MIX_W), 0) // c
    slane = lax.broadcasted_iota(jnp.int32, (sc, MIX_W), 1) // HEAD_DIM
    gt = lax.broadcasted_iota(jnp.int32, (2 * sc, 2 * sc), 0)
    gs = lax.broadcasted_iota(jnp.int32, (2 * sc, 2 * sc), 1)
    t_in, s_in = gt % sc, gs % sc
    same = (t_in // c) == (s_in // c)
    upper = gt < sc
    st = lax.broadcasted_iota(jnp.int32, (sc, sc), 0)
    ss = lax.broadcasted_iota(jnp.int32, (sc, sc), 1)
    r2 = lax.broadcasted_iota(jnp.int32, (MIX_W, MIX_W), 0)
    c2 = lax.broadcasted_iota(jnp.int32, (MIX_W, MIX_W), 1)
    return {
        "tri": {False: jnp.where(si <= ti, 1.0, 0.0).astype(BF16), True: jnp.where(si >= ti, 1.0, 0.0).astype(BF16)},
        "hmask": srow == slane,
        "gmask": {False: same & ((s_in < t_in) | ((s_in == t_in) & ~upper)),
                  True: same & ((s_in > t_in) | ((s_in == t_in) & ~upper))},
        "eye": jnp.where(st == ss, 1.0, 0.0),
        "diag": r2 == c2,
    }


def _wkv_phase1(chunk, consts, probs):
    c = chunk
    sc = D_HEADS * c
    hmask = consts["hmask"]

    def stack(a):
        return jnp.where(hmask, jnp.concatenate([a] * D_HEADS, axis=0), 0.0).astype(BF16)

    pre = []
    for rev, r, kh, v, kk, bb, lw in probs:
        tri = consts["tri"][rev]
        l1, l2, l3 = _split3(lw)
        cl = _dot(tri, l1) + _dot(tri, l2) + _dot(tri, l3)
        tot = jnp.sum(lw, axis=0, keepdims=True)
        pre.append((cl, tot))
    yield
    ops = []
    for (rev, r, kh, v, kk, bb, lw), (cl, tot) in zip(probs, pre):
        w_inv = jnp.exp(-cl)
        w_end = jnp.exp(tot - cl)
        lhs = jnp.concatenate([stack(kk * jnp.exp(cl - lw)), stack(r * jnp.exp(cl))], axis=0)
        rhs = jnp.concatenate([stack(bb * w_inv), stack(kh * w_inv)], axis=0)
        kb = jnp.concatenate([stack(kh * w_end), stack(bb * w_end)], axis=0)
        wc = jnp.sum(jnp.where(consts["diag"], jnp.exp(tot), 0.0), axis=1, keepdims=True)
        wc = jnp.broadcast_to(wc, (MIX_W, MIX_W))
        ops.append((lhs, rhs, stack(v), kb, wc))
    yield
    gram = [jnp.where(consts["gmask"][p[0]], _dot_nt(o[0], o[1]), 0.0) for p, o in zip(probs, ops)]
    l_ab = [g[:sc, :sc] for g in gram]
    l_ak = [g[:sc, sc:].astype(BF16) for g in gram]
    m_r = [g[sc:, :].astype(BF16) for g in gram]
    yield

    steps = int(math.log2(c))
    pw = [(-l).astype(BF16) for l in l_ab]
    inv = [consts["eye"] - l for l in l_ab]
    w = [_dot(a, o[2]).astype(BF16) for a, o in zip(l_ak, ops)]
    for k in range(1, steps):
        if k == 1:
            pw = [_dot(p, p).astype(BF16) for p in pw]
            yield
        if k < steps - 1:
            nxt = [_dot(p, jnp.concatenate([p, i.astype(BF16)], axis=1)) for p, i in zip(pw, inv)]
            inv = [i + n[:, sc:] for i, n in zip(inv, nxt)]
            pw = [n[:, :sc].astype(BF16) for n in nxt]
        else:
            inv = [i + _dot(p, i.astype(BF16)) for p, i in zip(pw, inv)]
        yield
    inv = [i.astype(BF16) for i in inv]

    gu = [_dot(i, jnp.concatenate([o[0][:sc], wv], axis=1)) for i, o, wv in zip(inv, ops, w)]
    yield
    kb_t = [o[3].astype(F32).T.astype(BF16) for o in ops]
    mgu = [_dot(m[:, :sc], x.astype(BF16)) for m, x in zip(m_r, gu)]
    mv = [_dot(m[:, sc:], o[2]) for m, o in zip(m_r, ops)]
    yield
    return [{
        "qg": jnp.concatenate([(o[0][sc:].astype(F32) - mg[:, :MIX_W]).astype(BF16),
                               g[:, :MIX_W].astype(BF16)], axis=0),
        "y_hat": a - mg[:, MIX_W:],
        "u_hat": g[:, MIX_W:],
        "kb_t": kt,
        "v": o[2],
        "wc": o[4],
    } for o, g, mg, a, kt in zip(ops, gu, mgu, mv, kb_t)]


def _wkv_scan_kernel(chunk, group, rf, kf, vf, kkf, bf, lf, rb, kb, vb, kkb, bb, lb, yf_o, yb_o,
                     xf_s, xb_s, qg_s, yh_s, uh_s, kbt_s, v_s, wc_s):
    @pl.when(pl.program_id(1) == 0)
    def _():
        xf_s[...] = jnp.zeros_like(xf_s)
        xb_s[...] = jnp.zeros_like(xb_s)

    tm = rf.shape[1]
    n = tm // chunk
    sc = D_HEADS * chunk
    consts = _wkv_chunk_consts(chunk)
    sol_refs = {"qg": qg_s, "y_hat": yh_s, "u_hat": uh_s, "kb_t": kbt_s, "v": v_s, "wc": wc_s}

    def group_rows(g):
        rows = []
        for u in range(group):
            rows.append(pl.ds(pl.multiple_of((g * group + u) * chunk, chunk), chunk))
            rows.append(pl.ds(pl.multiple_of((n - 1 - g * group - u) * chunk, chunk), chunk))
        return rows

    def problems(g):
        probs = []
        for slot, rows in enumerate(group_rows(g)):
            if slot % 2:
                probs.append((True, rb[0, rows, :], kb[0, rows, :], vb[0, rows, :], kkb[0, rows, :],
                              bb[0, rows, :], lb[0, rows, :]))
            else:
                probs.append((False, rf[0, rows, :], kf[0, rows, :], vf[0, rows, :], kkf[0, rows, :],
                              bf[0, rows, :], lf[0, rows, :]))
        return probs

    def advance(slot, rows):
        x_s, y_o = (xb_s, yb_o) if slot % 2 else (xf_s, yf_o)
        x = x_s[...]
        res = _dot(qg_s[slot], x.astype(BF16))
        ys = res[:sc] + yh_s[slot]
        y = ys[0:chunk]
        for h in range(1, D_HEADS):
            y = y + ys[h * chunk:(h + 1) * chunk]
        y_o[0, rows, :] = y
        u = -(res[sc:] + uh_s[slot])
        x_s[...] = x * wc_s[slot] + _dot(kbt_s[slot], jnp.concatenate([v_s[slot], u.astype(BF16)], axis=0))

    def solve(g, pending):
        gen = _wkv_phase1(chunk, consts, problems(g))
        pending = list(pending)
        while True:
            try:
                next(gen)
            except StopIteration as done:
                sols = done.value
                break
            for step in pending[:2]:
                step()
            pending = pending[2:]
        for step in pending:
            step()
        for slot, sol in enumerate(sols):
            for name, ref in sol_refs.items():
                ref[slot] = sol[name]

    def steps_for(g):
        return [functools.partial(advance, slot, r) for slot, r in enumerate(group_rows(g))]

    n_groups = n // group
    solve(0, [])

    for g in range(1, n_groups):
        solve(g, steps_for(g - 1))
    for step in steps_for(n_groups - 1):
        step()


def _wkv_scan(r, kh, v, kk, bb, ldf, ldb, tm, chunk, group):
    b, t, _ = r.shape
    nt = t // tm
    sc = D_HEADS * chunk
    slots = 2 * group
    fspec = pl.BlockSpec((1, tm, MIX_W), lambda bi, i: (bi, i, 0))
    bspec = pl.BlockSpec((1, tm, MIX_W), lambda bi, i: (bi, nt - 1 - i, 0))
    shape = jax.ShapeDtypeStruct((b, t, MIX_W), F32)
    return pl.pallas_call(
        functools.partial(_wkv_scan_kernel, chunk, group),
        grid=(b, nt),
        in_specs=[fspec] * 6 + [bspec] * 6,
        out_specs=[fspec, bspec],
        out_shape=[shape, shape],
        scratch_shapes=[
            pltpu.VMEM((MIX_W, MIX_W), F32), pltpu.VMEM((MIX_W, MIX_W), F32),
            pltpu.VMEM((slots, 2 * sc, MIX_W), BF16),
            pltpu.VMEM((slots, sc, MIX_W), F32),
            pltpu.VMEM((slots, sc, MIX_W), F32),
            pltpu.VMEM((slots, MIX_W, 2 * sc), BF16),
            pltpu.VMEM((slots, sc, MIX_W), BF16),
            pltpu.VMEM((slots, MIX_W, MIX_W), F32),
        ],
        compiler_params=_params("parallel", "arbitrary"),
        name="wkv_scan",
    )(r, kh, v, kk, bb, ldf, r, kh, v, kk, bb, ldb)


def _wkv_out(yf, yb, bonus, g, gn_w, gn_b):
    y = yf + yb
    mean = _head_sum(y, HEAD_DIM) * (1.0 / HEAD_DIM)
    yc = y - mean
    var = _head_sum(yc * yc, HEAD_DIM) * (1.0 / HEAD_DIM)
    yn = yc * lax.rsqrt(var + WKV_GN_EPS) * gn_w + gn_b
    return ((yn + bonus) * g).astype(BF16)


def _post_kernel(x_ref, oa_ref, ob_ref, oc_ref, yf_ref, yb_ref, bonus_ref, g_ref, gnw_ref, gnb_ref,
                 wo_ref, gpost_ref, gpre_ref, wg_ref, wu_ref, wd_ref, gfpost_ref, y_ref):
    o_d = _wkv_out(yf_ref[...], yb_ref[...], bonus_ref[...], g_ref[...], gnw_ref[...], gnb_ref[...])
    mix = (_dot(oa_ref[...], wo_ref[0]) + _dot(ob_ref[...], wo_ref[1])
           + _dot(oc_ref[...], wo_ref[2]) + _dot(o_d, wo_ref[3]))
    x = x_ref[...] + _rms_rows(mix, gpost_ref[...])
    h = _rms_rows(x, gpre_ref[...]).astype(BF16)
    f = None
    for j in range(wg_ref.shape[1] // MXU_COLS):
        cols = slice(j * MXU_COLS, (j + 1) * MXU_COLS)
        gate = _dot(h, wg_ref[:, cols])
        up = _dot(h, wu_ref[:, cols])
        part = _dot((gate * _sigmoid(gate) * up).astype(BF16), wd_ref[cols, :])
        f = part if f is None else f + part
    y_ref[...] = x + _rms_rows(f, gfpost_ref[...])


def _post(x2, oa, ob, oc, wkv_parts, gn_w, gn_b, w_out4, g_post, g_pre, wg, wu, wd, gf_post, tm):
    n, d = x2.shape
    dff = wg.shape[1]
    tok = lambda w: pl.BlockSpec((tm, w), lambda i: (i, 0))
    row = pl.BlockSpec((1, d), lambda i: (0, 0))
    mrow = pl.BlockSpec((1, MIX_W), lambda i: (0, 0))
    return pl.pallas_call(
        _post_kernel,
        grid=(n // tm,),
        in_specs=[tok(d)] + [tok(MIX_W)] * 7 + [mrow, mrow,
                  pl.BlockSpec((4, MIX_W, d), lambda i: (0, 0, 0)), row, row,
                  pl.BlockSpec((d, dff), lambda i: (0, 0)),
                  pl.BlockSpec((d, dff), lambda i: (0, 0)),
                  pl.BlockSpec((dff, d), lambda i: (0, 0)), row],
        out_specs=tok(d),
        out_shape=jax.ShapeDtypeStruct((n, d), F32),
        compiler_params=_params("parallel"),
        name="out_proj_ffn",
    )(x2, oa, ob, oc, *wkv_parts, gn_w, gn_b, w_out4, g_post, g_pre, wg, wu, wd, gf_post)


def _angles(pos, rot_dim, theta):
    inv = theta ** (-jnp.arange(0, rot_dim, 2, dtype=F32) / rot_dim)
    return pos.astype(F32)[:, None] * inv[None, :]


def _rope_tables(t):
    rows = t // GRID_W
    row_idx = jnp.repeat(jnp.arange(rows), GRID_W)
    col_idx = jnp.tile(jnp.arange(GRID_W), rows)
    pos = jnp.arange(t)
    ar = _angles(row_idx, HEAD_DIM // 2, A_THETA)
    ac = _angles(col_idx, HEAD_DIM // 2, A_THETA)
    cos_a = jnp.concatenate([jnp.cos(ar), jnp.cos(ar), jnp.cos(ac), jnp.cos(ac)], axis=1)
    sin_a = jnp.concatenate([-jnp.sin(ar), jnp.sin(ar), -jnp.sin(ac), jnp.sin(ac)], axis=1)
    tabs_a = (jnp.tile(cos_a, (1, 4)), jnp.tile(sin_a, (1, 4)), jnp.tile(cos_a, (1, 2)), jnp.tile(sin_a, (1, 2)))
    ab = _angles(pos, B_ROT, B_THETA)
    pad1 = jnp.ones((t, B_SUB - B_ROT), F32)
    pad0 = jnp.zeros((t, B_SUB - B_ROT), F32)
    cos_b = jnp.concatenate([jnp.cos(ab), jnp.cos(ab), pad1], axis=1)
    sin_b = jnp.concatenate([-jnp.sin(ab), jnp.sin(ab), pad0], axis=1)
    tabs_b = (jnp.tile(cos_b, (1, 8)), jnp.tile(sin_b, (1, 8)))
    ang_c = _angles(pos, HEAD_DIM, C_THETA)
    cos_c = jnp.concatenate([jnp.cos(ang_c), jnp.cos(ang_c)], axis=1)
    sin_c = jnp.concatenate([-jnp.sin(ang_c), jnp.sin(ang_c)], axis=1)
    tabs_c = (jnp.tile(cos_c, (1, 4)), jnp.tile(sin_c, (1, 4)))
    return tabs_a, tabs_b, tabs_c


def _tile(n, pref):
    while n % pref:
        pref //= 2
    return pref


def _layer_weights(l, w):
    d_model = w["w_in"].shape[1]
    row = lambda a: a.reshape(1, -1).astype(F32)
    zeros = lambda r: jnp.zeros((r, MIX_W), F32)
    w_lora = w["d_w_up"].shape[2]
    return {
        "g_mix_pre": row(w["norm_mix_pre"][l]), "g_mix_post": row(w["norm_mix_post"][l]),
        "g_ffn_pre": row(w["norm_ffn_pre"][l]), "g_ffn_post": row(w["norm_ffn_post"][l]),
        "w_abc": w["w_in"][l][:, :ABC_COLS].astype(BF16), "w_d": w["w_in"][l][:, ABC_COLS:].astype(BF16),
        "w_out4": w["w_out"][l].reshape(4, MIX_W, d_model).astype(BF16),
        "a_gq": jnp.tile(row(w["a_q_gain"][l]), (1, A_HEADS)),
        "a_gk": jnp.tile(row(w["a_k_gain"][l]), (1, A_KV_HEADS)),
        "b_lambda": w["b_lambda"][l].astype(F32), "b_gain": w["b_subln_gain"][l].reshape(-1, 1).astype(F32),
        "c_gain": row(w["c_gn_gain"][l]),
        "wkv": {
            "mu_prev": row(w["d_mu_prev"][l]), "mu_next": row(w["d_mu_next"][l]),
            "w0": w["d_w0"][l].astype(F32),
            "wup_f": jnp.concatenate([w["d_w_up"][l, 0], zeros(w_lora)], axis=0).astype(BF16),
            "wup_b": jnp.concatenate([zeros(w_lora), w["d_w_up"][l, 1]], axis=0).astype(BF16),
            "a0": row(w["d_a0"][l]),
            "aup": jnp.concatenate([w["d_a_up"][l], zeros(64)], axis=0).astype(BF16),
            "gup1": jnp.concatenate([zeros(64), w["d_g_up"][l][:64]], axis=0).astype(BF16),
            "gup2": w["d_g_up"][l][64:].astype(BF16),
            "k_k": row(w["d_k_k"][l]), "k_a": row(w["d_k_a"][l]), "r_k": row(w["d_r_k"][l]),
        },
        "gn_w": row(w["d_gn_w"][l]), "gn_b": row(w["d_gn_b"][l]),
        "wg": w["ffn_w_gate"][l].astype(BF16), "wu": w["ffn_w_up"][l].astype(BF16),
        "wd": w["ffn_w_down"][l].astype(BF16),
    }


def _trunk(x, layers):
    b, t, d = x.shape
    n = b * t
    tabs_a, tabs_b, tabs_c = _rope_tables(t)
    tm_proj = _tile(n, 512)
    tm_post = _tile(n, 256)
    tq = _tile(t, 256)
    tm_wkv = _tile(t, 256)
    tm_scan = _tile(t, 1024)
    chunk = 32
    x2 = x.reshape(n, d)
    for l, p in enumerate(layers):
        lam_init = 0.8 - 0.6 * math.exp(-0.3 * l)
        zabc, zd = _in_proj(x2, p["g_mix_pre"], p["w_abc"], p["w_d"], tm_proj)
        zabc = zabc.reshape(b, t, ABC_COLS)
        zd = zd.reshape(b, t, D_COLS)
        o_a = _mixer_gqa(zabc, tabs_a, p["a_gq"], p["a_gk"], tq)
        o_b = _mixer_diff(zabc, tabs_b, p["b_lambda"], p["b_gain"], lam_init, tq)
        o_c = _mixer_ret(zabc, tabs_c, p["c_gain"], _tile(t, 256))
        r, kh, v, kk, bb, ldf, ldb, g, bonus = _wkv_prep(zd, p["wkv"], tm_wkv)
        yf, yb = _wkv_scan(r, kh, v, kk, bb, ldf, ldb, tm_scan, chunk, min(4, tm_scan // chunk))
        wkv_parts = [a.reshape(n, MIX_W) for a in (yf, yb, bonus, g)]
        x2 = _post(x2, o_a.reshape(n, MIX_W), o_b.reshape(n, MIX_W), o_c.reshape(n, MIX_W), wkv_parts,
                   p["gn_w"], p["gn_b"], p["w_out4"], p["g_mix_post"], p["g_ffn_pre"], p["wg"], p["wu"], p["wd"],
                   p["g_ffn_post"], tm_post)
    return x2.reshape(b, t, d)


def kernel(x_prompt, x_sample, norm_mix_pre, norm_mix_post, norm_ffn_pre, norm_ffn_post, w_in, w_out,
           a_q_gain, a_k_gain, b_lambda, b_subln_gain, c_gn_gain, d_mu_prev, d_mu_next, d_w0, d_w_up,
           d_a0, d_a_up, d_g_up, d_k_k, d_k_a, d_r_k, d_gn_w, d_gn_b, ffn_w_gate, ffn_w_up, ffn_w_down):
    w = {
        "norm_mix_pre": norm_mix_pre, "norm_mix_post": norm_mix_post,
        "norm_ffn_pre": norm_ffn_pre, "norm_ffn_post": norm_ffn_post,
        "w_in": w_in, "w_out": w_out, "a_q_gain": a_q_gain, "a_k_gain": a_k_gain,
        "b_lambda": b_lambda, "b_subln_gain": b_subln_gain, "c_gn_gain": c_gn_gain,
        "d_mu_prev": d_mu_prev, "d_mu_next": d_mu_next, "d_w0": d_w0, "d_w_up": d_w_up,
        "d_a0": d_a0, "d_a_up": d_a_up, "d_g_up": d_g_up, "d_k_k": d_k_k, "d_k_a": d_k_a,
        "d_r_k": d_r_k, "d_gn_w": d_gn_w, "d_gn_b": d_gn_b,
        "ffn_w_gate": ffn_w_gate, "ffn_w_up": ffn_w_up, "ffn_w_down": ffn_w_down,
    }
    layers = [_layer_weights(l, w) for l in range(w_in.shape[0])]
    return (_trunk(x_prompt, layers), _trunk(x_sample, layers))
```

```python
import functools
import math

import jax
import jax.numpy as jnp
from jax import lax
from jax.experimental import pallas as pl
from jax.experimental.pallas import tpu as pltpu

F32 = jnp.float32
BF16 = jnp.bfloat16

HEAD_DIM = 64
GRID_W = 64
NORM_EPS = 1e-6
A_HEADS, A_KV_HEADS, A_THETA = 4, 2, 10000.0
B_HEADS, B_SUB, B_ROT, B_THETA = 4, 32, 8, 500000.0
C_HEADS, C_THETA = 4, 10000.0
D_HEADS = 4
WKV_GN_EPS = 64e-5
MIX_W = 256
ABC_COLS = 2304
D_COLS = 1088
BF16_SUBLANES = 16
VEXT_ROWS = HEAD_DIM + BF16_SUBLANES
LOG2E = math.log2(math.e)
KEY_CHUNK = 512
ROW_GROUPS = 2
SHIFT_LIMIT = 100.0
VMEM_LIMIT = 56 * 1024 * 1024


def _params(*sem):
    return pltpu.CompilerParams(dimension_semantics=sem, vmem_limit_bytes=VMEM_LIMIT)


def _rms_rows(x, gain):
    return x * lax.rsqrt(jnp.mean(x * x, axis=-1, keepdims=True) + NORM_EPS) * gain


def _split2(x):
    hi = x.astype(BF16)
    lo = (x - hi.astype(F32)).astype(BF16)
    return hi, lo


def _split3(x):
    h1 = x.astype(BF16)
    r1 = x - h1.astype(F32)
    h2 = r1.astype(BF16)
    h3 = (r1 - h2.astype(F32)).astype(BF16)
    return h1, h2, h3


def _head_sum(x, seg):
    w = x.shape[-1]
    r = lax.broadcasted_iota(jnp.int32, (w, w), 0) // seg
    c = lax.broadcasted_iota(jnp.int32, (w, w), 1) // seg
    bd = jnp.where(r == c, 1.0, 0.0).astype(BF16)
    hi, lo = _split2(x)
    return (jnp.dot(hi, bd, preferred_element_type=F32)
            + jnp.dot(lo, bd, preferred_element_type=F32))


def _rope(x, cos, sin, half):
    w = x.shape[-1]
    lane = lax.broadcasted_iota(jnp.int32, x.shape, 1)
    nxt = pltpu.roll(x, w - half, 1)
    prv = pltpu.roll(x, half, 1)
    return x * cos + jnp.where((lane % (2 * half)) < half, nxt, prv) * sin


def _sigmoid(x):
    return 0.5 * jnp.tanh(0.5 * x) + 0.5


def _dot(a, b):
    return jnp.dot(a, b, preferred_element_type=F32)


def _dot_nt(a, b):
    return lax.dot_general(a, b, (((1,), (1,)), ((), ())), preferred_element_type=F32)


def _in_proj_kernel(x_ref, g_ref, wabc_ref, wd_ref, zabc_ref, zd_ref):
    tm = x_ref.shape[0]
    groups = [slice(s, s + tm // ROW_GROUPS) for s in range(0, tm, tm // ROW_GROUPS)]
    h = [_rms_rows(x_ref[r], g_ref[...]).astype(BF16) for r in groups]
    for r, hg in zip(groups, h):
        zabc_ref[r] = _dot(hg, wabc_ref[...]).astype(BF16)
    for r, hg in zip(groups, h):
        zd_ref[r] = _dot(hg, wd_ref[...])


def _in_proj(x2, gain, w_abc, w_d, tm):
    n, d = x2.shape
    return pl.pallas_call(
        _in_proj_kernel,
        grid=(n // tm,),
        in_specs=[
            pl.BlockSpec((tm, d), lambda i: (i, 0)),
            pl.BlockSpec((1, d), lambda i: (0, 0)),
            pl.BlockSpec((d, ABC_COLS), lambda i: (0, 0)),
            pl.BlockSpec((d, D_COLS), lambda i: (0, 0)),
        ],
        out_specs=[
            pl.BlockSpec((tm, ABC_COLS), lambda i: (i, 0)),
            pl.BlockSpec((tm, D_COLS), lambda i: (i, 0)),
        ],
        out_shape=[
            jax.ShapeDtypeStruct((n, ABC_COLS), BF16),
            jax.ShapeDtypeStruct((n, D_COLS), F32),
        ],
        compiler_params=_params("parallel"),
        name="in_proj",
    )(x2, gain, w_abc, w_d)


def _value_ext_t(v_t):
    row = lax.broadcasted_iota(jnp.int32, (VEXT_ROWS - HEAD_DIM, v_t.shape[1]), 0)
    return jnp.concatenate([v_t, jnp.where(row == 0, 1.0, 0.0)], axis=0).astype(BF16)


def _attend_exact(k_ref, q_ts, vx_refs):
    outs = []
    for q_t, vx in zip(q_ts, vx_refs):
        s_t = _dot(k_ref[...], q_t)
        p_t = jnp.exp2(s_t - jnp.max(s_t, axis=0, keepdims=True)).astype(BF16)
        o = _dot(vx[...], p_t)
        outs.append(o[:HEAD_DIM] / o[HEAD_DIM:HEAD_DIM + 1])
    return outs


def _attend_shifted(k_ref, q_ts, shifts, vx_refs):
    t = k_ref.shape[0]
    ck = min(t, KEY_CHUNK)
    steps = [(j, slice(c * ck, (c + 1) * ck)) for j in range(len(q_ts)) for c in range(t // ck)]
    acc = [None] * len(q_ts)
    s_prev = p_prev = None
    for i in range(len(steps) + 2):
        s_new = p_new = None
        if i < len(steps):
            j, keys = steps[i]
            s_new = _dot(k_ref[keys, :], q_ts[j])
        if 1 <= i <= len(steps):
            p_new = jnp.exp2(s_prev - shifts[steps[i - 1][0]]).astype(BF16)
        if i >= 2:
            j, keys = steps[i - 2]
            o = _dot(vx_refs[j][:, keys], p_prev)
            acc[j] = o if acc[j] is None else acc[j] + o
        s_prev, p_prev = s_new, p_new
    return [a[:HEAD_DIM] / a[HEAD_DIM:HEAD_DIM + 1] for a in acc]


def _attend_t(k_ref, kn_ref, q_t, head_rows, head_keys, vx_refs, finish):
    c = k_ref.shape[1]
    tq = q_t.shape[1]
    qsq = q_t * q_t
    q_ts, shifts = [], []
    for rows, keys in zip(head_rows, head_keys):
        pieces = [jnp.zeros((keys.start, tq), F32), q_t[rows], jnp.zeros((c - keys.stop, tq), F32)]
        q_ts.append(jnp.concatenate([x for x in pieces if x.shape[0]], axis=0).astype(BF16))
        qn = jnp.sqrt(jnp.sum(qsq[rows], axis=0, keepdims=True))
        shifts.append(qn * kn_ref[0:1, keys.start:keys.start + 1])
    worst = jnp.max(jnp.concatenate(shifts, axis=0))

    @pl.when(2.0 * worst < SHIFT_LIMIT)
    def _():
        finish(_attend_shifted(k_ref, q_ts, shifts, vx_refs))

    @pl.when(jnp.logical_not(2.0 * worst < SHIFT_LIMIT))
    def _():
        finish(_attend_exact(k_ref, q_ts, vx_refs))


def _gqa_kernel(q_ref, k_ref, v_ref, cq_ref, sq_ref, ck_ref, sk_ref, gq_ref, gk_ref, o_ref,
                k_s, kn_s, vx_s):
    @pl.when(pl.program_id(1) == 0)
    def _():
        k = k_ref[0].astype(F32)
        k = k * lax.rsqrt(_head_sum(k * k, HEAD_DIM) * (1.0 / HEAD_DIM) + NORM_EPS) * gk_ref[...]
        k = _rope(k, ck_ref[...], sk_ref[...], HEAD_DIM // 4)
        k_s[...] = k.astype(BF16)
        kn_s[...] = jnp.sqrt(jnp.max(_head_sum(k * k, HEAD_DIM), axis=0, keepdims=True))
        v_t = v_ref[0].astype(F32).T
        for g in range(A_KV_HEADS):
            vx_s[g] = _value_ext_t(v_t[g * HEAD_DIM:(g + 1) * HEAD_DIM])

    q = q_ref[0].astype(F32)
    q = q * lax.rsqrt(_head_sum(q * q, HEAD_DIM) * (1.0 / HEAD_DIM) + NORM_EPS) * gq_ref[...]
    q_t = (_rope(q, cq_ref[...], sq_ref[...], HEAD_DIM // 4) * (HEAD_DIM ** -0.5 * LOG2E)).T
    group = A_HEADS // A_KV_HEADS
    head_rows = [slice(h * HEAD_DIM, (h + 1) * HEAD_DIM) for h in range(A_HEADS)]
    head_keys = [slice((h // group) * HEAD_DIM, (h // group + 1) * HEAD_DIM) for h in range(A_HEADS)]

    def finish(outs):
        o_ref[0] = jnp.concatenate(outs, axis=0).T.astype(BF16)

    _attend_t(k_s, kn_s, q_t, head_rows, head_keys, [vx_s.at[h // group] for h in range(A_HEADS)], finish)


def _mixer_gqa(zabc, tabs, gq, gk, tq):
    b, t, _ = zabc.shape
    cq, sq, ck, sk = tabs
    return pl.pallas_call(
        _gqa_kernel,
        grid=(b, t // tq),
        in_specs=[
            pl.BlockSpec((1, tq, 256), lambda bi, i: (bi, i, 0)),
            pl.BlockSpec((1, t, 128), lambda bi, i: (bi, 0, 2)),
            pl.BlockSpec((1, t, 128), lambda bi, i: (bi, 0, 3)),
            pl.BlockSpec((tq, 256), lambda bi, i: (i, 0)),
            pl.BlockSpec((tq, 256), lambda bi, i: (i, 0)),
            pl.BlockSpec((t, 128), lambda bi, i: (0, 0)),
            pl.BlockSpec((t, 128), lambda bi, i: (0, 0)),
            pl.BlockSpec((1, 256), lambda bi, i: (0, 0)),
            pl.BlockSpec((1, 128), lambda bi, i: (0, 0)),
        ],
        out_specs=pl.BlockSpec((1, tq, MIX_W), lambda bi, i: (bi, i, 0)),
        out_shape=jax.ShapeDtypeStruct((b, t, MIX_W), BF16),
        scratch_shapes=[
            pltpu.VMEM((t, A_KV_HEADS * HEAD_DIM), BF16),
            pltpu.VMEM((1, A_KV_HEADS * HEAD_DIM), F32),
            pltpu.VMEM((A_KV_HEADS, VEXT_ROWS, t), BF16),
        ],
        compiler_params=_params("parallel", "arbitrary"),
        name="mixer_gqa",
    )(zabc, zabc, zabc, cq, sq, ck, sk, gq, gk)


def _diff_kernel(lam_init, q_ref, k_ref, v_ref, cos_ref, sin_ref, cosk_ref, sink_ref, lam_ref, gain_ref,
                 o_ref, k_s, kn_s, vx_s):
    @pl.when(pl.program_id(1) == 0)
    def _():
        k = _rope(k_ref[0].astype(F32), cosk_ref[...], sink_ref[...], B_ROT // 2)
        k_s[...] = k.astype(BF16)
        kn_s[...] = jnp.sqrt(jnp.max(_head_sum(k * k, B_SUB), axis=0, keepdims=True))
        v_t = v_ref[0].astype(F32).T
        for h in range(B_HEADS):
            vx_s[h] = _value_ext_t(v_t[h * HEAD_DIM:(h + 1) * HEAD_DIM])

    lp = lam_ref[...]
    lam = (jnp.exp(jnp.sum(lp[0:1] * lp[1:2], axis=1, keepdims=True))
           - jnp.exp(jnp.sum(lp[2:3] * lp[3:4], axis=1, keepdims=True)) + lam_init)
    q = _rope(q_ref[0].astype(F32), cos_ref[...], sin_ref[...], B_ROT // 2) * (B_SUB ** -0.5 * LOG2E)
    q_t = q.T
    sub_rows = [slice(j * B_SUB, (j + 1) * B_SUB) for j in range(2 * B_HEADS)]

    def finish(parts):
        outs = []
        for h in range(B_HEADS):
            o = parts[2 * h] - lam * parts[2 * h + 1]
            o = o * lax.rsqrt(jnp.mean(o * o, axis=0, keepdims=True) + NORM_EPS) * gain_ref[...]
            outs.append(o * (1.0 - lam_init))
        o_ref[0] = jnp.concatenate(outs, axis=0).T.astype(BF16)

    _attend_t(k_s, kn_s, q_t, sub_rows, sub_rows, [vx_s.at[j // 2] for j in range(2 * B_HEADS)], finish)


def _mixer_diff(zabc, tabs, lam_params, gain, lam_init, tq):
    b, t, _ = zabc.shape
    cos, sin = tabs
    return pl.pallas_call(
        functools.partial(_diff_kernel, lam_init),
        grid=(b, t // tq),
        in_specs=[
            pl.BlockSpec((1, tq, 256), lambda bi, i: (bi, i, 2)),
            pl.BlockSpec((1, t, 256), lambda bi, i: (bi, 0, 3)),
            pl.BlockSpec((1, t, 256), lambda bi, i: (bi, 0, 4)),
            pl.BlockSpec((tq, 256), lambda bi, i: (i, 0)),
            pl.BlockSpec((tq, 256), lambda bi, i: (i, 0)),
            pl.BlockSpec((t, 256), lambda bi, i: (0, 0)),
            pl.BlockSpec((t, 256), lambda bi, i: (0, 0)),
            pl.BlockSpec((4, B_SUB), lambda bi, i: (0, 0)),
            pl.BlockSpec((HEAD_DIM, 1), lambda bi, i: (0, 0)),
        ],
        out_specs=pl.BlockSpec((1, tq, MIX_W), lambda bi, i: (bi, i, 0)),
        out_shape=jax.ShapeDtypeStruct((b, t, MIX_W), BF16),
        scratch_shapes=[
            pltpu.VMEM((t, 2 * B_HEADS * B_SUB), BF16),
            pltpu.VMEM((1, 2 * B_HEADS * B_SUB), F32),
            pltpu.VMEM((B_HEADS, VEXT_ROWS, t), BF16),
        ],
        compiler_params=_params("parallel", "arbitrary"),
        name="mixer_diff",
    )(zabc, zabc, zabc, cos, sin, cos, sin, lam_params, gain)


def _ret_log_gammas():
    lg = [math.log1p(-(2.0 ** (-5.0 - h))) for h in range(C_HEADS)]
    return lg, lg[::-1]


def _lane_consts(vals, shape):
    head = lax.broadcasted_iota(jnp.int32, shape, len(shape) - 1) // HEAD_DIM
    out = jnp.full(shape, vals[-1], F32)
    for h in range(len(vals) - 2, -1, -1):
        out = jnp.where(head == h, vals[h], out)
    return out


def _ret_kernel(chunk, q_ref, k_ref, v_ref, g_ref, cos_ref, sin_ref, gain_ref, o_ref,
                qr_s, kr_s, acc_s, sf_s, sb_s):
    t = q_ref.shape[1]
    c = chunk
    n = t // c
    lgf, lgb = _ret_log_gammas()
    qr_s[...] = _rope(q_ref[0].astype(F32), cos_ref[...], sin_ref[...], HEAD_DIM // 2).astype(BF16)
    kr_s[...] = (_rope(k_ref[0].astype(F32), cos_ref[...], sin_ref[...], HEAD_DIM // 2)
                 * (HEAD_DIM ** -0.5)).astype(BF16)

    ti = lax.broadcasted_iota(jnp.int32, (c, c), 0)
    si = lax.broadcasted_iota(jnp.int32, (c, c), 1)
    dist = (ti - si).astype(F32)
    lane_head = lax.broadcasted_iota(jnp.int32, (1, MIX_W), 1) // HEAD_DIM
    row = lax.broadcasted_iota(jnp.int32, (c, MIX_W), 0).astype(F32)
    lgf_l = _lane_consts(lgf, (c, MIX_W))
    lgb_l = _lane_consts(lgb, (c, MIX_W))
    qdec_f = jnp.exp(lgf_l * (row + 1.0))
    kdec_f = jnp.exp(lgf_l * (c - 1.0 - row))
    qdec_b = jnp.exp(lgb_l * (c - row))
    kdec_b = jnp.exp(lgb_l * row)
    r2 = lax.broadcasted_iota(jnp.int32, (MIX_W, MIX_W), 0) // HEAD_DIM
    c2 = lax.broadcasted_iota(jnp.int32, (MIX_W, MIX_W), 1) // HEAD_DIM
    same_head = r2 == c2
    gf_blk = jnp.where(same_head, jnp.exp(_lane_consts(lgf, (MIX_W, MIX_W)) * c), 0.0)
    gb_blk = jnp.where(same_head, jnp.exp(_lane_consts(lgb, (MIX_W, MIX_W)) * c), 0.0)

    def intra(qc, kc, vc):
        out = jnp.zeros((c, MIX_W), F32)
        for h in range(C_HEADS):
            mh = (lane_head == h).astype(F32)
            dm = jnp.where(dist > 0, jnp.exp(lgf[h] * dist),
                           jnp.where(dist < 0, jnp.exp(-lgb[h] * dist), 2.0))
            s = _dot_nt((qc * mh).astype(BF16), kc) * dm
            out = out + _dot(s.astype(BF16), (vc * mh).astype(BF16))
        return out

    sf_s[...] = jnp.zeros_like(sf_s)
    sb_s[...] = jnp.zeros_like(sb_s)

    def fwd(i, carry):
        rows = pl.ds(pl.multiple_of(i * c, c), c)
        qc = qr_s[rows, :].astype(F32)
        kc = kr_s[rows, :]
        vc = v_ref[0, rows, :].astype(F32)
        o = intra(qc, kc, vc) + _dot((qc * qdec_f).astype(BF16), sf_s[...].astype(BF16))
        acc_s[rows, :] = o
        kd = (kc.astype(F32) * kdec_f).T.astype(BF16)
        sf_s[...] = gf_blk * sf_s[...] + jnp.where(same_head, _dot(kd, vc.astype(BF16)), 0.0)
        return carry

    lax.fori_loop(0, n, fwd, 0)

    def bwd(i, carry):
        rows = pl.ds(pl.multiple_of((n - 1 - i) * c, c), c)
        qc = qr_s[rows, :].astype(F32)
        kc = kr_s[rows, :].astype(F32)
        vc = v_ref[0, rows, :]
        acc_s[rows, :] = acc_s[rows, :] + _dot((qc * qdec_b).astype(BF16), sb_s[...].astype(BF16))
        kd = (kc * kdec_b).T.astype(BF16)
        sb_s[...] = gb_blk * sb_s[...] + jnp.where(same_head, _dot(kd, vc), 0.0)
        return carry

    lax.fori_loop(0, n, bwd, 0)

    o = acc_s[...]
    o = o * lax.rsqrt(_head_sum(o * o, HEAD_DIM) * (1.0 / HEAD_DIM) + NORM_EPS) * gain_ref[...]
    g = g_ref[0].astype(F32)
    o_ref[0] = (o * (g * _sigmoid(g))).astype(BF16)


def _mixer_ret(zabc, tabs, gain, chunk):
    b, t, _ = zabc.shape
    cos, sin = tabs
    blk = lambda j: pl.BlockSpec((1, t, 256), lambda bi: (bi, 0, j))
    return pl.pallas_call(
        functools.partial(_ret_kernel, chunk),
        grid=(b,),
        in_specs=[blk(5), blk(6), blk(7), blk(8),
                  pl.BlockSpec((t, 256), lambda bi: (0, 0)),
                  pl.BlockSpec((t, 256), lambda bi: (0, 0)),
                  pl.BlockSpec((1, 256), lambda bi: (0, 0))],
        out_specs=pl.BlockSpec((1, t, MIX_W), lambda bi: (bi, 0, 0)),
        out_shape=jax.ShapeDtypeStruct((b, t, MIX_W), BF16),
        scratch_shapes=[
            pltpu.VMEM((t, MIX_W), BF16),
            pltpu.VMEM((t, MIX_W), BF16),
            pltpu.VMEM((t, MIX_W), F32),
            pltpu.VMEM((MIX_W, MIX_W), F32),
            pltpu.VMEM((MIX_W, MIX_W), F32),
        ],
        compiler_params=_params("parallel"),
        name="mixer_ret",
    )(zabc, zabc, zabc, zabc, cos, sin, gain)


def _wkv_prep_kernel(z_ref, zp_ref, zn_ref, mup_ref, mun_ref, w0_ref, wupf_ref, wupb_ref, a0_ref, aup_ref,
                     gup1_ref, gup2_ref, kk_ref, ka_ref, rk_ref,
                     r_o, k_o, v_o, kk_o, b_o, ldf_o, ldb_o, g_o, bonus_o):
    i = pl.program_id(1)
    last = pl.num_programs(1) - 1
    z = z_ref[0]
    tm = z.shape[0]
    row = lax.broadcasted_iota(jnp.int32, z.shape, 0)
    prev_row = zp_ref[0, 0, 7:8, :] * jnp.where(i > 0, 1.0, 0.0)
    next_row = zn_ref[0, 0, 0:1, :] * jnp.where(i < last, 1.0, 0.0)
    z_prev = jnp.where(row == 0, prev_row, pltpu.roll(z, 1, 0))
    z_next = jnp.where(row == tm - 1, next_row, pltpu.roll(z, tm - 1, 0))
    u = z + mup_ref[...] * (z_prev - z) + mun_ref[...] * (z_next - z)
    r = u[:, 0:256]
    k = u[:, 256:512]
    v = u[:, 512:768]
    wd = jnp.tanh(u[:, 768:896]).astype(BF16)
    ag = u[:, 896:1024]
    g2 = u[:, 1024:1088]

    def log_decay(w0, wup):
        x = w0 + _dot(wd, wup)
        w = -(jnp.maximum(-x, 0.0) + jnp.log(1.0 + jnp.exp(-jnp.abs(x)))) - 0.5
        return -jnp.exp(w)

    ldf_o[0] = log_decay(w0_ref[0:1, :], wupf_ref[...])
    ldb_o[0] = log_decay(w0_ref[1:2, :], wupb_ref[...])
    a = _sigmoid(a0_ref[...] + _dot(ag.astype(BF16), aup_ref[...]))
    g_o[0] = (_dot(_sigmoid(ag).astype(BF16), gup1_ref[...])
              + _dot(_sigmoid(g2).astype(BF16), gup2_ref[...]))
    kk = k * kk_ref[...]
    kk = kk / jnp.maximum(jnp.sqrt(_head_sum(kk * kk, HEAD_DIM)), 1e-12)
    kh = k * (1.0 + (a - 1.0) * ka_ref[...])
    r_o[0] = r
    k_o[0] = kh
    v_o[0] = v
    kk_o[0] = kk
    b_o[0] = kk * a
    bonus_o[0] = _head_sum(r * kh * rk_ref[...], HEAD_DIM) * v


def _wkv_prep(zd, p, tm):
    b, t, _ = zd.shape
    zd8 = zd.reshape(b, t // 8, 8, D_COLS)
    r8 = tm // 8
    nb8 = t // 8
    row = lambda w: pl.BlockSpec((1, w), lambda bi, i: (0, 0))
    full = lambda a: pl.BlockSpec(a.shape, lambda bi, i: (0,) * a.ndim)
    out_spec = pl.BlockSpec((1, tm, MIX_W), lambda bi, i: (bi, i, 0))
    out_shape = jax.ShapeDtypeStruct((b, t, MIX_W), F32)
    return pl.pallas_call(
        _wkv_prep_kernel,
        grid=(b, t // tm),
        in_specs=[
            pl.BlockSpec((1, tm, D_COLS), lambda bi, i: (bi, i, 0)),
            pl.BlockSpec((1, 1, 8, D_COLS), lambda bi, i: (bi, jnp.maximum(i * r8 - 1, 0), 0, 0)),
            pl.BlockSpec((1, 1, 8, D_COLS), lambda bi, i: (bi, jnp.minimum((i + 1) * r8, nb8 - 1), 0, 0)),
            row(D_COLS), row(D_COLS),
            full(p["w0"]), full(p["wup_f"]), full(p["wup_b"]), row(256), full(p["aup"]),
            full(p["gup1"]), full(p["gup2"]), row(256), row(256), row(256),
        ],
        out_specs=[out_spec] * 9,
        out_shape=[out_shape] * 9,
        compiler_params=_params("parallel", "parallel"),
        name="wkv_prep",
    )(zd, zd8, zd8, p["mu_prev"], p["mu_next"], p["w0"], p["wup_f"], p["wup_b"], p["a0"], p["aup"],
      p["gup1"], p["gup2"], p["k_k"], p["k_a"], p["r_k"])


def _wkv_chunk_consts(chunk):
    c = chunk
    sc = D_HEADS * c
    ti = lax.broadcasted_iota(jnp.int32, (c, c), 0)
    si = lax.broadcasted_iota(jnp.int32, (c, c), 1)
    srow = lax.broadcasted_iota(jnp.int32, (sc, MIX_W), 0) // c
    slane = lax.broadcasted_iota(jnp.int32, (sc, MIX_W), 1) // HEAD_DIM
    gt = lax.broadcasted_iota(jnp.int32, (2 * sc, 2 * sc), 0)
    gs = lax.broadcasted_iota(jnp.int32, (2 * sc, 2 * sc), 1)
    t_in, s_in = gt % sc, gs % sc
    same = (t_in // c) == (s_in // c)
    upper = gt < sc
    st = lax.broadcasted_iota(jnp.int32, (sc, sc), 0)
    ss = lax.broadcasted_iota(jnp.int32, (sc, sc), 1)
    r2 = lax.broadcasted_iota(jnp.int32, (MIX_W, MIX_W), 0)
    c2 = lax.broadcasted_iota(jnp.int32, (MIX_W, MIX_W), 1)
    return {
        "tri": {False: jnp.where(si <= ti, 1.0, 0.0).astype(BF16), True: jnp.where(si >= ti, 1.0, 0.0).astype(BF16)},
        "hmask": srow == slane,
        "gmask": {False: same & ((s_in < t_in) | ((s_in == t_in) & ~upper)),
                  True: same & ((s_in > t_in) | ((s_in == t_in) & ~upper))},
        "eye": jnp.where(st == ss, 1.0, 0.0),
        "diag": r2 == c2,
    }


def _wkv_phase1(chunk, consts, probs):
    c = chunk
    sc = D_HEADS * c
    hmask = consts["hmask"]

    def stack(a):
        return jnp.where(hmask, jnp.concatenate([a] * D_HEADS, axis=0), 0.0).astype(BF16)

    pre = []
    for rev, r, kh, v, kk, bb, lw in probs:
        tri = consts["tri"][rev]
        l1, l2, l3 = _split3(lw)
        cl = _dot(tri, l1) + _dot(tri, l2) + _dot(tri, l3)
        tot = jnp.sum(lw, axis=0, keepdims=True)
        pre.append((cl, tot))
    yield
    ops = []
    for (rev, r, kh, v, kk, bb, lw), (cl, tot) in zip(probs, pre):
        w_inv = jnp.exp(-cl)
        w_end = jnp.exp(tot - cl)
        lhs = jnp.concatenate([stack(kk * jnp.exp(cl - lw)), stack(r * jnp.exp(cl))], axis=0)
        rhs = jnp.concatenate([stack(bb * w_inv), stack(kh * w_inv)], axis=0)
        kb = jnp.concatenate([stack(kh * w_end), stack(bb * w_end)], axis=0)
        wc = jnp.sum(jnp.where(consts["diag"], jnp.exp(tot), 0.0), axis=1, keepdims=True)
        wc = jnp.broadcast_to(wc, (MIX_W, MIX_W))
        ops.append((lhs, rhs, stack(v), kb, wc))
    yield
    gram = [jnp.where(consts["gmask"][p[0]], _dot_nt(o[0], o[1]), 0.0) for p, o in zip(probs, ops)]
    l_ab = [g[:sc, :sc] for g in gram]
    l_ak = [g[:sc, sc:].astype(BF16) for g in gram]
    m_r = [g[sc:, :].astype(BF16) for g in gram]
    yield

    steps = int(math.log2(c))
    pw = [(-l).astype(BF16) for l in l_ab]
    inv = [consts["eye"] - l for l in l_ab]
    w = [_dot(a, o[2]).astype(BF16) for a, o in zip(l_ak, ops)]
    for k in range(1, steps):
        if k == 1:
            pw = [_dot(p, p).astype(BF16) for p in pw]
            yield
        if k < steps - 1:
            nxt = [_dot(p, jnp.concatenate([p, i.astype(BF16)], axis=1)) for p, i in zip(pw, inv)]
            inv = [i + n[:, sc:] for i, n in zip(inv, nxt)]
            pw = [n[:, :sc].astype(BF16) for n in nxt]
        else:
            inv = [i + _dot(p, i.astype(BF16)) for p, i in zip(pw, inv)]
        yield
    inv = [i.astype(BF16) for i in inv]

    gu = [_dot(i, jnp.concatenate([o[0][:sc], wv], axis=1)) for i, o, wv in zip(inv, ops, w)]
    yield
    kb_t = [o[3].astype(F32).T.astype(BF16) for o in ops]
    mgu = [_dot(m[:, :sc], x.astype(BF16)) for m, x in zip(m_r, gu)]
    mv = [_dot(m[:, sc:], o[2]) for m, o in zip(m_r, ops)]
    yield
    return [{
        "qg": jnp.concatenate([(o[0][sc:].astype(F32) - mg[:, :MIX_W]).astype(BF16),
                               g[:, :MIX_W].astype(BF16)], axis=0),
        "y_hat": a - mg[:, MIX_W:],
        "u_hat": g[:, MIX_W:],
        "kb_t": kt,
        "v": o[2],
        "wc": o[4],
    } for o, g, mg, a, kt in zip(ops, gu, mgu, mv, kb_t)]


def _wkv_scan_kernel(chunk, group, rf, kf, vf, kkf, bf, lf, rb, kb, vb, kkb, bb, lb, yf_o, yb_o,
                     xf_s, xb_s, qg_s, yh_s, uh_s, kbt_s, v_s, wc_s):
    @pl.when(pl.program_id(1) == 0)
    def _():
        xf_s[...] = jnp.zeros_like(xf_s)
        xb_s[...] = jnp.zeros_like(xb_s)

    tm = rf.shape[1]
    n = tm // chunk
    sc = D_HEADS * chunk
    consts = _wkv_chunk_consts(chunk)
    sol_refs = {"qg": qg_s, "y_hat": yh_s, "u_hat": uh_s, "kb_t": kbt_s, "v": v_s, "wc": wc_s}

    def group_rows(g):
        rows = []
        for u in range(group):
            rows.append(pl.ds(pl.multiple_of((g * group + u) * chunk, chunk), chunk))
            rows.append(pl.ds(pl.multiple_of((n - 1 - g * group - u) * chunk, chunk), chunk))
        return rows

    def problems(g):
        probs = []
        for slot, rows in enumerate(group_rows(g)):
            if slot % 2:
                probs.append((True, rb[0, rows, :], kb[0, rows, :], vb[0, rows, :], kkb[0, rows, :],
                              bb[0, rows, :], lb[0, rows, :]))
            else:
                probs.append((False, rf[0, rows, :], kf[0, rows, :], vf[0, rows, :], kkf[0, rows, :],
                              bf[0, rows, :], lf[0, rows, :]))
        return probs

    def advance(slot, rows):
        x_s, y_o = (xb_s, yb_o) if slot % 2 else (xf_s, yf_o)
        x = x_s[...]
        res = _dot(qg_s[slot], x.astype(BF16))
        ys = res[:sc] + yh_s[slot]
        y = ys[0:chunk]
        for h in range(1, D_HEADS):
            y = y + ys[h * chunk:(h + 1) * chunk]
        y_o[0, rows, :] = y
        u = -(res[sc:] + uh_s[slot])
        x_s[...] = x * wc_s[slot] + _dot(kbt_s[slot], jnp.concatenate([v_s[slot], u.astype(BF16)], axis=0))

    def solve(g, pending):
        gen = _wkv_phase1(chunk, consts, problems(g))
        pending = list(pending)
        while True:
            try:
                next(gen)
            except StopIteration as done:
                sols = done.value
                break
            for step in pending[:2]:
                step()
            pending = pending[2:]
        for step in pending:
            step()
        for slot, sol in enumerate(sols):
            for name, ref in sol_refs.items():
                ref[slot] = sol[name]

    def steps_for(g):
        return [functools.partial(advance, slot, r) for slot, r in enumerate(group_rows(g))]

    n_groups = n // group
    solve(0, [])

    for g in range(1, n_groups):
        solve(g, steps_for(g - 1))
    for step in steps_for(n_groups - 1):
        step()


def _wkv_scan(r, kh, v, kk, bb, ldf, ldb, tm, chunk, group):
    b, t, _ = r.shape
    nt = t // tm
    sc = D_HEADS * chunk
    slots = 2 * group
    fspec = pl.BlockSpec((1, tm, MIX_W), lambda bi, i: (bi, i, 0))
    bspec = pl.BlockSpec((1, tm, MIX_W), lambda bi, i: (bi, nt - 1 - i, 0))
    shape = jax.ShapeDtypeStruct((b, t, MIX_W), F32)
    return pl.pallas_call(
        functools.partial(_wkv_scan_kernel, chunk, group),
        grid=(b, nt),
        in_specs=[fspec] * 6 + [bspec] * 6,
        out_specs=[fspec, bspec],
        out_shape=[shape, shape],
        scratch_shapes=[
            pltpu.VMEM((MIX_W, MIX_W), F32), pltpu.VMEM((MIX_W, MIX_W), F32),
            pltpu.VMEM((slots, 2 * sc, MIX_W), BF16),
            pltpu.VMEM((slots, sc, MIX_W), F32),
            pltpu.VMEM((slots, sc, MIX_W), F32),
            pltpu.VMEM((slots, MIX_W, 2 * sc), BF16),
            pltpu.VMEM((slots, sc, MIX_W), BF16),
            pltpu.VMEM((slots, MIX_W, MIX_W), F32),
        ],
        compiler_params=_params("parallel", "arbitrary"),
        name="wkv_scan",
    )(r, kh, v, kk, bb, ldf, r, kh, v, kk, bb, ldb)


def _wkv_out(yf, yb, bonus, g, gn_w, gn_b):
    y = yf + yb
    mean = _head_sum(y, HEAD_DIM) * (1.0 / HEAD_DIM)
    yc = y - mean
    var = _head_sum(yc * yc, HEAD_DIM) * (1.0 / HEAD_DIM)
    yn = yc * lax.rsqrt(var + WKV_GN_EPS) * gn_w + gn_b
    return ((yn + bonus) * g).astype(BF16)


def _post_kernel(x_ref, oa_ref, ob_ref, oc_ref, yf_ref, yb_ref, bonus_ref, g_ref, gnw_ref, gnb_ref,
                 wo_ref, gpost_ref, gpre_ref, wg_ref, wu_ref, wd_ref, gfpost_ref, y_ref):
    tm = x_ref.shape[0]
    groups = [slice(s, s + tm // ROW_GROUPS) for s in range(0, tm, tm // ROW_GROUPS)]
    o_d = [_wkv_out(yf_ref[r], yb_ref[r], bonus_ref[r], g_ref[r], gnw_ref[...], gnb_ref[...]) for r in groups]
    mix = [_dot(oa_ref[r], wo_ref[0]) + _dot(ob_ref[r], wo_ref[1]) + _dot(oc_ref[r], wo_ref[2])
           + _dot(od, wo_ref[3]) for r, od in zip(groups, o_d)]
    x = [x_ref[r] + _rms_rows(m, gpost_ref[...]) for r, m in zip(groups, mix)]
    h = [_rms_rows(xg, gpre_ref[...]).astype(BF16) for xg in x]
    gate = [_dot(hg, wg_ref[...]) for hg in h]
    up = [_dot(hg, wu_ref[...]) for hg in h]
    act = [(g * _sigmoid(g) * u).astype(BF16) for g, u in zip(gate, up)]
    f = [_dot(a, wd_ref[...]) for a in act]
    for r, xg, fg in zip(groups, x, f):
        y_ref[r] = xg + _rms_rows(fg, gfpost_ref[...])


def _post(x2, oa, ob, oc, wkv_parts, gn_w, gn_b, w_out4, g_post, g_pre, wg, wu, wd, gf_post, tm):
    n, d = x2.shape
    dff = wg.shape[1]
    tok = lambda w: pl.BlockSpec((tm, w), lambda i: (i, 0))
    row = pl.BlockSpec((1, d), lambda i: (0, 0))
    mrow = pl.BlockSpec((1, MIX_W), lambda i: (0, 0))
    return pl.pallas_call(
        _post_kernel,
        grid=(n // tm,),
        in_specs=[tok(d)] + [tok(MIX_W)] * 7 + [mrow, mrow,
                  pl.BlockSpec((4, MIX_W, d), lambda i: (0, 0, 0)), row, row,
                  pl.BlockSpec((d, dff), lambda i: (0, 0)),
                  pl.BlockSpec((d, dff), lambda i: (0, 0)),
                  pl.BlockSpec((dff, d), lambda i: (0, 0)), row],
        out_specs=tok(d),
        out_shape=jax.ShapeDtypeStruct((n, d), F32),
        compiler_params=_params("parallel"),
        name="out_proj_ffn",
    )(x2, oa, ob, oc, *wkv_parts, gn_w, gn_b, w_out4, g_post, g_pre, wg, wu, wd, gf_post)


def _angles(pos, rot_dim, theta):
    inv = theta ** (-jnp.arange(0, rot_dim, 2, dtype=F32) / rot_dim)
    return pos.astype(F32)[:, None] * inv[None, :]


def _rope_tables(t):
    rows = t // GRID_W
    row_idx = jnp.repeat(jnp.arange(rows), GRID_W)
    col_idx = jnp.tile(jnp.arange(GRID_W), rows)
    pos = jnp.arange(t)
    ar = _angles(row_idx, HEAD_DIM // 2, A_THETA)
    ac = _angles(col_idx, HEAD_DIM // 2, A_THETA)
    cos_a = jnp.concatenate([jnp.cos(ar), jnp.cos(ar), jnp.cos(ac), jnp.cos(ac)], axis=1)
    sin_a = jnp.concatenate([-jnp.sin(ar), jnp.sin(ar), -jnp.sin(ac), jnp.sin(ac)], axis=1)
    tabs_a = (jnp.tile(cos_a, (1, 4)), jnp.tile(sin_a, (1, 4)), jnp.tile(cos_a, (1, 2)), jnp.tile(sin_a, (1, 2)))
    ab = _angles(pos, B_ROT, B_THETA)
    pad1 = jnp.ones((t, B_SUB - B_ROT), F32)
    pad0 = jnp.zeros((t, B_SUB - B_ROT), F32)
    cos_b = jnp.concatenate([jnp.cos(ab), jnp.cos(ab), pad1], axis=1)
    sin_b = jnp.concatenate([-jnp.sin(ab), jnp.sin(ab), pad0], axis=1)
    tabs_b = (jnp.tile(cos_b, (1, 8)), jnp.tile(sin_b, (1, 8)))
    ang_c = _angles(pos, HEAD_DIM, C_THETA)
    cos_c = jnp.concatenate([jnp.cos(ang_c), jnp.cos(ang_c)], axis=1)
    sin_c = jnp.concatenate([-jnp.sin(ang_c), jnp.sin(ang_c)], axis=1)
    tabs_c = (jnp.tile(cos_c, (1, 4)), jnp.tile(sin_c, (1, 4)))
    return tabs_a, tabs_b, tabs_c


def _tile(n, pref):
    while n % pref:
        pref //= 2
    return pref


def _layer_weights(l, w):
    d_model = w["w_in"].shape[1]
    row = lambda a: a.reshape(1, -1).astype(F32)
    zeros = lambda r: jnp.zeros((r, MIX_W), F32)
    w_lora = w["d_w_up"].shape[2]
    return {
        "g_mix_pre": row(w["norm_mix_pre"][l]), "g_mix_post": row(w["norm_mix_post"][l]),
        "g_ffn_pre": row(w["norm_ffn_pre"][l]), "g_ffn_post": row(w["norm_ffn_post"][l]),
        "w_abc": w["w_in"][l][:, :ABC_COLS].astype(BF16), "w_d": w["w_in"][l][:, ABC_COLS:].astype(BF16),
        "w_out4": w["w_out"][l].reshape(4, MIX_W, d_model).astype(BF16),
        "a_gq": jnp.tile(row(w["a_q_gain"][l]), (1, A_HEADS)),
        "a_gk": jnp.tile(row(w["a_k_gain"][l]), (1, A_KV_HEADS)),
        "b_lambda": w["b_lambda"][l].astype(F32), "b_gain": w["b_subln_gain"][l].reshape(-1, 1).astype(F32),
        "c_gain": row(w["c_gn_gain"][l]),
        "wkv": {
            "mu_prev": row(w["d_mu_prev"][l]), "mu_next": row(w["d_mu_next"][l]),
            "w0": w["d_w0"][l].astype(F32),
            "wup_f": jnp.concatenate([w["d_w_up"][l, 0], zeros(w_lora)], axis=0).astype(BF16),
            "wup_b": jnp.concatenate([zeros(w_lora), w["d_w_up"][l, 1]], axis=0).astype(BF16),
            "a0": row(w["d_a0"][l]),
            "aup": jnp.concatenate([w["d_a_up"][l], zeros(64)], axis=0).astype(BF16),
            "gup1": jnp.concatenate([zeros(64), w["d_g_up"][l][:64]], axis=0).astype(BF16),
            "gup2": w["d_g_up"][l][64:].astype(BF16),
            "k_k": row(w["d_k_k"][l]), "k_a": row(w["d_k_a"][l]), "r_k": row(w["d_r_k"][l]),
        },
        "gn_w": row(w["d_gn_w"][l]), "gn_b": row(w["d_gn_b"][l]),
        "wg": w["ffn_w_gate"][l].astype(BF16), "wu": w["ffn_w_up"][l].astype(BF16),
        "wd": w["ffn_w_down"][l].astype(BF16),
    }


def _trunk(x, layers):
    b, t, d = x.shape
    n = b * t
    tabs_a, tabs_b, tabs_c = _rope_tables(t)
    tm_proj = _tile(n, 512)
    tm_post = _tile(n, 512)
    tq = _tile(t, 256)
    tm_wkv = _tile(t, 256)
    tm_scan = _tile(t, 1024)
    chunk = 32
    x2 = x.reshape(n, d)
    for l, p in enumerate(layers):
        lam_init = 0.8 - 0.6 * math.exp(-0.3 * l)
        zabc, zd = _in_proj(x2, p["g_mix_pre"], p["w_abc"], p["w_d"], tm_proj)
        zabc = zabc.reshape(b, t, ABC_COLS)
        zd = zd.reshape(b, t, D_COLS)
        o_a = _mixer_gqa(zabc, tabs_a, p["a_gq"], p["a_gk"], tq)
        o_b = _mixer_diff(zabc, tabs_b, p["b_lambda"], p["b_gain"], lam_init, tq)
        o_c = _mixer_ret(zabc, tabs_c, p["c_gain"], _tile(t, 256))
        r, kh, v, kk, bb, ldf, ldb, g, bonus = _wkv_prep(zd, p["wkv"], tm_wkv)
        yf, yb = _wkv_scan(r, kh, v, kk, bb, ldf, ldb, tm_scan, chunk, min(4, tm_scan // chunk))
        wkv_parts = [a.reshape(n, MIX_W) for a in (yf, yb, bonus, g)]
        x2 = _post(x2, o_a.reshape(n, MIX_W), o_b.reshape(n, MIX_W), o_c.reshape(n, MIX_W), wkv_parts,
                   p["gn_w"], p["gn_b"], p["w_out4"], p["g_mix_post"], p["g_ffn_pre"], p["wg"], p["wu"], p["wd"],
                   p["g_ffn_post"], tm_post)
    return x2.reshape(b, t, d)


def kernel(x_prompt, x_sample, norm_mix_pre, norm_mix_post, norm_ffn_pre, norm_ffn_post, w_in, w_out,
           a_q_gain, a_k_gain, b_lambda, b_subln_gain, c_gn_gain, d_mu_prev, d_mu_next, d_w0, d_w_up,
           d_a0, d_a_up, d_g_up, d_k_k, d_k_a, d_r_k, d_gn_w, d_gn_b, ffn_w_gate, ffn_w_up, ffn_w_down):
    w = {
        "norm_mix_pre": norm_mix_pre, "norm_mix_post": norm_mix_post,
        "norm_ffn_pre": norm_ffn_pre, "norm_ffn_post": norm_ffn_post,
        "w_in": w_in, "w_out": w_out, "a_q_gain": a_q_gain, "a_k_gain": a_k_gain,
        "b_lambda": b_lambda, "b_subln_gain": b_subln_gain, "c_gn_gain": c_gn_gain,
        "d_mu_prev": d_mu_prev, "d_mu_next": d_mu_next, "d_w0": d_w0, "d_w_up": d_w_up,
        "d_a0": d_a0, "d_a_up": d_a_up, "d_g_up": d_g_up, "d_k_k": d_k_k, "d_k_a": d_k_a,
        "d_r_k": d_r_k, "d_gn_w": d_gn_w, "d_gn_b": d_gn_b,
        "ffn_w_gate": ffn_w_gate, "ffn_w_up": ffn_w_up, "ffn_w_down": ffn_w_down,
    }
    layers = [_layer_weights(l, w) for l in range(w_in.shape[0])]
    return (_trunk(x_prompt, layers), _trunk(x_sample, layers))
```

```python
import functools
import math

import jax
import jax.numpy as jnp
from jax import lax
from jax.experimental import pallas as pl
from jax.experimental.pallas import tpu as pltpu

F32 = jnp.float32
BF16 = jnp.bfloat16

HEAD_DIM = 64
GRID_W = 64
NORM_EPS = 1e-6
A_HEADS, A_KV_HEADS, A_THETA = 4, 2, 10000.0
B_HEADS, B_SUB, B_ROT, B_THETA = 4, 32, 8, 500000.0
C_HEADS, C_THETA = 4, 10000.0
D_HEADS = 4
WKV_GN_EPS = 64e-5
MIX_W = 256
ABC_COLS = 2304
D_COLS = 1088
BF16_SUBLANES = 16
VEXT_ROWS = HEAD_DIM + BF16_SUBLANES
LOG2E = math.log2(math.e)
KEY_CHUNK = 512
SUM_ROWS = 32
ROW_GROUPS = 2
SHIFT_LIMIT = 100.0
VMEM_LIMIT = 56 * 1024 * 1024


def _params(*sem):
    return pltpu.CompilerParams(dimension_semantics=sem, vmem_limit_bytes=VMEM_LIMIT)


def _rms_rows(x, gain):
    return x * lax.rsqrt(jnp.mean(x * x, axis=-1, keepdims=True) + NORM_EPS) * gain


def _split2(x):
    hi = x.astype(BF16)
    lo = (x - hi.astype(F32)).astype(BF16)
    return hi, lo


def _split3(x):
    h1 = x.astype(BF16)
    r1 = x - h1.astype(F32)
    h2 = r1.astype(BF16)
    h3 = (r1 - h2.astype(F32)).astype(BF16)
    return h1, h2, h3


def _head_sum(x, seg):
    w = x.shape[-1]
    r = lax.broadcasted_iota(jnp.int32, (w, w), 0) // seg
    c = lax.broadcasted_iota(jnp.int32, (w, w), 1) // seg
    bd = jnp.where(r == c, 1.0, 0.0).astype(BF16)
    hi, lo = _split2(x)
    return (jnp.dot(hi, bd, preferred_element_type=F32)
            + jnp.dot(lo, bd, preferred_element_type=F32))


def _rope(x, cos, sin, half):
    w = x.shape[-1]
    lane = lax.broadcasted_iota(jnp.int32, x.shape, 1)
    nxt = pltpu.roll(x, w - half, 1)
    prv = pltpu.roll(x, half, 1)
    return x * cos + jnp.where((lane % (2 * half)) < half, nxt, prv) * sin


def _sigmoid(x):
    return 0.5 * jnp.tanh(0.5 * x) + 0.5


def _dot(a, b):
    return jnp.dot(a, b, preferred_element_type=F32)


def _dot_nt(a, b):
    return lax.dot_general(a, b, (((1,), (1,)), ((), ())), preferred_element_type=F32)


def _in_proj_kernel(x_ref, g_ref, wabc_ref, wd_ref, zabc_ref, zd_ref):
    tm = x_ref.shape[0]
    groups = [slice(s, s + tm // ROW_GROUPS) for s in range(0, tm, tm // ROW_GROUPS)]
    h = [_rms_rows(x_ref[r], g_ref[...]).astype(BF16) for r in groups]
    for r, hg in zip(groups, h):
        zabc_ref[r] = _dot(hg, wabc_ref[...]).astype(BF16)
    for r, hg in zip(groups, h):
        zd_ref[r] = _dot(hg, wd_ref[...])


def _in_proj(x2, gain, w_abc, w_d, tm):
    n, d = x2.shape
    return pl.pallas_call(
        _in_proj_kernel,
        grid=(n // tm,),
        in_specs=[
            pl.BlockSpec((tm, d), lambda i: (i, 0)),
            pl.BlockSpec((1, d), lambda i: (0, 0)),
            pl.BlockSpec((d, ABC_COLS), lambda i: (0, 0)),
            pl.BlockSpec((d, D_COLS), lambda i: (0, 0)),
        ],
        out_specs=[
            pl.BlockSpec((tm, ABC_COLS), lambda i: (i, 0)),
            pl.BlockSpec((tm, D_COLS), lambda i: (i, 0)),
        ],
        out_shape=[
            jax.ShapeDtypeStruct((n, ABC_COLS), BF16),
            jax.ShapeDtypeStruct((n, D_COLS), F32),
        ],
        compiler_params=_params("parallel"),
        name="in_proj",
    )(x2, gain, w_abc, w_d)


def _value_ext_t(v_t):
    row = lax.broadcasted_iota(jnp.int32, (VEXT_ROWS - HEAD_DIM, v_t.shape[1]), 0)
    return jnp.concatenate([v_t, jnp.where(row == 0, 1.0, 0.0)], axis=0).astype(BF16)


def _attend_exact(k_ref, q_ts, vx_refs):
    outs = []
    for q_t, vx in zip(q_ts, vx_refs):
        s_t = _dot(k_ref[...], q_t)
        p_t = jnp.exp2(s_t - jnp.max(s_t, axis=0, keepdims=True)).astype(BF16)
        o = _dot(vx[...], p_t)
        outs.append(o[:HEAD_DIM] / o[HEAD_DIM:HEAD_DIM + 1])
    return outs


def _attend_shifted(k_ref, q_ts, shifts, vx_refs):
    t = k_ref.shape[0]
    ck = min(t, KEY_CHUNK)
    steps = [(j, slice(c * ck, (c + 1) * ck)) for j in range(len(q_ts)) for c in range(t // ck)]
    tq = q_ts[0].shape[1]
    acc = [None] * len(q_ts)
    den = [None] * len(q_ts)
    s_prev = p_prev = None
    for i in range(len(steps) + 2):
        s_new = p_new = None
        if i < len(steps):
            j, keys = steps[i]
            s_new = _dot(k_ref[keys, :], q_ts[j])
        if 1 <= i <= len(steps):
            j = steps[i - 1][0]
            p = jnp.exp2(s_prev - shifts[j])
            part = jnp.sum(p.reshape(ck // SUM_ROWS, SUM_ROWS, tq), axis=0)
            den[j] = part if den[j] is None else den[j] + part
            p_new = p.astype(BF16)
        if i >= 2:
            j, keys = steps[i - 2]
            o = _dot(vx_refs[j][:HEAD_DIM, keys], p_prev)
            acc[j] = o if acc[j] is None else acc[j] + o
        s_prev, p_prev = s_new, p_new
    return [a / jnp.sum(d, axis=0, keepdims=True) for a, d in zip(acc, den)]


def _attend_t(k_ref, kn_ref, q_t, head_rows, head_keys, vx_refs, finish):
    c = k_ref.shape[1]
    tq = q_t.shape[1]
    qsq = q_t * q_t
    q_ts, shifts = [], []
    for rows, keys in zip(head_rows, head_keys):
        pieces = [jnp.zeros((keys.start, tq), F32), q_t[rows], jnp.zeros((c - keys.stop, tq), F32)]
        q_ts.append(jnp.concatenate([x for x in pieces if x.shape[0]], axis=0).astype(BF16))
        qn = jnp.sqrt(jnp.sum(qsq[rows], axis=0, keepdims=True))
        shifts.append(qn * kn_ref[0:1, keys.start:keys.start + 1])
    worst = jnp.max(jnp.concatenate(shifts, axis=0))

    @pl.when(2.0 * worst < SHIFT_LIMIT)
    def _():
        finish(_attend_shifted(k_ref, q_ts, shifts, vx_refs))

    @pl.when(jnp.logical_not(2.0 * worst < SHIFT_LIMIT))
    def _():
        finish(_attend_exact(k_ref, q_ts, vx_refs))


def _gqa_kernel(q_ref, k_ref, v_ref, cq_ref, sq_ref, ck_ref, sk_ref, gq_ref, gk_ref, o_ref,
                k_s, kn_s, vx_s):
    @pl.when(pl.program_id(1) == 0)
    def _():
        k = k_ref[0].astype(F32)
        k = k * lax.rsqrt(_head_sum(k * k, HEAD_DIM) * (1.0 / HEAD_DIM) + NORM_EPS) * gk_ref[...]
        k = _rope(k, ck_ref[...], sk_ref[...], HEAD_DIM // 4)
        k_s[...] = k.astype(BF16)
        kn_s[...] = jnp.sqrt(jnp.max(_head_sum(k * k, HEAD_DIM), axis=0, keepdims=True))
        v_t = v_ref[0].astype(F32).T
        for g in range(A_KV_HEADS):
            vx_s[g] = _value_ext_t(v_t[g * HEAD_DIM:(g + 1) * HEAD_DIM])

    q = q_ref[0].astype(F32)
    q = q * lax.rsqrt(_head_sum(q * q, HEAD_DIM) * (1.0 / HEAD_DIM) + NORM_EPS) * gq_ref[...]
    q_t = (_rope(q, cq_ref[...], sq_ref[...], HEAD_DIM // 4) * (HEAD_DIM ** -0.5 * LOG2E)).T
    group = A_HEADS // A_KV_HEADS
    head_rows = [slice(h * HEAD_DIM, (h + 1) * HEAD_DIM) for h in range(A_HEADS)]
    head_keys = [slice((h // group) * HEAD_DIM, (h // group + 1) * HEAD_DIM) for h in range(A_HEADS)]

    def finish(outs):
        o_ref[0] = jnp.concatenate(outs, axis=0).T.astype(BF16)

    _attend_t(k_s, kn_s, q_t, head_rows, head_keys, [vx_s.at[h // group] for h in range(A_HEADS)], finish)


def _mixer_gqa(zabc, tabs, gq, gk, tq):
    b, t, _ = zabc.shape
    cq, sq, ck, sk = tabs
    return pl.pallas_call(
        _gqa_kernel,
        grid=(b, t // tq),
        in_specs=[
            pl.BlockSpec((1, tq, 256), lambda bi, i: (bi, i, 0)),
            pl.BlockSpec((1, t, 128), lambda bi, i: (bi, 0, 2)),
            pl.BlockSpec((1, t, 128), lambda bi, i: (bi, 0, 3)),
            pl.BlockSpec((tq, 256), lambda bi, i: (i, 0)),
            pl.BlockSpec((tq, 256), lambda bi, i: (i, 0)),
            pl.BlockSpec((t, 128), lambda bi, i: (0, 0)),
            pl.BlockSpec((t, 128), lambda bi, i: (0, 0)),
            pl.BlockSpec((1, 256), lambda bi, i: (0, 0)),
            pl.BlockSpec((1, 128), lambda bi, i: (0, 0)),
        ],
        out_specs=pl.BlockSpec((1, tq, MIX_W), lambda bi, i: (bi, i, 0)),
        out_shape=jax.ShapeDtypeStruct((b, t, MIX_W), BF16),
        scratch_shapes=[
            pltpu.VMEM((t, A_KV_HEADS * HEAD_DIM), BF16),
            pltpu.VMEM((1, A_KV_HEADS * HEAD_DIM), F32),
            pltpu.VMEM((A_KV_HEADS, VEXT_ROWS, t), BF16),
        ],
        compiler_params=_params("parallel", "arbitrary"),
        name="mixer_gqa",
    )(zabc, zabc, zabc, cq, sq, ck, sk, gq, gk)


def _diff_kernel(lam_init, q_ref, k_ref, v_ref, cos_ref, sin_ref, cosk_ref, sink_ref, lam_ref, gain_ref,
                 o_ref, k_s, kn_s, vx_s):
    @pl.when(pl.program_id(1) == 0)
    def _():
        k = _rope(k_ref[0].astype(F32), cosk_ref[...], sink_ref[...], B_ROT // 2)
        k_s[...] = k.astype(BF16)
        kn_s[...] = jnp.sqrt(jnp.max(_head_sum(k * k, B_SUB), axis=0, keepdims=True))
        v_t = v_ref[0].astype(F32).T
        for h in range(B_HEADS):
            vx_s[h] = _value_ext_t(v_t[h * HEAD_DIM:(h + 1) * HEAD_DIM])

    lp = lam_ref[...]
    lam = (jnp.exp(jnp.sum(lp[0:1] * lp[1:2], axis=1, keepdims=True))
           - jnp.exp(jnp.sum(lp[2:3] * lp[3:4], axis=1, keepdims=True)) + lam_init)
    q = _rope(q_ref[0].astype(F32), cos_ref[...], sin_ref[...], B_ROT // 2) * (B_SUB ** -0.5 * LOG2E)
    q_t = q.T
    sub_rows = [slice(j * B_SUB, (j + 1) * B_SUB) for j in range(2 * B_HEADS)]

    def finish(parts):
        outs = []
        for h in range(B_HEADS):
            o = parts[2 * h] - lam * parts[2 * h + 1]
            o = o * lax.rsqrt(jnp.mean(o * o, axis=0, keepdims=True) + NORM_EPS) * gain_ref[...]
            outs.append(o * (1.0 - lam_init))
        o_ref[0] = jnp.concatenate(outs, axis=0).T.astype(BF16)

    _attend_t(k_s, kn_s, q_t, sub_rows, sub_rows, [vx_s.at[j // 2] for j in range(2 * B_HEADS)], finish)


def _mixer_diff(zabc, tabs, lam_params, gain, lam_init, tq):
    b, t, _ = zabc.shape
    cos, sin = tabs
    return pl.pallas_call(
        functools.partial(_diff_kernel, lam_init),
        grid=(b, t // tq),
        in_specs=[
            pl.BlockSpec((1, tq, 256), lambda bi, i: (bi, i, 2)),
            pl.BlockSpec((1, t, 256), lambda bi, i: (bi, 0, 3)),
            pl.BlockSpec((1, t, 256), lambda bi, i: (bi, 0, 4)),
            pl.BlockSpec((tq, 256), lambda bi, i: (i, 0)),
            pl.BlockSpec((tq, 256), lambda bi, i: (i, 0)),
            pl.BlockSpec((t, 256), lambda bi, i: (0, 0)),
            pl.BlockSpec((t, 256), lambda bi, i: (0, 0)),
            pl.BlockSpec((4, B_SUB), lambda bi, i: (0, 0)),
            pl.BlockSpec((HEAD_DIM, 1), lambda bi, i: (0, 0)),
        ],
        out_specs=pl.BlockSpec((1, tq, MIX_W), lambda bi, i: (bi, i, 0)),
        out_shape=jax.ShapeDtypeStruct((b, t, MIX_W), BF16),
        scratch_shapes=[
            pltpu.VMEM((t, 2 * B_HEADS * B_SUB), BF16),
            pltpu.VMEM((1, 2 * B_HEADS * B_SUB), F32),
            pltpu.VMEM((B_HEADS, VEXT_ROWS, t), BF16),
        ],
        compiler_params=_params("parallel", "arbitrary"),
        name="mixer_diff",
    )(zabc, zabc, zabc, cos, sin, cos, sin, lam_params, gain)


def _ret_log_gammas():
    lg = [math.log1p(-(2.0 ** (-5.0 - h))) for h in range(C_HEADS)]
    return lg, lg[::-1]


def _lane_consts(vals, shape):
    head = lax.broadcasted_iota(jnp.int32, shape, len(shape) - 1) // HEAD_DIM
    out = jnp.full(shape, vals[-1], F32)
    for h in range(len(vals) - 2, -1, -1):
        out = jnp.where(head == h, vals[h], out)
    return out


def _ret_kernel(chunk, q_ref, k_ref, v_ref, g_ref, cos_ref, sin_ref, gain_ref, o_ref,
                qr_s, kr_s, acc_s, sf_s, sb_s, dm_s):
    t = q_ref.shape[1]
    c = chunk
    n = t // c
    lgf, lgb = _ret_log_gammas()
    qr_s[...] = _rope(q_ref[0].astype(F32), cos_ref[...], sin_ref[...], HEAD_DIM // 2).astype(BF16)
    kr_s[...] = (_rope(k_ref[0].astype(F32), cos_ref[...], sin_ref[...], HEAD_DIM // 2)
                 * (HEAD_DIM ** -0.5)).astype(BF16)

    ti = lax.broadcasted_iota(jnp.int32, (c, c), 0)
    si = lax.broadcasted_iota(jnp.int32, (c, c), 1)
    dist = (ti - si).astype(F32)
    lane_head = lax.broadcasted_iota(jnp.int32, (1, MIX_W), 1) // HEAD_DIM
    row = lax.broadcasted_iota(jnp.int32, (c, MIX_W), 0).astype(F32)
    lgf_l = _lane_consts(lgf, (c, MIX_W))
    lgb_l = _lane_consts(lgb, (c, MIX_W))
    qdec_f = jnp.exp(lgf_l * (row + 1.0))
    kdec_f = jnp.exp(lgf_l * (c - 1.0 - row))
    qdec_b = jnp.exp(lgb_l * (c - row))
    kdec_b = jnp.exp(lgb_l * row)
    r2 = lax.broadcasted_iota(jnp.int32, (MIX_W, MIX_W), 0) // HEAD_DIM
    c2 = lax.broadcasted_iota(jnp.int32, (MIX_W, MIX_W), 1) // HEAD_DIM
    same_head = r2 == c2
    gf_blk = jnp.where(same_head, jnp.exp(_lane_consts(lgf, (MIX_W, MIX_W)) * c), 0.0)
    gb_blk = jnp.where(same_head, jnp.exp(_lane_consts(lgb, (MIX_W, MIX_W)) * c), 0.0)

    head_mask = [(lane_head == h).astype(BF16) for h in range(C_HEADS)]
    for h in range(C_HEADS):
        dm_s[h] = jnp.where(dist > 0, jnp.exp(lgf[h] * dist),
                            jnp.where(dist < 0, jnp.exp(-lgb[h] * dist), 2.0))

    def intra(qc, kc, vc):
        ps = [(_dot_nt(qc * head_mask[h], kc) * dm_s[h]).astype(BF16) for h in range(C_HEADS)]
        v_stack = jnp.concatenate([vc * head_mask[h] for h in range(C_HEADS)], axis=0)
        return _dot(jnp.concatenate(ps, axis=1), v_stack)

    sf_s[...] = jnp.zeros_like(sf_s)
    sb_s[...] = jnp.zeros_like(sb_s)

    def fwd(i, carry):
        rows = pl.ds(pl.multiple_of(i * c, c), c)
        qc = qr_s[rows, :]
        kc = kr_s[rows, :]
        vc = v_ref[0, rows, :]
        o = intra(qc, kc, vc) + _dot((qc.astype(F32) * qdec_f).astype(BF16), sf_s[...].astype(BF16))
        acc_s[rows, :] = o
        kd = (kc.astype(F32) * kdec_f).T.astype(BF16)
        sf_s[...] = gf_blk * sf_s[...] + jnp.where(same_head, _dot(kd, vc), 0.0)
        return carry

    lax.fori_loop(0, n, fwd, 0)

    def bwd(i, carry):
        rows = pl.ds(pl.multiple_of((n - 1 - i) * c, c), c)
        qc = qr_s[rows, :].astype(F32)
        kc = kr_s[rows, :].astype(F32)
        vc = v_ref[0, rows, :]
        acc_s[rows, :] = acc_s[rows, :] + _dot((qc * qdec_b).astype(BF16), sb_s[...].astype(BF16))
        kd = (kc * kdec_b).T.astype(BF16)
        sb_s[...] = gb_blk * sb_s[...] + jnp.where(same_head, _dot(kd, vc), 0.0)
        return carry

    lax.fori_loop(0, n, bwd, 0)

    o = acc_s[...]
    o = o * lax.rsqrt(_head_sum(o * o, HEAD_DIM) * (1.0 / HEAD_DIM) + NORM_EPS) * gain_ref[...]
    g = g_ref[0].astype(F32)
    o_ref[0] = (o * (g * _sigmoid(g))).astype(BF16)


def _mixer_ret(zabc, tabs, gain, chunk):
    b, t, _ = zabc.shape
    cos, sin = tabs
    blk = lambda j: pl.BlockSpec((1, t, 256), lambda bi: (bi, 0, j))
    return pl.pallas_call(
        functools.partial(_ret_kernel, chunk),
        grid=(b,),
        in_specs=[blk(5), blk(6), blk(7), blk(8),
                  pl.BlockSpec((t, 256), lambda bi: (0, 0)),
                  pl.BlockSpec((t, 256), lambda bi: (0, 0)),
                  pl.BlockSpec((1, 256), lambda bi: (0, 0))],
        out_specs=pl.BlockSpec((1, t, MIX_W), lambda bi: (bi, 0, 0)),
        out_shape=jax.ShapeDtypeStruct((b, t, MIX_W), BF16),
        scratch_shapes=[
            pltpu.VMEM((t, MIX_W), BF16),
            pltpu.VMEM((t, MIX_W), BF16),
            pltpu.VMEM((t, MIX_W), F32),
            pltpu.VMEM((MIX_W, MIX_W), F32),
            pltpu.VMEM((MIX_W, MIX_W), F32),
            pltpu.VMEM((C_HEADS, chunk, chunk), F32),
        ],
        compiler_params=_params("parallel"),
        name="mixer_ret",
    )(zabc, zabc, zabc, zabc, cos, sin, gain)


def _wkv_prep_kernel(z_ref, zp_ref, zn_ref, mup_ref, mun_ref, w0_ref, wupf_ref, wupb_ref, a0_ref, aup_ref,
                     gup1_ref, gup2_ref, kk_ref, ka_ref, rk_ref,
                     r_o, k_o, v_o, kk_o, b_o, ldf_o, ldb_o, g_o, bonus_o):
    i = pl.program_id(1)
    last = pl.num_programs(1) - 1
    z = z_ref[0]
    tm = z.shape[0]
    row = lax.broadcasted_iota(jnp.int32, z.shape, 0)
    prev_row = zp_ref[0, 0, 7:8, :] * jnp.where(i > 0, 1.0, 0.0)
    next_row = zn_ref[0, 0, 0:1, :] * jnp.where(i < last, 1.0, 0.0)
    z_prev = jnp.where(row == 0, prev_row, pltpu.roll(z, 1, 0))
    z_next = jnp.where(row == tm - 1, next_row, pltpu.roll(z, tm - 1, 0))
    u = z + mup_ref[...] * (z_prev - z) + mun_ref[...] * (z_next - z)
    r = u[:, 0:256]
    k = u[:, 256:512]
    v = u[:, 512:768]
    wd = jnp.tanh(u[:, 768:896]).astype(BF16)
    ag = u[:, 896:1024]
    g2 = u[:, 1024:1088]

    def log_decay(w0, wup):
        x = w0 + _dot(wd, wup)
        w = -(jnp.maximum(-x, 0.0) + jnp.log(1.0 + jnp.exp(-jnp.abs(x)))) - 0.5
        return -jnp.exp(w)

    ldf_o[0] = log_decay(w0_ref[0:1, :], wupf_ref[...])
    ldb_o[0] = log_decay(w0_ref[1:2, :], wupb_ref[...])
    a = _sigmoid(a0_ref[...] + _dot(ag.astype(BF16), aup_ref[...]))
    g_o[0] = (_dot(_sigmoid(ag).astype(BF16), gup1_ref[...])
              + _dot(_sigmoid(g2).astype(BF16), gup2_ref[...]))
    kk = k * kk_ref[...]
    kk = kk / jnp.maximum(jnp.sqrt(_head_sum(kk * kk, HEAD_DIM)), 1e-12)
    kh = k * (1.0 + (a - 1.0) * ka_ref[...])
    r_o[0] = r
    k_o[0] = kh
    v_o[0] = v
    kk_o[0] = kk
    b_o[0] = kk * a
    bonus_o[0] = _head_sum(r * kh * rk_ref[...], HEAD_DIM) * v


def _wkv_prep(zd, p, tm):
    b, t, _ = zd.shape
    zd8 = zd.reshape(b, t // 8, 8, D_COLS)
    r8 = tm // 8
    nb8 = t // 8
    row = lambda w: pl.BlockSpec((1, w), lambda bi, i: (0, 0))
    full = lambda a: pl.BlockSpec(a.shape, lambda bi, i: (0,) * a.ndim)
    out_spec = pl.BlockSpec((1, tm, MIX_W), lambda bi, i: (bi, i, 0))
    out_shape = jax.ShapeDtypeStruct((b, t, MIX_W), F32)
    return pl.pallas_call(
        _wkv_prep_kernel,
        grid=(b, t // tm),
        in_specs=[
            pl.BlockSpec((1, tm, D_COLS), lambda bi, i: (bi, i, 0)),
            pl.BlockSpec((1, 1, 8, D_COLS), lambda bi, i: (bi, jnp.maximum(i * r8 - 1, 0), 0, 0)),
            pl.BlockSpec((1, 1, 8, D_COLS), lambda bi, i: (bi, jnp.minimum((i + 1) * r8, nb8 - 1), 0, 0)),
            row(D_COLS), row(D_COLS),
            full(p["w0"]), full(p["wup_f"]), full(p["wup_b"]), row(256), full(p["aup"]),
            full(p["gup1"]), full(p["gup2"]), row(256), row(256), row(256),
        ],
        out_specs=[out_spec] * 9,
        out_shape=[out_shape] * 9,
        compiler_params=_params("parallel", "parallel"),
        name="wkv_prep",
    )(zd, zd8, zd8, p["mu_prev"], p["mu_next"], p["w0"], p["wup_f"], p["wup_b"], p["a0"], p["aup"],
      p["gup1"], p["gup2"], p["k_k"], p["k_a"], p["r_k"])


def _wkv_chunk_consts(chunk):
    c = chunk
    sc = D_HEADS * c
    ti = lax.broadcasted_iota(jnp.int32, (c, c), 0)
    si = lax.broadcasted_iota(jnp.int32, (c, c), 1)
    srow = lax.broadcasted_iota(jnp.int32, (sc, MIX_W), 0) // c
    slane = lax.broadcasted_iota(jnp.int32, (sc, MIX_W), 1) // HEAD_DIM
    gt = lax.broadcasted_iota(jnp.int32, (2 * sc, 2 * sc), 0)
    gs = lax.broadcasted_iota(jnp.int32, (2 * sc, 2 * sc), 1)
    t_in, s_in = gt % sc, gs % sc
    same = (t_in // c) == (s_in // c)
    upper = gt < sc
    st = lax.broadcasted_iota(jnp.int32, (sc, sc), 0)
    ss = lax.broadcasted_iota(jnp.int32, (sc, sc), 1)
    r2 = lax.broadcasted_iota(jnp.int32, (MIX_W, MIX_W), 0)
    c2 = lax.broadcasted_iota(jnp.int32, (MIX_W, MIX_W), 1)
    return {
        "tri": {False: jnp.where(si <= ti, 1.0, 0.0).astype(BF16), True: jnp.where(si >= ti, 1.0, 0.0).astype(BF16)},
        "hmask": srow == slane,
        "gmask": {False: same & ((s_in < t_in) | ((s_in == t_in) & ~upper)),
                  True: same & ((s_in > t_in) | ((s_in == t_in) & ~upper))},
        "eye": jnp.where(st == ss, 1.0, 0.0),
        "diag": r2 == c2,
    }


def _wkv_phase1(chunk, consts, probs):
    c = chunk
    sc = D_HEADS * c
    hmask = consts["hmask"]

    def stack(a):
        return jnp.where(hmask, jnp.concatenate([a] * D_HEADS, axis=0), 0.0).astype(BF16)

    pre = []
    for rev, r, kh, v, kk, bb, lw in probs:
        tri = consts["tri"][rev]
        l1, l2, l3 = _split3(lw)
        cl = _dot(tri, l1) + _dot(tri, l2) + _dot(tri, l3)
        tot = jnp.sum(lw, axis=0, keepdims=True)
        pre.append((cl, tot))
    yield
    ops = []
    for (rev, r, kh, v, kk, bb, lw), (cl, tot) in zip(probs, pre):
        w_inv = jnp.exp(-cl)
        w_end = jnp.exp(tot - cl)
        lhs = jnp.concatenate([stack(kk * jnp.exp(cl - lw)), stack(r * jnp.exp(cl))], axis=0)
        rhs = jnp.concatenate([stack(bb * w_inv), stack(kh * w_inv)], axis=0)
        kb = jnp.concatenate([stack(kh * w_end), stack(bb * w_end)], axis=0)
        wc = jnp.sum(jnp.where(consts["diag"], jnp.exp(tot), 0.0), axis=1, keepdims=True)
        wc = jnp.broadcast_to(wc, (MIX_W, MIX_W))
        ops.append((lhs, rhs, stack(v), kb, wc))
    yield
    gram = [jnp.where(consts["gmask"][p[0]], _dot_nt(o[0], o[1]), 0.0) for p, o in zip(probs, ops)]
    l_ab = [g[:sc, :sc] for g in gram]
    l_ak = [g[:sc, sc:].astype(BF16) for g in gram]
    m_r = [g[sc:, :].astype(BF16) for g in gram]
    yield

    steps = int(math.log2(c))
    pw = [(-l).astype(BF16) for l in l_ab]
    inv = [consts["eye"] - l for l in l_ab]
    w = [_dot(a, o[2]).astype(BF16) for a, o in zip(l_ak, ops)]
    for k in range(1, steps):
        if k == 1:
            pw = [_dot(p, p).astype(BF16) for p in pw]
            yield
        if k < steps - 1:
            nxt = [_dot(p, jnp.concatenate([p, i.astype(BF16)], axis=1)) for p, i in zip(pw, inv)]
            inv = [i + n[:, sc:] for i, n in zip(inv, nxt)]
            pw = [n[:, :sc].astype(BF16) for n in nxt]
        else:
            inv = [i + _dot(p, i.astype(BF16)) for p, i in zip(pw, inv)]
        yield
    inv = [i.astype(BF16) for i in inv]

    gu = [_dot(i, jnp.concatenate([o[0][:sc], wv], axis=1)) for i, o, wv in zip(inv, ops, w)]
    yield
    kb_t = [o[3].astype(F32).T.astype(BF16) for o in ops]
    mgu = [_dot(m[:, :sc], x.astype(BF16)) for m, x in zip(m_r, gu)]
    mv = [_dot(m[:, sc:], o[2]) for m, o in zip(m_r, ops)]
    yield
    return [{
        "qg": jnp.concatenate([(o[0][sc:].astype(F32) - mg[:, :MIX_W]).astype(BF16),
                               g[:, :MIX_W].astype(BF16)], axis=0),
        "y_hat": a - mg[:, MIX_W:],
        "u_hat": g[:, MIX_W:],
        "kb_t": kt,
        "v": o[2],
        "wc": o[4],
    } for o, g, mg, a, kt in zip(ops, gu, mgu, mv, kb_t)]


def _wkv_scan_kernel(chunk, group, rf, kf, vf, kkf, bf, lf, rb, kb, vb, kkb, bb, lb, yf_o, yb_o,
                     xf_s, xb_s, qg_s, yh_s, uh_s, kbt_s, v_s, wc_s):
    @pl.when(pl.program_id(1) == 0)
    def _():
        xf_s[...] = jnp.zeros_like(xf_s)
        xb_s[...] = jnp.zeros_like(xb_s)

    tm = rf.shape[1]
    n = tm // chunk
    sc = D_HEADS * chunk
    consts = _wkv_chunk_consts(chunk)
    sol_refs = {"qg": qg_s, "y_hat": yh_s, "u_hat": uh_s, "kb_t": kbt_s, "v": v_s, "wc": wc_s}

    def group_rows(g):
        rows = []
        for u in range(group):
            rows.append(pl.ds(pl.multiple_of((g * group + u) * chunk, chunk), chunk))
            rows.append(pl.ds(pl.multiple_of((n - 1 - g * group - u) * chunk, chunk), chunk))
        return rows

    def problems(g):
        probs = []
        for slot, rows in enumerate(group_rows(g)):
            if slot % 2:
                probs.append((True, rb[0, rows, :], kb[0, rows, :], vb[0, rows, :], kkb[0, rows, :],
                              bb[0, rows, :], lb[0, rows, :]))
            else:
                probs.append((False, rf[0, rows, :], kf[0, rows, :], vf[0, rows, :], kkf[0, rows, :],
                              bf[0, rows, :], lf[0, rows, :]))
        return probs

    def advance(slot, rows):
        x_s, y_o = (xb_s, yb_o) if slot % 2 else (xf_s, yf_o)
        x = x_s[...]
        res = _dot(qg_s[slot], x.astype(BF16))
        ys = res[:sc] + yh_s[slot]
        y = ys[0:chunk]
        for h in range(1, D_HEADS):
            y = y + ys[h * chunk:(h + 1) * chunk]
        y_o[0, rows, :] = y
        u = -(res[sc:] + uh_s[slot])
        x_s[...] = x * wc_s[slot] + _dot(kbt_s[slot], jnp.concatenate([v_s[slot], u.astype(BF16)], axis=0))

    def solve(g, pending):
        gen = _wkv_phase1(chunk, consts, problems(g))
        pending = list(pending)
        while True:
            try:
                next(gen)
            except StopIteration as done:
                sols = done.value
                break
            for step in pending[:2]:
                step()
            pending = pending[2:]
        for step in pending:
            step()
        for slot, sol in enumerate(sols):
            for name, ref in sol_refs.items():
                ref[slot] = sol[name]

    def steps_for(g):
        return [functools.partial(advance, slot, r) for slot, r in enumerate(group_rows(g))]

    n_groups = n // group
    solve(0, [])

    for g in range(1, n_groups):
        solve(g, steps_for(g - 1))
    for step in steps_for(n_groups - 1):
        step()


def _wkv_scan(r, kh, v, kk, bb, ldf, ldb, tm, chunk, group):
    b, t, _ = r.shape
    nt = t // tm
    sc = D_HEADS * chunk
    slots = 2 * group
    fspec = pl.BlockSpec((1, tm, MIX_W), lambda bi, i: (bi, i, 0))
    bspec = pl.BlockSpec((1, tm, MIX_W), lambda bi, i: (bi, nt - 1 - i, 0))
    shape = jax.ShapeDtypeStruct((b, t, MIX_W), F32)
    return pl.pallas_call(
        functools.partial(_wkv_scan_kernel, chunk, group),
        grid=(b, nt),
        in_specs=[fspec] * 6 + [bspec] * 6,
        out_specs=[fspec, bspec],
        out_shape=[shape, shape],
        scratch_shapes=[
            pltpu.VMEM((MIX_W, MIX_W), F32), pltpu.VMEM((MIX_W, MIX_W), F32),
            pltpu.VMEM((slots, 2 * sc, MIX_W), BF16),
            pltpu.VMEM((slots, sc, MIX_W), F32),
            pltpu.VMEM((slots, sc, MIX_W), F32),
            pltpu.VMEM((slots, MIX_W, 2 * sc), BF16),
            pltpu.VMEM((slots, sc, MIX_W), BF16),
            pltpu.VMEM((slots, MIX_W, MIX_W), F32),
        ],
        compiler_params=_params("parallel", "arbitrary"),
        name="wkv_scan",
    )(r, kh, v, kk, bb, ldf, r, kh, v, kk, bb, ldb)


def _wkv_out(yf, yb, bonus, g, gn_w, gn_b):
    y = yf + yb
    mean = _head_sum(y, HEAD_DIM) * (1.0 / HEAD_DIM)
    yc = y - mean
    var = _head_sum(yc * yc, HEAD_DIM) * (1.0 / HEAD_DIM)
    yn = yc * lax.rsqrt(var + WKV_GN_EPS) * gn_w + gn_b
    return ((yn + bonus) * g).astype(BF16)


def _post_kernel(x_ref, oa_ref, ob_ref, oc_ref, yf_ref, yb_ref, bonus_ref, g_ref, gnw_ref, gnb_ref,
                 wo_ref, gpost_ref, gpre_ref, wg_ref, wu_ref, wd_ref, gfpost_ref, y_ref):
    tm = x_ref.shape[0]
    groups = [slice(s, s + tm // ROW_GROUPS) for s in range(0, tm, tm // ROW_GROUPS)]
    o_d = [_wkv_out(yf_ref[r], yb_ref[r], bonus_ref[r], g_ref[r], gnw_ref[...], gnb_ref[...]) for r in groups]
    mix = [_dot(oa_ref[r], wo_ref[0]) + _dot(ob_ref[r], wo_ref[1]) + _dot(oc_ref[r], wo_ref[2])
           + _dot(od, wo_ref[3]) for r, od in zip(groups, o_d)]
    x = [x_ref[r] + _rms_rows(m, gpost_ref[...]) for r, m in zip(groups, mix)]
    h = [_rms_rows(xg, gpre_ref[...]).astype(BF16) for xg in x]
    gate = [_dot(hg, wg_ref[...]) for hg in h]
    up = [_dot(hg, wu_ref[...]) for hg in h]
    act = [(g * _sigmoid(g) * u).astype(BF16) for g, u in zip(gate, up)]
    f = [_dot(a, wd_ref[...]) for a in act]
    for r, xg, fg in zip(groups, x, f):
        y_ref[r] = xg + _rms_rows(fg, gfpost_ref[...])


def _post(x2, oa, ob, oc, wkv_parts, gn_w, gn_b, w_out4, g_post, g_pre, wg, wu, wd, gf_post, tm):
    n, d = x2.shape
    dff = wg.shape[1]
    tok = lambda w: pl.BlockSpec((tm, w), lambda i: (i, 0))
    row = pl.BlockSpec((1, d), lambda i: (0, 0))
    mrow = pl.BlockSpec((1, MIX_W), lambda i: (0, 0))
    return pl.pallas_call(
        _post_kernel,
        grid=(n // tm,),
        in_specs=[tok(d)] + [tok(MIX_W)] * 7 + [mrow, mrow,
                  pl.BlockSpec((4, MIX_W, d), lambda i: (0, 0, 0)), row, row,
                  pl.BlockSpec((d, dff), lambda i: (0, 0)),
                  pl.BlockSpec((d, dff), lambda i: (0, 0)),
                  pl.BlockSpec((dff, d), lambda i: (0, 0)), row],
        out_specs=tok(d),
        out_shape=jax.ShapeDtypeStruct((n, d), F32),
        compiler_params=_params("parallel"),
        name="out_proj_ffn",
    )(x2, oa, ob, oc, *wkv_parts, gn_w, gn_b, w_out4, g_post, g_pre, wg, wu, wd, gf_post)


def _angles(pos, rot_dim, theta):
    inv = theta ** (-jnp.arange(0, rot_dim, 2, dtype=F32) / rot_dim)
    return pos.astype(F32)[:, None] * inv[None, :]


def _rope_tables(t):
    rows = t // GRID_W
    row_idx = jnp.repeat(jnp.arange(rows), GRID_W)
    col_idx = jnp.tile(jnp.arange(GRID_W), rows)
    pos = jnp.arange(t)
    ar = _angles(row_idx, HEAD_DIM // 2, A_THETA)
    ac = _angles(col_idx, HEAD_DIM // 2, A_THETA)
    cos_a = jnp.concatenate([jnp.cos(ar), jnp.cos(ar), jnp.cos(ac), jnp.cos(ac)], axis=1)
    sin_a = jnp.concatenate([-jnp.sin(ar), jnp.sin(ar), -jnp.sin(ac), jnp.sin(ac)], axis=1)
    tabs_a = (jnp.tile(cos_a, (1, 4)), jnp.tile(sin_a, (1, 4)), jnp.tile(cos_a, (1, 2)), jnp.tile(sin_a, (1, 2)))
    ab = _angles(pos, B_ROT, B_THETA)
    pad1 = jnp.ones((t, B_SUB - B_ROT), F32)
    pad0 = jnp.zeros((t, B_SUB - B_ROT), F32)
    cos_b = jnp.concatenate([jnp.cos(ab), jnp.cos(ab), pad1], axis=1)
    sin_b = jnp.concatenate([-jnp.sin(ab), jnp.sin(ab), pad0], axis=1)
    tabs_b = (jnp.tile(cos_b, (1, 8)), jnp.tile(sin_b, (1, 8)))
    ang_c = _angles(pos, HEAD_DIM, C_THETA)
    cos_c = jnp.concatenate([jnp.cos(ang_c), jnp.cos(ang_c)], axis=1)
    sin_c = jnp.concatenate([-jnp.sin(ang_c), jnp.sin(ang_c)], axis=1)
    tabs_c = (jnp.tile(cos_c, (1, 4)), jnp.tile(sin_c, (1, 4)))
    return tabs_a, tabs_b, tabs_c


def _tile(n, pref):
    while n % pref:
        pref //= 2
    return pref


def _layer_weights(l, w):
    d_model = w["w_in"].shape[1]
    row = lambda a: a.reshape(1, -1).astype(F32)
    zeros = lambda r: jnp.zeros((r, MIX_W), F32)
    w_lora = w["d_w_up"].shape[2]
    return {
        "g_mix_pre": row(w["norm_mix_pre"][l]), "g_mix_post": row(w["norm_mix_post"][l]),
        "g_ffn_pre": row(w["norm_ffn_pre"][l]), "g_ffn_post": row(w["norm_ffn_post"][l]),
        "w_abc": w["w_in"][l][:, :ABC_COLS].astype(BF16), "w_d": w["w_in"][l][:, ABC_COLS:].astype(BF16),
        "w_out4": w["w_out"][l].reshape(4, MIX_W, d_model).astype(BF16),
        "a_gq": jnp.tile(row(w["a_q_gain"][l]), (1, A_HEADS)),
        "a_gk": jnp.tile(row(w["a_k_gain"][l]), (1, A_KV_HEADS)),
        "b_lambda": w["b_lambda"][l].astype(F32), "b_gain": w["b_subln_gain"][l].reshape(-1, 1).astype(F32),
        "c_gain": row(w["c_gn_gain"][l]),
        "wkv": {
            "mu_prev": row(w["d_mu_prev"][l]), "mu_next": row(w["d_mu_next"][l]),
            "w0": w["d_w0"][l].astype(F32),
            "wup_f": jnp.concatenate([w["d_w_up"][l, 0], zeros(w_lora)], axis=0).astype(BF16),
            "wup_b": jnp.concatenate([zeros(w_lora), w["d_w_up"][l, 1]], axis=0).astype(BF16),
            "a0": row(w["d_a0"][l]),
            "aup": jnp.concatenate([w["d_a_up"][l], zeros(64)], axis=0).astype(BF16),
            "gup1": jnp.concatenate([zeros(64), w["d_g_up"][l][:64]], axis=0).astype(BF16),
            "gup2": w["d_g_up"][l][64:].astype(BF16),
            "k_k": row(w["d_k_k"][l]), "k_a": row(w["d_k_a"][l]), "r_k": row(w["d_r_k"][l]),
        },
        "gn_w": row(w["d_gn_w"][l]), "gn_b": row(w["d_gn_b"][l]),
        "wg": w["ffn_w_gate"][l].astype(BF16), "wu": w["ffn_w_up"][l].astype(BF16),
        "wd": w["ffn_w_down"][l].astype(BF16),
    }


def _trunk(x, layers):
    b, t, d = x.shape
    n = b * t
    tabs_a, tabs_b, tabs_c = _rope_tables(t)
    tm_proj = _tile(n, 512)
    tm_post = _tile(n, 512)
    tq = _tile(t, 256)
    tm_wkv = _tile(t, 256)
    tm_scan = _tile(t, 1024)
    chunk = 32
    x2 = x.reshape(n, d)
    for l, p in enumerate(layers):
        lam_init = 0.8 - 0.6 * math.exp(-0.3 * l)
        zabc, zd = _in_proj(x2, p["g_mix_pre"], p["w_abc"], p["w_d"], tm_proj)
        zabc = zabc.reshape(b, t, ABC_COLS)
        zd = zd.reshape(b, t, D_COLS)
        o_a = _mixer_gqa(zabc, tabs_a, p["a_gq"], p["a_gk"], tq)
        o_b = _mixer_diff(zabc, tabs_b, p["b_lambda"], p["b_gain"], lam_init, tq)
        o_c = _mixer_ret(zabc, tabs_c, p["c_gain"], _tile(t, 256))
        r, kh, v, kk, bb, ldf, ldb, g, bonus = _wkv_prep(zd, p["wkv"], tm_wkv)
        yf, yb = _wkv_scan(r, kh, v, kk, bb, ldf, ldb, tm_scan, chunk, min(4, tm_scan // chunk))
        wkv_parts = [a.reshape(n, MIX_W) for a in (yf, yb, bonus, g)]
        x2 = _post(x2, o_a.reshape(n, MIX_W), o_b.reshape(n, MIX_W), o_c.reshape(n, MIX_W), wkv_parts,
                   p["gn_w"], p["gn_b"], p["w_out4"], p["g_mix_post"], p["g_ffn_pre"], p["wg"], p["wu"], p["wd"],
                   p["g_ffn_post"], tm_post)
    return x2.reshape(b, t, d)


def kernel(x_prompt, x_sample, norm_mix_pre, norm_mix_post, norm_ffn_pre, norm_ffn_post, w_in, w_out,
           a_q_gain, a_k_gain, b_lambda, b_subln_gain, c_gn_gain, d_mu_prev, d_mu_next, d_w0, d_w_up,
           d_a0, d_a_up, d_g_up, d_k_k, d_k_a, d_r_k, d_gn_w, d_gn_b, ffn_w_gate, ffn_w_up, ffn_w_down):
    w = {
        "norm_mix_pre": norm_mix_pre, "norm_mix_post": norm_mix_post,
        "norm_ffn_pre": norm_ffn_pre, "norm_ffn_post": norm_ffn_post,
        "w_in": w_in, "w_out": w_out, "a_q_gain": a_q_gain, "a_k_gain": a_k_gain,
        "b_lambda": b_lambda, "b_subln_gain": b_subln_gain, "c_gn_gain": c_gn_gain,
        "d_mu_prev": d_mu_prev, "d_mu_next": d_mu_next, "d_w0": d_w0, "d_w_up": d_w_up,
        "d_a0": d_a0, "d_a_up": d_a_up, "d_g_up": d_g_up, "d_k_k": d_k_k, "d_k_a": d_k_a,
        "d_r_k": d_r_k, "d_gn_w": d_gn_w, "d_gn_b": d_gn_b,
        "ffn_w_gate": ffn_w_gate, "ffn_w_up": ffn_w_up, "ffn_w_down": ffn_w_down,
    }
    layers = [_layer_weights(l, w) for l in range(w_in.shape[0])]
    return (_trunk(x_prompt, layers), _trunk(x_sample, layers))
```

```python
import functools
import math

import jax
import jax.numpy as jnp
from jax import lax
from jax.experimental import pallas as pl
from jax.experimental.pallas import tpu as pltpu

F32 = jnp.float32
BF16 = jnp.bfloat16

HEAD_DIM = 64
GRID_W = 64
NORM_EPS = 1e-6
A_HEADS, A_KV_HEADS, A_THETA = 4, 2, 10000.0
B_HEADS, B_SUB, B_ROT, B_THETA = 4, 32, 8, 500000.0
C_HEADS, C_THETA = 4, 10000.0
D_HEADS = 4
WKV_GN_EPS = 64e-5
MIX_W = 256
ABC_COLS = 2304
A_Q, A_K, A_V = slice(0, 256), slice(256, 384), slice(384, 512)
B_Q, B_K, B_V = slice(512, 768), slice(768, 1024), slice(1024, 1280)
C_Q, C_K, C_VG = slice(1280, 1536), slice(1536, 1792), slice(1792, 2304)
D_COLS = 1088
BF16_SUBLANES = 16
VEXT_ROWS = HEAD_DIM + BF16_SUBLANES
LOG2E = math.log2(math.e)
KEY_CHUNK = 512
ROW_GROUPS = 2
SHIFT_LIMIT = 100.0
VMEM_LIMIT = 56 * 1024 * 1024


def _params(*sem):
    return pltpu.CompilerParams(dimension_semantics=sem, vmem_limit_bytes=VMEM_LIMIT)


def _rms_rows(x, gain):
    return x * lax.rsqrt(jnp.mean(x * x, axis=-1, keepdims=True) + NORM_EPS) * gain


def _split2(x):
    hi = x.astype(BF16)
    lo = (x - hi.astype(F32)).astype(BF16)
    return hi, lo


def _split3(x):
    h1 = x.astype(BF16)
    r1 = x - h1.astype(F32)
    h2 = r1.astype(BF16)
    h3 = (r1 - h2.astype(F32)).astype(BF16)
    return h1, h2, h3


def _head_sum(x, seg):
    w = x.shape[-1]
    r = lax.broadcasted_iota(jnp.int32, (w, w), 0) // seg
    c = lax.broadcasted_iota(jnp.int32, (w, w), 1) // seg
    bd = jnp.where(r == c, 1.0, 0.0).astype(BF16)
    hi, lo = _split2(x)
    return (jnp.dot(hi, bd, preferred_element_type=F32)
            + jnp.dot(lo, bd, preferred_element_type=F32))


def _rope(x, cos, sin, half):
    w = x.shape[-1]
    lane = lax.broadcasted_iota(jnp.int32, x.shape, 1)
    nxt = pltpu.roll(x, w - half, 1)
    prv = pltpu.roll(x, half, 1)
    return x * cos + jnp.where((lane % (2 * half)) < half, nxt, prv) * sin


def _sigmoid(x):
    return 0.5 * jnp.tanh(0.5 * x) + 0.5


def _dot(a, b):
    return jnp.dot(a, b, preferred_element_type=F32)


def _dot_nt(a, b):
    return lax.dot_general(a, b, (((1,), (1,)), ((), ())), preferred_element_type=F32)


def _store_qk_prepared(z, r, tabs, gain_a, out_ref):
    (cos_a, sin_a), (cos_b, sin_b), (cos_c, sin_c) = tabs

    def head_norm(x, gain):
        return x * lax.rsqrt(_head_sum(x * x, HEAD_DIM) * (1.0 / HEAD_DIM) + NORM_EPS) * gain

    a_qk = head_norm(z[:, A_Q.start:A_K.stop], gain_a)
    kw = A_K.stop - A_K.start
    pieces = {
        A_Q: _rope(a_qk[:, :A_Q.stop], cos_a, sin_a, HEAD_DIM // 4) * (HEAD_DIM ** -0.5 * LOG2E),
        A_K: _rope(a_qk[:, A_Q.stop:], cos_a[:, :kw], sin_a[:, :kw], HEAD_DIM // 4),
        B_Q: _rope(z[:, B_Q], cos_b, sin_b, B_ROT // 2) * (B_SUB ** -0.5 * LOG2E),
        B_K: _rope(z[:, B_K], cos_b, sin_b, B_ROT // 2),
        C_Q: _rope(z[:, C_Q], cos_c, sin_c, HEAD_DIM // 2),
        C_K: _rope(z[:, C_K], cos_c, sin_c, HEAD_DIM // 2) * (HEAD_DIM ** -0.5),
    }
    for cols in (A_V, B_V, C_VG):
        pieces[cols] = z[:, cols]
    for cols, val in pieces.items():
        out_ref[r, cols] = val.astype(BF16)


def _in_proj_kernel(x_ref, g_ref, wabc_ref, wd_ref, ca_ref, sa_ref, cb_ref, sb_ref, cc_ref, sc_ref, ga_ref,
                    zabc_ref, zd_ref):
    tm = x_ref.shape[0]
    groups = [slice(s, s + tm // ROW_GROUPS) for s in range(0, tm, tm // ROW_GROUPS)]
    h = [_rms_rows(x_ref[r], g_ref[...]).astype(BF16) for r in groups]
    z = [_dot(hg, wabc_ref[...]) for hg in h]
    for r, hg in zip(groups, h):
        zd_ref[r] = _dot(hg, wd_ref[...])
    for r, zg in zip(groups, z):
        tabs = ((ca_ref[r], sa_ref[r]), (cb_ref[r], sb_ref[r]), (cc_ref[r], sc_ref[r]))
        _store_qk_prepared(zg, r, tabs, ga_ref[...], zabc_ref)


def _in_proj(x2, gain, w_abc, w_d, tabs, gain_a, t, tm):
    n, d = x2.shape
    per_seq = t // tm
    tab = pl.BlockSpec((tm, MIX_W), lambda i: (i % per_seq, 0))
    return pl.pallas_call(
        _in_proj_kernel,
        grid=(n // tm,),
        in_specs=[
            pl.BlockSpec((tm, d), lambda i: (i, 0)),
            pl.BlockSpec((1, d), lambda i: (0, 0)),
            pl.BlockSpec((d, ABC_COLS), lambda i: (0, 0)),
            pl.BlockSpec((d, D_COLS), lambda i: (0, 0)),
            tab, tab, tab, tab, tab, tab,
            pl.BlockSpec((1, A_K.stop), lambda i: (0, 0)),
        ],
        out_specs=[
            pl.BlockSpec((tm, ABC_COLS), lambda i: (i, 0)),
            pl.BlockSpec((tm, D_COLS), lambda i: (i, 0)),
        ],
        out_shape=[
            jax.ShapeDtypeStruct((n, ABC_COLS), BF16),
            jax.ShapeDtypeStruct((n, D_COLS), F32),
        ],
        compiler_params=_params("parallel"),
        name="in_proj",
    )(x2, gain, w_abc, w_d, *tabs[0], *tabs[1], *tabs[2], gain_a)


def _value_ext_t(v_t):
    row = lax.broadcasted_iota(jnp.int32, (VEXT_ROWS - HEAD_DIM, v_t.shape[1]), 0)
    return jnp.concatenate([v_t, jnp.where(row == 0, 1.0, 0.0)], axis=0).astype(BF16)


def _attend_exact(k_ref, q_ts, vx_refs):
    outs = []
    for q_t, vx in zip(q_ts, vx_refs):
        s_t = _dot(k_ref[...], q_t)
        p_t = jnp.exp2(s_t - jnp.max(s_t, axis=0, keepdims=True)).astype(BF16)
        o = _dot(vx[...], p_t)
        outs.append(o[:HEAD_DIM] / o[HEAD_DIM:HEAD_DIM + 1])
    return outs


def _attend_shifted(k_ref, q_ts, shifts, vx_refs):
    t = k_ref.shape[0]
    ck = min(t, KEY_CHUNK)
    steps = [(j, slice(c * ck, (c + 1) * ck)) for j in range(len(q_ts)) for c in range(t // ck)]
    acc = [None] * len(q_ts)
    s_prev = p_prev = None
    for i in range(len(steps) + 2):
        s_new = p_new = None
        if i < len(steps):
            j, keys = steps[i]
            s_new = _dot(k_ref[keys, :], q_ts[j])
        if 1 <= i <= len(steps):
            p_new = jnp.exp2(s_prev - shifts[steps[i - 1][0]]).astype(BF16)
        if i >= 2:
            j, keys = steps[i - 2]
            o = _dot(vx_refs[j][:, keys], p_prev)
            acc[j] = o if acc[j] is None else acc[j] + o
        s_prev, p_prev = s_new, p_new
    return [a[:HEAD_DIM] / a[HEAD_DIM:HEAD_DIM + 1] for a in acc]


def _attend_t(k_ref, kn_ref, q_t, head_rows, head_keys, vx_refs, finish):
    c = k_ref.shape[1]
    tq = q_t.shape[1]
    qsq = q_t * q_t
    q_ts, shifts = [], []
    for rows, keys in zip(head_rows, head_keys):
        pieces = [jnp.zeros((keys.start, tq), F32), q_t[rows], jnp.zeros((c - keys.stop, tq), F32)]
        q_ts.append(jnp.concatenate([x for x in pieces if x.shape[0]], axis=0).astype(BF16))
        qn = jnp.sqrt(jnp.sum(qsq[rows], axis=0, keepdims=True))
        shifts.append(qn * kn_ref[0:1, keys.start:keys.start + 1])
    worst = jnp.max(jnp.concatenate(shifts, axis=0))

    @pl.when(2.0 * worst < SHIFT_LIMIT)
    def _():
        finish(_attend_shifted(k_ref, q_ts, shifts, vx_refs))

    @pl.when(jnp.logical_not(2.0 * worst < SHIFT_LIMIT))
    def _():
        finish(_attend_exact(k_ref, q_ts, vx_refs))


def _gqa_kernel(q_ref, k_ref, v_ref, o_ref, kn_s, vx_s):
    @pl.when(pl.program_id(1) == 0)
    def _():
        k = k_ref[0].astype(F32)
        kn_s[...] = jnp.sqrt(jnp.max(_head_sum(k * k, HEAD_DIM), axis=0, keepdims=True))
        v_t = v_ref[0].astype(F32).T
        for g in range(A_KV_HEADS):
            vx_s[g] = _value_ext_t(v_t[g * HEAD_DIM:(g + 1) * HEAD_DIM])

    q_t = q_ref[0].astype(F32).T
    k_s = k_ref.at[0]
    group = A_HEADS // A_KV_HEADS
    head_rows = [slice(h * HEAD_DIM, (h + 1) * HEAD_DIM) for h in range(A_HEADS)]
    head_keys = [slice((h // group) * HEAD_DIM, (h // group + 1) * HEAD_DIM) for h in range(A_HEADS)]

    def finish(outs):
        o_ref[0] = jnp.concatenate(outs, axis=0).T.astype(BF16)

    _attend_t(k_s, kn_s, q_t, head_rows, head_keys, [vx_s.at[h // group] for h in range(A_HEADS)], finish)


def _mixer_gqa(zabc, tq):
    b, t, _ = zabc.shape
    return pl.pallas_call(
        _gqa_kernel,
        grid=(b, t // tq),
        in_specs=[
            pl.BlockSpec((1, tq, 256), lambda bi, i: (bi, i, 0)),
            pl.BlockSpec((1, t, 128), lambda bi, i: (bi, 0, 2)),
            pl.BlockSpec((1, t, 128), lambda bi, i: (bi, 0, 3)),
        ],
        out_specs=pl.BlockSpec((1, tq, MIX_W), lambda bi, i: (bi, i, 0)),
        out_shape=jax.ShapeDtypeStruct((b, t, MIX_W), BF16),
        scratch_shapes=[
            pltpu.VMEM((1, A_KV_HEADS * HEAD_DIM), F32),
            pltpu.VMEM((A_KV_HEADS, VEXT_ROWS, t), BF16),
        ],
        compiler_params=_params("parallel", "arbitrary"),
        name="mixer_gqa",
    )(zabc, zabc, zabc)


def _diff_kernel(lam_init, q_ref, k_ref, v_ref, lam_ref, gain_ref, o_ref, kn_s, vx_s):
    @pl.when(pl.program_id(1) == 0)
    def _():
        k = k_ref[0].astype(F32)
        kn_s[...] = jnp.sqrt(jnp.max(_head_sum(k * k, B_SUB), axis=0, keepdims=True))
        v_t = v_ref[0].astype(F32).T
        for h in range(B_HEADS):
            vx_s[h] = _value_ext_t(v_t[h * HEAD_DIM:(h + 1) * HEAD_DIM])

    lp = lam_ref[...]
    lam = (jnp.exp(jnp.sum(lp[0:1] * lp[1:2], axis=1, keepdims=True))
           - jnp.exp(jnp.sum(lp[2:3] * lp[3:4], axis=1, keepdims=True)) + lam_init)
    q_t = q_ref[0].astype(F32).T
    k_s = k_ref.at[0]
    sub_rows = [slice(j * B_SUB, (j + 1) * B_SUB) for j in range(2 * B_HEADS)]

    def finish(parts):
        outs = []
        for h in range(B_HEADS):
            o = parts[2 * h] - lam * parts[2 * h + 1]
            o = o * lax.rsqrt(jnp.mean(o * o, axis=0, keepdims=True) + NORM_EPS) * gain_ref[...]
            outs.append(o * (1.0 - lam_init))
        o_ref[0] = jnp.concatenate(outs, axis=0).T.astype(BF16)

    _attend_t(k_s, kn_s, q_t, sub_rows, sub_rows, [vx_s.at[j // 2] for j in range(2 * B_HEADS)], finish)


def _mixer_diff(zabc, lam_params, gain, lam_init, tq):
    b, t, _ = zabc.shape
    return pl.pallas_call(
        functools.partial(_diff_kernel, lam_init),
        grid=(b, t // tq),
        in_specs=[
            pl.BlockSpec((1, tq, 256), lambda bi, i: (bi, i, 2)),
            pl.BlockSpec((1, t, 256), lambda bi, i: (bi, 0, 3)),
            pl.BlockSpec((1, t, 256), lambda bi, i: (bi, 0, 4)),
            pl.BlockSpec((4, B_SUB), lambda bi, i: (0, 0)),
            pl.BlockSpec((HEAD_DIM, 1), lambda bi, i: (0, 0)),
        ],
        out_specs=pl.BlockSpec((1, tq, MIX_W), lambda bi, i: (bi, i, 0)),
        out_shape=jax.ShapeDtypeStruct((b, t, MIX_W), BF16),
        scratch_shapes=[
            pltpu.VMEM((1, 2 * B_HEADS * B_SUB), F32),
            pltpu.VMEM((B_HEADS, VEXT_ROWS, t), BF16),
        ],
        compiler_params=_params("parallel", "arbitrary"),
        name="mixer_diff",
    )(zabc, zabc, zabc, lam_params, gain)


def _ret_log_gammas():
    lg = [math.log1p(-(2.0 ** (-5.0 - h))) for h in range(C_HEADS)]
    return lg, lg[::-1]


def _lane_consts(vals, shape):
    head = lax.broadcasted_iota(jnp.int32, shape, len(shape) - 1) // HEAD_DIM
    out = jnp.full(shape, vals[-1], F32)
    for h in range(len(vals) - 2, -1, -1):
        out = jnp.where(head == h, vals[h], out)
    return out


def _ret_kernel(chunk, q_ref, k_ref, v_ref, g_ref, gain_ref, o_ref, acc_s, sf_s, sb_s, dm_s):
    t = q_ref.shape[1]
    c = chunk
    n = t // c
    lgf, lgb = _ret_log_gammas()
    qr_s = q_ref.at[0]
    kr_s = k_ref.at[0]

    ti = lax.broadcasted_iota(jnp.int32, (c, c), 0)
    si = lax.broadcasted_iota(jnp.int32, (c, c), 1)
    dist = (ti - si).astype(F32)
    lane_head = lax.broadcasted_iota(jnp.int32, (1, MIX_W), 1) // HEAD_DIM
    row = lax.broadcasted_iota(jnp.int32, (c, MIX_W), 0).astype(F32)
    lgf_l = _lane_consts(lgf, (c, MIX_W))
    lgb_l = _lane_consts(lgb, (c, MIX_W))
    qdec_f = jnp.exp(lgf_l * (row + 1.0))
    kdec_f = jnp.exp(lgf_l * (c - 1.0 - row))
    qdec_b = jnp.exp(lgb_l * (c - row))
    kdec_b = jnp.exp(lgb_l * row)
    r2 = lax.broadcasted_iota(jnp.int32, (MIX_W, MIX_W), 0) // HEAD_DIM
    c2 = lax.broadcasted_iota(jnp.int32, (MIX_W, MIX_W), 1) // HEAD_DIM
    same_head = r2 == c2
    gf_blk = jnp.where(same_head, jnp.exp(_lane_consts(lgf, (MIX_W, MIX_W)) * c), 0.0)
    gb_blk = jnp.where(same_head, jnp.exp(_lane_consts(lgb, (MIX_W, MIX_W)) * c), 0.0)

    head_mask = [(lane_head == h).astype(BF16) for h in range(C_HEADS)]
    for h in range(C_HEADS):
        dm_s[h] = jnp.where(dist > 0, jnp.exp(lgf[h] * dist),
                            jnp.where(dist < 0, jnp.exp(-lgb[h] * dist), 2.0))

    def intra(qc, kc, vc):
        ps = [(_dot_nt(qc * head_mask[h], kc) * dm_s[h]).astype(BF16) for h in range(C_HEADS)]
        v_stack = jnp.concatenate([vc * head_mask[h] for h in range(C_HEADS)], axis=0)
        return _dot(jnp.concatenate(ps, axis=1), v_stack)

    sf_s[...] = jnp.zeros_like(sf_s)
    sb_s[...] = jnp.zeros_like(sb_s)

    def fwd(i, carry):
        rows = pl.ds(pl.multiple_of(i * c, c), c)
        qc = qr_s[rows, :]
        kc = kr_s[rows, :]
        vc = v_ref[0, rows, :]
        o = intra(qc, kc, vc) + _dot((qc.astype(F32) * qdec_f).astype(BF16), sf_s[...].astype(BF16))
        acc_s[rows, :] = o
        kd = (kc.astype(F32) * kdec_f).T.astype(BF16)
        sf_s[...] = gf_blk * sf_s[...] + jnp.where(same_head, _dot(kd, vc), 0.0)
        return carry

    lax.fori_loop(0, n, fwd, 0)

    def bwd(i, carry):
        rows = pl.ds(pl.multiple_of((n - 1 - i) * c, c), c)
        qc = qr_s[rows, :].astype(F32)
        kc = kr_s[rows, :].astype(F32)
        vc = v_ref[0, rows, :]
        acc_s[rows, :] = acc_s[rows, :] + _dot((qc * qdec_b).astype(BF16), sb_s[...].astype(BF16))
        kd = (kc * kdec_b).T.astype(BF16)
        sb_s[...] = gb_blk * sb_s[...] + jnp.where(same_head, _dot(kd, vc), 0.0)
        return carry

    lax.fori_loop(0, n, bwd, 0)

    o = acc_s[...]
    o = o * lax.rsqrt(_head_sum(o * o, HEAD_DIM) * (1.0 / HEAD_DIM) + NORM_EPS) * gain_ref[...]
    g = g_ref[0].astype(F32)
    o_ref[0] = (o * (g * _sigmoid(g))).astype(BF16)


def _mixer_ret(zabc, gain, chunk):
    b, t, _ = zabc.shape
    blk = lambda j: pl.BlockSpec((1, t, 256), lambda bi: (bi, 0, j))
    return pl.pallas_call(
        functools.partial(_ret_kernel, chunk),
        grid=(b,),
        in_specs=[blk(5), blk(6), blk(7), blk(8),
                  pl.BlockSpec((1, 256), lambda bi: (0, 0))],
        out_specs=pl.BlockSpec((1, t, MIX_W), lambda bi: (bi, 0, 0)),
        out_shape=jax.ShapeDtypeStruct((b, t, MIX_W), BF16),
        scratch_shapes=[
            pltpu.VMEM((t, MIX_W), F32),
            pltpu.VMEM((MIX_W, MIX_W), F32),
            pltpu.VMEM((MIX_W, MIX_W), F32),
            pltpu.VMEM((C_HEADS, chunk, chunk), F32),
        ],
        compiler_params=_params("parallel"),
        name="mixer_ret",
    )(zabc, zabc, zabc, zabc, gain)


def _wkv_prep_kernel(z_ref, zp_ref, zn_ref, mup_ref, mun_ref, w0_ref, wupf_ref, wupb_ref, a0_ref, aup_ref,
                     gup1_ref, gup2_ref, kk_ref, ka_ref, rk_ref,
                     r_o, k_o, v_o, kk_o, b_o, ldf_o, ldb_o, g_o, bonus_o):
    i = pl.program_id(1)
    last = pl.num_programs(1) - 1
    z = z_ref[0]
    tm = z.shape[0]
    row = lax.broadcasted_iota(jnp.int32, z.shape, 0)
    prev_row = zp_ref[0, 0, 7:8, :] * jnp.where(i > 0, 1.0, 0.0)
    next_row = zn_ref[0, 0, 0:1, :] * jnp.where(i < last, 1.0, 0.0)
    z_prev = jnp.where(row == 0, prev_row, pltpu.roll(z, 1, 0))
    z_next = jnp.where(row == tm - 1, next_row, pltpu.roll(z, tm - 1, 0))
    u = z + mup_ref[...] * (z_prev - z) + mun_ref[...] * (z_next - z)
    r = u[:, 0:256]
    k = u[:, 256:512]
    v = u[:, 512:768]
    wd = jnp.tanh(u[:, 768:896]).astype(BF16)
    ag = u[:, 896:1024]
    g2 = u[:, 1024:1088]

    def log_decay(w0, wup):
        x = w0 + _dot(wd, wup)
        w = -(jnp.maximum(-x, 0.0) + jnp.log(1.0 + jnp.exp(-jnp.abs(x)))) - 0.5
        return -jnp.exp(w)

    ldf_o[0] = log_decay(w0_ref[0:1, :], wupf_ref[...])
    ldb_o[0] = log_decay(w0_ref[1:2, :], wupb_ref[...])
    a = _sigmoid(a0_ref[...] + _dot(ag.astype(BF16), aup_ref[...]))
    g_o[0] = (_dot(_sigmoid(ag).astype(BF16), gup1_ref[...])
              + _dot(_sigmoid(g2).astype(BF16), gup2_ref[...]))
    kk = k * kk_ref[...]
    kk = kk / jnp.maximum(jnp.sqrt(_head_sum(kk * kk, HEAD_DIM)), 1e-12)
    kh = k * (1.0 + (a - 1.0) * ka_ref[...])
    r_o[0] = r
    k_o[0] = kh
    v_o[0] = v
    kk_o[0] = kk
    b_o[0] = kk * a
    bonus_o[0] = _head_sum(r * kh * rk_ref[...], HEAD_DIM) * v


def _wkv_prep(zd, p, tm):
    b, t, _ = zd.shape
    zd8 = zd.reshape(b, t // 8, 8, D_COLS)
    r8 = tm // 8
    nb8 = t // 8
    row = lambda w: pl.BlockSpec((1, w), lambda bi, i: (0, 0))
    full = lambda a: pl.BlockSpec(a.shape, lambda bi, i: (0,) * a.ndim)
    out_spec = pl.BlockSpec((1, tm, MIX_W), lambda bi, i: (bi, i, 0))
    out_shape = jax.ShapeDtypeStruct((b, t, MIX_W), F32)
    return pl.pallas_call(
        _wkv_prep_kernel,
        grid=(b, t // tm),
        in_specs=[
            pl.BlockSpec((1, tm, D_COLS), lambda bi, i: (bi, i, 0)),
            pl.BlockSpec((1, 1, 8, D_COLS), lambda bi, i: (bi, jnp.maximum(i * r8 - 1, 0), 0, 0)),
            pl.BlockSpec((1, 1, 8, D_COLS), lambda bi, i: (bi, jnp.minimum((i + 1) * r8, nb8 - 1), 0, 0)),
            row(D_COLS), row(D_COLS),
            full(p["w0"]), full(p["wup_f"]), full(p["wup_b"]), row(256), full(p["aup"]),
            full(p["gup1"]), full(p["gup2"]), row(256), row(256), row(256),
        ],
        out_specs=[out_spec] * 9,
        out_shape=[out_shape] * 9,
        compiler_params=_params("parallel", "parallel"),
        name="wkv_prep",
    )(zd, zd8, zd8, p["mu_prev"], p["mu_next"], p["w0"], p["wup_f"], p["wup_b"], p["a0"], p["aup"],
      p["gup1"], p["gup2"], p["k_k"], p["k_a"], p["r_k"])


def _wkv_chunk_consts(chunk):
    c = chunk
    sc = D_HEADS * c
    ti = lax.broadcasted_iota(jnp.int32, (c, c), 0)
    si = lax.broadcasted_iota(jnp.int32, (c, c), 1)
    srow = lax.broadcasted_iota(jnp.int32, (sc, MIX_W), 0) // c
    slane = lax.broadcasted_iota(jnp.int32, (sc, MIX_W), 1) // HEAD_DIM
    gt = lax.broadcasted_iota(jnp.int32, (2 * sc, 2 * sc), 0)
    gs = lax.broadcasted_iota(jnp.int32, (2 * sc, 2 * sc), 1)
    t_in, s_in = gt % sc, gs % sc
    same = (t_in // c) == (s_in // c)
    upper = gt < sc
    st = lax.broadcasted_iota(jnp.int32, (sc, sc), 0)
    ss = lax.broadcasted_iota(jnp.int32, (sc, sc), 1)
    r2 = lax.broadcasted_iota(jnp.int32, (MIX_W, MIX_W), 0)
    c2 = lax.broadcasted_iota(jnp.int32, (MIX_W, MIX_W), 1)
    return {
        "tri": {False: jnp.where(si <= ti, 1.0, 0.0).astype(BF16), True: jnp.where(si >= ti, 1.0, 0.0).astype(BF16)},
        "hmask": srow == slane,
        "gmask": {False: same & ((s_in < t_in) | ((s_in == t_in) & ~upper)),
                  True: same & ((s_in > t_in) | ((s_in == t_in) & ~upper))},
        "eye": jnp.where(st == ss, 1.0, 0.0),
        "diag": r2 == c2,
    }


def _wkv_phase1(chunk, consts, probs):
    c = chunk
    sc = D_HEADS * c
    hmask = consts["hmask"]

    def stack(a):
        return jnp.where(hmask, jnp.concatenate([a] * D_HEADS, axis=0), 0.0).astype(BF16)

    pre = []
    for rev, r, kh, v, kk, bb, lw in probs:
        tri = consts["tri"][rev]
        l1, l2, l3 = _split3(lw)
        cl = _dot(tri, l1) + _dot(tri, l2) + _dot(tri, l3)
        tot = jnp.sum(lw, axis=0, keepdims=True)
        pre.append((cl, tot))
    yield
    ops = []
    for (rev, r, kh, v, kk, bb, lw), (cl, tot) in zip(probs, pre):
        w_inv = jnp.exp(-cl)
        w_end = jnp.exp(tot - cl)
        lhs = jnp.concatenate([stack(kk * jnp.exp(cl - lw)), stack(r * jnp.exp(cl))], axis=0)
        rhs = jnp.concatenate([stack(bb * w_inv), stack(kh * w_inv)], axis=0)
        kb = jnp.concatenate([stack(kh * w_end), stack(bb * w_end)], axis=0)
        wc = jnp.sum(jnp.where(consts["diag"], jnp.exp(tot), 0.0), axis=1, keepdims=True)
        wc = jnp.broadcast_to(wc, (MIX_W, MIX_W))
        ops.append((lhs, rhs, stack(v), kb, wc))
    yield
    gram = [jnp.where(consts["gmask"][p[0]], _dot_nt(o[0], o[1]), 0.0) for p, o in zip(probs, ops)]
    l_ab = [g[:sc, :sc] for g in gram]
    l_ak = [g[:sc, sc:].astype(BF16) for g in gram]
    m_r = [g[sc:, :].astype(BF16) for g in gram]
    yield

    steps = int(math.log2(c))
    pw = [(-l).astype(BF16) for l in l_ab]
    inv = [consts["eye"] - l for l in l_ab]
    w = [_dot(a, o[2]).astype(BF16) for a, o in zip(l_ak, ops)]
    for k in range(1, steps):
        if k == 1:
            pw = [_dot(p, p).astype(BF16) for p in pw]
            yield
        if k < steps - 1:
            nxt = [_dot(p, jnp.concatenate([p, i.astype(BF16)], axis=1)) for p, i in zip(pw, inv)]
            inv = [i + n[:, sc:] for i, n in zip(inv, nxt)]
            pw = [n[:, :sc].astype(BF16) for n in nxt]
        else:
            inv = [i + _dot(p, i.astype(BF16)) for p, i in zip(pw, inv)]
        yield
    inv = [i.astype(BF16) for i in inv]

    gu = [_dot(i, jnp.concatenate([o[0][:sc], wv], axis=1)) for i, o, wv in zip(inv, ops, w)]
    yield
    kb_t = [o[3].astype(F32).T.astype(BF16) for o in ops]
    mgu = [_dot(m[:, :sc], x.astype(BF16)) for m, x in zip(m_r, gu)]
    mv = [_dot(m[:, sc:], o[2]) for m, o in zip(m_r, ops)]
    yield
    return [{
        "qg": jnp.concatenate([(o[0][sc:].astype(F32) - mg[:, :MIX_W]).astype(BF16),
                               g[:, :MIX_W].astype(BF16)], axis=0),
        "y_hat": a - mg[:, MIX_W:],
        "u_hat": g[:, MIX_W:],
        "kb_t": kt,
        "v": o[2],
        "wc": o[4],
    } for o, g, mg, a, kt in zip(ops, gu, mgu, mv, kb_t)]


def _wkv_scan_kernel(chunk, group, rf, kf, vf, kkf, bf, lf, rb, kb, vb, kkb, bb, lb, yf_o, yb_o,
                     xf_s, xb_s, qg_s, yh_s, uh_s, kbt_s, v_s, wc_s):
    @pl.when(pl.program_id(1) == 0)
    def _():
        xf_s[...] = jnp.zeros_like(xf_s)
        xb_s[...] = jnp.zeros_like(xb_s)

    tm = rf.shape[1]
    n = tm // chunk
    sc = D_HEADS * chunk
    consts = _wkv_chunk_consts(chunk)
    sol_refs = {"qg": qg_s, "y_hat": yh_s, "u_hat": uh_s, "kb_t": kbt_s, "v": v_s, "wc": wc_s}

    def group_rows(g):
        rows = []
        for u in range(group):
            rows.append(pl.ds(pl.multiple_of((g * group + u) * chunk, chunk), chunk))
            rows.append(pl.ds(pl.multiple_of((n - 1 - g * group - u) * chunk, chunk), chunk))
        return rows

    def problems(g):
        probs = []
        for slot, rows in enumerate(group_rows(g)):
            if slot % 2:
                probs.append((True, rb[0, rows, :], kb[0, rows, :], vb[0, rows, :], kkb[0, rows, :],
                              bb[0, rows, :], lb[0, rows, :]))
            else:
                probs.append((False, rf[0, rows, :], kf[0, rows, :], vf[0, rows, :], kkf[0, rows, :],
                              bf[0, rows, :], lf[0, rows, :]))
        return probs

    def advance(slot, rows):
        x_s, y_o = (xb_s, yb_o) if slot % 2 else (xf_s, yf_o)
        x = x_s[...]
        res = _dot(qg_s[slot], x.astype(BF16))
        ys = res[:sc] + yh_s[slot]
        y = ys[0:chunk]
        for h in range(1, D_HEADS):
            y = y + ys[h * chunk:(h + 1) * chunk]
        y_o[0, rows, :] = y
        u = -(res[sc:] + uh_s[slot])
        x_s[...] = x * wc_s[slot] + _dot(kbt_s[slot], jnp.concatenate([v_s[slot], u.astype(BF16)], axis=0))

    def solve(g, pending):
        gen = _wkv_phase1(chunk, consts, problems(g))
        pending = list(pending)
        while True:
            try:
                next(gen)
            except StopIteration as done:
                sols = done.value
                break
            for step in pending[:2]:
                step()
            pending = pending[2:]
        for step in pending:
            step()
        for slot, sol in enumerate(sols):
            for name, ref in sol_refs.items():
                ref[slot] = sol[name]

    def steps_for(g):
        return [functools.partial(advance, slot, r) for slot, r in enumerate(group_rows(g))]

    n_groups = n // group
    solve(0, [])

    for g in range(1, n_groups):
        solve(g, steps_for(g - 1))
    for step in steps_for(n_groups - 1):
        step()


def _wkv_scan(r, kh, v, kk, bb, ldf, ldb, tm, chunk, group):
    b, t, _ = r.shape
    nt = t // tm
    sc = D_HEADS * chunk
    slots = 2 * group
    fspec = pl.BlockSpec((1, tm, MIX_W), lambda bi, i: (bi, i, 0))
    bspec = pl.BlockSpec((1, tm, MIX_W), lambda bi, i: (bi, nt - 1 - i, 0))
    shape = jax.ShapeDtypeStruct((b, t, MIX_W), F32)
    return pl.pallas_call(
        functools.partial(_wkv_scan_kernel, chunk, group),
        grid=(b, nt),
        in_specs=[fspec] * 6 + [bspec] * 6,
        out_specs=[fspec, bspec],
        out_shape=[shape, shape],
        scratch_shapes=[
            pltpu.VMEM((MIX_W, MIX_W), F32), pltpu.VMEM((MIX_W, MIX_W), F32),
            pltpu.VMEM((slots, 2 * sc, MIX_W), BF16),
            pltpu.VMEM((slots, sc, MIX_W), F32),
            pltpu.VMEM((slots, sc, MIX_W), F32),
            pltpu.VMEM((slots, MIX_W, 2 * sc), BF16),
            pltpu.VMEM((slots, sc, MIX_W), BF16),
            pltpu.VMEM((slots, MIX_W, MIX_W), F32),
        ],
        compiler_params=_params("parallel", "arbitrary"),
        name="wkv_scan",
    )(r, kh, v, kk, bb, ldf, r, kh, v, kk, bb, ldb)


def _wkv_out(yf, yb, bonus, g, gn_w, gn_b):
    y = yf + yb
    mean = _head_sum(y, HEAD_DIM) * (1.0 / HEAD_DIM)
    yc = y - mean
    var = _head_sum(yc * yc, HEAD_DIM) * (1.0 / HEAD_DIM)
    yn = yc * lax.rsqrt(var + WKV_GN_EPS) * gn_w + gn_b
    return ((yn + bonus) * g).astype(BF16)


def _post_kernel(x_ref, oa_ref, ob_ref, oc_ref, yf_ref, yb_ref, bonus_ref, g_ref, gnw_ref, gnb_ref,
                 wo_ref, gpost_ref, gpre_ref, wg_ref, wu_ref, wd_ref, gfpost_ref, y_ref):
    tm = x_ref.shape[0]
    groups = [slice(s, s + tm // ROW_GROUPS) for s in range(0, tm, tm // ROW_GROUPS)]
    o_d = [_wkv_out(yf_ref[r], yb_ref[r], bonus_ref[r], g_ref[r], gnw_ref[...], gnb_ref[...]) for r in groups]
    mix = [_dot(oa_ref[r], wo_ref[0]) + _dot(ob_ref[r], wo_ref[1]) + _dot(oc_ref[r], wo_ref[2])
           + _dot(od, wo_ref[3]) for r, od in zip(groups, o_d)]
    x = [x_ref[r] + _rms_rows(m, gpost_ref[...]) for r, m in zip(groups, mix)]
    h = [_rms_rows(xg, gpre_ref[...]).astype(BF16) for xg in x]
    gate = [_dot(hg, wg_ref[...]) for hg in h]
    up = [_dot(hg, wu_ref[...]) for hg in h]
    act = [(g * _sigmoid(g) * u).astype(BF16) for g, u in zip(gate, up)]
    f = [_dot(a, wd_ref[...]) for a in act]
    for r, xg, fg in zip(groups, x, f):
        y_ref[r] = xg + _rms_rows(fg, gfpost_ref[...])


def _post(x2, oa, ob, oc, wkv_parts, gn_w, gn_b, w_out4, g_post, g_pre, wg, wu, wd, gf_post, tm):
    n, d = x2.shape
    dff = wg.shape[1]
    tok = lambda w: pl.BlockSpec((tm, w), lambda i: (i, 0))
    row = pl.BlockSpec((1, d), lambda i: (0, 0))
    mrow = pl.BlockSpec((1, MIX_W), lambda i: (0, 0))
    return pl.pallas_call(
        _post_kernel,
        grid=(n // tm,),
        in_specs=[tok(d)] + [tok(MIX_W)] * 7 + [mrow, mrow,
                  pl.BlockSpec((4, MIX_W, d), lambda i: (0, 0, 0)), row, row,
                  pl.BlockSpec((d, dff), lambda i: (0, 0)),
                  pl.BlockSpec((d, dff), lambda i: (0, 0)),
                  pl.BlockSpec((dff, d), lambda i: (0, 0)), row],
        out_specs=tok(d),
        out_shape=jax.ShapeDtypeStruct((n, d), F32),
        compiler_params=_params("parallel"),
        name="out_proj_ffn",
    )(x2, oa, ob, oc, *wkv_parts, gn_w, gn_b, w_out4, g_post, g_pre, wg, wu, wd, gf_post)


def _angles(pos, rot_dim, theta):
    inv = theta ** (-jnp.arange(0, rot_dim, 2, dtype=F32) / rot_dim)
    return pos.astype(F32)[:, None] * inv[None, :]


def _rope_tables(t):
    rows = t // GRID_W
    row_idx = jnp.repeat(jnp.arange(rows), GRID_W)
    col_idx = jnp.tile(jnp.arange(GRID_W), rows)
    pos = jnp.arange(t)
    ar = _angles(row_idx, HEAD_DIM // 2, A_THETA)
    ac = _angles(col_idx, HEAD_DIM // 2, A_THETA)
    cos_a = jnp.concatenate([jnp.cos(ar), jnp.cos(ar), jnp.cos(ac), jnp.cos(ac)], axis=1)
    sin_a = jnp.concatenate([-jnp.sin(ar), jnp.sin(ar), -jnp.sin(ac), jnp.sin(ac)], axis=1)
    tabs_a = (jnp.tile(cos_a, (1, 4)), jnp.tile(sin_a, (1, 4)))
    ab = _angles(pos, B_ROT, B_THETA)
    pad1 = jnp.ones((t, B_SUB - B_ROT), F32)
    pad0 = jnp.zeros((t, B_SUB - B_ROT), F32)
    cos_b = jnp.concatenate([jnp.cos(ab), jnp.cos(ab), pad1], axis=1)
    sin_b = jnp.concatenate([-jnp.sin(ab), jnp.sin(ab), pad0], axis=1)
    tabs_b = (jnp.tile(cos_b, (1, 8)), jnp.tile(sin_b, (1, 8)))
    ang_c = _angles(pos, HEAD_DIM, C_THETA)
    cos_c = jnp.concatenate([jnp.cos(ang_c), jnp.cos(ang_c)], axis=1)
    sin_c = jnp.concatenate([-jnp.sin(ang_c), jnp.sin(ang_c)], axis=1)
    tabs_c = (jnp.tile(cos_c, (1, 4)), jnp.tile(sin_c, (1, 4)))
    return tabs_a, tabs_b, tabs_c


def _tile(n, pref):
    while n % pref:
        pref //= 2
    return pref


def _layer_weights(l, w):
    d_model = w["w_in"].shape[1]
    row = lambda a: a.reshape(1, -1).astype(F32)
    zeros = lambda r: jnp.zeros((r, MIX_W), F32)
    w_lora = w["d_w_up"].shape[2]
    return {
        "g_mix_pre": row(w["norm_mix_pre"][l]), "g_mix_post": row(w["norm_mix_post"][l]),
        "g_ffn_pre": row(w["norm_ffn_pre"][l]), "g_ffn_post": row(w["norm_ffn_post"][l]),
        "w_abc": w["w_in"][l][:, :ABC_COLS].astype(BF16), "w_d": w["w_in"][l][:, ABC_COLS:].astype(BF16),
        "w_out4": w["w_out"][l].reshape(4, MIX_W, d_model).astype(BF16),
        "a_gain": jnp.concatenate([jnp.tile(row(w["a_q_gain"][l]), (1, A_HEADS)),
                                   jnp.tile(row(w["a_k_gain"][l]), (1, A_KV_HEADS))], axis=1),
        "b_lambda": w["b_lambda"][l].astype(F32), "b_gain": w["b_subln_gain"][l].reshape(-1, 1).astype(F32),
        "c_gain": row(w["c_gn_gain"][l]),
        "wkv": {
            "mu_prev": row(w["d_mu_prev"][l]), "mu_next": row(w["d_mu_next"][l]),
            "w0": w["d_w0"][l].astype(F32),
            "wup_f": jnp.concatenate([w["d_w_up"][l, 0], zeros(w_lora)], axis=0).astype(BF16),
            "wup_b": jnp.concatenate([zeros(w_lora), w["d_w_up"][l, 1]], axis=0).astype(BF16),
            "a0": row(w["d_a0"][l]),
            "aup": jnp.concatenate([w["d_a_up"][l], zeros(64)], axis=0).astype(BF16),
            "gup1": jnp.concatenate([zeros(64), w["d_g_up"][l][:64]], axis=0).astype(BF16),
            "gup2": w["d_g_up"][l][64:].astype(BF16),
            "k_k": row(w["d_k_k"][l]), "k_a": row(w["d_k_a"][l]), "r_k": row(w["d_r_k"][l]),
        },
        "gn_w": row(w["d_gn_w"][l]), "gn_b": row(w["d_gn_b"][l]),
        "wg": w["ffn_w_gate"][l].astype(BF16), "wu": w["ffn_w_up"][l].astype(BF16),
        "wd": w["ffn_w_down"][l].astype(BF16),
    }


def _trunk(x, layers):
    b, t, d = x.shape
    n = b * t
    tabs = _rope_tables(t)
    tm_proj = _tile(t, 512)
    tm_post = _tile(n, 512)
    tq = _tile(t, 256)
    tm_wkv = _tile(t, 256)
    tm_scan = _tile(t, 1024)
    chunk = 32
    x2 = x.reshape(n, d)
    for l, p in enumerate(layers):
        lam_init = 0.8 - 0.6 * math.exp(-0.3 * l)
        zabc, zd = _in_proj(x2, p["g_mix_pre"], p["w_abc"], p["w_d"], tabs, p["a_gain"], t, tm_proj)
        zabc = zabc.reshape(b, t, ABC_COLS)
        zd = zd.reshape(b, t, D_COLS)
        o_a = _mixer_gqa(zabc, tq)
        o_b = _mixer_diff(zabc, p["b_lambda"], p["b_gain"], lam_init, tq)
        o_c = _mixer_ret(zabc, p["c_gain"], _tile(t, 256))
        r, kh, v, kk, bb, ldf, ldb, g, bonus = _wkv_prep(zd, p["wkv"], tm_wkv)
        yf, yb = _wkv_scan(r, kh, v, kk, bb, ldf, ldb, tm_scan, chunk, min(4, tm_scan // chunk))
        wkv_parts = [a.reshape(n, MIX_W) for a in (yf, yb, bonus, g)]
        x2 = _post(x2, o_a.reshape(n, MIX_W), o_b.reshape(n, MIX_W), o_c.reshape(n, MIX_W), wkv_parts,
                   p["gn_w"], p["gn_b"], p["w_out4"], p["g_mix_post"], p["g_ffn_pre"], p["wg"], p["wu"], p["wd"],
                   p["g_ffn_post"], tm_post)
    return x2.reshape(b, t, d)


def kernel(x_prompt, x_sample, norm_mix_pre, norm_mix_post, norm_ffn_pre, norm_ffn_post, w_in, w_out,
           a_q_gain, a_k_gain, b_lambda, b_subln_gain, c_gn_gain, d_mu_prev, d_mu_next, d_w0, d_w_up,
           d_a0, d_a_up, d_g_up, d_k_k, d_k_a, d_r_k, d_gn_w, d_gn_b, ffn_w_gate, ffn_w_up, ffn_w_down):
    w = {
        "norm_mix_pre": norm_mix_pre, "norm_mix_post": norm_mix_post,
        "norm_ffn_pre": norm_ffn_pre, "norm_ffn_post": norm_ffn_post,
        "w_in": w_in, "w_out": w_out, "a_q_gain": a_q_gain, "a_k_gain": a_k_gain,
        "b_lambda": b_lambda, "b_subln_gain": b_subln_gain, "c_gn_gain": c_gn_gain,
        "d_mu_prev": d_mu_prev, "d_mu_next": d_mu_next, "d_w0": d_w0, "d_w_up": d_w_up,
        "d_a0": d_a0, "d_a_up": d_a_up, "d_g_up": d_g_up, "d_k_k": d_k_k, "d_k_a": d_k_a,
        "d_r_k": d_r_k, "d_gn_w": d_gn_w, "d_gn_b": d_gn_b,
        "ffn_w_gate": ffn_w_gate, "ffn_w_up": ffn_w_up, "ffn_w_down": ffn_w_down,
    }
    layers = [_layer_weights(l, w) for l in range(w_in.shape[0])]
    return (_trunk(x_prompt, layers), _trunk(x_sample, layers))
```

```python
import functools
import math

import jax
import jax.numpy as jnp
from jax import lax
from jax.experimental import pallas as pl
from jax.experimental.pallas import tpu as pltpu

F32 = jnp.float32
BF16 = jnp.bfloat16

HEAD_DIM = 64
GRID_W = 64
NORM_EPS = 1e-6
A_HEADS, A_KV_HEADS, A_THETA = 4, 2, 10000.0
B_HEADS, B_SUB, B_ROT, B_THETA = 4, 32, 8, 500000.0
C_HEADS, C_THETA = 4, 10000.0
D_HEADS = 4
WKV_GN_EPS = 64e-5
MIX_W = 256
ABC_COLS = 2304
A_Q, A_K, A_V = slice(0, 256), slice(256, 384), slice(384, 512)
B_Q, B_K, B_V = slice(512, 768), slice(768, 1024), slice(1024, 1280)
C_Q, C_K, C_VG = slice(1280, 1536), slice(1536, 1792), slice(1792, 2304)
D_COLS = 1088
BF16_SUBLANES = 16
VEXT_ROWS = HEAD_DIM + BF16_SUBLANES
LOG2E = math.log2(math.e)
KEY_CHUNK = 512
ROW_GROUPS = 2
SHIFT_LIMIT = 100.0
VMEM_LIMIT = 56 * 1024 * 1024


def _params(*sem):
    return pltpu.CompilerParams(dimension_semantics=sem, vmem_limit_bytes=VMEM_LIMIT)


def _rms_rows(x, gain):
    return x * lax.rsqrt(jnp.mean(x * x, axis=-1, keepdims=True) + NORM_EPS) * gain


def _split2(x):
    hi = x.astype(BF16)
    lo = (x - hi.astype(F32)).astype(BF16)
    return hi, lo


def _split3(x):
    h1 = x.astype(BF16)
    r1 = x - h1.astype(F32)
    h2 = r1.astype(BF16)
    h3 = (r1 - h2.astype(F32)).astype(BF16)
    return h1, h2, h3


def _head_sum(x, seg):
    w = x.shape[-1]
    r = lax.broadcasted_iota(jnp.int32, (w, w), 0) // seg
    c = lax.broadcasted_iota(jnp.int32, (w, w), 1) // seg
    bd = jnp.where(r == c, 1.0, 0.0).astype(BF16)
    hi, lo = _split2(x)
    return (jnp.dot(hi, bd, preferred_element_type=F32)
            + jnp.dot(lo, bd, preferred_element_type=F32))


def _rope(x, cos, sin, half):
    w = x.shape[-1]
    lane = lax.broadcasted_iota(jnp.int32, x.shape, 1)
    nxt = pltpu.roll(x, w - half, 1)
    prv = pltpu.roll(x, half, 1)
    return x * cos + jnp.where((lane % (2 * half)) < half, nxt, prv) * sin


def _sigmoid(x):
    return 0.5 * jnp.tanh(0.5 * x) + 0.5


def _dot(a, b):
    return jnp.dot(a, b, preferred_element_type=F32)


def _dot_nt(a, b):
    return lax.dot_general(a, b, (((1,), (1,)), ((), ())), preferred_element_type=F32)


def _store_qk_prepared(z, r, tabs, gain_a, out_ref):
    (cos_a, sin_a), (cos_b, sin_b), (cos_c, sin_c) = tabs

    def head_norm(x, gain):
        return x * lax.rsqrt(_head_sum(x * x, HEAD_DIM) * (1.0 / HEAD_DIM) + NORM_EPS) * gain

    a_qk = head_norm(z[:, A_Q.start:A_K.stop], gain_a)
    kw = A_K.stop - A_K.start
    pieces = {
        A_Q: _rope(a_qk[:, :A_Q.stop], cos_a, sin_a, HEAD_DIM // 4) * (HEAD_DIM ** -0.5 * LOG2E),
        A_K: _rope(a_qk[:, A_Q.stop:], cos_a[:, :kw], sin_a[:, :kw], HEAD_DIM // 4),
        B_Q: _rope(z[:, B_Q], cos_b, sin_b, B_ROT // 2) * (B_SUB ** -0.5 * LOG2E),
        B_K: _rope(z[:, B_K], cos_b, sin_b, B_ROT // 2),
        C_Q: _rope(z[:, C_Q], cos_c, sin_c, HEAD_DIM // 2),
        C_K: _rope(z[:, C_K], cos_c, sin_c, HEAD_DIM // 2) * (HEAD_DIM ** -0.5),
    }
    for cols in (A_V, B_V, C_VG):
        pieces[cols] = z[:, cols]
    for cols, val in pieces.items():
        out_ref[r, cols] = val.astype(BF16)


def _in_proj_kernel(x_ref, g_ref, wabc_ref, wd_ref, ca_ref, sa_ref, cb_ref, sb_ref, cc_ref, sc_ref, ga_ref,
                    zabc_ref, zd_ref):
    tm = x_ref.shape[0]
    groups = [slice(s, s + tm // ROW_GROUPS) for s in range(0, tm, tm // ROW_GROUPS)]
    h = [_rms_rows(x_ref[r], g_ref[...]).astype(BF16) for r in groups]
    z = [_dot(hg, wabc_ref[...]) for hg in h]
    for r, hg in zip(groups, h):
        zd_ref[r] = _dot(hg, wd_ref[...])
    for r, zg in zip(groups, z):
        tabs = ((ca_ref[r], sa_ref[r]), (cb_ref[r], sb_ref[r]), (cc_ref[r], sc_ref[r]))
        _store_qk_prepared(zg, r, tabs, ga_ref[...], zabc_ref)


def _in_proj(x2, gain, w_abc, w_d, tabs, gain_a, t, tm):
    n, d = x2.shape
    per_seq = t // tm
    tab = pl.BlockSpec((tm, MIX_W), lambda i: (i % per_seq, 0))
    return pl.pallas_call(
        _in_proj_kernel,
        grid=(n // tm,),
        in_specs=[
            pl.BlockSpec((tm, d), lambda i: (i, 0)),
            pl.BlockSpec((1, d), lambda i: (0, 0)),
            pl.BlockSpec((d, ABC_COLS), lambda i: (0, 0)),
            pl.BlockSpec((d, D_COLS), lambda i: (0, 0)),
            tab, tab, tab, tab, tab, tab,
            pl.BlockSpec((1, A_K.stop), lambda i: (0, 0)),
        ],
        out_specs=[
            pl.BlockSpec((tm, ABC_COLS), lambda i: (i, 0)),
            pl.BlockSpec((tm, D_COLS), lambda i: (i, 0)),
        ],
        out_shape=[
            jax.ShapeDtypeStruct((n, ABC_COLS), BF16),
            jax.ShapeDtypeStruct((n, D_COLS), F32),
        ],
        compiler_params=_params("parallel"),
        name="in_proj",
    )(x2, gain, w_abc, w_d, *tabs[0], *tabs[1], *tabs[2], gain_a)


def _value_ext_t(v_t):
    row = lax.broadcasted_iota(jnp.int32, (VEXT_ROWS - HEAD_DIM, v_t.shape[1]), 0)
    return jnp.concatenate([v_t, jnp.where(row == 0, 1.0, 0.0)], axis=0).astype(BF16)


def _attend_exact(k_ref, q_ts, vx_refs):
    outs = []
    for q_t, vx in zip(q_ts, vx_refs):
        s_t = _dot(k_ref[...], q_t)
        p_t = jnp.exp2(s_t - jnp.max(s_t, axis=0, keepdims=True)).astype(BF16)
        o = _dot(vx[...], p_t)
        outs.append(o[:HEAD_DIM] / o[HEAD_DIM:HEAD_DIM + 1])
    return outs


def _attend_shifted(k_ref, q_ts, shifts, vx_refs):
    t = k_ref.shape[0]
    ck = min(t, KEY_CHUNK)
    steps = [(j, slice(c * ck, (c + 1) * ck)) for j in range(len(q_ts)) for c in range(t // ck)]
    acc = [None] * len(q_ts)
    s_prev = p_prev = None
    for i in range(len(steps) + 2):
        s_new = p_new = None
        if i < len(steps):
            j, keys = steps[i]
            s_new = _dot(k_ref[keys, :], q_ts[j])
        if 1 <= i <= len(steps):
            p_new = jnp.exp2(s_prev - shifts[steps[i - 1][0]]).astype(BF16)
        if i >= 2:
            j, keys = steps[i - 2]
            o = _dot(vx_refs[j][:, keys], p_prev)
            acc[j] = o if acc[j] is None else acc[j] + o
        s_prev, p_prev = s_new, p_new
    return [a[:HEAD_DIM] / a[HEAD_DIM:HEAD_DIM + 1] for a in acc]


def _attend_t(k_ref, kn_ref, q_t, head_rows, head_keys, vx_refs, finish):
    c = k_ref.shape[1]
    tq = q_t.shape[1]
    qsq = q_t * q_t
    q_ts, shifts = [], []
    for rows, keys in zip(head_rows, head_keys):
        pieces = [jnp.zeros((keys.start, tq), F32), q_t[rows], jnp.zeros((c - keys.stop, tq), F32)]
        q_ts.append(jnp.concatenate([x for x in pieces if x.shape[0]], axis=0).astype(BF16))
        qn = jnp.sqrt(jnp.sum(qsq[rows], axis=0, keepdims=True))
        shifts.append(qn * kn_ref[0:1, keys.start:keys.start + 1])
    worst = jnp.max(jnp.concatenate(shifts, axis=0))

    @pl.when(2.0 * worst < SHIFT_LIMIT)
    def _():
        finish(_attend_shifted(k_ref, q_ts, shifts, vx_refs))

    @pl.when(jnp.logical_not(2.0 * worst < SHIFT_LIMIT))
    def _():
        finish(_attend_exact(k_ref, q_ts, vx_refs))


def _gqa_kernel(q_ref, k_ref, v_ref, o_ref, kn_s, vx_s):
    @pl.when(pl.program_id(1) == 0)
    def _():
        k = k_ref[0].astype(F32)
        kn_s[...] = jnp.sqrt(jnp.max(_head_sum(k * k, HEAD_DIM), axis=0, keepdims=True))
        v_t = v_ref[0].astype(F32).T
        for g in range(A_KV_HEADS):
            vx_s[g] = _value_ext_t(v_t[g * HEAD_DIM:(g + 1) * HEAD_DIM])

    q_t = q_ref[0].astype(F32).T
    k_s = k_ref.at[0]
    group = A_HEADS // A_KV_HEADS
    head_rows = [slice(h * HEAD_DIM, (h + 1) * HEAD_DIM) for h in range(A_HEADS)]
    head_keys = [slice((h // group) * HEAD_DIM, (h // group + 1) * HEAD_DIM) for h in range(A_HEADS)]

    def finish(outs):
        o_ref[0] = jnp.concatenate(outs, axis=0).T.astype(BF16)

    _attend_t(k_s, kn_s, q_t, head_rows, head_keys, [vx_s.at[h // group] for h in range(A_HEADS)], finish)


def _mixer_gqa(zabc, tq):
    b, t, _ = zabc.shape
    return pl.pallas_call(
        _gqa_kernel,
        grid=(b, t // tq),
        in_specs=[
            pl.BlockSpec((1, tq, 256), lambda bi, i: (bi, i, 0)),
            pl.BlockSpec((1, t, 128), lambda bi, i: (bi, 0, 2)),
            pl.BlockSpec((1, t, 128), lambda bi, i: (bi, 0, 3)),
        ],
        out_specs=pl.BlockSpec((1, tq, MIX_W), lambda bi, i: (bi, i, 0)),
        out_shape=jax.ShapeDtypeStruct((b, t, MIX_W), BF16),
        scratch_shapes=[
            pltpu.VMEM((1, A_KV_HEADS * HEAD_DIM), F32),
            pltpu.VMEM((A_KV_HEADS, VEXT_ROWS, t), BF16),
        ],
        compiler_params=_params("parallel", "arbitrary"),
        name="mixer_gqa",
    )(zabc, zabc, zabc)


def _diff_kernel(lam_init, q_ref, k_ref, v_ref, lam_ref, gain_ref, o_ref, kn_s, vx_s):
    @pl.when(pl.program_id(1) == 0)
    def _():
        k = k_ref[0].astype(F32)
        kn_s[...] = jnp.sqrt(jnp.max(_head_sum(k * k, B_SUB), axis=0, keepdims=True))
        v_t = v_ref[0].astype(F32).T
        for h in range(B_HEADS):
            vx_s[h] = _value_ext_t(v_t[h * HEAD_DIM:(h + 1) * HEAD_DIM])

    lp = lam_ref[...]
    lam = (jnp.exp(jnp.sum(lp[0:1] * lp[1:2], axis=1, keepdims=True))
           - jnp.exp(jnp.sum(lp[2:3] * lp[3:4], axis=1, keepdims=True)) + lam_init)
    q_t = q_ref[0].astype(F32).T
    k_s = k_ref.at[0]
    sub_rows = [slice(j * B_SUB, (j + 1) * B_SUB) for j in range(2 * B_HEADS)]

    def finish(parts):
        outs = []
        for h in range(B_HEADS):
            o = parts[2 * h] - lam * parts[2 * h + 1]
            o = o * lax.rsqrt(jnp.mean(o * o, axis=0, keepdims=True) + NORM_EPS) * gain_ref[...]
            outs.append(o * (1.0 - lam_init))
        o_ref[0] = jnp.concatenate(outs, axis=0).T.astype(BF16)

    _attend_t(k_s, kn_s, q_t, sub_rows, sub_rows, [vx_s.at[j // 2] for j in range(2 * B_HEADS)], finish)


def _mixer_diff(zabc, lam_params, gain, lam_init, tq):
    b, t, _ = zabc.shape
    return pl.pallas_call(
        functools.partial(_diff_kernel, lam_init),
        grid=(b, t // tq),
        in_specs=[
            pl.BlockSpec((1, tq, 256), lambda bi, i: (bi, i, 2)),
            pl.BlockSpec((1, t, 256), lambda bi, i: (bi, 0, 3)),
            pl.BlockSpec((1, t, 256), lambda bi, i: (bi, 0, 4)),
            pl.BlockSpec((4, B_SUB), lambda bi, i: (0, 0)),
            pl.BlockSpec((HEAD_DIM, 1), lambda bi, i: (0, 0)),
        ],
        out_specs=pl.BlockSpec((1, tq, MIX_W), lambda bi, i: (bi, i, 0)),
        out_shape=jax.ShapeDtypeStruct((b, t, MIX_W), BF16),
        scratch_shapes=[
            pltpu.VMEM((1, 2 * B_HEADS * B_SUB), F32),
            pltpu.VMEM((B_HEADS, VEXT_ROWS, t), BF16),
        ],
        compiler_params=_params("parallel", "arbitrary"),
        name="mixer_diff",
    )(zabc, zabc, zabc, lam_params, gain)


def _ret_log_gammas():
    lg = [math.log1p(-(2.0 ** (-5.0 - h))) for h in range(C_HEADS)]
    return lg, lg[::-1]


def _lane_consts(vals, shape):
    head = lax.broadcasted_iota(jnp.int32, shape, len(shape) - 1) // HEAD_DIM
    out = jnp.full(shape, vals[-1], F32)
    for h in range(len(vals) - 2, -1, -1):
        out = jnp.where(head == h, vals[h], out)
    return out


def _ret_kernel(chunk, q_ref, k_ref, v_ref, g_ref, gain_ref, o_ref, acc_s, sf_s, sb_s, dm_s):
    t = q_ref.shape[1]
    c = chunk
    n = t // c
    lgf, lgb = _ret_log_gammas()
    qr_s = q_ref.at[0]
    kr_s = k_ref.at[0]

    ti = lax.broadcasted_iota(jnp.int32, (c, c), 0)
    si = lax.broadcasted_iota(jnp.int32, (c, c), 1)
    dist = (ti - si).astype(F32)
    lane_head = lax.broadcasted_iota(jnp.int32, (1, MIX_W), 1) // HEAD_DIM
    row = lax.broadcasted_iota(jnp.int32, (c, MIX_W), 0).astype(F32)
    lgf_l = _lane_consts(lgf, (c, MIX_W))
    lgb_l = _lane_consts(lgb, (c, MIX_W))
    qdec_f = jnp.exp(lgf_l * (row + 1.0))
    kdec_f = jnp.exp(lgf_l * (c - 1.0 - row))
    qdec_b = jnp.exp(lgb_l * (c - row))
    kdec_b = jnp.exp(lgb_l * row)
    r2 = lax.broadcasted_iota(jnp.int32, (MIX_W, MIX_W), 0) // HEAD_DIM
    c2 = lax.broadcasted_iota(jnp.int32, (MIX_W, MIX_W), 1) // HEAD_DIM
    same_head = r2 == c2
    gf_blk = jnp.where(same_head, jnp.exp(_lane_consts(lgf, (MIX_W, MIX_W)) * c), 0.0)
    gb_blk = jnp.where(same_head, jnp.exp(_lane_consts(lgb, (MIX_W, MIX_W)) * c), 0.0)

    head_mask = [(lane_head == h).astype(BF16) for h in range(C_HEADS)]
    for h in range(C_HEADS):
        dm_s[h] = jnp.where(dist > 0, jnp.exp(lgf[h] * dist),
                            jnp.where(dist < 0, jnp.exp(-lgb[h] * dist), 2.0))

    def intra(qc, kc, vc):
        ps = [(_dot_nt(qc * head_mask[h], kc) * dm_s[h]).astype(BF16) for h in range(C_HEADS)]
        v_stack = jnp.concatenate([vc * head_mask[h] for h in range(C_HEADS)], axis=0)
        return _dot(jnp.concatenate(ps, axis=1), v_stack)

    sf_s[...] = jnp.zeros_like(sf_s)
    sb_s[...] = jnp.zeros_like(sb_s)

    def fwd(i, carry):
        rows = pl.ds(pl.multiple_of(i * c, c), c)
        qc = qr_s[rows, :]
        kc = kr_s[rows, :]
        vc = v_ref[0, rows, :]
        o = intra(qc, kc, vc) + _dot((qc.astype(F32) * qdec_f).astype(BF16), sf_s[...].astype(BF16))
        acc_s[rows, :] = o
        kd = (kc.astype(F32) * kdec_f).T.astype(BF16)
        sf_s[...] = gf_blk * sf_s[...] + jnp.where(same_head, _dot(kd, vc), 0.0)
        return carry

    lax.fori_loop(0, n, fwd, 0)

    def bwd(i, carry):
        rows = pl.ds(pl.multiple_of((n - 1 - i) * c, c), c)
        qc = qr_s[rows, :].astype(F32)
        kc = kr_s[rows, :].astype(F32)
        vc = v_ref[0, rows, :]
        acc_s[rows, :] = acc_s[rows, :] + _dot((qc * qdec_b).astype(BF16), sb_s[...].astype(BF16))
        kd = (kc * kdec_b).T.astype(BF16)
        sb_s[...] = gb_blk * sb_s[...] + jnp.where(same_head, _dot(kd, vc), 0.0)
        return carry

    lax.fori_loop(0, n, bwd, 0)

    o = acc_s[...]
    o = o * lax.rsqrt(_head_sum(o * o, HEAD_DIM) * (1.0 / HEAD_DIM) + NORM_EPS) * gain_ref[...]
    g = g_ref[0].astype(F32)
    o_ref[0] = (o * (g * _sigmoid(g))).astype(BF16)


def _mixer_ret(zabc, gain, chunk):
    b, t, _ = zabc.shape
    blk = lambda j: pl.BlockSpec((1, t, 256), lambda bi: (bi, 0, j))
    return pl.pallas_call(
        functools.partial(_ret_kernel, chunk),
        grid=(b,),
        in_specs=[blk(5), blk(6), blk(7), blk(8),
                  pl.BlockSpec((1, 256), lambda bi: (0, 0))],
        out_specs=pl.BlockSpec((1, t, MIX_W), lambda bi: (bi, 0, 0)),
        out_shape=jax.ShapeDtypeStruct((b, t, MIX_W), BF16),
        scratch_shapes=[
            pltpu.VMEM((t, MIX_W), F32),
            pltpu.VMEM((MIX_W, MIX_W), F32),
            pltpu.VMEM((MIX_W, MIX_W), F32),
            pltpu.VMEM((C_HEADS, chunk, chunk), F32),
        ],
        compiler_params=_params("parallel"),
        name="mixer_ret",
    )(zabc, zabc, zabc, zabc, gain)


def _wkv_prep_kernel(z_ref, zp_ref, zn_ref, mup_ref, mun_ref, w0_ref, wupf_ref, wupb_ref, a0_ref, aup_ref,
                     gup1_ref, gup2_ref, kk_ref, ka_ref, rk_ref,
                     r_o, k_o, v_o, kk_o, b_o, ldf_o, ldb_o, g_o, bonus_o):
    i = pl.program_id(1)
    last = pl.num_programs(1) - 1
    z = z_ref[0]
    tm = z.shape[0]
    row = lax.broadcasted_iota(jnp.int32, z.shape, 0)
    prev_row = zp_ref[0, 0, 7:8, :] * jnp.where(i > 0, 1.0, 0.0)
    next_row = zn_ref[0, 0, 0:1, :] * jnp.where(i < last, 1.0, 0.0)
    z_prev = jnp.where(row == 0, prev_row, pltpu.roll(z, 1, 0))
    z_next = jnp.where(row == tm - 1, next_row, pltpu.roll(z, tm - 1, 0))
    mu_p, mu_n = mup_ref[...], mun_ref[...]
    u = z * (1.0 - mu_p - mu_n) + z_prev * mu_p + z_next * mu_n
    r = u[:, 0:256]
    k = u[:, 256:512]
    v = u[:, 512:768]
    wd = jnp.tanh(u[:, 768:896]).astype(BF16)
    ag = u[:, 896:1024]
    g2 = u[:, 1024:1088]

    def log_decay(w0, wup):
        return _sigmoid(w0 + _dot(wd, wup)) * (-math.exp(-0.5))

    ldf_o[0] = log_decay(w0_ref[0:1, :], wupf_ref[...])
    ldb_o[0] = log_decay(w0_ref[1:2, :], wupb_ref[...])
    a = _sigmoid(a0_ref[...] + _dot(ag.astype(BF16), aup_ref[...]))
    g_o[0] = (_dot(_sigmoid(ag).astype(BF16), gup1_ref[...])
              + _dot(_sigmoid(g2).astype(BF16), gup2_ref[...]))
    kk = k * kk_ref[...]
    kk = kk / jnp.maximum(jnp.sqrt(_head_sum(kk * kk, HEAD_DIM)), 1e-12)
    kh = k * (1.0 + (a - 1.0) * ka_ref[...])
    r_o[0] = r
    k_o[0] = kh
    v_o[0] = v
    kk_o[0] = kk
    b_o[0] = kk * a
    bonus_o[0] = _head_sum(r * kh * rk_ref[...], HEAD_DIM) * v


def _wkv_prep(zd, p, tm):
    b, t, _ = zd.shape
    zd8 = zd.reshape(b, t // 8, 8, D_COLS)
    r8 = tm // 8
    nb8 = t // 8
    row = lambda w: pl.BlockSpec((1, w), lambda bi, i: (0, 0))
    full = lambda a: pl.BlockSpec(a.shape, lambda bi, i: (0,) * a.ndim)
    out_spec = pl.BlockSpec((1, tm, MIX_W), lambda bi, i: (bi, i, 0))
    out_shape = jax.ShapeDtypeStruct((b, t, MIX_W), F32)
    return pl.pallas_call(
        _wkv_prep_kernel,
        grid=(b, t // tm),
        in_specs=[
            pl.BlockSpec((1, tm, D_COLS), lambda bi, i: (bi, i, 0)),
            pl.BlockSpec((1, 1, 8, D_COLS), lambda bi, i: (bi, jnp.maximum(i * r8 - 1, 0), 0, 0)),
            pl.BlockSpec((1, 1, 8, D_COLS), lambda bi, i: (bi, jnp.minimum((i + 1) * r8, nb8 - 1), 0, 0)),
            row(D_COLS), row(D_COLS),
            full(p["w0"]), full(p["wup_f"]), full(p["wup_b"]), row(256), full(p["aup"]),
            full(p["gup1"]), full(p["gup2"]), row(256), row(256), row(256),
        ],
        out_specs=[out_spec] * 9,
        out_shape=[out_shape] * 9,
        compiler_params=_params("parallel", "parallel"),
        name="wkv_prep",
    )(zd, zd8, zd8, p["mu_prev"], p["mu_next"], p["w0"], p["wup_f"], p["wup_b"], p["a0"], p["aup"],
      p["gup1"], p["gup2"], p["k_k"], p["k_a"], p["r_k"])


def _wkv_chunk_consts(chunk):
    c = chunk
    sc = D_HEADS * c
    ti = lax.broadcasted_iota(jnp.int32, (c, c), 0)
    si = lax.broadcasted_iota(jnp.int32, (c, c), 1)
    srow = lax.broadcasted_iota(jnp.int32, (sc, MIX_W), 0) // c
    slane = lax.broadcasted_iota(jnp.int32, (sc, MIX_W), 1) // HEAD_DIM
    gt = lax.broadcasted_iota(jnp.int32, (2 * sc, 2 * sc), 0)
    gs = lax.broadcasted_iota(jnp.int32, (2 * sc, 2 * sc), 1)
    t_in, s_in = gt % sc, gs % sc
    same = (t_in // c) == (s_in // c)
    upper = gt < sc
    st = lax.broadcasted_iota(jnp.int32, (sc, sc), 0)
    ss = lax.broadcasted_iota(jnp.int32, (sc, sc), 1)
    r2 = lax.broadcasted_iota(jnp.int32, (MIX_W, MIX_W), 0)
    c2 = lax.broadcasted_iota(jnp.int32, (MIX_W, MIX_W), 1)
    return {
        "tri": {False: jnp.where(si <= ti, 1.0, 0.0).astype(BF16), True: jnp.where(si >= ti, 1.0, 0.0).astype(BF16)},
        "hmask": srow == slane,
        "gmask": {False: same & ((s_in < t_in) | ((s_in == t_in) & ~upper)),
                  True: same & ((s_in > t_in) | ((s_in == t_in) & ~upper))},
        "eye": jnp.where(st == ss, 1.0, 0.0),
        "diag": r2 == c2,
    }


def _wkv_phase1(chunk, consts, probs):
    c = chunk
    sc = D_HEADS * c
    hmask = consts["hmask"]

    def stack(a):
        return jnp.where(hmask, jnp.concatenate([a] * D_HEADS, axis=0), 0.0).astype(BF16)

    pre = []
    for rev, r, kh, v, kk, bb, lw in probs:
        tri = consts["tri"][rev]
        l1, l2, l3 = _split3(lw)
        cl = _dot(tri, l1) + _dot(tri, l2) + _dot(tri, l3)
        tot = jnp.sum(lw, axis=0, keepdims=True)
        pre.append((cl, tot))
    yield
    ops = []
    for (rev, r, kh, v, kk, bb, lw), (cl, tot) in zip(probs, pre):
        w_inv = jnp.exp(-cl)
        w_end = jnp.exp(tot - cl)
        lhs = jnp.concatenate([stack(kk * jnp.exp(cl - lw)), stack(r * jnp.exp(cl))], axis=0)
        rhs = jnp.concatenate([stack(bb * w_inv), stack(kh * w_inv)], axis=0)
        kb = jnp.concatenate([stack(kh * w_end), stack(bb * w_end)], axis=0)
        wc = jnp.sum(jnp.where(consts["diag"], jnp.exp(tot), 0.0), axis=1, keepdims=True)
        wc = jnp.broadcast_to(wc, (MIX_W, MIX_W))
        ops.append((lhs, rhs, stack(v), kb, wc))
    yield
    gram = [jnp.where(consts["gmask"][p[0]], _dot_nt(o[0], o[1]), 0.0) for p, o in zip(probs, ops)]
    l_ab = [g[:sc, :sc] for g in gram]
    l_ak = [g[:sc, sc:].astype(BF16) for g in gram]
    m_r = [g[sc:, :].astype(BF16) for g in gram]
    yield

    steps = int(math.log2(c))
    pw = [(-l).astype(BF16) for l in l_ab]
    inv = [consts["eye"] - l for l in l_ab]
    w = [_dot(a, o[2]).astype(BF16) for a, o in zip(l_ak, ops)]
    for k in range(1, steps):
        if k == 1:
            pw = [_dot(p, p).astype(BF16) for p in pw]
            yield
        if k < steps - 1:
            nxt = [_dot(p, jnp.concatenate([p, i.astype(BF16)], axis=1)) for p, i in zip(pw, inv)]
            inv = [i + n[:, sc:] for i, n in zip(inv, nxt)]
            pw = [n[:, :sc].astype(BF16) for n in nxt]
        else:
            inv = [i + _dot(p, i.astype(BF16)) for p, i in zip(pw, inv)]
        yield
    inv = [i.astype(BF16) for i in inv]

    gu = [_dot(i, jnp.concatenate([o[0][:sc], wv], axis=1)) for i, o, wv in zip(inv, ops, w)]
    yield
    kb_t = [o[3].astype(F32).T.astype(BF16) for o in ops]
    mgu = [_dot(m[:, :sc], x.astype(BF16)) for m, x in zip(m_r, gu)]
    mv = [_dot(m[:, sc:], o[2]) for m, o in zip(m_r, ops)]
    yield
    return [{
        "qg": jnp.concatenate([(o[0][sc:].astype(F32) - mg[:, :MIX_W]).astype(BF16),
                               g[:, :MIX_W].astype(BF16)], axis=0),
        "y_hat": a - mg[:, MIX_W:],
        "u_hat": g[:, MIX_W:],
        "kb_t": kt,
        "v": o[2],
        "wc": o[4],
    } for o, g, mg, a, kt in zip(ops, gu, mgu, mv, kb_t)]


def _wkv_scan_kernel(chunk, group, rf, kf, vf, kkf, bf, lf, rb, kb, vb, kkb, bb, lb, yf_o, yb_o,
                     xf_s, xb_s, qg_s, yh_s, uh_s, kbt_s, v_s, wc_s):
    @pl.when(pl.program_id(1) == 0)
    def _():
        xf_s[...] = jnp.zeros_like(xf_s)
        xb_s[...] = jnp.zeros_like(xb_s)

    tm = rf.shape[1]
    n = tm // chunk
    sc = D_HEADS * chunk
    consts = _wkv_chunk_consts(chunk)
    sol_refs = {"qg": qg_s, "y_hat": yh_s, "u_hat": uh_s, "kb_t": kbt_s, "v": v_s, "wc": wc_s}

    def group_rows(g):
        rows = []
        for u in range(group):
            rows.append(pl.ds(pl.multiple_of((g * group + u) * chunk, chunk), chunk))
            rows.append(pl.ds(pl.multiple_of((n - 1 - g * group - u) * chunk, chunk), chunk))
        return rows

    def problems(g):
        probs = []
        for slot, rows in enumerate(group_rows(g)):
            if slot % 2:
                probs.append((True, rb[0, rows, :], kb[0, rows, :], vb[0, rows, :], kkb[0, rows, :],
                              bb[0, rows, :], lb[0, rows, :]))
            else:
                probs.append((False, rf[0, rows, :], kf[0, rows, :], vf[0, rows, :], kkf[0, rows, :],
                              bf[0, rows, :], lf[0, rows, :]))
        return probs

    def advance(slot, rows):
        x_s, y_o = (xb_s, yb_o) if slot % 2 else (xf_s, yf_o)
        x = x_s[...]
        res = _dot(qg_s[slot], x.astype(BF16))
        ys = res[:sc] + yh_s[slot]
        y = ys[0:chunk]
        for h in range(1, D_HEADS):
            y = y + ys[h * chunk:(h + 1) * chunk]
        y_o[0, rows, :] = y
        u = -(res[sc:] + uh_s[slot])
        x_s[...] = x * wc_s[slot] + _dot(kbt_s[slot], jnp.concatenate([v_s[slot], u.astype(BF16)], axis=0))

    def solve(g, pending):
        gen = _wkv_phase1(chunk, consts, problems(g))
        pending = list(pending)
        while True:
            try:
                next(gen)
            except StopIteration as done:
                sols = done.value
                break
            for step in pending[:2]:
                step()
            pending = pending[2:]
        for step in pending:
            step()
        for slot, sol in enumerate(sols):
            for name, ref in sol_refs.items():
                ref[slot] = sol[name]

    def steps_for(g):
        return [functools.partial(advance, slot, r) for slot, r in enumerate(group_rows(g))]

    n_groups = n // group
    solve(0, [])

    for g in range(1, n_groups):
        solve(g, steps_for(g - 1))
    for step in steps_for(n_groups - 1):
        step()


def _wkv_scan(r, kh, v, kk, bb, ldf, ldb, tm, chunk, group):
    b, t, _ = r.shape
    nt = t // tm
    sc = D_HEADS * chunk
    slots = 2 * group
    fspec = pl.BlockSpec((1, tm, MIX_W), lambda bi, i: (bi, i, 0))
    bspec = pl.BlockSpec((1, tm, MIX_W), lambda bi, i: (bi, nt - 1 - i, 0))
    shape = jax.ShapeDtypeStruct((b, t, MIX_W), F32)
    return pl.pallas_call(
        functools.partial(_wkv_scan_kernel, chunk, group),
        grid=(b, nt),
        in_specs=[fspec] * 6 + [bspec] * 6,
        out_specs=[fspec, bspec],
        out_shape=[shape, shape],
        scratch_shapes=[
            pltpu.VMEM((MIX_W, MIX_W), F32), pltpu.VMEM((MIX_W, MIX_W), F32),
            pltpu.VMEM((slots, 2 * sc, MIX_W), BF16),
            pltpu.VMEM((slots, sc, MIX_W), F32),
            pltpu.VMEM((slots, sc, MIX_W), F32),
            pltpu.VMEM((slots, MIX_W, 2 * sc), BF16),
            pltpu.VMEM((slots, sc, MIX_W), BF16),
            pltpu.VMEM((slots, MIX_W, MIX_W), F32),
        ],
        compiler_params=_params("parallel", "arbitrary"),
        name="wkv_scan",
    )(r, kh, v, kk, bb, ldf, r, kh, v, kk, bb, ldb)


def _wkv_out(yf, yb, bonus, g, gn_w, gn_b):
    y = yf + yb
    mean = _head_sum(y, HEAD_DIM) * (1.0 / HEAD_DIM)
    yc = y - mean
    var = _head_sum(yc * yc, HEAD_DIM) * (1.0 / HEAD_DIM)
    yn = yc * lax.rsqrt(var + WKV_GN_EPS) * gn_w + gn_b
    return ((yn + bonus) * g).astype(BF16)


def _post_kernel(x_ref, oa_ref, ob_ref, oc_ref, yf_ref, yb_ref, bonus_ref, g_ref, gnw_ref, gnb_ref,
                 wo_ref, gpost_ref, gpre_ref, wg_ref, wu_ref, wd_ref, gfpost_ref, y_ref):
    tm = x_ref.shape[0]
    groups = [slice(s, s + tm // ROW_GROUPS) for s in range(0, tm, tm // ROW_GROUPS)]
    o_d = [_wkv_out(yf_ref[r], yb_ref[r], bonus_ref[r], g_ref[r], gnw_ref[...], gnb_ref[...]) for r in groups]
    mix = [_dot(oa_ref[r], wo_ref[0]) + _dot(ob_ref[r], wo_ref[1]) + _dot(oc_ref[r], wo_ref[2])
           + _dot(od, wo_ref[3]) for r, od in zip(groups, o_d)]
    x = [x_ref[r] + _rms_rows(m, gpost_ref[...]) for r, m in zip(groups, mix)]
    h = [_rms_rows(xg, gpre_ref[...]).astype(BF16) for xg in x]
    gate = [_dot(hg, wg_ref[...]) for hg in h]
    up = [_dot(hg, wu_ref[...]) for hg in h]
    act = [(g * _sigmoid(g) * u).astype(BF16) for g, u in zip(gate, up)]
    f = [_dot(a, wd_ref[...]) for a in act]
    for r, xg, fg in zip(groups, x, f):
        y_ref[r] = xg + _rms_rows(fg, gfpost_ref[...])


def _post(x2, oa, ob, oc, wkv_parts, gn_w, gn_b, w_out4, g_post, g_pre, wg, wu, wd, gf_post, tm):
    n, d = x2.shape
    dff = wg.shape[1]
    tok = lambda w: pl.BlockSpec((tm, w), lambda i: (i, 0))
    row = pl.BlockSpec((1, d), lambda i: (0, 0))
    mrow = pl.BlockSpec((1, MIX_W), lambda i: (0, 0))
    return pl.pallas_call(
        _post_kernel,
        grid=(n // tm,),
        in_specs=[tok(d)] + [tok(MIX_W)] * 7 + [mrow, mrow,
                  pl.BlockSpec((4, MIX_W, d), lambda i: (0, 0, 0)), row, row,
                  pl.BlockSpec((d, dff), lambda i: (0, 0)),
                  pl.BlockSpec((d, dff), lambda i: (0, 0)),
                  pl.BlockSpec((dff, d), lambda i: (0, 0)), row],
        out_specs=tok(d),
        out_shape=jax.ShapeDtypeStruct((n, d), F32),
        compiler_params=_params("parallel"),
        name="out_proj_ffn",
    )(x2, oa, ob, oc, *wkv_parts, gn_w, gn_b, w_out4, g_post, g_pre, wg, wu, wd, gf_post)


def _angles(pos, rot_dim, theta):
    inv = theta ** (-jnp.arange(0, rot_dim, 2, dtype=F32) / rot_dim)
    return pos.astype(F32)[:, None] * inv[None, :]


def _rope_tables(t):
    rows = t // GRID_W
    row_idx = jnp.repeat(jnp.arange(rows), GRID_W)
    col_idx = jnp.tile(jnp.arange(GRID_W), rows)
    pos = jnp.arange(t)
    ar = _angles(row_idx, HEAD_DIM // 2, A_THETA)
    ac = _angles(col_idx, HEAD_DIM // 2, A_THETA)
    cos_a = jnp.concatenate([jnp.cos(ar), jnp.cos(ar), jnp.cos(ac), jnp.cos(ac)], axis=1)
    sin_a = jnp.concatenate([-jnp.sin(ar), jnp.sin(ar), -jnp.sin(ac), jnp.sin(ac)], axis=1)
    tabs_a = (jnp.tile(cos_a, (1, 4)), jnp.tile(sin_a, (1, 4)))
    ab = _angles(pos, B_ROT, B_THETA)
    pad1 = jnp.ones((t, B_SUB - B_ROT), F32)
    pad0 = jnp.zeros((t, B_SUB - B_ROT), F32)
    cos_b = jnp.concatenate([jnp.cos(ab), jnp.cos(ab), pad1], axis=1)
    sin_b = jnp.concatenate([-jnp.sin(ab), jnp.sin(ab), pad0], axis=1)
    tabs_b = (jnp.tile(cos_b, (1, 8)), jnp.tile(sin_b, (1, 8)))
    ang_c = _angles(pos, HEAD_DIM, C_THETA)
    cos_c = jnp.concatenate([jnp.cos(ang_c), jnp.cos(ang_c)], axis=1)
    sin_c = jnp.concatenate([-jnp.sin(ang_c), jnp.sin(ang_c)], axis=1)
    tabs_c = (jnp.tile(cos_c, (1, 4)), jnp.tile(sin_c, (1, 4)))
    return tabs_a, tabs_b, tabs_c


def _tile(n, pref):
    while n % pref:
        pref //= 2
    return pref


def _layer_weights(l, w):
    d_model = w["w_in"].shape[1]
    row = lambda a: a.reshape(1, -1).astype(F32)
    zeros = lambda r: jnp.zeros((r, MIX_W), F32)
    w_lora = w["d_w_up"].shape[2]
    return {
        "g_mix_pre": row(w["norm_mix_pre"][l]), "g_mix_post": row(w["norm_mix_post"][l]),
        "g_ffn_pre": row(w["norm_ffn_pre"][l]), "g_ffn_post": row(w["norm_ffn_post"][l]),
        "w_abc": w["w_in"][l][:, :ABC_COLS].astype(BF16), "w_d": w["w_in"][l][:, ABC_COLS:].astype(BF16),
        "w_out4": w["w_out"][l].reshape(4, MIX_W, d_model).astype(BF16),
        "a_gain": jnp.concatenate([jnp.tile(row(w["a_q_gain"][l]), (1, A_HEADS)),
                                   jnp.tile(row(w["a_k_gain"][l]), (1, A_KV_HEADS))], axis=1),
        "b_lambda": w["b_lambda"][l].astype(F32), "b_gain": w["b_subln_gain"][l].reshape(-1, 1).astype(F32),
        "c_gain": row(w["c_gn_gain"][l]),
        "wkv": {
            "mu_prev": row(w["d_mu_prev"][l]), "mu_next": row(w["d_mu_next"][l]),
            "w0": w["d_w0"][l].astype(F32),
            "wup_f": jnp.concatenate([w["d_w_up"][l, 0], zeros(w_lora)], axis=0).astype(BF16),
            "wup_b": jnp.concatenate([zeros(w_lora), w["d_w_up"][l, 1]], axis=0).astype(BF16),
            "a0": row(w["d_a0"][l]),
            "aup": jnp.concatenate([w["d_a_up"][l], zeros(64)], axis=0).astype(BF16),
            "gup1": jnp.concatenate([zeros(64), w["d_g_up"][l][:64]], axis=0).astype(BF16),
            "gup2": w["d_g_up"][l][64:].astype(BF16),
            "k_k": row(w["d_k_k"][l]), "k_a": row(w["d_k_a"][l]), "r_k": row(w["d_r_k"][l]),
        },
        "gn_w": row(w["d_gn_w"][l]), "gn_b": row(w["d_gn_b"][l]),
        "wg": w["ffn_w_gate"][l].astype(BF16), "wu": w["ffn_w_up"][l].astype(BF16),
        "wd": w["ffn_w_down"][l].astype(BF16),
    }


def _trunk(x, layers):
    b, t, d = x.shape
    n = b * t
    tabs = _rope_tables(t)
    tm_proj = _tile(t, 512)
    tm_post = _tile(n, 512)
    tq = _tile(t, 256)
    tm_wkv = _tile(t, 256)
    tm_scan = _tile(t, 1024)
    chunk = 32
    x2 = x.reshape(n, d)
    for l, p in enumerate(layers):
        lam_init = 0.8 - 0.6 * math.exp(-0.3 * l)
        zabc, zd = _in_proj(x2, p["g_mix_pre"], p["w_abc"], p["w_d"], tabs, p["a_gain"], t, tm_proj)
        zabc = zabc.reshape(b, t, ABC_COLS)
        zd = zd.reshape(b, t, D_COLS)
        o_a = _mixer_gqa(zabc, tq)
        o_b = _mixer_diff(zabc, p["b_lambda"], p["b_gain"], lam_init, tq)
        o_c = _mixer_ret(zabc, p["c_gain"], _tile(t, 256))
        r, kh, v, kk, bb, ldf, ldb, g, bonus = _wkv_prep(zd, p["wkv"], tm_wkv)
        yf, yb = _wkv_scan(r, kh, v, kk, bb, ldf, ldb, tm_scan, chunk, min(4, tm_scan // chunk))
        wkv_parts = [a.reshape(n, MIX_W) for a in (yf, yb, bonus, g)]
        x2 = _post(x2, o_a.reshape(n, MIX_W), o_b.reshape(n, MIX_W), o_c.reshape(n, MIX_W), wkv_parts,
                   p["gn_w"], p["gn_b"], p["w_out4"], p["g_mix_post"], p["g_ffn_pre"], p["wg"], p["wu"], p["wd"],
                   p["g_ffn_post"], tm_post)
    return x2.reshape(b, t, d)


def kernel(x_prompt, x_sample, norm_mix_pre, norm_mix_post, norm_ffn_pre, norm_ffn_post, w_in, w_out,
           a_q_gain, a_k_gain, b_lambda, b_subln_gain, c_gn_gain, d_mu_prev, d_mu_next, d_w0, d_w_up,
           d_a0, d_a_up, d_g_up, d_k_k, d_k_a, d_r_k, d_gn_w, d_gn_b, ffn_w_gate, ffn_w_up, ffn_w_down):
    w = {
        "norm_mix_pre": norm_mix_pre, "norm_mix_post": norm_mix_post,
        "norm_ffn_pre": norm_ffn_pre, "norm_ffn_post": norm_ffn_post,
        "w_in": w_in, "w_out": w_out, "a_q_gain": a_q_gain, "a_k_gain": a_k_gain,
        "b_lambda": b_lambda, "b_subln_gain": b_subln_gain, "c_gn_gain": c_gn_gain,
        "d_mu_prev": d_mu_prev, "d_mu_next": d_mu_next, "d_w0": d_w0, "d_w_up": d_w_up,
        "d_a0": d_a0, "d_a_up": d_a_up, "d_g_up": d_g_up, "d_k_k": d_k_k, "d_k_a": d_k_a,
        "d_r_k": d_r_k, "d_gn_w": d_gn_w, "d_gn_b": d_gn_b,
        "ffn_w_gate": ffn_w_gate, "ffn_w_up": ffn_w_up, "ffn_w_down": ffn_w_down,
    }
    layers = [_layer_weights(l, w) for l in range(w_in.shape[0])]
    return (_trunk(x_prompt, layers), _trunk(x_sample, layers))
```

```python
import functools
import math

import jax
import jax.numpy as jnp
from jax import lax
from jax.experimental import pallas as pl
from jax.experimental.pallas import tpu as pltpu

F32 = jnp.float32
BF16 = jnp.bfloat16

HEAD_DIM = 64
GRID_W = 64
NORM_EPS = 1e-6
A_HEADS, A_KV_HEADS, A_THETA = 4, 2, 10000.0
B_HEADS, B_SUB, B_ROT, B_THETA = 4, 32, 8, 500000.0
C_HEADS, C_THETA = 4, 10000.0
D_HEADS = 4
WKV_GN_EPS = 64e-5
MIX_W = 256
ABC_COLS = 2304
A_Q, A_K, A_V = slice(0, 256), slice(256, 384), slice(384, 512)
B_Q, B_K, B_V = slice(512, 768), slice(768, 1024), slice(1024, 1280)
C_Q, C_K, C_VG = slice(1280, 1536), slice(1536, 1792), slice(1792, 2304)
D_COLS = 1088
BF16_SUBLANES = 16
VEXT_ROWS = HEAD_DIM + BF16_SUBLANES
LOG2E = math.log2(math.e)
KEY_CHUNK = 512
ROW_GROUPS = 2
SHIFT_LIMIT = 100.0
VMEM_LIMIT = 56 * 1024 * 1024


def _params(*sem):
    return pltpu.CompilerParams(dimension_semantics=sem, vmem_limit_bytes=VMEM_LIMIT)


def _rms_rows(x, gain):
    return x * lax.rsqrt(jnp.mean(x * x, axis=-1, keepdims=True) + NORM_EPS) * gain


def _split2(x):
    hi = x.astype(BF16)
    lo = (x - hi.astype(F32)).astype(BF16)
    return hi, lo


def _split3(x):
    h1 = x.astype(BF16)
    r1 = x - h1.astype(F32)
    h2 = r1.astype(BF16)
    h3 = (r1 - h2.astype(F32)).astype(BF16)
    return h1, h2, h3


def _head_sum(x, seg):
    w = x.shape[-1]
    r = lax.broadcasted_iota(jnp.int32, (w, w), 0) // seg
    c = lax.broadcasted_iota(jnp.int32, (w, w), 1) // seg
    bd = jnp.where(r == c, 1.0, 0.0).astype(BF16)
    hi, lo = _split2(x)
    return (jnp.dot(hi, bd, preferred_element_type=F32)
            + jnp.dot(lo, bd, preferred_element_type=F32))


def _rope(x, cos, sin, half):
    w = x.shape[-1]
    lane = lax.broadcasted_iota(jnp.int32, x.shape, 1)
    nxt = pltpu.roll(x, w - half, 1)
    prv = pltpu.roll(x, half, 1)
    return x * cos + jnp.where((lane % (2 * half)) < half, nxt, prv) * sin


def _sigmoid(x):
    return 0.5 * jnp.tanh(0.5 * x) + 0.5


def _dot(a, b):
    return jnp.dot(a, b, preferred_element_type=F32)


def _dot_nt(a, b):
    return lax.dot_general(a, b, (((1,), (1,)), ((), ())), preferred_element_type=F32)


def _store_qk_prepared(z, r, tabs, gain_a, out_ref):
    (cos_a, sin_a), (cos_b, sin_b), (cos_c, sin_c) = tabs

    def head_norm(x, gain):
        return x * lax.rsqrt(_head_sum(x * x, HEAD_DIM) * (1.0 / HEAD_DIM) + NORM_EPS) * gain

    a_qk = head_norm(z[:, A_Q.start:A_K.stop], gain_a)
    kw = A_K.stop - A_K.start
    pieces = {
        A_Q: _rope(a_qk[:, :A_Q.stop], cos_a, sin_a, HEAD_DIM // 4) * (HEAD_DIM ** -0.5 * LOG2E),
        A_K: _rope(a_qk[:, A_Q.stop:], cos_a[:, :kw], sin_a[:, :kw], HEAD_DIM // 4),
        B_Q: _rope(z[:, B_Q], cos_b, sin_b, B_ROT // 2) * (B_SUB ** -0.5 * LOG2E),
        B_K: _rope(z[:, B_K], cos_b, sin_b, B_ROT // 2),
        C_Q: _rope(z[:, C_Q], cos_c, sin_c, HEAD_DIM // 2),
        C_K: _rope(z[:, C_K], cos_c, sin_c, HEAD_DIM // 2) * (HEAD_DIM ** -0.5),
    }
    for cols in (A_V, B_V, C_VG):
        pieces[cols] = z[:, cols]
    for cols, val in pieces.items():
        out_ref[r, cols] = val.astype(BF16)


def _in_proj_kernel(x_ref, g_ref, wabc_ref, wd_ref, ca_ref, sa_ref, cb_ref, sb_ref, cc_ref, sc_ref, ga_ref,
                    zabc_ref, zd_ref):
    tm = x_ref.shape[0]
    groups = [slice(s, s + tm // ROW_GROUPS) for s in range(0, tm, tm // ROW_GROUPS)]
    h = [_rms_rows(x_ref[r], g_ref[...]).astype(BF16) for r in groups]
    z = [_dot(hg, wabc_ref[...]) for hg in h]
    for r, hg in zip(groups, h):
        zd_ref[r] = _dot(hg, wd_ref[...])
    for r, zg in zip(groups, z):
        tabs = ((ca_ref[r], sa_ref[r]), (cb_ref[r], sb_ref[r]), (cc_ref[r], sc_ref[r]))
        _store_qk_prepared(zg, r, tabs, ga_ref[...], zabc_ref)


def _in_proj(x2, gain, w_abc, w_d, tabs, gain_a, t, tm):
    n, d = x2.shape
    per_seq = t // tm
    tab = pl.BlockSpec((tm, MIX_W), lambda i: (i % per_seq, 0))
    return pl.pallas_call(
        _in_proj_kernel,
        grid=(n // tm,),
        in_specs=[
            pl.BlockSpec((tm, d), lambda i: (i, 0)),
            pl.BlockSpec((1, d), lambda i: (0, 0)),
            pl.BlockSpec((d, ABC_COLS), lambda i: (0, 0)),
            pl.BlockSpec((d, D_COLS), lambda i: (0, 0)),
            tab, tab, tab, tab, tab, tab,
            pl.BlockSpec((1, A_K.stop), lambda i: (0, 0)),
        ],
        out_specs=[
            pl.BlockSpec((tm, ABC_COLS), lambda i: (i, 0)),
            pl.BlockSpec((tm, D_COLS), lambda i: (i, 0)),
        ],
        out_shape=[
            jax.ShapeDtypeStruct((n, ABC_COLS), BF16),
            jax.ShapeDtypeStruct((n, D_COLS), F32),
        ],
        compiler_params=_params("parallel"),
        name="in_proj",
    )(x2, gain, w_abc, w_d, *tabs[0], *tabs[1], *tabs[2], gain_a)


def _value_ext_t(v_t):
    row = lax.broadcasted_iota(jnp.int32, (VEXT_ROWS - HEAD_DIM, v_t.shape[1]), 0)
    return jnp.concatenate([v_t, jnp.where(row == 0, 1.0, 0.0)], axis=0).astype(BF16)


def _attend_exact(k_ref, q_ts, vx_refs):
    outs = []
    for q_t, vx in zip(q_ts, vx_refs):
        s_t = _dot(k_ref[...], q_t)
        p_t = jnp.exp2(s_t - jnp.max(s_t, axis=0, keepdims=True)).astype(BF16)
        o = _dot(vx[...], p_t)
        outs.append(o[:HEAD_DIM] / o[HEAD_DIM:HEAD_DIM + 1])
    return outs


def _attend_shifted(k_ref, q_ts, shifts, vx_refs):
    t = k_ref.shape[0]
    ck = min(t, KEY_CHUNK)
    steps = [(j, slice(c * ck, (c + 1) * ck)) for j in range(len(q_ts)) for c in range(t // ck)]
    acc = [None] * len(q_ts)
    s_prev = p_prev = None
    for i in range(len(steps) + 2):
        s_new = p_new = None
        if i < len(steps):
            j, keys = steps[i]
            s_new = _dot(k_ref[keys, :], q_ts[j])
        if 1 <= i <= len(steps):
            p_new = jnp.exp2(s_prev - shifts[steps[i - 1][0]]).astype(BF16)
        if i >= 2:
            j, keys = steps[i - 2]
            o = _dot(vx_refs[j][:, keys], p_prev)
            acc[j] = o if acc[j] is None else acc[j] + o
        s_prev, p_prev = s_new, p_new
    return [a[:HEAD_DIM] / a[HEAD_DIM:HEAD_DIM + 1] for a in acc]


def _attend_t(k_ref, kn_ref, q_t, head_rows, head_keys, vx_refs, finish):
    c = k_ref.shape[1]
    tq = q_t.shape[1]
    qsq = q_t * q_t
    q_ts, shifts = [], []
    for rows, keys in zip(head_rows, head_keys):
        pieces = [jnp.zeros((keys.start, tq), F32), q_t[rows], jnp.zeros((c - keys.stop, tq), F32)]
        q_ts.append(jnp.concatenate([x for x in pieces if x.shape[0]], axis=0).astype(BF16))
        qn = jnp.sqrt(jnp.sum(qsq[rows], axis=0, keepdims=True))
        shifts.append(qn * kn_ref[0:1, keys.start:keys.start + 1])
    worst = jnp.max(jnp.concatenate(shifts, axis=0))

    @pl.when(2.0 * worst < SHIFT_LIMIT)
    def _():
        finish(_attend_shifted(k_ref, q_ts, shifts, vx_refs))

    @pl.when(jnp.logical_not(2.0 * worst < SHIFT_LIMIT))
    def _():
        finish(_attend_exact(k_ref, q_ts, vx_refs))


def _gqa_kernel(q_ref, k_ref, v_ref, o_ref, kn_s, vx_s):
    @pl.when(pl.program_id(1) == 0)
    def _():
        k = k_ref[0].astype(F32)
        kn_s[...] = jnp.sqrt(jnp.max(_head_sum(k * k, HEAD_DIM), axis=0, keepdims=True))
        v_t = v_ref[0].astype(F32).T
        for g in range(A_KV_HEADS):
            vx_s[g] = _value_ext_t(v_t[g * HEAD_DIM:(g + 1) * HEAD_DIM])

    q_t = q_ref[0].astype(F32).T
    k_s = k_ref.at[0]
    group = A_HEADS // A_KV_HEADS
    head_rows = [slice(h * HEAD_DIM, (h + 1) * HEAD_DIM) for h in range(A_HEADS)]
    head_keys = [slice((h // group) * HEAD_DIM, (h // group + 1) * HEAD_DIM) for h in range(A_HEADS)]

    def finish(outs):
        o_ref[0] = jnp.concatenate(outs, axis=0).T.astype(BF16)

    _attend_t(k_s, kn_s, q_t, head_rows, head_keys, [vx_s.at[h // group] for h in range(A_HEADS)], finish)


def _mixer_gqa(zabc, tq):
    b, t, _ = zabc.shape
    return pl.pallas_call(
        _gqa_kernel,
        grid=(b, t // tq),
        in_specs=[
            pl.BlockSpec((1, tq, 256), lambda bi, i: (bi, i, 0)),
            pl.BlockSpec((1, t, 128), lambda bi, i: (bi, 0, 2)),
            pl.BlockSpec((1, t, 128), lambda bi, i: (bi, 0, 3)),
        ],
        out_specs=pl.BlockSpec((1, tq, MIX_W), lambda bi, i: (bi, i, 0)),
        out_shape=jax.ShapeDtypeStruct((b, t, MIX_W), BF16),
        scratch_shapes=[
            pltpu.VMEM((1, A_KV_HEADS * HEAD_DIM), F32),
            pltpu.VMEM((A_KV_HEADS, VEXT_ROWS, t), BF16),
        ],
        compiler_params=_params("parallel", "arbitrary"),
        name="mixer_gqa",
    )(zabc, zabc, zabc)


def _diff_kernel(lam_init, q_ref, k_ref, v_ref, lam_ref, gain_ref, o_ref, kn_s, vx_s):
    @pl.when(pl.program_id(1) == 0)
    def _():
        k = k_ref[0].astype(F32)
        kn_s[...] = jnp.sqrt(jnp.max(_head_sum(k * k, B_SUB), axis=0, keepdims=True))
        v_t = v_ref[0].astype(F32).T
        for h in range(B_HEADS):
            vx_s[h] = _value_ext_t(v_t[h * HEAD_DIM:(h + 1) * HEAD_DIM])

    lp = lam_ref[...]
    lam = (jnp.exp(jnp.sum(lp[0:1] * lp[1:2], axis=1, keepdims=True))
           - jnp.exp(jnp.sum(lp[2:3] * lp[3:4], axis=1, keepdims=True)) + lam_init)
    q_t = q_ref[0].astype(F32).T
    k_s = k_ref.at[0]
    sub_rows = [slice(j * B_SUB, (j + 1) * B_SUB) for j in range(2 * B_HEADS)]

    def finish(parts):
        outs = []
        for h in range(B_HEADS):
            o = parts[2 * h] - lam * parts[2 * h + 1]
            o = o * lax.rsqrt(jnp.mean(o * o, axis=0, keepdims=True) + NORM_EPS) * gain_ref[...]
            outs.append(o * (1.0 - lam_init))
        o_ref[0] = jnp.concatenate(outs, axis=0).T.astype(BF16)

    _attend_t(k_s, kn_s, q_t, sub_rows, sub_rows, [vx_s.at[j // 2] for j in range(2 * B_HEADS)], finish)


def _mixer_diff(zabc, lam_params, gain, lam_init, tq):
    b, t, _ = zabc.shape
    return pl.pallas_call(
        functools.partial(_diff_kernel, lam_init),
        grid=(b, t // tq),
        in_specs=[
            pl.BlockSpec((1, tq, 256), lambda bi, i: (bi, i, 2)),
            pl.BlockSpec((1, t, 256), lambda bi, i: (bi, 0, 3)),
            pl.BlockSpec((1, t, 256), lambda bi, i: (bi, 0, 4)),
            pl.BlockSpec((4, B_SUB), lambda bi, i: (0, 0)),
            pl.BlockSpec((HEAD_DIM, 1), lambda bi, i: (0, 0)),
        ],
        out_specs=pl.BlockSpec((1, tq, MIX_W), lambda bi, i: (bi, i, 0)),
        out_shape=jax.ShapeDtypeStruct((b, t, MIX_W), BF16),
        scratch_shapes=[
            pltpu.VMEM((1, 2 * B_HEADS * B_SUB), F32),
            pltpu.VMEM((B_HEADS, VEXT_ROWS, t), BF16),
        ],
        compiler_params=_params("parallel", "arbitrary"),
        name="mixer_diff",
    )(zabc, zabc, zabc, lam_params, gain)


def _ret_log_gammas():
    lg = [math.log1p(-(2.0 ** (-5.0 - h))) for h in range(C_HEADS)]
    return lg, lg[::-1]


def _lane_consts(vals, shape):
    head = lax.broadcasted_iota(jnp.int32, shape, len(shape) - 1) // HEAD_DIM
    out = jnp.full(shape, vals[-1], F32)
    for h in range(len(vals) - 2, -1, -1):
        out = jnp.where(head == h, vals[h], out)
    return out


def _ret_kernel(chunk, q_ref, k_ref, v_ref, g_ref, gain_ref, o_ref, acc_s, sf_s, sb_s, dm_s):
    t = q_ref.shape[1]
    c = chunk
    n = t // c
    lgf, lgb = _ret_log_gammas()
    qr_s = q_ref.at[0]
    kr_s = k_ref.at[0]

    ti = lax.broadcasted_iota(jnp.int32, (c, c), 0)
    si = lax.broadcasted_iota(jnp.int32, (c, c), 1)
    dist = (ti - si).astype(F32)
    lane_head = lax.broadcasted_iota(jnp.int32, (1, MIX_W), 1) // HEAD_DIM
    row = lax.broadcasted_iota(jnp.int32, (c, MIX_W), 0).astype(F32)
    lgf_l = _lane_consts(lgf, (c, MIX_W))
    lgb_l = _lane_consts(lgb, (c, MIX_W))
    qdec_f = jnp.exp(lgf_l * (row + 1.0))
    kdec_f = jnp.exp(lgf_l * (c - 1.0 - row))
    qdec_b = jnp.exp(lgb_l * (c - row))
    kdec_b = jnp.exp(lgb_l * row)
    r2 = lax.broadcasted_iota(jnp.int32, (MIX_W, MIX_W), 0) // HEAD_DIM
    c2 = lax.broadcasted_iota(jnp.int32, (MIX_W, MIX_W), 1) // HEAD_DIM
    same_head = r2 == c2
    gf_blk = jnp.where(same_head, jnp.exp(_lane_consts(lgf, (MIX_W, MIX_W)) * c), 0.0)
    gb_blk = jnp.where(same_head, jnp.exp(_lane_consts(lgb, (MIX_W, MIX_W)) * c), 0.0)

    head_mask = [(lane_head == h).astype(BF16) for h in range(C_HEADS)]
    for h in range(C_HEADS):
        dm_s[h] = jnp.where(dist > 0, jnp.exp(lgf[h] * dist),
                            jnp.where(dist < 0, jnp.exp(-lgb[h] * dist), 2.0))

    def intra(qc, kc, vc):
        ps = [(_dot_nt(qc * head_mask[h], kc) * dm_s[h]).astype(BF16) for h in range(C_HEADS)]
        v_stack = jnp.concatenate([vc * head_mask[h] for h in range(C_HEADS)], axis=0)
        return _dot(jnp.concatenate(ps, axis=1), v_stack)

    sf_s[...] = jnp.zeros_like(sf_s)
    sb_s[...] = jnp.zeros_like(sb_s)

    def fwd(i, carry):
        rows = pl.ds(pl.multiple_of(i * c, c), c)
        qc = qr_s[rows, :]
        kc = kr_s[rows, :]
        vc = v_ref[0, rows, :]
        o = intra(qc, kc, vc) + _dot((qc.astype(F32) * qdec_f).astype(BF16), sf_s[...].astype(BF16))
        acc_s[rows, :] = o
        kd = (kc.astype(F32) * kdec_f).T.astype(BF16)
        sf_s[...] = gf_blk * sf_s[...] + jnp.where(same_head, _dot(kd, vc), 0.0)
        return carry

    lax.fori_loop(0, n, fwd, 0)

    def bwd(i, carry):
        rows = pl.ds(pl.multiple_of((n - 1 - i) * c, c), c)
        qc = qr_s[rows, :].astype(F32)
        kc = kr_s[rows, :].astype(F32)
        vc = v_ref[0, rows, :]
        acc_s[rows, :] = acc_s[rows, :] + _dot((qc * qdec_b).astype(BF16), sb_s[...].astype(BF16))
        kd = (kc * kdec_b).T.astype(BF16)
        sb_s[...] = gb_blk * sb_s[...] + jnp.where(same_head, _dot(kd, vc), 0.0)
        return carry

    lax.fori_loop(0, n, bwd, 0)

    o = acc_s[...]
    o = o * lax.rsqrt(_head_sum(o * o, HEAD_DIM) * (1.0 / HEAD_DIM) + NORM_EPS) * gain_ref[...]
    g = g_ref[0].astype(F32)
    o_ref[0] = (o * (g * _sigmoid(g))).astype(BF16)


def _mixer_ret(zabc, gain, chunk):
    b, t, _ = zabc.shape
    blk = lambda j: pl.BlockSpec((1, t, 256), lambda bi: (bi, 0, j))
    return pl.pallas_call(
        functools.partial(_ret_kernel, chunk),
        grid=(b,),
        in_specs=[blk(5), blk(6), blk(7), blk(8),
                  pl.BlockSpec((1, 256), lambda bi: (0, 0))],
        out_specs=pl.BlockSpec((1, t, MIX_W), lambda bi: (bi, 0, 0)),
        out_shape=jax.ShapeDtypeStruct((b, t, MIX_W), BF16),
        scratch_shapes=[
            pltpu.VMEM((t, MIX_W), F32),
            pltpu.VMEM((MIX_W, MIX_W), F32),
            pltpu.VMEM((MIX_W, MIX_W), F32),
            pltpu.VMEM((C_HEADS, chunk, chunk), F32),
        ],
        compiler_params=_params("parallel"),
        name="mixer_ret",
    )(zabc, zabc, zabc, zabc, gain)


def _wkv_prep_kernel(z_ref, zp_ref, zn_ref, mup_ref, mun_ref, w0_ref, wupf_ref, wupb_ref, a0_ref, aup_ref,
                     gup1_ref, gup2_ref, kk_ref, ka_ref, rk_ref,
                     r_o, k_o, v_o, kk_o, b_o, ldf_o, ldb_o, g_o, bonus_o):
    i = pl.program_id(1)
    last = pl.num_programs(1) - 1
    z = z_ref[0]
    tm = z.shape[0]
    row = lax.broadcasted_iota(jnp.int32, z.shape, 0)
    prev_row = zp_ref[0, 0, 7:8, :] * jnp.where(i > 0, 1.0, 0.0)
    next_row = zn_ref[0, 0, 0:1, :] * jnp.where(i < last, 1.0, 0.0)
    z_prev = jnp.where(row == 0, prev_row, pltpu.roll(z, 1, 0))
    z_next = jnp.where(row == tm - 1, next_row, pltpu.roll(z, tm - 1, 0))
    mu_p, mu_n = mup_ref[...], mun_ref[...]
    u = z * (1.0 - mu_p - mu_n) + z_prev * mu_p + z_next * mu_n
    r = u[:, 0:256]
    k = u[:, 256:512]
    v = u[:, 512:768]
    wd = jnp.tanh(u[:, 768:896]).astype(BF16)
    ag = u[:, 896:1024]
    g2 = u[:, 1024:1088]

    def log_decay(w0, wup):
        return _sigmoid(w0 + _dot(wd, wup)) * (-math.exp(-0.5))

    ldf_o[0] = log_decay(w0_ref[0:1, :], wupf_ref[...])
    ldb_o[0] = log_decay(w0_ref[1:2, :], wupb_ref[...])
    a = _sigmoid(a0_ref[...] + _dot(ag.astype(BF16), aup_ref[...]))
    g_o[0] = (_dot(_sigmoid(ag).astype(BF16), gup1_ref[...])
              + _dot(_sigmoid(g2).astype(BF16), gup2_ref[...])).astype(BF16)
    kk = k * kk_ref[...]
    kk = kk / jnp.maximum(jnp.sqrt(_head_sum(kk * kk, HEAD_DIM)), 1e-12)
    kh = k * (1.0 + (a - 1.0) * ka_ref[...])
    r_o[0] = r.astype(BF16)
    k_o[0] = kh.astype(BF16)
    v_o[0] = v.astype(BF16)
    kk_o[0] = kk.astype(BF16)
    b_o[0] = (kk * a).astype(BF16)
    bonus_o[0] = (_head_sum(r * kh * rk_ref[...], HEAD_DIM) * v).astype(BF16)


def _wkv_prep(zd, p, tm):
    b, t, _ = zd.shape
    zd8 = zd.reshape(b, t // 8, 8, D_COLS)
    r8 = tm // 8
    nb8 = t // 8
    row = lambda w: pl.BlockSpec((1, w), lambda bi, i: (0, 0))
    full = lambda a: pl.BlockSpec(a.shape, lambda bi, i: (0,) * a.ndim)
    out_spec = pl.BlockSpec((1, tm, MIX_W), lambda bi, i: (bi, i, 0))
    out_shape = jax.ShapeDtypeStruct((b, t, MIX_W), F32)
    out_bf16 = jax.ShapeDtypeStruct((b, t, MIX_W), BF16)
    return pl.pallas_call(
        _wkv_prep_kernel,
        grid=(b, t // tm),
        in_specs=[
            pl.BlockSpec((1, tm, D_COLS), lambda bi, i: (bi, i, 0)),
            pl.BlockSpec((1, 1, 8, D_COLS), lambda bi, i: (bi, jnp.maximum(i * r8 - 1, 0), 0, 0)),
            pl.BlockSpec((1, 1, 8, D_COLS), lambda bi, i: (bi, jnp.minimum((i + 1) * r8, nb8 - 1), 0, 0)),
            row(D_COLS), row(D_COLS),
            full(p["w0"]), full(p["wup_f"]), full(p["wup_b"]), row(256), full(p["aup"]),
            full(p["gup1"]), full(p["gup2"]), row(256), row(256), row(256),
        ],
        out_specs=[out_spec] * 9,
        out_shape=[out_bf16] * 5 + [out_shape] * 2 + [out_bf16] * 2,
        compiler_params=_params("parallel", "parallel"),
        name="wkv_prep",
    )(zd, zd8, zd8, p["mu_prev"], p["mu_next"], p["w0"], p["wup_f"], p["wup_b"], p["a0"], p["aup"],
      p["gup1"], p["gup2"], p["k_k"], p["k_a"], p["r_k"])


def _wkv_chunk_consts(chunk):
    c = chunk
    sc = D_HEADS * c
    ti = lax.broadcasted_iota(jnp.int32, (c, c), 0)
    si = lax.broadcasted_iota(jnp.int32, (c, c), 1)
    srow = lax.broadcasted_iota(jnp.int32, (sc, MIX_W), 0) // c
    slane = lax.broadcasted_iota(jnp.int32, (sc, MIX_W), 1) // HEAD_DIM
    gt = lax.broadcasted_iota(jnp.int32, (2 * sc, 2 * sc), 0)
    gs = lax.broadcasted_iota(jnp.int32, (2 * sc, 2 * sc), 1)
    t_in, s_in = gt % sc, gs % sc
    same = (t_in // c) == (s_in // c)
    upper = gt < sc
    st = lax.broadcasted_iota(jnp.int32, (sc, sc), 0)
    ss = lax.broadcasted_iota(jnp.int32, (sc, sc), 1)
    r2 = lax.broadcasted_iota(jnp.int32, (MIX_W, MIX_W), 0)
    c2 = lax.broadcasted_iota(jnp.int32, (MIX_W, MIX_W), 1)
    return {
        "tri": {False: jnp.where(si <= ti, 1.0, 0.0).astype(BF16), True: jnp.where(si >= ti, 1.0, 0.0).astype(BF16)},
        "hmask": srow == slane,
        "gmask": {False: same & ((s_in < t_in) | ((s_in == t_in) & ~upper)),
                  True: same & ((s_in > t_in) | ((s_in == t_in) & ~upper))},
        "eye": jnp.where(st == ss, 1.0, 0.0),
        "diag": r2 == c2,
    }


def _wkv_phase1(chunk, consts, probs):
    c = chunk
    sc = D_HEADS * c
    hmask = consts["hmask"]

    def stack(a):
        return jnp.where(hmask, jnp.concatenate([a] * D_HEADS, axis=0), 0.0).astype(BF16)

    pre = []
    for rev, r, kh, v, kk, bb, lw in probs:
        tri = consts["tri"][rev]
        l1, l2, l3 = _split3(lw)
        cl = _dot(tri, l1) + _dot(tri, l2) + _dot(tri, l3)
        tot = jnp.sum(lw, axis=0, keepdims=True)
        pre.append((cl, tot))
    yield
    ops = []
    for (rev, r, kh, v, kk, bb, lw), (cl, tot) in zip(probs, pre):
        w_inv = jnp.exp(-cl)
        w_end = jnp.exp(tot - cl)
        lhs = jnp.concatenate([stack(kk * jnp.exp(cl - lw)), stack(r * jnp.exp(cl))], axis=0)
        rhs = jnp.concatenate([stack(bb * w_inv), stack(kh * w_inv)], axis=0)
        kb = jnp.concatenate([stack(kh * w_end), stack(bb * w_end)], axis=0)
        wc = jnp.sum(jnp.where(consts["diag"], jnp.exp(tot), 0.0), axis=1, keepdims=True)
        wc = jnp.broadcast_to(wc, (MIX_W, MIX_W))
        ops.append((lhs, rhs, stack(v), kb, wc))
    yield
    gram = [jnp.where(consts["gmask"][p[0]], _dot_nt(o[0], o[1]), 0.0) for p, o in zip(probs, ops)]
    l_ab = [g[:sc, :sc] for g in gram]
    l_ak = [g[:sc, sc:].astype(BF16) for g in gram]
    m_r = [g[sc:, :].astype(BF16) for g in gram]
    yield

    steps = int(math.log2(c))
    pw = [(-l).astype(BF16) for l in l_ab]
    inv = [consts["eye"] - l for l in l_ab]
    w = [_dot(a, o[2]).astype(BF16) for a, o in zip(l_ak, ops)]
    for k in range(1, steps):
        if k == 1:
            pw = [_dot(p, p).astype(BF16) for p in pw]
            yield
        if k < steps - 1:
            nxt = [_dot(p, jnp.concatenate([p, i.astype(BF16)], axis=1)) for p, i in zip(pw, inv)]
            inv = [i + n[:, sc:] for i, n in zip(inv, nxt)]
            pw = [n[:, :sc].astype(BF16) for n in nxt]
        else:
            inv = [i + _dot(p, i.astype(BF16)) for p, i in zip(pw, inv)]
        yield
    inv = [i.astype(BF16) for i in inv]

    gu = [_dot(i, jnp.concatenate([o[0][:sc], wv], axis=1)) for i, o, wv in zip(inv, ops, w)]
    yield
    kb_t = [o[3].astype(F32).T.astype(BF16) for o in ops]
    mgu = [_dot(m[:, :sc], x.astype(BF16)) for m, x in zip(m_r, gu)]
    mv = [_dot(m[:, sc:], o[2]) for m, o in zip(m_r, ops)]
    yield
    return [{
        "qg": jnp.concatenate([(o[0][sc:].astype(F32) - mg[:, :MIX_W]).astype(BF16),
                               g[:, :MIX_W].astype(BF16)], axis=0),
        "y_hat": a - mg[:, MIX_W:],
        "u_hat": g[:, MIX_W:],
        "kb_t": kt,
        "v": o[2],
        "wc": o[4],
    } for o, g, mg, a, kt in zip(ops, gu, mgu, mv, kb_t)]


def _wkv_scan_kernel(chunk, group, rf, kf, vf, kkf, bf, lf, rb, kb, vb, kkb, bb, lb, yf_o, yb_o,
                     xf_s, xb_s, qg_s, yh_s, uh_s, kbt_s, v_s, wc_s):
    @pl.when(pl.program_id(1) == 0)
    def _():
        xf_s[...] = jnp.zeros_like(xf_s)
        xb_s[...] = jnp.zeros_like(xb_s)

    tm = rf.shape[1]
    n = tm // chunk
    sc = D_HEADS * chunk
    consts = _wkv_chunk_consts(chunk)
    sol_refs = {"qg": qg_s, "y_hat": yh_s, "u_hat": uh_s, "kb_t": kbt_s, "v": v_s, "wc": wc_s}

    def group_rows(g):
        rows = []
        for u in range(group):
            rows.append(pl.ds(pl.multiple_of((g * group + u) * chunk, chunk), chunk))
            rows.append(pl.ds(pl.multiple_of((n - 1 - g * group - u) * chunk, chunk), chunk))
        return rows

    def problems(g):
        probs = []
        for slot, rows in enumerate(group_rows(g)):
            if slot % 2:
                probs.append((True, rb[0, rows, :], kb[0, rows, :], vb[0, rows, :], kkb[0, rows, :],
                              bb[0, rows, :], lb[0, rows, :]))
            else:
                probs.append((False, rf[0, rows, :], kf[0, rows, :], vf[0, rows, :], kkf[0, rows, :],
                              bf[0, rows, :], lf[0, rows, :]))
        return probs

    def advance(slot, rows):
        x_s, y_o = (xb_s, yb_o) if slot % 2 else (xf_s, yf_o)
        x = x_s[...]
        res = _dot(qg_s[slot], x.astype(BF16))
        ys = res[:sc] + yh_s[slot]
        y = ys[0:chunk]
        for h in range(1, D_HEADS):
            y = y + ys[h * chunk:(h + 1) * chunk]
        y_o[0, rows, :] = y
        u = -(res[sc:] + uh_s[slot])
        x_s[...] = x * wc_s[slot] + _dot(kbt_s[slot], jnp.concatenate([v_s[slot], u.astype(BF16)], axis=0))

    def solve(g, pending):
        gen = _wkv_phase1(chunk, consts, problems(g))
        pending = list(pending)
        while True:
            try:
                next(gen)
            except StopIteration as done:
                sols = done.value
                break
            for step in pending[:2]:
                step()
            pending = pending[2:]
        for step in pending:
            step()
        for slot, sol in enumerate(sols):
            for name, ref in sol_refs.items():
                ref[slot] = sol[name]

    def steps_for(g):
        return [functools.partial(advance, slot, r) for slot, r in enumerate(group_rows(g))]

    n_groups = n // group
    solve(0, [])

    for g in range(1, n_groups):
        solve(g, steps_for(g - 1))
    for step in steps_for(n_groups - 1):
        step()


def _wkv_scan(r, kh, v, kk, bb, ldf, ldb, tm, chunk, group):
    b, t, _ = r.shape
    nt = t // tm
    sc = D_HEADS * chunk
    slots = 2 * group
    fspec = pl.BlockSpec((1, tm, MIX_W), lambda bi, i: (bi, i, 0))
    bspec = pl.BlockSpec((1, tm, MIX_W), lambda bi, i: (bi, nt - 1 - i, 0))
    shape = jax.ShapeDtypeStruct((b, t, MIX_W), F32)
    return pl.pallas_call(
        functools.partial(_wkv_scan_kernel, chunk, group),
        grid=(b, nt),
        in_specs=[fspec] * 6 + [bspec] * 6,
        out_specs=[fspec, bspec],
        out_shape=[shape, shape],
        scratch_shapes=[
            pltpu.VMEM((MIX_W, MIX_W), F32), pltpu.VMEM((MIX_W, MIX_W), F32),
            pltpu.VMEM((slots, 2 * sc, MIX_W), BF16),
            pltpu.VMEM((slots, sc, MIX_W), F32),
            pltpu.VMEM((slots, sc, MIX_W), F32),
            pltpu.VMEM((slots, MIX_W, 2 * sc), BF16),
            pltpu.VMEM((slots, sc, MIX_W), BF16),
            pltpu.VMEM((slots, MIX_W, MIX_W), F32),
        ],
        compiler_params=_params("parallel", "arbitrary"),
        name="wkv_scan",
    )(r, kh, v, kk, bb, ldf, r, kh, v, kk, bb, ldb)


def _wkv_out(yf, yb, bonus, g, gn_w, gn_b):
    y = yf + yb
    mean = _head_sum(y, HEAD_DIM) * (1.0 / HEAD_DIM)
    yc = y - mean
    var = _head_sum(yc * yc, HEAD_DIM) * (1.0 / HEAD_DIM)
    yn = yc * lax.rsqrt(var + WKV_GN_EPS) * gn_w + gn_b
    return ((yn + bonus) * g).astype(BF16)


def _post_kernel(x_ref, oa_ref, ob_ref, oc_ref, yf_ref, yb_ref, bonus_ref, g_ref, gnw_ref, gnb_ref,
                 wo_ref, gpost_ref, gpre_ref, wg_ref, wu_ref, wd_ref, gfpost_ref, y_ref):
    tm = x_ref.shape[0]
    groups = [slice(s, s + tm // ROW_GROUPS) for s in range(0, tm, tm // ROW_GROUPS)]
    o_d = [_wkv_out(yf_ref[r], yb_ref[r], bonus_ref[r], g_ref[r], gnw_ref[...], gnb_ref[...]) for r in groups]
    mix = [_dot(oa_ref[r], wo_ref[0]) + _dot(ob_ref[r], wo_ref[1]) + _dot(oc_ref[r], wo_ref[2])
           + _dot(od, wo_ref[3]) for r, od in zip(groups, o_d)]
    x = [x_ref[r] + _rms_rows(m, gpost_ref[...]) for r, m in zip(groups, mix)]
    h = [_rms_rows(xg, gpre_ref[...]).astype(BF16) for xg in x]
    gate = [_dot(hg, wg_ref[...]) for hg in h]
    up = [_dot(hg, wu_ref[...]) for hg in h]
    act = [(g * _sigmoid(g) * u).astype(BF16) for g, u in zip(gate, up)]
    f = [_dot(a, wd_ref[...]) for a in act]
    for r, xg, fg in zip(groups, x, f):
        y_ref[r] = xg + _rms_rows(fg, gfpost_ref[...])


def _post(x2, oa, ob, oc, wkv_parts, gn_w, gn_b, w_out4, g_post, g_pre, wg, wu, wd, gf_post, tm):
    n, d = x2.shape
    dff = wg.shape[1]
    tok = lambda w: pl.BlockSpec((tm, w), lambda i: (i, 0))
    row = pl.BlockSpec((1, d), lambda i: (0, 0))
    mrow = pl.BlockSpec((1, MIX_W), lambda i: (0, 0))
    return pl.pallas_call(
        _post_kernel,
        grid=(n // tm,),
        in_specs=[tok(d)] + [tok(MIX_W)] * 7 + [mrow, mrow,
                  pl.BlockSpec((4, MIX_W, d), lambda i: (0, 0, 0)), row, row,
                  pl.BlockSpec((d, dff), lambda i: (0, 0)),
                  pl.BlockSpec((d, dff), lambda i: (0, 0)),
                  pl.BlockSpec((dff, d), lambda i: (0, 0)), row],
        out_specs=tok(d),
        out_shape=jax.ShapeDtypeStruct((n, d), F32),
        compiler_params=_params("parallel"),
        name="out_proj_ffn",
    )(x2, oa, ob, oc, *wkv_parts, gn_w, gn_b, w_out4, g_post, g_pre, wg, wu, wd, gf_post)


def _angles(pos, rot_dim, theta):
    inv = theta ** (-jnp.arange(0, rot_dim, 2, dtype=F32) / rot_dim)
    return pos.astype(F32)[:, None] * inv[None, :]


def _rope_tables(t):
    rows = t // GRID_W
    row_idx = jnp.repeat(jnp.arange(rows), GRID_W)
    col_idx = jnp.tile(jnp.arange(GRID_W), rows)
    pos = jnp.arange(t)
    ar = _angles(row_idx, HEAD_DIM // 2, A_THETA)
    ac = _angles(col_idx, HEAD_DIM // 2, A_THETA)
    cos_a = jnp.concatenate([jnp.cos(ar), jnp.cos(ar), jnp.cos(ac), jnp.cos(ac)], axis=1)
    sin_a = jnp.concatenate([-jnp.sin(ar), jnp.sin(ar), -jnp.sin(ac), jnp.sin(ac)], axis=1)
    tabs_a = (jnp.tile(cos_a, (1, 4)), jnp.tile(sin_a, (1, 4)))
    ab = _angles(pos, B_ROT, B_THETA)
    pad1 = jnp.ones((t, B_SUB - B_ROT), F32)
    pad0 = jnp.zeros((t, B_SUB - B_ROT), F32)
    cos_b = jnp.concatenate([jnp.cos(ab), jnp.cos(ab), pad1], axis=1)
    sin_b = jnp.concatenate([-jnp.sin(ab), jnp.sin(ab), pad0], axis=1)
    tabs_b = (jnp.tile(cos_b, (1, 8)), jnp.tile(sin_b, (1, 8)))
    ang_c = _angles(pos, HEAD_DIM, C_THETA)
    cos_c = jnp.concatenate([jnp.cos(ang_c), jnp.cos(ang_c)], axis=1)
    sin_c = jnp.concatenate([-jnp.sin(ang_c), jnp.sin(ang_c)], axis=1)
    tabs_c = (jnp.tile(cos_c, (1, 4)), jnp.tile(sin_c, (1, 4)))
    return tabs_a, tabs_b, tabs_c


def _tile(n, pref):
    while n % pref:
        pref //= 2
    return pref


def _layer_weights(l, w):
    d_model = w["w_in"].shape[1]
    row = lambda a: a.reshape(1, -1).astype(F32)
    zeros = lambda r: jnp.zeros((r, MIX_W), F32)
    w_lora = w["d_w_up"].shape[2]
    return {
        "g_mix_pre": row(w["norm_mix_pre"][l]), "g_mix_post": row(w["norm_mix_post"][l]),
        "g_ffn_pre": row(w["norm_ffn_pre"][l]), "g_ffn_post": row(w["norm_ffn_post"][l]),
        "w_abc": w["w_in"][l][:, :ABC_COLS].astype(BF16), "w_d": w["w_in"][l][:, ABC_COLS:].astype(BF16),
        "w_out4": w["w_out"][l].reshape(4, MIX_W, d_model).astype(BF16),
        "a_gain": jnp.concatenate([jnp.tile(row(w["a_q_gain"][l]), (1, A_HEADS)),
                                   jnp.tile(row(w["a_k_gain"][l]), (1, A_KV_HEADS))], axis=1),
        "b_lambda": w["b_lambda"][l].astype(F32), "b_gain": w["b_subln_gain"][l].reshape(-1, 1).astype(F32),
        "c_gain": row(w["c_gn_gain"][l]),
        "wkv": {
            "mu_prev": row(w["d_mu_prev"][l]), "mu_next": row(w["d_mu_next"][l]),
            "w0": w["d_w0"][l].astype(F32),
            "wup_f": jnp.concatenate([w["d_w_up"][l, 0], zeros(w_lora)], axis=0).astype(BF16),
            "wup_b": jnp.concatenate([zeros(w_lora), w["d_w_up"][l, 1]], axis=0).astype(BF16),
            "a0": row(w["d_a0"][l]),
            "aup": jnp.concatenate([w["d_a_up"][l], zeros(64)], axis=0).astype(BF16),
            "gup1": jnp.concatenate([zeros(64), w["d_g_up"][l][:64]], axis=0).astype(BF16),
            "gup2": w["d_g_up"][l][64:].astype(BF16),
            "k_k": row(w["d_k_k"][l]), "k_a": row(w["d_k_a"][l]), "r_k": row(w["d_r_k"][l]),
        },
        "gn_w": row(w["d_gn_w"][l]), "gn_b": row(w["d_gn_b"][l]),
        "wg": w["ffn_w_gate"][l].astype(BF16), "wu": w["ffn_w_up"][l].astype(BF16),
        "wd": w["ffn_w_down"][l].astype(BF16),
    }


def _trunk(x, layers):
    b, t, d = x.shape
    n = b * t
    tabs = _rope_tables(t)
    tm_proj = _tile(t, 512)
    tm_post = _tile(n, 512)
    tq = _tile(t, 256)
    tm_wkv = _tile(t, 256)
    tm_scan = _tile(t, 1024)
    chunk = 32
    x2 = x.reshape(n, d)
    for l, p in enumerate(layers):
        lam_init = 0.8 - 0.6 * math.exp(-0.3 * l)
        zabc, zd = _in_proj(x2, p["g_mix_pre"], p["w_abc"], p["w_d"], tabs, p["a_gain"], t, tm_proj)
        zabc = zabc.reshape(b, t, ABC_COLS)
        zd = zd.reshape(b, t, D_COLS)
        o_a = _mixer_gqa(zabc, tq)
        o_b = _mixer_diff(zabc, p["b_lambda"], p["b_gain"], lam_init, tq)
        o_c = _mixer_ret(zabc, p["c_gain"], _tile(t, 256))
        r, kh, v, kk, bb, ldf, ldb, g, bonus = _wkv_prep(zd, p["wkv"], tm_wkv)
        yf, yb = _wkv_scan(r, kh, v, kk, bb, ldf, ldb, tm_scan, chunk, min(4, tm_scan // chunk))
        wkv_parts = [a.reshape(n, MIX_W) for a in (yf, yb, bonus, g)]
        x2 = _post(x2, o_a.reshape(n, MIX_W), o_b.reshape(n, MIX_W), o_c.reshape(n, MIX_W), wkv_parts,
                   p["gn_w"], p["gn_b"], p["w_out4"], p["g_mix_post"], p["g_ffn_pre"], p["wg"], p["wu"], p["wd"],
                   p["g_ffn_post"], tm_post)
    return x2.reshape(b, t, d)


def kernel(x_prompt, x_sample, norm_mix_pre, norm_mix_post, norm_ffn_pre, norm_ffn_post, w_in, w_out,
           a_q_gain, a_k_gain, b_lambda, b_subln_gain, c_gn_gain, d_mu_prev, d_mu_next, d_w0, d_w_up,
           d_a0, d_a_up, d_g_up, d_k_k, d_k_a, d_r_k, d_gn_w, d_gn_b, ffn_w_gate, ffn_w_up, ffn_w_down):
    w = {
        "norm_mix_pre": norm_mix_pre, "norm_mix_post": norm_mix_post,
        "norm_ffn_pre": norm_ffn_pre, "norm_ffn_post": norm_ffn_post,
        "w_in": w_in, "w_out": w_out, "a_q_gain": a_q_gain, "a_k_gain": a_k_gain,
        "b_lambda": b_lambda, "b_subln_gain": b_subln_gain, "c_gn_gain": c_gn_gain,
        "d_mu_prev": d_mu_prev, "d_mu_next": d_mu_next, "d_w0": d_w0, "d_w_up": d_w_up,
        "d_a0": d_a0, "d_a_up": d_a_up, "d_g_up": d_g_up, "d_k_k": d_k_k, "d_k_a": d_k_a,
        "d_r_k": d_r_k, "d_gn_w": d_gn_w, "d_gn_b": d_gn_b,
        "ffn_w_gate": ffn_w_gate, "ffn_w_up": ffn_w_up, "ffn_w_down": ffn_w_down,
    }
    layers = [_layer_weights(l, w) for l in range(w_in.shape[0])]
    return (_trunk(x_prompt, layers), _trunk(x_sample, layers))
```

```python
import functools
import math

import jax
import jax.numpy as jnp
from jax import lax
from jax.experimental import pallas as pl
from jax.experimental.pallas import tpu as pltpu

F32 = jnp.float32
BF16 = jnp.bfloat16

HEAD_DIM = 64
GRID_W = 64
NORM_EPS = 1e-6
A_HEADS, A_KV_HEADS, A_THETA = 4, 2, 10000.0
B_HEADS, B_SUB, B_ROT, B_THETA = 4, 32, 8, 500000.0
C_HEADS, C_THETA = 4, 10000.0
D_HEADS = 4
WKV_GN_EPS = 64e-5
MIX_W = 256
ABC_COLS = 2304
A_Q, A_K, A_V = slice(0, 256), slice(256, 384), slice(384, 512)
B_Q, B_K, B_V = slice(512, 768), slice(768, 1024), slice(1024, 1280)
C_Q, C_K, C_VG = slice(1280, 1536), slice(1536, 1792), slice(1792, 2304)
D_COLS = 1088
BF16_SUBLANES = 16
VEXT_ROWS = HEAD_DIM + BF16_SUBLANES
LOG2E = math.log2(math.e)
KEY_CHUNK = 512
ROW_GROUPS = 2
SHIFT_LIMIT = 100.0
VMEM_LIMIT = 56 * 1024 * 1024


def _params(*sem):
    return pltpu.CompilerParams(dimension_semantics=sem, vmem_limit_bytes=VMEM_LIMIT)


def _rms_rows(x, gain):
    return x * lax.rsqrt(jnp.mean(x * x, axis=-1, keepdims=True) + NORM_EPS) * gain


def _split2(x):
    hi = x.astype(BF16)
    lo = (x - hi.astype(F32)).astype(BF16)
    return hi, lo


def _split3(x):
    h1 = x.astype(BF16)
    r1 = x - h1.astype(F32)
    h2 = r1.astype(BF16)
    h3 = (r1 - h2.astype(F32)).astype(BF16)
    return h1, h2, h3


def _head_sum(x, seg):
    w = x.shape[-1]
    r = lax.broadcasted_iota(jnp.int32, (w, w), 0) // seg
    c = lax.broadcasted_iota(jnp.int32, (w, w), 1) // seg
    bd = jnp.where(r == c, 1.0, 0.0).astype(BF16)
    hi, lo = _split2(x)
    return (jnp.dot(hi, bd, preferred_element_type=F32)
            + jnp.dot(lo, bd, preferred_element_type=F32))


def _rope(x, cos, sin, half):
    w = x.shape[-1]
    lane = lax.broadcasted_iota(jnp.int32, x.shape, 1)
    nxt = pltpu.roll(x, w - half, 1)
    prv = pltpu.roll(x, half, 1)
    return x * cos + jnp.where((lane % (2 * half)) < half, nxt, prv) * sin


def _sigmoid(x):
    return 0.5 * jnp.tanh(0.5 * x) + 0.5


def _dot(a, b):
    return jnp.dot(a, b, preferred_element_type=F32)


def _dot_nt(a, b):
    return lax.dot_general(a, b, (((1,), (1,)), ((), ())), preferred_element_type=F32)


def _store_qk_prepared(z, r, tabs, gain_a, out_ref):
    (cos_a, sin_a), (cos_b, sin_b), (cos_c, sin_c) = tabs

    def head_norm(x, gain):
        return x * lax.rsqrt(_head_sum(x * x, HEAD_DIM) * (1.0 / HEAD_DIM) + NORM_EPS) * gain

    a_qk = head_norm(z[:, A_Q.start:A_K.stop], gain_a)
    kw = A_K.stop - A_K.start
    pieces = {
        A_Q: _rope(a_qk[:, :A_Q.stop], cos_a, sin_a, HEAD_DIM // 4) * (HEAD_DIM ** -0.5 * LOG2E),
        A_K: _rope(a_qk[:, A_Q.stop:], cos_a[:, :kw], sin_a[:, :kw], HEAD_DIM // 4),
        B_Q: _rope(z[:, B_Q], cos_b, sin_b, B_ROT // 2) * (B_SUB ** -0.5 * LOG2E),
        B_K: _rope(z[:, B_K], cos_b, sin_b, B_ROT // 2),
        C_Q: _rope(z[:, C_Q], cos_c, sin_c, HEAD_DIM // 2),
        C_K: _rope(z[:, C_K], cos_c, sin_c, HEAD_DIM // 2) * (HEAD_DIM ** -0.5),
    }
    for cols in (A_V, B_V, C_VG):
        pieces[cols] = z[:, cols]
    for cols, val in pieces.items():
        out_ref[r, cols] = val.astype(BF16)


def _in_proj_kernel(x_ref, g_ref, wabc_ref, wd_ref, ca_ref, sa_ref, cb_ref, sb_ref, cc_ref, sc_ref, ga_ref,
                    zabc_ref, zd_ref):
    tm = x_ref.shape[0]
    groups = [slice(s, s + tm // ROW_GROUPS) for s in range(0, tm, tm // ROW_GROUPS)]
    h = [_rms_rows(x_ref[r], g_ref[...]).astype(BF16) for r in groups]
    z = [_dot(hg, wabc_ref[...]) for hg in h]
    for r, hg in zip(groups, h):
        zd_ref[r] = _dot(hg, wd_ref[...])
    for r, zg in zip(groups, z):
        tabs = ((ca_ref[r], sa_ref[r]), (cb_ref[r], sb_ref[r]), (cc_ref[r], sc_ref[r]))
        _store_qk_prepared(zg, r, tabs, ga_ref[...], zabc_ref)


def _in_proj(x2, gain, w_abc, w_d, tabs, gain_a, t, tm):
    n, d = x2.shape
    per_seq = t // tm
    tab = pl.BlockSpec((tm, MIX_W), lambda i: (i % per_seq, 0))
    return pl.pallas_call(
        _in_proj_kernel,
        grid=(n // tm,),
        in_specs=[
            pl.BlockSpec((tm, d), lambda i: (i, 0)),
            pl.BlockSpec((1, d), lambda i: (0, 0)),
            pl.BlockSpec((d, ABC_COLS), lambda i: (0, 0)),
            pl.BlockSpec((d, D_COLS), lambda i: (0, 0)),
            tab, tab, tab, tab, tab, tab,
            pl.BlockSpec((1, A_K.stop), lambda i: (0, 0)),
        ],
        out_specs=[
            pl.BlockSpec((tm, ABC_COLS), lambda i: (i, 0)),
            pl.BlockSpec((tm, D_COLS), lambda i: (i, 0)),
        ],
        out_shape=[
            jax.ShapeDtypeStruct((n, ABC_COLS), BF16),
            jax.ShapeDtypeStruct((n, D_COLS), F32),
        ],
        compiler_params=_params("parallel"),
        name="in_proj",
    )(x2, gain, w_abc, w_d, *tabs[0], *tabs[1], *tabs[2], gain_a)


def _value_ext_t(v_t):
    row = lax.broadcasted_iota(jnp.int32, (VEXT_ROWS - HEAD_DIM, v_t.shape[1]), 0)
    return jnp.concatenate([v_t, jnp.where(row == 0, 1.0, 0.0)], axis=0).astype(BF16)


def _attend_exact(k_ref, q_ts, vx_refs):
    outs = []
    for q_t, vx in zip(q_ts, vx_refs):
        s_t = _dot(k_ref[...], q_t)
        p_t = jnp.exp2(s_t - jnp.max(s_t, axis=0, keepdims=True)).astype(BF16)
        o = _dot(vx[...], p_t)
        outs.append(o[:HEAD_DIM] / o[HEAD_DIM:HEAD_DIM + 1])
    return outs


def _attend_shifted(k_ref, q_ts, shifts, vx_refs):
    t = k_ref.shape[0]
    ck = min(t, KEY_CHUNK)
    steps = [(j, slice(c * ck, (c + 1) * ck)) for j in range(len(q_ts)) for c in range(t // ck)]
    acc = [None] * len(q_ts)
    s_prev = p_prev = None
    for i in range(len(steps) + 2):
        s_new = p_new = None
        if i < len(steps):
            j, keys = steps[i]
            s_new = _dot(k_ref[keys, :], q_ts[j])
        if 1 <= i <= len(steps):
            p_new = jnp.exp2(s_prev - shifts[steps[i - 1][0]]).astype(BF16)
        if i >= 2:
            j, keys = steps[i - 2]
            o = _dot(vx_refs[j][:, keys], p_prev)
            acc[j] = o if acc[j] is None else acc[j] + o
        s_prev, p_prev = s_new, p_new
    return [a[:HEAD_DIM] / a[HEAD_DIM:HEAD_DIM + 1] for a in acc]


def _attend_t(k_ref, kn_ref, q_t, head_rows, head_keys, vx_refs, finish):
    c = k_ref.shape[1]
    tq = q_t.shape[1]
    qsq = q_t * q_t
    q_ts, shifts = [], []
    for rows, keys in zip(head_rows, head_keys):
        pieces = [jnp.zeros((keys.start, tq), F32), q_t[rows], jnp.zeros((c - keys.stop, tq), F32)]
        q_ts.append(jnp.concatenate([x for x in pieces if x.shape[0]], axis=0).astype(BF16))
        qn = jnp.sqrt(jnp.sum(qsq[rows], axis=0, keepdims=True))
        shifts.append(qn * kn_ref[0:1, keys.start:keys.start + 1])
    worst = jnp.max(jnp.concatenate(shifts, axis=0))

    @pl.when(2.0 * worst < SHIFT_LIMIT)
    def _():
        finish(_attend_shifted(k_ref, q_ts, shifts, vx_refs))

    @pl.when(jnp.logical_not(2.0 * worst < SHIFT_LIMIT))
    def _():
        finish(_attend_exact(k_ref, q_ts, vx_refs))


def _gqa_kernel(q_ref, k_ref, v_ref, o_ref, kn_s, vx_s):
    @pl.when(pl.program_id(1) == 0)
    def _():
        k = k_ref[0].astype(F32)
        kn_s[...] = jnp.sqrt(jnp.max(_head_sum(k * k, HEAD_DIM), axis=0, keepdims=True))
        v_t = v_ref[0].astype(F32).T
        for g in range(A_KV_HEADS):
            vx_s[g] = _value_ext_t(v_t[g * HEAD_DIM:(g + 1) * HEAD_DIM])

    q_t = q_ref[0].astype(F32).T
    k_s = k_ref.at[0]
    group = A_HEADS // A_KV_HEADS
    head_rows = [slice(h * HEAD_DIM, (h + 1) * HEAD_DIM) for h in range(A_HEADS)]
    head_keys = [slice((h // group) * HEAD_DIM, (h // group + 1) * HEAD_DIM) for h in range(A_HEADS)]

    def finish(outs):
        o_ref[0] = jnp.concatenate(outs, axis=0).T.astype(BF16)

    _attend_t(k_s, kn_s, q_t, head_rows, head_keys, [vx_s.at[h // group] for h in range(A_HEADS)], finish)


def _mixer_gqa(zabc, tq):
    b, t, _ = zabc.shape
    return pl.pallas_call(
        _gqa_kernel,
        grid=(b, t // tq),
        in_specs=[
            pl.BlockSpec((1, tq, 256), lambda bi, i: (bi, i, 0)),
            pl.BlockSpec((1, t, 128), lambda bi, i: (bi, 0, 2)),
            pl.BlockSpec((1, t, 128), lambda bi, i: (bi, 0, 3)),
        ],
        out_specs=pl.BlockSpec((1, tq, MIX_W), lambda bi, i: (bi, i, 0)),
        out_shape=jax.ShapeDtypeStruct((b, t, MIX_W), BF16),
        scratch_shapes=[
            pltpu.VMEM((1, A_KV_HEADS * HEAD_DIM), F32),
            pltpu.VMEM((A_KV_HEADS, VEXT_ROWS, t), BF16),
        ],
        compiler_params=_params("parallel", "arbitrary"),
        name="mixer_gqa",
    )(zabc, zabc, zabc)


def _diff_kernel(lam_init, q_ref, k_ref, v_ref, lam_ref, gain_ref, o_ref, kn_s, vx_s):
    @pl.when(pl.program_id(1) == 0)
    def _():
        k = k_ref[0].astype(F32)
        kn_s[...] = jnp.sqrt(jnp.max(_head_sum(k * k, B_SUB), axis=0, keepdims=True))
        v_t = v_ref[0].astype(F32).T
        for h in range(B_HEADS):
            vx_s[h] = _value_ext_t(v_t[h * HEAD_DIM:(h + 1) * HEAD_DIM])

    lp = lam_ref[...]
    lam = (jnp.exp(jnp.sum(lp[0:1] * lp[1:2], axis=1, keepdims=True))
           - jnp.exp(jnp.sum(lp[2:3] * lp[3:4], axis=1, keepdims=True)) + lam_init)
    q_t = q_ref[0].astype(F32).T
    k_s = k_ref.at[0]
    sub_rows = [slice(j * B_SUB, (j + 1) * B_SUB) for j in range(2 * B_HEADS)]

    def finish(parts):
        outs = []
        for h in range(B_HEADS):
            o = parts[2 * h] - lam * parts[2 * h + 1]
            o = o * lax.rsqrt(jnp.mean(o * o, axis=0, keepdims=True) + NORM_EPS) * gain_ref[...]
            outs.append(o * (1.0 - lam_init))
        o_ref[0] = jnp.concatenate(outs, axis=0).T.astype(BF16)

    _attend_t(k_s, kn_s, q_t, sub_rows, sub_rows, [vx_s.at[j // 2] for j in range(2 * B_HEADS)], finish)


def _mixer_diff(zabc, lam_params, gain, lam_init, tq):
    b, t, _ = zabc.shape
    return pl.pallas_call(
        functools.partial(_diff_kernel, lam_init),
        grid=(b, t // tq),
        in_specs=[
            pl.BlockSpec((1, tq, 256), lambda bi, i: (bi, i, 2)),
            pl.BlockSpec((1, t, 256), lambda bi, i: (bi, 0, 3)),
            pl.BlockSpec((1, t, 256), lambda bi, i: (bi, 0, 4)),
            pl.BlockSpec((4, B_SUB), lambda bi, i: (0, 0)),
            pl.BlockSpec((HEAD_DIM, 1), lambda bi, i: (0, 0)),
        ],
        out_specs=pl.BlockSpec((1, tq, MIX_W), lambda bi, i: (bi, i, 0)),
        out_shape=jax.ShapeDtypeStruct((b, t, MIX_W), BF16),
        scratch_shapes=[
            pltpu.VMEM((1, 2 * B_HEADS * B_SUB), F32),
            pltpu.VMEM((B_HEADS, VEXT_ROWS, t), BF16),
        ],
        compiler_params=_params("parallel", "arbitrary"),
        name="mixer_diff",
    )(zabc, zabc, zabc, lam_params, gain)


def _ret_log_gammas():
    lg = [math.log1p(-(2.0 ** (-5.0 - h))) for h in range(C_HEADS)]
    return lg, lg[::-1]


def _lane_consts(vals, shape):
    head = lax.broadcasted_iota(jnp.int32, shape, len(shape) - 1) // HEAD_DIM
    out = jnp.full(shape, vals[-1], F32)
    for h in range(len(vals) - 2, -1, -1):
        out = jnp.where(head == h, vals[h], out)
    return out


def _ret_kernel(chunk, q_ref, k_ref, v_ref, g_ref, gain_ref, o_ref, acc_s, sf_s, sb_s, dm_s):
    t = q_ref.shape[1]
    c = chunk
    n = t // c
    lgf, lgb = _ret_log_gammas()
    qr_s = q_ref.at[0]
    kr_s = k_ref.at[0]

    ti = lax.broadcasted_iota(jnp.int32, (c, c), 0)
    si = lax.broadcasted_iota(jnp.int32, (c, c), 1)
    dist = (ti - si).astype(F32)
    lane_head = lax.broadcasted_iota(jnp.int32, (1, MIX_W), 1) // HEAD_DIM
    row = lax.broadcasted_iota(jnp.int32, (c, MIX_W), 0).astype(F32)
    lgf_l = _lane_consts(lgf, (c, MIX_W))
    lgb_l = _lane_consts(lgb, (c, MIX_W))
    qdec_f = jnp.exp(lgf_l * (row + 1.0))
    kdec_f = jnp.exp(lgf_l * (c - 1.0 - row))
    qdec_b = jnp.exp(lgb_l * (c - row))
    kdec_b = jnp.exp(lgb_l * row)
    r2 = lax.broadcasted_iota(jnp.int32, (MIX_W, MIX_W), 0) // HEAD_DIM
    c2 = lax.broadcasted_iota(jnp.int32, (MIX_W, MIX_W), 1) // HEAD_DIM
    same_head = r2 == c2
    gf_blk = jnp.where(same_head, jnp.exp(_lane_consts(lgf, (MIX_W, MIX_W)) * c), 0.0)
    gb_blk = jnp.where(same_head, jnp.exp(_lane_consts(lgb, (MIX_W, MIX_W)) * c), 0.0)

    head_mask = [(lane_head == h).astype(BF16) for h in range(C_HEADS)]
    for h in range(C_HEADS):
        dm_s[h] = jnp.where(dist > 0, jnp.exp(lgf[h] * dist),
                            jnp.where(dist < 0, jnp.exp(-lgb[h] * dist), 2.0))

    def intra(qc, kc, vc):
        ps = [(_dot_nt(qc * head_mask[h], kc) * dm_s[h]).astype(BF16) for h in range(C_HEADS)]
        v_stack = jnp.concatenate([vc * head_mask[h] for h in range(C_HEADS)], axis=0)
        return _dot(jnp.concatenate(ps, axis=1), v_stack)

    sf_s[...] = jnp.zeros_like(sf_s)
    sb_s[...] = jnp.zeros_like(sb_s)

    def fwd(i, carry):
        rows = pl.ds(pl.multiple_of(i * c, c), c)
        qc = qr_s[rows, :]
        kc = kr_s[rows, :]
        vc = v_ref[0, rows, :]
        o = intra(qc, kc, vc) + _dot((qc.astype(F32) * qdec_f).astype(BF16), sf_s[...].astype(BF16))
        acc_s[rows, :] = o
        kd = (kc.astype(F32) * kdec_f).T.astype(BF16)
        sf_s[...] = gf_blk * sf_s[...] + jnp.where(same_head, _dot(kd, vc), 0.0)
        return carry

    lax.fori_loop(0, n, fwd, 0)

    def bwd(i, carry):
        rows = pl.ds(pl.multiple_of((n - 1 - i) * c, c), c)
        qc = qr_s[rows, :].astype(F32)
        kc = kr_s[rows, :].astype(F32)
        vc = v_ref[0, rows, :]
        acc_s[rows, :] = acc_s[rows, :] + _dot((qc * qdec_b).astype(BF16), sb_s[...].astype(BF16))
        kd = (kc * kdec_b).T.astype(BF16)
        sb_s[...] = gb_blk * sb_s[...] + jnp.where(same_head, _dot(kd, vc), 0.0)
        return carry

    lax.fori_loop(0, n, bwd, 0)

    o = acc_s[...]
    o = o * lax.rsqrt(_head_sum(o * o, HEAD_DIM) * (1.0 / HEAD_DIM) + NORM_EPS) * gain_ref[...]
    g = g_ref[0].astype(F32)
    o_ref[0] = (o * (g * _sigmoid(g))).astype(BF16)


def _mixer_ret(zabc, gain, chunk):
    b, t, _ = zabc.shape
    blk = lambda j: pl.BlockSpec((1, t, 256), lambda bi: (bi, 0, j))
    return pl.pallas_call(
        functools.partial(_ret_kernel, chunk),
        grid=(b,),
        in_specs=[blk(5), blk(6), blk(7), blk(8),
                  pl.BlockSpec((1, 256), lambda bi: (0, 0))],
        out_specs=pl.BlockSpec((1, t, MIX_W), lambda bi: (bi, 0, 0)),
        out_shape=jax.ShapeDtypeStruct((b, t, MIX_W), BF16),
        scratch_shapes=[
            pltpu.VMEM((t, MIX_W), F32),
            pltpu.VMEM((MIX_W, MIX_W), F32),
            pltpu.VMEM((MIX_W, MIX_W), F32),
            pltpu.VMEM((C_HEADS, chunk, chunk), F32),
        ],
        compiler_params=_params("parallel"),
        name="mixer_ret",
    )(zabc, zabc, zabc, zabc, gain)


def _wkv_prep_kernel(z_ref, zp_ref, zn_ref, mup_ref, mun_ref, w0_ref, wupf_ref, wupb_ref, a0_ref, aup_ref,
                     gup1_ref, gup2_ref, kk_ref, ka_ref, rk_ref,
                     r_o, k_o, v_o, kk_o, b_o, ldf_o, ldb_o, g_o, bonus_o):
    i = pl.program_id(1)
    last = pl.num_programs(1) - 1
    z = z_ref[0]
    tm = z.shape[0]
    row = lax.broadcasted_iota(jnp.int32, z.shape, 0)
    prev_row = zp_ref[0, 0, 7:8, :] * jnp.where(i > 0, 1.0, 0.0)
    next_row = zn_ref[0, 0, 0:1, :] * jnp.where(i < last, 1.0, 0.0)
    z_prev = jnp.where(row == 0, prev_row, pltpu.roll(z, 1, 0))
    z_next = jnp.where(row == tm - 1, next_row, pltpu.roll(z, tm - 1, 0))
    mu_p, mu_n = mup_ref[...], mun_ref[...]
    u = z * (1.0 - mu_p - mu_n) + z_prev * mu_p + z_next * mu_n
    r = u[:, 0:256]
    k = u[:, 256:512]
    v = u[:, 512:768]
    wd = jnp.tanh(u[:, 768:896]).astype(BF16)
    ag = u[:, 896:1024]
    g2 = u[:, 1024:1088]

    def log_decay(w0, wup):
        return _sigmoid(w0 + _dot(wd, wup)) * (-math.exp(-0.5))

    ldf_o[0] = log_decay(w0_ref[0:1, :], wupf_ref[...])
    ldb_o[0] = log_decay(w0_ref[1:2, :], wupb_ref[...])
    a = _sigmoid(a0_ref[...] + _dot(ag.astype(BF16), aup_ref[...]))
    g_o[0] = (_dot(_sigmoid(ag).astype(BF16), gup1_ref[...])
              + _dot(_sigmoid(g2).astype(BF16), gup2_ref[...])).astype(BF16)
    kk = k * kk_ref[...]
    kk = kk / jnp.maximum(jnp.sqrt(_head_sum(kk * kk, HEAD_DIM)), 1e-12)
    kh = k * (1.0 + (a - 1.0) * ka_ref[...])
    r_o[0] = r.astype(BF16)
    k_o[0] = kh.astype(BF16)
    v_o[0] = v.astype(BF16)
    kk_o[0] = kk.astype(BF16)
    b_o[0] = (kk * a).astype(BF16)
    bonus_o[0] = (_head_sum(r * kh * rk_ref[...], HEAD_DIM) * v).astype(BF16)


def _wkv_prep(zd, p, tm):
    b, t, _ = zd.shape
    zd8 = zd.reshape(b, t // 8, 8, D_COLS)
    r8 = tm // 8
    nb8 = t // 8
    row = lambda w: pl.BlockSpec((1, w), lambda bi, i: (0, 0))
    full = lambda a: pl.BlockSpec(a.shape, lambda bi, i: (0,) * a.ndim)
    out_spec = pl.BlockSpec((1, tm, MIX_W), lambda bi, i: (bi, i, 0))
    out_shape = jax.ShapeDtypeStruct((b, t, MIX_W), F32)
    out_bf16 = jax.ShapeDtypeStruct((b, t, MIX_W), BF16)
    return pl.pallas_call(
        _wkv_prep_kernel,
        grid=(b, t // tm),
        in_specs=[
            pl.BlockSpec((1, tm, D_COLS), lambda bi, i: (bi, i, 0)),
            pl.BlockSpec((1, 1, 8, D_COLS), lambda bi, i: (bi, jnp.maximum(i * r8 - 1, 0), 0, 0)),
            pl.BlockSpec((1, 1, 8, D_COLS), lambda bi, i: (bi, jnp.minimum((i + 1) * r8, nb8 - 1), 0, 0)),
            row(D_COLS), row(D_COLS),
            full(p["w0"]), full(p["wup_f"]), full(p["wup_b"]), row(256), full(p["aup"]),
            full(p["gup1"]), full(p["gup2"]), row(256), row(256), row(256),
        ],
        out_specs=[out_spec] * 9,
        out_shape=[out_bf16] * 5 + [out_shape] * 2 + [out_bf16] * 2,
        compiler_params=_params("parallel", "parallel"),
        name="wkv_prep",
    )(zd, zd8, zd8, p["mu_prev"], p["mu_next"], p["w0"], p["wup_f"], p["wup_b"], p["a0"], p["aup"],
      p["gup1"], p["gup2"], p["k_k"], p["k_a"], p["r_k"])


def _wkv_chunk_consts(chunk):
    c = chunk
    sc = D_HEADS * c
    ti = lax.broadcasted_iota(jnp.int32, (c, c), 0)
    si = lax.broadcasted_iota(jnp.int32, (c, c), 1)
    srow = lax.broadcasted_iota(jnp.int32, (sc, MIX_W), 0) // c
    slane = lax.broadcasted_iota(jnp.int32, (sc, MIX_W), 1) // HEAD_DIM
    gt = lax.broadcasted_iota(jnp.int32, (2 * sc, 2 * sc), 0)
    gs = lax.broadcasted_iota(jnp.int32, (2 * sc, 2 * sc), 1)
    t_in, s_in = gt % sc, gs % sc
    same = (t_in // c) == (s_in // c)
    upper = gt < sc
    st = lax.broadcasted_iota(jnp.int32, (sc, sc), 0)
    ss = lax.broadcasted_iota(jnp.int32, (sc, sc), 1)
    r2 = lax.broadcasted_iota(jnp.int32, (MIX_W, MIX_W), 0)
    c2 = lax.broadcasted_iota(jnp.int32, (MIX_W, MIX_W), 1)
    return {
        "tri": {False: jnp.where(si <= ti, 1.0, 0.0).astype(BF16), True: jnp.where(si >= ti, 1.0, 0.0).astype(BF16)},
        "hmask": srow == slane,
        "gmask": {False: same & ((s_in < t_in) | ((s_in == t_in) & ~upper)),
                  True: same & ((s_in > t_in) | ((s_in == t_in) & ~upper))},
        "eye": jnp.where(st == ss, 1.0, 0.0),
        "diag": r2 == c2,
    }


def _wkv_phase1(chunk, consts, probs):
    c = chunk
    sc = D_HEADS * c
    hmask = consts["hmask"]

    def stack(a):
        return jnp.where(hmask, jnp.concatenate([a] * D_HEADS, axis=0), 0.0).astype(BF16)

    pre = []
    for rev, r, kh, v, kk, bb, lw in probs:
        tri = consts["tri"][rev]
        l1, l2, l3 = _split3(lw)
        cl = _dot(tri, l1) + _dot(tri, l2) + _dot(tri, l3)
        tot = jnp.sum(lw, axis=0, keepdims=True)
        pre.append((cl, tot))
    yield
    ops = []
    for (rev, r, kh, v, kk, bb, lw), (cl, tot) in zip(probs, pre):
        w_inv = jnp.exp(-cl)
        w_end = jnp.exp(tot - cl)
        lhs = jnp.concatenate([stack(kk * jnp.exp(cl - lw)), stack(r * jnp.exp(cl))], axis=0)
        rhs = jnp.concatenate([stack(bb * w_inv), stack(kh * w_inv)], axis=0)
        kb = jnp.concatenate([stack(kh * w_end), stack(bb * w_end)], axis=0)
        wc = jnp.sum(jnp.where(consts["diag"], jnp.exp(tot), 0.0), axis=1, keepdims=True)
        wc = jnp.broadcast_to(wc, (MIX_W, MIX_W))
        ops.append((lhs, rhs, stack(v), kb, wc))
    yield
    gram = [jnp.where(consts["gmask"][p[0]], _dot_nt(o[0], o[1]), 0.0) for p, o in zip(probs, ops)]
    l_ab = [g[:sc, :sc] for g in gram]
    l_ak = [g[:sc, sc:].astype(BF16) for g in gram]
    m_r = [g[sc:, :].astype(BF16) for g in gram]
    yield

    steps = int(math.log2(c))
    pw = [(-l).astype(BF16) for l in l_ab]
    inv = [consts["eye"] - l for l in l_ab]
    w = [_dot(a, o[2]).astype(BF16) for a, o in zip(l_ak, ops)]
    for k in range(1, steps):
        if k == 1:
            pw = [_dot(p, p).astype(BF16) for p in pw]
            yield
        if k < steps - 1:
            nxt = [_dot(p, jnp.concatenate([p, i.astype(BF16)], axis=1)) for p, i in zip(pw, inv)]
            inv = [i + n[:, sc:] for i, n in zip(inv, nxt)]
            pw = [n[:, :sc].astype(BF16) for n in nxt]
        else:
            inv = [i + _dot(p, i.astype(BF16)) for p, i in zip(pw, inv)]
        yield
    inv = [i.astype(BF16) for i in inv]

    gu = [_dot(i, jnp.concatenate([o[0][:sc], wv], axis=1)) for i, o, wv in zip(inv, ops, w)]
    yield
    kb_t = [o[3].astype(F32).T.astype(BF16) for o in ops]
    mgu = [_dot(m[:, :sc], x.astype(BF16)) for m, x in zip(m_r, gu)]
    mv = [_dot(m[:, sc:], o[2]) for m, o in zip(m_r, ops)]
    yield
    return [{
        "qg": jnp.concatenate([(o[0][sc:].astype(F32) - mg[:, :MIX_W]).astype(BF16),
                               g[:, :MIX_W].astype(BF16)], axis=0),
        "y_hat": a - mg[:, MIX_W:],
        "u_hat": g[:, MIX_W:],
        "kb_t": kt,
        "v": o[2],
        "wc": o[4],
    } for o, g, mg, a, kt in zip(ops, gu, mgu, mv, kb_t)]


def _wkv_scan_kernel(chunk, group, rf, kf, vf, kkf, bf, lf, rb, kb, vb, kkb, bb, lb, yf_o, yb_o,
                     xf_s, xb_s, qg_s, yh_s, uh_s, kbt_s, v_s, wc_s):
    @pl.when(pl.program_id(1) == 0)
    def _():
        xf_s[...] = jnp.zeros_like(xf_s)
        xb_s[...] = jnp.zeros_like(xb_s)

    tm = rf.shape[1]
    n = tm // chunk
    sc = D_HEADS * chunk
    consts = _wkv_chunk_consts(chunk)
    sol_refs = {"qg": qg_s, "y_hat": yh_s, "u_hat": uh_s, "kb_t": kbt_s, "v": v_s, "wc": wc_s}

    def group_rows(g):
        rows = []
        for u in range(group):
            rows.append(pl.ds(pl.multiple_of((g * group + u) * chunk, chunk), chunk))
            rows.append(pl.ds(pl.multiple_of((n - 1 - g * group - u) * chunk, chunk), chunk))
        return rows

    def problems(g):
        probs = []
        for slot, rows in enumerate(group_rows(g)):
            if slot % 2:
                probs.append((True, rb[0, rows, :], kb[0, rows, :], vb[0, rows, :], kkb[0, rows, :],
                              bb[0, rows, :], lb[0, rows, :]))
            else:
                probs.append((False, rf[0, rows, :], kf[0, rows, :], vf[0, rows, :], kkf[0, rows, :],
                              bf[0, rows, :], lf[0, rows, :]))
        return probs

    def advance(slot, rows):
        x_s, y_o = (xb_s, yb_o) if slot % 2 else (xf_s, yf_o)
        x = x_s[...]
        res = _dot(qg_s[slot], x.astype(BF16))
        ys = res[:sc] + yh_s[slot]
        y = ys[0:chunk]
        for h in range(1, D_HEADS):
            y = y + ys[h * chunk:(h + 1) * chunk]
        y_o[0, rows, :] = y
        u = -(res[sc:] + uh_s[slot])
        x_s[...] = x * wc_s[slot] + _dot(kbt_s[slot], jnp.concatenate([v_s[slot], u.astype(BF16)], axis=0))

    def solve(g, pending):
        gen = _wkv_phase1(chunk, consts, problems(g))
        pending = list(pending)
        while True:
            try:
                next(gen)
            except StopIteration as done:
                sols = done.value
                break
            for step in pending[:2]:
                step()
            pending = pending[2:]
        for step in pending:
            step()
        for slot, sol in enumerate(sols):
            for name, ref in sol_refs.items():
                ref[slot] = sol[name]

    def steps_for(g):
        return [functools.partial(advance, slot, r) for slot, r in enumerate(group_rows(g))]

    n_groups = n // group
    solve(0, [])

    for g in range(1, n_groups):
        solve(g, steps_for(g - 1))
    for step in steps_for(n_groups - 1):
        step()


def _wkv_scan(r, kh, v, kk, bb, ldf, ldb, tm, chunk, group):
    b, t, _ = r.shape
    nt = t // tm
    sc = D_HEADS * chunk
    slots = 2 * group
    fspec = pl.BlockSpec((1, tm, MIX_W), lambda bi, i: (bi, i, 0))
    bspec = pl.BlockSpec((1, tm, MIX_W), lambda bi, i: (bi, nt - 1 - i, 0))
    shape = jax.ShapeDtypeStruct((b, t, MIX_W), F32)
    return pl.pallas_call(
        functools.partial(_wkv_scan_kernel, chunk, group),
        grid=(b, nt),
        in_specs=[fspec] * 6 + [bspec] * 6,
        out_specs=[fspec, bspec],
        out_shape=[shape, shape],
        scratch_shapes=[
            pltpu.VMEM((MIX_W, MIX_W), F32), pltpu.VMEM((MIX_W, MIX_W), F32),
            pltpu.VMEM((slots, 2 * sc, MIX_W), BF16),
            pltpu.VMEM((slots, sc, MIX_W), F32),
            pltpu.VMEM((slots, sc, MIX_W), F32),
            pltpu.VMEM((slots, MIX_W, 2 * sc), BF16),
            pltpu.VMEM((slots, sc, MIX_W), BF16),
            pltpu.VMEM((slots, MIX_W, MIX_W), F32),
        ],
        compiler_params=_params("parallel", "arbitrary"),
        name="wkv_scan",
    )(r, kh, v, kk, bb, ldf, r, kh, v, kk, bb, ldb)


def _wkv_out(yf, yb, bonus, g, gn_w, gn_b):
    y = yf + yb
    mean = _head_sum(y, HEAD_DIM) * (1.0 / HEAD_DIM)
    yc = y - mean
    var = _head_sum(yc * yc, HEAD_DIM) * (1.0 / HEAD_DIM)
    yn = yc * lax.rsqrt(var + WKV_GN_EPS) * gn_w + gn_b
    return ((yn + bonus) * g).astype(BF16)


def _post_kernel(x_ref, oa_ref, ob_ref, oc_ref, yf_ref, yb_ref, bonus_ref, g_ref, gnw_ref, gnb_ref,
                 wo_ref, gpost_ref, gpre_ref, wg_ref, wu_ref, wd_ref, gfpost_ref, y_ref):
    tm = x_ref.shape[0]
    groups = [slice(s, s + tm // ROW_GROUPS) for s in range(0, tm, tm // ROW_GROUPS)]
    o_d = [_wkv_out(yf_ref[r], yb_ref[r], bonus_ref[r], g_ref[r], gnw_ref[...], gnb_ref[...]) for r in groups]
    mix = [_dot(oa_ref[r], wo_ref[0]) + _dot(ob_ref[r], wo_ref[1]) + _dot(oc_ref[r], wo_ref[2])
           + _dot(od, wo_ref[3]) for r, od in zip(groups, o_d)]
    x = [x_ref[r] + _rms_rows(m, gpost_ref[...]) for r, m in zip(groups, mix)]
    h = [_rms_rows(xg, gpre_ref[...]).astype(BF16) for xg in x]
    gate = [_dot(hg, wg_ref[...]) for hg in h]
    up = [_dot(hg, wu_ref[...]) for hg in h]
    act = [(g * _sigmoid(g) * u).astype(BF16) for g, u in zip(gate, up)]
    f = [_dot(a, wd_ref[...]) for a in act]
    for r, xg, fg in zip(groups, x, f):
        y_ref[r] = xg + _rms_rows(fg, gfpost_ref[...])


def _post(x2, oa, ob, oc, wkv_parts, gn_w, gn_b, w_out4, g_post, g_pre, wg, wu, wd, gf_post, tm):
    n, d = x2.shape
    dff = wg.shape[1]
    tok = lambda w: pl.BlockSpec((tm, w), lambda i: (i, 0))
    row = pl.BlockSpec((1, d), lambda i: (0, 0))
    mrow = pl.BlockSpec((1, MIX_W), lambda i: (0, 0))
    return pl.pallas_call(
        _post_kernel,
        grid=(n // tm,),
        in_specs=[tok(d)] + [tok(MIX_W)] * 7 + [mrow, mrow,
                  pl.BlockSpec((4, MIX_W, d), lambda i: (0, 0, 0)), row, row,
                  pl.BlockSpec((d, dff), lambda i: (0, 0)),
                  pl.BlockSpec((d, dff), lambda i: (0, 0)),
                  pl.BlockSpec((dff, d), lambda i: (0, 0)), row],
        out_specs=tok(d),
        out_shape=jax.ShapeDtypeStruct((n, d), F32),
        compiler_params=_params("parallel"),
        name="out_proj_ffn",
    )(x2, oa, ob, oc, *wkv_parts, gn_w, gn_b, w_out4, g_post, g_pre, wg, wu, wd, gf_post)


def _angles(pos, rot_dim, theta):
    inv = theta ** (-jnp.arange(0, rot_dim, 2, dtype=F32) / rot_dim)
    return pos.astype(F32)[:, None] * inv[None, :]


def _rope_tables(t):
    rows = t // GRID_W
    row_idx = jnp.repeat(jnp.arange(rows), GRID_W)
    col_idx = jnp.tile(jnp.arange(GRID_W), rows)
    pos = jnp.arange(t)
    ar = _angles(row_idx, HEAD_DIM // 2, A_THETA)
    ac = _angles(col_idx, HEAD_DIM // 2, A_THETA)
    cos_a = jnp.concatenate([jnp.cos(ar), jnp.cos(ar), jnp.cos(ac), jnp.cos(ac)], axis=1)
    sin_a = jnp.concatenate([-jnp.sin(ar), jnp.sin(ar), -jnp.sin(ac), jnp.sin(ac)], axis=1)
    tabs_a = (jnp.tile(cos_a, (1, 4)), jnp.tile(sin_a, (1, 4)))
    ab = _angles(pos, B_ROT, B_THETA)
    pad1 = jnp.ones((t, B_SUB - B_ROT), F32)
    pad0 = jnp.zeros((t, B_SUB - B_ROT), F32)
    cos_b = jnp.concatenate([jnp.cos(ab), jnp.cos(ab), pad1], axis=1)
    sin_b = jnp.concatenate([-jnp.sin(ab), jnp.sin(ab), pad0], axis=1)
    tabs_b = (jnp.tile(cos_b, (1, 8)), jnp.tile(sin_b, (1, 8)))
    ang_c = _angles(pos, HEAD_DIM, C_THETA)
    cos_c = jnp.concatenate([jnp.cos(ang_c), jnp.cos(ang_c)], axis=1)
    sin_c = jnp.concatenate([-jnp.sin(ang_c), jnp.sin(ang_c)], axis=1)
    tabs_c = (jnp.tile(cos_c, (1, 4)), jnp.tile(sin_c, (1, 4)))
    return tabs_a, tabs_b, tabs_c


def _tile(n, pref):
    while n % pref:
        pref //= 2
    return pref


def _layer_weights(l, w):
    d_model = w["w_in"].shape[1]
    row = lambda a: a.reshape(1, -1).astype(F32)
    zeros = lambda r: jnp.zeros((r, MIX_W), F32)
    w_lora = w["d_w_up"].shape[2]
    return {
        "g_mix_pre": row(w["norm_mix_pre"][l]), "g_mix_post": row(w["norm_mix_post"][l]),
        "g_ffn_pre": row(w["norm_ffn_pre"][l]), "g_ffn_post": row(w["norm_ffn_post"][l]),
        "w_abc": w["w_in"][l][:, :ABC_COLS].astype(BF16), "w_d": w["w_in"][l][:, ABC_COLS:].astype(BF16),
        "w_out4": w["w_out"][l].reshape(4, MIX_W, d_model).astype(BF16),
        "a_gain": jnp.concatenate([jnp.tile(row(w["a_q_gain"][l]), (1, A_HEADS)),
                                   jnp.tile(row(w["a_k_gain"][l]), (1, A_KV_HEADS))], axis=1),
        "b_lambda": w["b_lambda"][l].astype(F32), "b_gain": w["b_subln_gain"][l].reshape(-1, 1).astype(F32),
        "c_gain": row(w["c_gn_gain"][l]),
        "wkv": {
            "mu_prev": row(w["d_mu_prev"][l]), "mu_next": row(w["d_mu_next"][l]),
            "w0": w["d_w0"][l].astype(F32),
            "wup_f": jnp.concatenate([w["d_w_up"][l, 0], zeros(w_lora)], axis=0).astype(BF16),
            "wup_b": jnp.concatenate([zeros(w_lora), w["d_w_up"][l, 1]], axis=0).astype(BF16),
            "a0": row(w["d_a0"][l]),
            "aup": jnp.concatenate([w["d_a_up"][l], zeros(64)], axis=0).astype(BF16),
            "gup1": jnp.concatenate([zeros(64), w["d_g_up"][l][:64]], axis=0).astype(BF16),
            "gup2": w["d_g_up"][l][64:].astype(BF16),
            "k_k": row(w["d_k_k"][l]), "k_a": row(w["d_k_a"][l]), "r_k": row(w["d_r_k"][l]),
        },
        "gn_w": row(w["d_gn_w"][l]), "gn_b": row(w["d_gn_b"][l]),
        "wg": w["ffn_w_gate"][l].astype(BF16), "wu": w["ffn_w_up"][l].astype(BF16),
        "wd": w["ffn_w_down"][l].astype(BF16),
    }


def _trunk(x, layers):
    b, t, d = x.shape
    n = b * t
    tabs = _rope_tables(t)
    tm_proj = _tile(t, 512)
    tm_post = _tile(n, 512)
    tq = _tile(t, 256)
    tm_wkv = _tile(t, 512)
    tm_scan = _tile(t, 2048)
    chunk = 32
    x2 = x.reshape(n, d)
    for l, p in enumerate(layers):
        lam_init = 0.8 - 0.6 * math.exp(-0.3 * l)
        zabc, zd = _in_proj(x2, p["g_mix_pre"], p["w_abc"], p["w_d"], tabs, p["a_gain"], t, tm_proj)
        zabc = zabc.reshape(b, t, ABC_COLS)
        zd = zd.reshape(b, t, D_COLS)
        o_a = _mixer_gqa(zabc, tq)
        o_b = _mixer_diff(zabc, p["b_lambda"], p["b_gain"], lam_init, tq)
        o_c = _mixer_ret(zabc, p["c_gain"], _tile(t, 256))
        r, kh, v, kk, bb, ldf, ldb, g, bonus = _wkv_prep(zd, p["wkv"], tm_wkv)
        yf, yb = _wkv_scan(r, kh, v, kk, bb, ldf, ldb, tm_scan, chunk, min(4, tm_scan // chunk))
        wkv_parts = [a.reshape(n, MIX_W) for a in (yf, yb, bonus, g)]
        x2 = _post(x2, o_a.reshape(n, MIX_W), o_b.reshape(n, MIX_W), o_c.reshape(n, MIX_W), wkv_parts,
                   p["gn_w"], p["gn_b"], p["w_out4"], p["g_mix_post"], p["g_ffn_pre"], p["wg"], p["wu"], p["wd"],
                   p["g_ffn_post"], tm_post)
    return x2.reshape(b, t, d)


def kernel(x_prompt, x_sample, norm_mix_pre, norm_mix_post, norm_ffn_pre, norm_ffn_post, w_in, w_out,
           a_q_gain, a_k_gain, b_lambda, b_subln_gain, c_gn_gain, d_mu_prev, d_mu_next, d_w0, d_w_up,
           d_a0, d_a_up, d_g_up, d_k_k, d_k_a, d_r_k, d_gn_w, d_gn_b, ffn_w_gate, ffn_w_up, ffn_w_down):
    w = {
        "norm_mix_pre": norm_mix_pre, "norm_mix_post": norm_mix_post,
        "norm_ffn_pre": norm_ffn_pre, "norm_ffn_post": norm_ffn_post,
        "w_in": w_in, "w_out": w_out, "a_q_gain": a_q_gain, "a_k_gain": a_k_gain,
        "b_lambda": b_lambda, "b_subln_gain": b_subln_gain, "c_gn_gain": c_gn_gain,
        "d_mu_prev": d_mu_prev, "d_mu_next": d_mu_next, "d_w0": d_w0, "d_w_up": d_w_up,
        "d_a0": d_a0, "d_a_up": d_a_up, "d_g_up": d_g_up, "d_k_k": d_k_k, "d_k_a": d_k_a,
        "d_r_k": d_r_k, "d_gn_w": d_gn_w, "d_gn_b": d_gn_b,
        "ffn_w_gate": ffn_w_gate, "ffn_w_up": ffn_w_up, "ffn_w_down": ffn_w_down,
    }
    layers = [_layer_weights(l, w) for l in range(w_in.shape[0])]
    return (_trunk(x_prompt, layers), _trunk(x_sample, layers))
```

```python
import functools
import math

import jax
import jax.numpy as jnp
from jax import lax
from jax.experimental import pallas as pl
from jax.experimental.pallas import tpu as pltpu

F32 = jnp.float32
BF16 = jnp.bfloat16

HEAD_DIM = 64
GRID_W = 64
NORM_EPS = 1e-6
A_HEADS, A_KV_HEADS, A_THETA = 4, 2, 10000.0
B_HEADS, B_SUB, B_ROT, B_THETA = 4, 32, 8, 500000.0
C_HEADS, C_THETA = 4, 10000.0
D_HEADS = 4
WKV_GN_EPS = 64e-5
MIX_W = 256
ABC_COLS = 2304
A_Q, A_K, A_V = slice(0, 256), slice(256, 384), slice(384, 512)
B_Q, B_K, B_V = slice(512, 768), slice(768, 1024), slice(1024, 1280)
C_Q, C_K, C_VG = slice(1280, 1536), slice(1536, 1792), slice(1792, 2304)
D_COLS = 1088
BF16_SUBLANES = 16
VEXT_ROWS = HEAD_DIM + BF16_SUBLANES
LOG2E = math.log2(math.e)
KEY_CHUNK = 512
ROW_GROUPS = 2
SHIFT_LIMIT = 100.0
VMEM_LIMIT = 56 * 1024 * 1024


def _params(*sem):
    return pltpu.CompilerParams(dimension_semantics=sem, vmem_limit_bytes=VMEM_LIMIT)


def _rms_rows(x, gain):
    return x * lax.rsqrt(jnp.mean(x * x, axis=-1, keepdims=True) + NORM_EPS) * gain


def _split2(x):
    hi = x.astype(BF16)
    lo = (x - hi.astype(F32)).astype(BF16)
    return hi, lo


def _split3(x):
    h1 = x.astype(BF16)
    r1 = x - h1.astype(F32)
    h2 = r1.astype(BF16)
    h3 = (r1 - h2.astype(F32)).astype(BF16)
    return h1, h2, h3


def _head_sum(x, seg):
    w = x.shape[-1]
    r = lax.broadcasted_iota(jnp.int32, (w, w), 0) // seg
    c = lax.broadcasted_iota(jnp.int32, (w, w), 1) // seg
    bd = jnp.where(r == c, 1.0, 0.0).astype(BF16)
    hi, lo = _split2(x)
    return (jnp.dot(hi, bd, preferred_element_type=F32)
            + jnp.dot(lo, bd, preferred_element_type=F32))


def _rope(x, cos, sin, half):
    w = x.shape[-1]
    lane = lax.broadcasted_iota(jnp.int32, x.shape, 1)
    nxt = pltpu.roll(x, w - half, 1)
    prv = pltpu.roll(x, half, 1)
    return x * cos + jnp.where((lane % (2 * half)) < half, nxt, prv) * sin


def _sigmoid(x):
    return 0.5 * jnp.tanh(0.5 * x) + 0.5


def _dot(a, b):
    return jnp.dot(a, b, preferred_element_type=F32)


def _dot_nt(a, b):
    return lax.dot_general(a, b, (((1,), (1,)), ((), ())), preferred_element_type=F32)


def _store_qk_prepared(z, r, tabs, gain_a, out_ref):
    (cos_a, sin_a), (cos_b, sin_b), (cos_c, sin_c) = tabs

    def head_norm(x, gain):
        return x * lax.rsqrt(_head_sum(x * x, HEAD_DIM) * (1.0 / HEAD_DIM) + NORM_EPS) * gain

    a_qk = head_norm(z[:, A_Q.start:A_K.stop], gain_a)
    kw = A_K.stop - A_K.start
    pieces = {
        A_Q: _rope(a_qk[:, :A_Q.stop], cos_a, sin_a, HEAD_DIM // 4) * (HEAD_DIM ** -0.5 * LOG2E),
        A_K: _rope(a_qk[:, A_Q.stop:], cos_a[:, :kw], sin_a[:, :kw], HEAD_DIM // 4),
        B_Q: _rope(z[:, B_Q], cos_b, sin_b, B_ROT // 2) * (B_SUB ** -0.5 * LOG2E),
        B_K: _rope(z[:, B_K], cos_b, sin_b, B_ROT // 2),
        C_Q: _rope(z[:, C_Q], cos_c, sin_c, HEAD_DIM // 2),
        C_K: _rope(z[:, C_K], cos_c, sin_c, HEAD_DIM // 2) * (HEAD_DIM ** -0.5),
    }
    for cols in (A_V, B_V, C_VG):
        pieces[cols] = z[:, cols]
    for cols, val in pieces.items():
        out_ref[r, cols] = val.astype(BF16)


def _in_proj_kernel(x_ref, g_ref, wabc_ref, wd_ref, ca_ref, sa_ref, cb_ref, sb_ref, cc_ref, sc_ref, ga_ref,
                    zabc_ref, zd_ref):
    tm = x_ref.shape[0]
    groups = [slice(s, s + tm // ROW_GROUPS) for s in range(0, tm, tm // ROW_GROUPS)]
    h = [_rms_rows(x_ref[r], g_ref[...]).astype(BF16) for r in groups]
    z = [_dot(hg, wabc_ref[...]) for hg in h]
    for r, hg in zip(groups, h):
        zd_ref[r] = _dot(hg, wd_ref[...])
    for r, zg in zip(groups, z):
        tabs = ((ca_ref[r], sa_ref[r]), (cb_ref[r], sb_ref[r]), (cc_ref[r], sc_ref[r]))
        _store_qk_prepared(zg, r, tabs, ga_ref[...], zabc_ref)


def _in_proj(x2, gain, w_abc, w_d, tabs, gain_a, t, tm):
    n, d = x2.shape
    per_seq = t // tm
    tab = pl.BlockSpec((tm, MIX_W), lambda i: (i % per_seq, 0))
    return pl.pallas_call(
        _in_proj_kernel,
        grid=(n // tm,),
        in_specs=[
            pl.BlockSpec((tm, d), lambda i: (i, 0)),
            pl.BlockSpec((1, d), lambda i: (0, 0)),
            pl.BlockSpec((d, ABC_COLS), lambda i: (0, 0)),
            pl.BlockSpec((d, D_COLS), lambda i: (0, 0)),
            tab, tab, tab, tab, tab, tab,
            pl.BlockSpec((1, A_K.stop), lambda i: (0, 0)),
        ],
        out_specs=[
            pl.BlockSpec((tm, ABC_COLS), lambda i: (i, 0)),
            pl.BlockSpec((tm, D_COLS), lambda i: (i, 0)),
        ],
        out_shape=[
            jax.ShapeDtypeStruct((n, ABC_COLS), BF16),
            jax.ShapeDtypeStruct((n, D_COLS), F32),
        ],
        compiler_params=_params("parallel"),
        name="in_proj",
    )(x2, gain, w_abc, w_d, *tabs[0], *tabs[1], *tabs[2], gain_a)


def _value_ext_t(v_t):
    row = lax.broadcasted_iota(jnp.int32, (VEXT_ROWS - HEAD_DIM, v_t.shape[1]), 0)
    return jnp.concatenate([v_t, jnp.where(row == 0, 1.0, 0.0)], axis=0).astype(BF16)


def _attend_exact(k_ref, q_ts, vx_refs):
    outs = []
    for q_t, vx in zip(q_ts, vx_refs):
        s_t = _dot(k_ref[...], q_t)
        p_t = jnp.exp2(s_t - jnp.max(s_t, axis=0, keepdims=True)).astype(BF16)
        o = _dot(vx[...], p_t)
        outs.append(o[:HEAD_DIM] / o[HEAD_DIM:HEAD_DIM + 1])
    return outs


def _attend_shifted(k_ref, q_ts, shifts, vx_refs):
    t = k_ref.shape[0]
    ck = min(t, KEY_CHUNK)
    steps = [(j, slice(c * ck, (c + 1) * ck)) for j in range(len(q_ts)) for c in range(t // ck)]
    acc = [None] * len(q_ts)
    s_prev = p_prev = None
    for i in range(len(steps) + 2):
        s_new = p_new = None
        if i < len(steps):
            j, keys = steps[i]
            s_new = _dot(k_ref[keys, :], q_ts[j])
        if 1 <= i <= len(steps):
            p_new = jnp.exp2(s_prev - shifts[steps[i - 1][0]]).astype(BF16)
        if i >= 2:
            j, keys = steps[i - 2]
            o = _dot(vx_refs[j][:, keys], p_prev)
            acc[j] = o if acc[j] is None else acc[j] + o
        s_prev, p_prev = s_new, p_new
    return [a[:HEAD_DIM] / a[HEAD_DIM:HEAD_DIM + 1] for a in acc]


def _attend_t(k_ref, kn_ref, q_t, head_rows, head_keys, vx_refs, finish):
    c = k_ref.shape[1]
    tq = q_t.shape[1]
    qsq = q_t * q_t
    q_ts, shifts = [], []
    for rows, keys in zip(head_rows, head_keys):
        pieces = [jnp.zeros((keys.start, tq), F32), q_t[rows], jnp.zeros((c - keys.stop, tq), F32)]
        q_ts.append(jnp.concatenate([x for x in pieces if x.shape[0]], axis=0).astype(BF16))
        qn = jnp.sqrt(jnp.sum(qsq[rows], axis=0, keepdims=True))
        shifts.append(qn * kn_ref[0:1, keys.start:keys.start + 1])
    worst = jnp.max(jnp.concatenate(shifts, axis=0))

    @pl.when(2.0 * worst < SHIFT_LIMIT)
    def _():
        finish(_attend_shifted(k_ref, q_ts, shifts, vx_refs))

    @pl.when(jnp.logical_not(2.0 * worst < SHIFT_LIMIT))
    def _():
        finish(_attend_exact(k_ref, q_ts, vx_refs))


def _gqa_kernel(q_ref, k_ref, v_ref, o_ref, kn_s, vx_s):
    @pl.when(pl.program_id(1) == 0)
    def _():
        k = k_ref[0].astype(F32)
        kn_s[...] = jnp.sqrt(jnp.max(_head_sum(k * k, HEAD_DIM), axis=0, keepdims=True))
        v_t = v_ref[0].astype(F32).T
        for g in range(A_KV_HEADS):
            vx_s[g] = _value_ext_t(v_t[g * HEAD_DIM:(g + 1) * HEAD_DIM])

    q_t = q_ref[0].astype(F32).T
    k_s = k_ref.at[0]
    group = A_HEADS // A_KV_HEADS
    head_rows = [slice(h * HEAD_DIM, (h + 1) * HEAD_DIM) for h in range(A_HEADS)]
    head_keys = [slice((h // group) * HEAD_DIM, (h // group + 1) * HEAD_DIM) for h in range(A_HEADS)]

    def finish(outs):
        o_ref[0] = jnp.concatenate(outs, axis=0).T.astype(BF16)

    _attend_t(k_s, kn_s, q_t, head_rows, head_keys, [vx_s.at[h // group] for h in range(A_HEADS)], finish)


def _mixer_gqa(zabc, tq):
    b, t, _ = zabc.shape
    return pl.pallas_call(
        _gqa_kernel,
        grid=(b, t // tq),
        in_specs=[
            pl.BlockSpec((1, tq, 256), lambda bi, i: (bi, i, 0)),
            pl.BlockSpec((1, t, 128), lambda bi, i: (bi, 0, 2)),
            pl.BlockSpec((1, t, 128), lambda bi, i: (bi, 0, 3)),
        ],
        out_specs=pl.BlockSpec((1, tq, MIX_W), lambda bi, i: (bi, i, 0)),
        out_shape=jax.ShapeDtypeStruct((b, t, MIX_W), BF16),
        scratch_shapes=[
            pltpu.VMEM((1, A_KV_HEADS * HEAD_DIM), F32),
            pltpu.VMEM((A_KV_HEADS, VEXT_ROWS, t), BF16),
        ],
        compiler_params=_params("parallel", "arbitrary"),
        name="mixer_gqa",
    )(zabc, zabc, zabc)


def _diff_kernel(lam_init, q_ref, k_ref, v_ref, lam_ref, gain_ref, o_ref, kn_s, vx_s):
    @pl.when(pl.program_id(1) == 0)
    def _():
        k = k_ref[0].astype(F32)
        kn_s[...] = jnp.sqrt(jnp.max(_head_sum(k * k, B_SUB), axis=0, keepdims=True))
        v_t = v_ref[0].astype(F32).T
        for h in range(B_HEADS):
            vx_s[h] = _value_ext_t(v_t[h * HEAD_DIM:(h + 1) * HEAD_DIM])

    lp = lam_ref[...]
    lam = (jnp.exp(jnp.sum(lp[0:1] * lp[1:2], axis=1, keepdims=True))
           - jnp.exp(jnp.sum(lp[2:3] * lp[3:4], axis=1, keepdims=True)) + lam_init)
    q_t = q_ref[0].astype(F32).T
    k_s = k_ref.at[0]
    sub_rows = [slice(j * B_SUB, (j + 1) * B_SUB) for j in range(2 * B_HEADS)]

    def finish(parts):
        outs = []
        for h in range(B_HEADS):
            o = parts[2 * h] - lam * parts[2 * h + 1]
            o = o * lax.rsqrt(jnp.mean(o * o, axis=0, keepdims=True) + NORM_EPS) * gain_ref[...]
            outs.append(o * (1.0 - lam_init))
        o_ref[0] = jnp.concatenate(outs, axis=0).T.astype(BF16)

    _attend_t(k_s, kn_s, q_t, sub_rows, sub_rows, [vx_s.at[j // 2] for j in range(2 * B_HEADS)], finish)


def _mixer_diff(zabc, lam_params, gain, lam_init, tq):
    b, t, _ = zabc.shape
    return pl.pallas_call(
        functools.partial(_diff_kernel, lam_init),
        grid=(b, t // tq),
        in_specs=[
            pl.BlockSpec((1, tq, 256), lambda bi, i: (bi, i, 2)),
            pl.BlockSpec((1, t, 256), lambda bi, i: (bi, 0, 3)),
            pl.BlockSpec((1, t, 256), lambda bi, i: (bi, 0, 4)),
            pl.BlockSpec((4, B_SUB), lambda bi, i: (0, 0)),
            pl.BlockSpec((HEAD_DIM, 1), lambda bi, i: (0, 0)),
        ],
        out_specs=pl.BlockSpec((1, tq, MIX_W), lambda bi, i: (bi, i, 0)),
        out_shape=jax.ShapeDtypeStruct((b, t, MIX_W), BF16),
        scratch_shapes=[
            pltpu.VMEM((1, 2 * B_HEADS * B_SUB), F32),
            pltpu.VMEM((B_HEADS, VEXT_ROWS, t), BF16),
        ],
        compiler_params=_params("parallel", "arbitrary"),
        name="mixer_diff",
    )(zabc, zabc, zabc, lam_params, gain)


def _ret_log_gammas():
    lg = [math.log1p(-(2.0 ** (-5.0 - h))) for h in range(C_HEADS)]
    return lg, lg[::-1]


def _lane_consts(vals, shape):
    head = lax.broadcasted_iota(jnp.int32, shape, len(shape) - 1) // HEAD_DIM
    out = jnp.full(shape, vals[-1], F32)
    for h in range(len(vals) - 2, -1, -1):
        out = jnp.where(head == h, vals[h], out)
    return out


def _ret_kernel(chunk, q_ref, k_ref, v_ref, g_ref, gain_ref, o_ref, acc_s, sf_s, sb_s, dm_s):
    t = q_ref.shape[1]
    c = chunk
    n = t // c
    lgf, lgb = _ret_log_gammas()
    qr_s = q_ref.at[0]
    kr_s = k_ref.at[0]

    ti = lax.broadcasted_iota(jnp.int32, (c, c), 0)
    si = lax.broadcasted_iota(jnp.int32, (c, c), 1)
    dist = (ti - si).astype(F32)
    lane_head = lax.broadcasted_iota(jnp.int32, (1, MIX_W), 1) // HEAD_DIM
    row = lax.broadcasted_iota(jnp.int32, (c, MIX_W), 0).astype(F32)
    lgf_l = _lane_consts(lgf, (c, MIX_W))
    lgb_l = _lane_consts(lgb, (c, MIX_W))
    qdec_f = jnp.exp(lgf_l * (row + 1.0))
    kdec_f = jnp.exp(lgf_l * (c - 1.0 - row))
    qdec_b = jnp.exp(lgb_l * (c - row))
    kdec_b = jnp.exp(lgb_l * row)
    r2 = lax.broadcasted_iota(jnp.int32, (MIX_W, MIX_W), 0) // HEAD_DIM
    c2 = lax.broadcasted_iota(jnp.int32, (MIX_W, MIX_W), 1) // HEAD_DIM
    same_head = r2 == c2
    gf_blk = jnp.where(same_head, jnp.exp(_lane_consts(lgf, (MIX_W, MIX_W)) * c), 0.0)
    gb_blk = jnp.where(same_head, jnp.exp(_lane_consts(lgb, (MIX_W, MIX_W)) * c), 0.0)

    head_mask = [(lane_head == h).astype(BF16) for h in range(C_HEADS)]
    for h in range(C_HEADS):
        dm_s[h] = jnp.where(dist > 0, jnp.exp(lgf[h] * dist),
                            jnp.where(dist < 0, jnp.exp(-lgb[h] * dist), 2.0))

    def intra(qc, kc, vc):
        ps = [(_dot_nt(qc * head_mask[h], kc) * dm_s[h]).astype(BF16) for h in range(C_HEADS)]
        v_stack = jnp.concatenate([vc * head_mask[h] for h in range(C_HEADS)], axis=0)
        return _dot(jnp.concatenate(ps, axis=1), v_stack)

    sf_s[...] = jnp.zeros_like(sf_s)
    sb_s[...] = jnp.zeros_like(sb_s)

    def fwd(i, carry):
        rows = pl.ds(pl.multiple_of(i * c, c), c)
        qc = qr_s[rows, :]
        kc = kr_s[rows, :]
        vc = v_ref[0, rows, :]
        o = intra(qc, kc, vc) + _dot((qc.astype(F32) * qdec_f).astype(BF16), sf_s[...].astype(BF16))
        acc_s[rows, :] = o
        kd = (kc.astype(F32) * kdec_f).T.astype(BF16)
        sf_s[...] = gf_blk * sf_s[...] + jnp.where(same_head, _dot(kd, vc), 0.0)
        return carry

    lax.fori_loop(0, n, fwd, 0)

    def bwd(i, carry):
        rows = pl.ds(pl.multiple_of((n - 1 - i) * c, c), c)
        qc = qr_s[rows, :].astype(F32)
        kc = kr_s[rows, :].astype(F32)
        vc = v_ref[0, rows, :]
        acc_s[rows, :] = acc_s[rows, :] + _dot((qc * qdec_b).astype(BF16), sb_s[...].astype(BF16))
        kd = (kc * kdec_b).T.astype(BF16)
        sb_s[...] = gb_blk * sb_s[...] + jnp.where(same_head, _dot(kd, vc), 0.0)
        return carry

    lax.fori_loop(0, n, bwd, 0)

    o = acc_s[...]
    o = o * lax.rsqrt(_head_sum(o * o, HEAD_DIM) * (1.0 / HEAD_DIM) + NORM_EPS) * gain_ref[...]
    g = g_ref[0].astype(F32)
    o_ref[0] = (o * (g * _sigmoid(g))).astype(BF16)


def _mixer_ret(zabc, gain, chunk):
    b, t, _ = zabc.shape
    blk = lambda j: pl.BlockSpec((1, t, 256), lambda bi: (bi, 0, j))
    return pl.pallas_call(
        functools.partial(_ret_kernel, chunk),
        grid=(b,),
        in_specs=[blk(5), blk(6), blk(7), blk(8),
                  pl.BlockSpec((1, 256), lambda bi: (0, 0))],
        out_specs=pl.BlockSpec((1, t, MIX_W), lambda bi: (bi, 0, 0)),
        out_shape=jax.ShapeDtypeStruct((b, t, MIX_W), BF16),
        scratch_shapes=[
            pltpu.VMEM((t, MIX_W), F32),
            pltpu.VMEM((MIX_W, MIX_W), F32),
            pltpu.VMEM((MIX_W, MIX_W), F32),
            pltpu.VMEM((C_HEADS, chunk, chunk), F32),
        ],
        compiler_params=_params("parallel"),
        name="mixer_ret",
    )(zabc, zabc, zabc, zabc, gain)


def _wkv_prep_kernel(z_ref, zp_ref, zn_ref, mup_ref, mun_ref, w0_ref, wupf_ref, wupb_ref, a0_ref, aup_ref,
                     gup1_ref, gup2_ref, kk_ref, ka_ref, rk_ref,
                     r_o, k_o, v_o, kk_o, b_o, ldf_o, ldb_o, g_o, bonus_o):
    i = pl.program_id(1)
    last = pl.num_programs(1) - 1
    z = z_ref[0]
    tm = z.shape[0]
    row = lax.broadcasted_iota(jnp.int32, z.shape, 0)
    prev_row = zp_ref[0, 0, 7:8, :] * jnp.where(i > 0, 1.0, 0.0)
    next_row = zn_ref[0, 0, 0:1, :] * jnp.where(i < last, 1.0, 0.0)
    z_prev = jnp.where(row == 0, prev_row, pltpu.roll(z, 1, 0))
    z_next = jnp.where(row == tm - 1, next_row, pltpu.roll(z, tm - 1, 0))
    mu_p, mu_n = mup_ref[...], mun_ref[...]
    u = z * (1.0 - mu_p - mu_n) + z_prev * mu_p + z_next * mu_n
    r = u[:, 0:256]
    k = u[:, 256:512]
    v = u[:, 512:768]
    wd = jnp.tanh(u[:, 768:896]).astype(BF16)
    ag = u[:, 896:1024]
    g2 = u[:, 1024:1088]

    def log_decay(w0, wup):
        return _sigmoid(w0 + _dot(wd, wup)) * (-math.exp(-0.5))

    ldf_o[0] = log_decay(w0_ref[0:1, :], wupf_ref[...])
    ldb_o[0] = log_decay(w0_ref[1:2, :], wupb_ref[...])
    a = _sigmoid(a0_ref[...] + _dot(ag.astype(BF16), aup_ref[...]))
    g_o[0] = (_dot(_sigmoid(ag).astype(BF16), gup1_ref[...])
              + _dot(_sigmoid(g2).astype(BF16), gup2_ref[...])).astype(BF16)
    kk = k * kk_ref[...]
    kk = kk / jnp.maximum(jnp.sqrt(_head_sum(kk * kk, HEAD_DIM)), 1e-12)
    kh = k * (1.0 + (a - 1.0) * ka_ref[...])
    r_o[0] = r.astype(BF16)
    k_o[0] = kh.astype(BF16)
    v_o[0] = v.astype(BF16)
    kk_o[0] = kk.astype(BF16)
    b_o[0] = (kk * a).astype(BF16)
    bonus_o[0] = (_head_sum(r * kh * rk_ref[...], HEAD_DIM) * v).astype(BF16)


def _wkv_prep(zd, p, tm):
    b, t, _ = zd.shape
    zd8 = zd.reshape(b, t // 8, 8, D_COLS)
    r8 = tm // 8
    nb8 = t // 8
    row = lambda w: pl.BlockSpec((1, w), lambda bi, i: (0, 0))
    full = lambda a: pl.BlockSpec(a.shape, lambda bi, i: (0,) * a.ndim)
    out_spec = pl.BlockSpec((1, tm, MIX_W), lambda bi, i: (bi, i, 0))
    out_shape = jax.ShapeDtypeStruct((b, t, MIX_W), F32)
    out_bf16 = jax.ShapeDtypeStruct((b, t, MIX_W), BF16)
    return pl.pallas_call(
        _wkv_prep_kernel,
        grid=(b, t // tm),
        in_specs=[
            pl.BlockSpec((1, tm, D_COLS), lambda bi, i: (bi, i, 0)),
            pl.BlockSpec((1, 1, 8, D_COLS), lambda bi, i: (bi, jnp.maximum(i * r8 - 1, 0), 0, 0)),
            pl.BlockSpec((1, 1, 8, D_COLS), lambda bi, i: (bi, jnp.minimum((i + 1) * r8, nb8 - 1), 0, 0)),
            row(D_COLS), row(D_COLS),
            full(p["w0"]), full(p["wup_f"]), full(p["wup_b"]), row(256), full(p["aup"]),
            full(p["gup1"]), full(p["gup2"]), row(256), row(256), row(256),
        ],
        out_specs=[out_spec] * 9,
        out_shape=[out_bf16] * 5 + [out_shape] * 2 + [out_bf16] * 2,
        compiler_params=_params("parallel", "parallel"),
        name="wkv_prep",
    )(zd, zd8, zd8, p["mu_prev"], p["mu_next"], p["w0"], p["wup_f"], p["wup_b"], p["a0"], p["aup"],
      p["gup1"], p["gup2"], p["k_k"], p["k_a"], p["r_k"])


def _wkv_chunk_consts(chunk):
    c = chunk
    sc = D_HEADS * c
    ti = lax.broadcasted_iota(jnp.int32, (c, c), 0)
    si = lax.broadcasted_iota(jnp.int32, (c, c), 1)
    srow = lax.broadcasted_iota(jnp.int32, (sc, MIX_W), 0) // c
    slane = lax.broadcasted_iota(jnp.int32, (sc, MIX_W), 1) // HEAD_DIM
    gt = lax.broadcasted_iota(jnp.int32, (2 * sc, 2 * sc), 0)
    gs = lax.broadcasted_iota(jnp.int32, (2 * sc, 2 * sc), 1)
    t_in, s_in = gt % sc, gs % sc
    same = (t_in // c) == (s_in // c)
    upper = gt < sc
    st = lax.broadcasted_iota(jnp.int32, (sc, sc), 0)
    ss = lax.broadcasted_iota(jnp.int32, (sc, sc), 1)
    r2 = lax.broadcasted_iota(jnp.int32, (MIX_W, MIX_W), 0)
    c2 = lax.broadcasted_iota(jnp.int32, (MIX_W, MIX_W), 1)
    return {
        "tri": {False: jnp.where(si <= ti, 1.0, 0.0).astype(BF16), True: jnp.where(si >= ti, 1.0, 0.0).astype(BF16)},
        "hmask": srow == slane,
        "gmask": {False: same & ((s_in < t_in) | ((s_in == t_in) & ~upper)),
                  True: same & ((s_in > t_in) | ((s_in == t_in) & ~upper))},
        "eye": jnp.where(st == ss, 1.0, 0.0),
        "diag": r2 == c2,
    }


def _wkv_phase1(chunk, consts, probs):
    c = chunk
    sc = D_HEADS * c
    hmask = consts["hmask"]

    def stack(a):
        return jnp.where(hmask, jnp.concatenate([a] * D_HEADS, axis=0), 0.0).astype(BF16)

    pre = []
    for rev, r, kh, v, kk, bb, lw in probs:
        tri = consts["tri"][rev]
        l1, l2, l3 = _split3(lw)
        cl = _dot(tri, l1) + _dot(tri, l2) + _dot(tri, l3)
        tot = jnp.sum(lw, axis=0, keepdims=True)
        pre.append((cl, tot))
    yield
    ops = []
    for (rev, r, kh, v, kk, bb, lw), (cl, tot) in zip(probs, pre):
        w_inv = jnp.exp(-cl)
        w_end = jnp.exp(tot - cl)
        lhs = jnp.concatenate([stack(kk * jnp.exp(cl - lw)), stack(r * jnp.exp(cl))], axis=0)
        rhs = jnp.concatenate([stack(bb * w_inv), stack(kh * w_inv)], axis=0)
        kb = jnp.concatenate([stack(kh * w_end), stack(bb * w_end)], axis=0)
        wc = jnp.sum(jnp.where(consts["diag"], jnp.exp(tot), 0.0), axis=1, keepdims=True)
        wc = jnp.broadcast_to(wc, (MIX_W, MIX_W))
        ops.append((lhs, rhs, stack(v), kb, wc))
    yield
    gram = [jnp.where(consts["gmask"][p[0]], _dot_nt(o[0], o[1]), 0.0) for p, o in zip(probs, ops)]
    l_ab = [g[:sc, :sc] for g in gram]
    l_ak = [g[:sc, sc:].astype(BF16) for g in gram]
    m_r = [g[sc:, :].astype(BF16) for g in gram]
    yield

    steps = int(math.log2(c))
    pw = [(-l).astype(BF16) for l in l_ab]
    inv = [consts["eye"] - l for l in l_ab]
    w = [_dot(a, o[2]).astype(BF16) for a, o in zip(l_ak, ops)]
    for k in range(1, steps):
        if k == 1:
            pw = [_dot(p, p).astype(BF16) for p in pw]
            yield
        if k < steps - 1:
            nxt = [_dot(p, jnp.concatenate([p, i.astype(BF16)], axis=1)) for p, i in zip(pw, inv)]
            inv = [i + n[:, sc:] for i, n in zip(inv, nxt)]
            pw = [n[:, :sc].astype(BF16) for n in nxt]
        else:
            inv = [i + _dot(p, i.astype(BF16)) for p, i in zip(pw, inv)]
        yield
    inv = [i.astype(BF16) for i in inv]

    gu = [_dot(i, jnp.concatenate([o[0][:sc], wv], axis=1)) for i, o, wv in zip(inv, ops, w)]
    yield
    kb_t = [o[3].astype(F32).T.astype(BF16) for o in ops]
    mgu = [_dot(m[:, :sc], x.astype(BF16)) for m, x in zip(m_r, gu)]
    mv = [_dot(m[:, sc:], o[2]) for m, o in zip(m_r, ops)]
    yield
    return [{
        "qg": jnp.concatenate([(o[0][sc:].astype(F32) - mg[:, :MIX_W]).astype(BF16),
                               g[:, :MIX_W].astype(BF16)], axis=0),
        "y_hat": a - mg[:, MIX_W:],
        "u_hat": g[:, MIX_W:],
        "kb_t": kt,
        "v": o[2],
        "wc": o[4],
    } for o, g, mg, a, kt in zip(ops, gu, mgu, mv, kb_t)]


def _wkv_scan_kernel(chunk, group, rf, kf, vf, kkf, bf, lf, rb, kb, vb, kkb, bb, lb, yf_o, yb_o,
                     xf_s, xb_s, qg_s, yh_s, uh_s, kbt_s, v_s, wc_s):
    @pl.when(pl.program_id(1) == 0)
    def _():
        xf_s[...] = jnp.zeros_like(xf_s)
        xb_s[...] = jnp.zeros_like(xb_s)

    tm = rf.shape[1]
    n = tm // chunk
    sc = D_HEADS * chunk
    consts = _wkv_chunk_consts(chunk)
    sol_refs = {"qg": qg_s, "y_hat": yh_s, "u_hat": uh_s, "kb_t": kbt_s, "v": v_s, "wc": wc_s}

    def group_rows(g):
        rows = []
        for u in range(group):
            rows.append(pl.ds(pl.multiple_of((g * group + u) * chunk, chunk), chunk))
            rows.append(pl.ds(pl.multiple_of((n - 1 - g * group - u) * chunk, chunk), chunk))
        return rows

    def problems(g):
        probs = []
        for slot, rows in enumerate(group_rows(g)):
            if slot % 2:
                probs.append((True, rb[0, rows, :], kb[0, rows, :], vb[0, rows, :], kkb[0, rows, :],
                              bb[0, rows, :], lb[0, rows, :]))
            else:
                probs.append((False, rf[0, rows, :], kf[0, rows, :], vf[0, rows, :], kkf[0, rows, :],
                              bf[0, rows, :], lf[0, rows, :]))
        return probs

    def advance(slot, rows):
        x_s, y_o = (xb_s, yb_o) if slot % 2 else (xf_s, yf_o)
        x = x_s[...]
        res = _dot(qg_s[slot], x.astype(BF16))
        ys = res[:sc] + yh_s[slot]
        y = ys[0:chunk]
        for h in range(1, D_HEADS):
            y = y + ys[h * chunk:(h + 1) * chunk]
        y_o[0, rows, :] = y
        u = -(res[sc:] + uh_s[slot])
        x_s[...] = x * wc_s[slot] + _dot(kbt_s[slot], jnp.concatenate([v_s[slot], u.astype(BF16)], axis=0))

    def solve(g, pending):
        gen = _wkv_phase1(chunk, consts, problems(g))
        pending = list(pending)
        while True:
            try:
                next(gen)
            except StopIteration as done:
                sols = done.value
                break
            for step in pending[:2]:
                step()
            pending = pending[2:]
        for step in pending:
            step()
        for slot, sol in enumerate(sols):
            for name, ref in sol_refs.items():
                ref[slot] = sol[name]

    def steps_for(g):
        return [functools.partial(advance, slot, r) for slot, r in enumerate(group_rows(g))]

    n_groups = n // group
    solve(0, [])

    for g in range(1, n_groups):
        solve(g, steps_for(g - 1))
    for step in steps_for(n_groups - 1):
        step()


def _wkv_scan(r, kh, v, kk, bb, ldf, ldb, tm, chunk, group):
    b, t, _ = r.shape
    nt = t // tm
    sc = D_HEADS * chunk
    slots = 2 * group
    fspec = pl.BlockSpec((1, tm, MIX_W), lambda bi, i: (bi, i, 0))
    bspec = pl.BlockSpec((1, tm, MIX_W), lambda bi, i: (bi, nt - 1 - i, 0))
    shape = jax.ShapeDtypeStruct((b, t, MIX_W), F32)
    return pl.pallas_call(
        functools.partial(_wkv_scan_kernel, chunk, group),
        grid=(b, nt),
        in_specs=[fspec] * 6 + [bspec] * 6,
        out_specs=[fspec, bspec],
        out_shape=[shape, shape],
        scratch_shapes=[
            pltpu.VMEM((MIX_W, MIX_W), F32), pltpu.VMEM((MIX_W, MIX_W), F32),
            pltpu.VMEM((slots, 2 * sc, MIX_W), BF16),
            pltpu.VMEM((slots, sc, MIX_W), F32),
            pltpu.VMEM((slots, sc, MIX_W), F32),
            pltpu.VMEM((slots, MIX_W, 2 * sc), BF16),
            pltpu.VMEM((slots, sc, MIX_W), BF16),
            pltpu.VMEM((slots, MIX_W, MIX_W), F32),
        ],
        compiler_params=_params("parallel", "arbitrary"),
        name="wkv_scan",
    )(r, kh, v, kk, bb, ldf, r, kh, v, kk, bb, ldb)


def _wkv_out(yf, yb, bonus, g, gn_w, gn_b):
    y = yf + yb
    mean = _head_sum(y, HEAD_DIM) * (1.0 / HEAD_DIM)
    yc = y - mean
    var = _head_sum(yc * yc, HEAD_DIM) * (1.0 / HEAD_DIM)
    yn = yc * lax.rsqrt(var + WKV_GN_EPS) * gn_w + gn_b
    return ((yn + bonus) * g).astype(BF16)


def _post_kernel(x_ref, oa_ref, ob_ref, oc_ref, yf_ref, yb_ref, bonus_ref, g_ref, gnw_ref, gnb_ref,
                 wo_ref, gpost_ref, gpre_ref, wg_ref, wu_ref, wd_ref, gfpost_ref, y_ref):
    tm = x_ref.shape[0]
    groups = [slice(s, s + tm // ROW_GROUPS) for s in range(0, tm, tm // ROW_GROUPS)]
    o_d = [_wkv_out(yf_ref[r], yb_ref[r], bonus_ref[r], g_ref[r], gnw_ref[...], gnb_ref[...]) for r in groups]
    mix = [_dot(oa_ref[r], wo_ref[0]) + _dot(ob_ref[r], wo_ref[1]) + _dot(oc_ref[r], wo_ref[2])
           + _dot(od, wo_ref[3]) for r, od in zip(groups, o_d)]
    x = [x_ref[r] + _rms_rows(m, gpost_ref[...]) for r, m in zip(groups, mix)]
    h = [_rms_rows(xg, gpre_ref[...]).astype(BF16) for xg in x]
    gate = [_dot(hg, wg_ref[...]) for hg in h]
    up = [_dot(hg, wu_ref[...]) for hg in h]
    act = [(g * _sigmoid(g) * u).astype(BF16) for g, u in zip(gate, up)]
    f = [_dot(a, wd_ref[...]) for a in act]
    for r, xg, fg in zip(groups, x, f):
        y_ref[r] = xg + _rms_rows(fg, gfpost_ref[...])


def _post(x2, oa, ob, oc, wkv_parts, gn_w, gn_b, w_out4, g_post, g_pre, wg, wu, wd, gf_post, tm):
    n, d = x2.shape
    dff = wg.shape[1]
    tok = lambda w: pl.BlockSpec((tm, w), lambda i: (i, 0))
    row = pl.BlockSpec((1, d), lambda i: (0, 0))
    mrow = pl.BlockSpec((1, MIX_W), lambda i: (0, 0))
    return pl.pallas_call(
        _post_kernel,
        grid=(n // tm,),
        in_specs=[tok(d)] + [tok(MIX_W)] * 7 + [mrow, mrow,
                  pl.BlockSpec((4, MIX_W, d), lambda i: (0, 0, 0)), row, row,
                  pl.BlockSpec((d, dff), lambda i: (0, 0)),
                  pl.BlockSpec((d, dff), lambda i: (0, 0)),
                  pl.BlockSpec((dff, d), lambda i: (0, 0)), row],
        out_specs=tok(d),
        out_shape=jax.ShapeDtypeStruct((n, d), F32),
        compiler_params=_params("parallel"),
        name="out_proj_ffn",
    )(x2, oa, ob, oc, *wkv_parts, gn_w, gn_b, w_out4, g_post, g_pre, wg, wu, wd, gf_post)


def _angles(pos, rot_dim, theta):
    inv = theta ** (-jnp.arange(0, rot_dim, 2, dtype=F32) / rot_dim)
    return pos.astype(F32)[:, None] * inv[None, :]


def _rope_tables(t):
    rows = t // GRID_W
    row_idx = jnp.repeat(jnp.arange(rows), GRID_W)
    col_idx = jnp.tile(jnp.arange(GRID_W), rows)
    pos = jnp.arange(t)
    ar = _angles(row_idx, HEAD_DIM // 2, A_THETA)
    ac = _angles(col_idx, HEAD_DIM // 2, A_THETA)
    cos_a = jnp.concatenate([jnp.cos(ar), jnp.cos(ar), jnp.cos(ac), jnp.cos(ac)], axis=1)
    sin_a = jnp.concatenate([-jnp.sin(ar), jnp.sin(ar), -jnp.sin(ac), jnp.sin(ac)], axis=1)
    tabs_a = (jnp.tile(cos_a, (1, 4)), jnp.tile(sin_a, (1, 4)))
    ab = _angles(pos, B_ROT, B_THETA)
    pad1 = jnp.ones((t, B_SUB - B_ROT), F32)
    pad0 = jnp.zeros((t, B_SUB - B_ROT), F32)
    cos_b = jnp.concatenate([jnp.cos(ab), jnp.cos(ab), pad1], axis=1)
    sin_b = jnp.concatenate([-jnp.sin(ab), jnp.sin(ab), pad0], axis=1)
    tabs_b = (jnp.tile(cos_b, (1, 8)), jnp.tile(sin_b, (1, 8)))
    ang_c = _angles(pos, HEAD_DIM, C_THETA)
    cos_c = jnp.concatenate([jnp.cos(ang_c), jnp.cos(ang_c)], axis=1)
    sin_c = jnp.concatenate([-jnp.sin(ang_c), jnp.sin(ang_c)], axis=1)
    tabs_c = (jnp.tile(cos_c, (1, 4)), jnp.tile(sin_c, (1, 4)))
    return tabs_a, tabs_b, tabs_c


def _tile(n, pref):
    while n % pref:
        pref //= 2
    return pref


def _layer_weights(l, w):
    d_model = w["w_in"].shape[1]
    row = lambda a: a.reshape(1, -1).astype(F32)
    zeros = lambda r: jnp.zeros((r, MIX_W), F32)
    w_lora = w["d_w_up"].shape[2]
    return {
        "g_mix_pre": row(w["norm_mix_pre"][l]), "g_mix_post": row(w["norm_mix_post"][l]),
        "g_ffn_pre": row(w["norm_ffn_pre"][l]), "g_ffn_post": row(w["norm_ffn_post"][l]),
        "w_abc": w["w_in"][l][:, :ABC_COLS].astype(BF16), "w_d": w["w_in"][l][:, ABC_COLS:].astype(BF16),
        "w_out4": w["w_out"][l].reshape(4, MIX_W, d_model).astype(BF16),
        "a_gain": jnp.concatenate([jnp.tile(row(w["a_q_gain"][l]), (1, A_HEADS)),
                                   jnp.tile(row(w["a_k_gain"][l]), (1, A_KV_HEADS))], axis=1),
        "b_lambda": w["b_lambda"][l].astype(F32), "b_gain": w["b_subln_gain"][l].reshape(-1, 1).astype(F32),
        "c_gain": row(w["c_gn_gain"][l]),
        "wkv": {
            "mu_prev": row(w["d_mu_prev"][l]), "mu_next": row(w["d_mu_next"][l]),
            "w0": w["d_w0"][l].astype(F32),
            "wup_f": jnp.concatenate([w["d_w_up"][l, 0], zeros(w_lora)], axis=0).astype(BF16),
            "wup_b": jnp.concatenate([zeros(w_lora), w["d_w_up"][l, 1]], axis=0).astype(BF16),
            "a0": row(w["d_a0"][l]),
            "aup": jnp.concatenate([w["d_a_up"][l], zeros(64)], axis=0).astype(BF16),
            "gup1": jnp.concatenate([zeros(64), w["d_g_up"][l][:64]], axis=0).astype(BF16),
            "gup2": w["d_g_up"][l][64:].astype(BF16),
            "k_k": row(w["d_k_k"][l]), "k_a": row(w["d_k_a"][l]), "r_k": row(w["d_r_k"][l]),
        },
        "gn_w": row(w["d_gn_w"][l]), "gn_b": row(w["d_gn_b"][l]),
        "wg": w["ffn_w_gate"][l].astype(BF16), "wu": w["ffn_w_up"][l].astype(BF16),
        "wd": w["ffn_w_down"][l].astype(BF16),
    }


def _trunk(x, layers):
    b, t, d = x.shape
    n = b * t
    tabs = _rope_tables(t)
    tm_proj = _tile(t, 512)
    tm_post = _tile(n, 512)
    tq = _tile(t, 256)
    tm_wkv = _tile(t, 1024)
    tm_scan = _tile(t, 1024)
    chunk = 32
    x2 = x.reshape(n, d)
    for l, p in enumerate(layers):
        lam_init = 0.8 - 0.6 * math.exp(-0.3 * l)
        zabc, zd = _in_proj(x2, p["g_mix_pre"], p["w_abc"], p["w_d"], tabs, p["a_gain"], t, tm_proj)
        zabc = zabc.reshape(b, t, ABC_COLS)
        zd = zd.reshape(b, t, D_COLS)
        o_a = _mixer_gqa(zabc, tq)
        o_b = _mixer_diff(zabc, p["b_lambda"], p["b_gain"], lam_init, tq)
        o_c = _mixer_ret(zabc, p["c_gain"], _tile(t, 256))
        r, kh, v, kk, bb, ldf, ldb, g, bonus = _wkv_prep(zd, p["wkv"], tm_wkv)
        yf, yb = _wkv_scan(r, kh, v, kk, bb, ldf, ldb, tm_scan, chunk, min(4, tm_scan // chunk))
        wkv_parts = [a.reshape(n, MIX_W) for a in (yf, yb, bonus, g)]
        x2 = _post(x2, o_a.reshape(n, MIX_W), o_b.reshape(n, MIX_W), o_c.reshape(n, MIX_W), wkv_parts,
                   p["gn_w"], p["gn_b"], p["w_out4"], p["g_mix_post"], p["g_ffn_pre"], p["wg"], p["wu"], p["wd"],
                   p["g_ffn_post"], tm_post)
    return x2.reshape(b, t, d)


def kernel(x_prompt, x_sample, norm_mix_pre, norm_mix_post, norm_ffn_pre, norm_ffn_post, w_in, w_out,
           a_q_gain, a_k_gain, b_lambda, b_subln_gain, c_gn_gain, d_mu_prev, d_mu_next, d_w0, d_w_up,
           d_a0, d_a_up, d_g_up, d_k_k, d_k_a, d_r_k, d_gn_w, d_gn_b, ffn_w_gate, ffn_w_up, ffn_w_down):
    w = {
        "norm_mix_pre": norm_mix_pre, "norm_mix_post": norm_mix_post,
        "norm_ffn_pre": norm_ffn_pre, "norm_ffn_post": norm_ffn_post,
        "w_in": w_in, "w_out": w_out, "a_q_gain": a_q_gain, "a_k_gain": a_k_gain,
        "b_lambda": b_lambda, "b_subln_gain": b_subln_gain, "c_gn_gain": c_gn_gain,
        "d_mu_prev": d_mu_prev, "d_mu_next": d_mu_next, "d_w0": d_w0, "d_w_up": d_w_up,
        "d_a0": d_a0, "d_a_up": d_a_up, "d_g_up": d_g_up, "d_k_k": d_k_k, "d_k_a": d_k_a,
        "d_r_k": d_r_k, "d_gn_w": d_gn_w, "d_gn_b": d_gn_b,
        "ffn_w_gate": ffn_w_gate, "ffn_w_up": ffn_w_up, "ffn_w_down": ffn_w_down,
    }
    layers = [_layer_weights(l, w) for l in range(w_in.shape[0])]
    return (_trunk(x_prompt, layers), _trunk(x_sample, layers))
```

```python
import functools
import math

import jax
import jax.numpy as jnp
from jax import lax
from jax.experimental import pallas as pl
from jax.experimental.pallas import tpu as pltpu

F32 = jnp.float32
BF16 = jnp.bfloat16

HEAD_DIM = 64
GRID_W = 64
NORM_EPS = 1e-6
A_HEADS, A_KV_HEADS, A_THETA = 4, 2, 10000.0
B_HEADS, B_SUB, B_ROT, B_THETA = 4, 32, 8, 500000.0
C_HEADS, C_THETA = 4, 10000.0
D_HEADS = 4
WKV_GN_EPS = 64e-5
MIX_W = 256
ABC_COLS = 2304
A_Q, A_K, A_V = slice(0, 256), slice(256, 384), slice(384, 512)
B_Q, B_K, B_V = slice(512, 768), slice(768, 1024), slice(1024, 1280)
C_Q, C_K, C_VG = slice(1280, 1536), slice(1536, 1792), slice(1792, 2304)
D_COLS = 1088
BF16_SUBLANES = 16
VEXT_ROWS = HEAD_DIM + BF16_SUBLANES
LOG2E = math.log2(math.e)
KEY_CHUNK = 512
ROW_GROUPS = 2
SHIFT_LIMIT = 100.0
VMEM_LIMIT = 56 * 1024 * 1024


def _params(*sem):
    return pltpu.CompilerParams(dimension_semantics=sem, vmem_limit_bytes=VMEM_LIMIT)


def _rms_rows(x, gain):
    return x * lax.rsqrt(jnp.mean(x * x, axis=-1, keepdims=True) + NORM_EPS) * gain


def _split2(x):
    hi = x.astype(BF16)
    lo = (x - hi.astype(F32)).astype(BF16)
    return hi, lo


def _split3(x):
    h1 = x.astype(BF16)
    r1 = x - h1.astype(F32)
    h2 = r1.astype(BF16)
    h3 = (r1 - h2.astype(F32)).astype(BF16)
    return h1, h2, h3


def _head_sum(x, seg):
    w = x.shape[-1]
    r = lax.broadcasted_iota(jnp.int32, (w, w), 0) // seg
    c = lax.broadcasted_iota(jnp.int32, (w, w), 1) // seg
    bd = jnp.where(r == c, 1.0, 0.0).astype(BF16)
    hi, lo = _split2(x)
    return (jnp.dot(hi, bd, preferred_element_type=F32)
            + jnp.dot(lo, bd, preferred_element_type=F32))


def _rope(x, cos, sin, half):
    w = x.shape[-1]
    lane = lax.broadcasted_iota(jnp.int32, x.shape, 1)
    nxt = pltpu.roll(x, w - half, 1)
    prv = pltpu.roll(x, half, 1)
    return x * cos + jnp.where((lane % (2 * half)) < half, nxt, prv) * sin


def _sigmoid(x):
    return 0.5 * jnp.tanh(0.5 * x) + 0.5


def _dot(a, b):
    return jnp.dot(a, b, preferred_element_type=F32)


def _dot_nt(a, b):
    return lax.dot_general(a, b, (((1,), (1,)), ((), ())), preferred_element_type=F32)


def _store_qk_prepared(z, r, tabs, gain_a, out_ref):
    (cos_a, sin_a), (cos_b, sin_b), (cos_c, sin_c) = tabs

    def head_norm(x, gain):
        return x * lax.rsqrt(_head_sum(x * x, HEAD_DIM) * (1.0 / HEAD_DIM) + NORM_EPS) * gain

    a_qk = head_norm(z[:, A_Q.start:A_K.stop], gain_a)
    kw = A_K.stop - A_K.start
    pieces = {
        A_Q: _rope(a_qk[:, :A_Q.stop], cos_a, sin_a, HEAD_DIM // 4) * (HEAD_DIM ** -0.5 * LOG2E),
        A_K: _rope(a_qk[:, A_Q.stop:], cos_a[:, :kw], sin_a[:, :kw], HEAD_DIM // 4),
        B_Q: _rope(z[:, B_Q], cos_b, sin_b, B_ROT // 2) * (B_SUB ** -0.5 * LOG2E),
        B_K: _rope(z[:, B_K], cos_b, sin_b, B_ROT // 2),
        C_Q: _rope(z[:, C_Q], cos_c, sin_c, HEAD_DIM // 2),
        C_K: _rope(z[:, C_K], cos_c, sin_c, HEAD_DIM // 2) * (HEAD_DIM ** -0.5),
    }
    for cols in (A_V, B_V, C_VG):
        pieces[cols] = z[:, cols]
    for cols, val in pieces.items():
        out_ref[r, cols] = val.astype(BF16)


def _in_proj_kernel(x_ref, g_ref, wabc_ref, wd_ref, ca_ref, sa_ref, cb_ref, sb_ref, cc_ref, sc_ref, ga_ref,
                    zabc_ref, zd_ref):
    tm = x_ref.shape[0]
    groups = [slice(s, s + tm // ROW_GROUPS) for s in range(0, tm, tm // ROW_GROUPS)]
    h = [_rms_rows(x_ref[r], g_ref[...]).astype(BF16) for r in groups]
    z = [_dot(hg, wabc_ref[...]) for hg in h]
    for r, hg, zg in zip(groups, h, z):
        tabs = ((ca_ref[r], sa_ref[r]), (cb_ref[r], sb_ref[r]), (cc_ref[r], sc_ref[r]))
        _store_qk_prepared(zg, r, tabs, ga_ref[...], zabc_ref)
        zd_ref[r] = _dot(hg, wd_ref[...])


def _in_proj(x2, gain, w_abc, w_d, tabs, gain_a, t, tm):
    n, d = x2.shape
    per_seq = t // tm
    tab = pl.BlockSpec((tm, MIX_W), lambda i: (i % per_seq, 0))
    return pl.pallas_call(
        _in_proj_kernel,
        grid=(n // tm,),
        in_specs=[
            pl.BlockSpec((tm, d), lambda i: (i, 0)),
            pl.BlockSpec((1, d), lambda i: (0, 0)),
            pl.BlockSpec((d, ABC_COLS), lambda i: (0, 0)),
            pl.BlockSpec((d, D_COLS), lambda i: (0, 0)),
            tab, tab, tab, tab, tab, tab,
            pl.BlockSpec((1, A_K.stop), lambda i: (0, 0)),
        ],
        out_specs=[
            pl.BlockSpec((tm, ABC_COLS), lambda i: (i, 0)),
            pl.BlockSpec((tm, D_COLS), lambda i: (i, 0)),
        ],
        out_shape=[
            jax.ShapeDtypeStruct((n, ABC_COLS), BF16),
            jax.ShapeDtypeStruct((n, D_COLS), F32),
        ],
        compiler_params=_params("parallel"),
        name="in_proj",
    )(x2, gain, w_abc, w_d, *tabs[0], *tabs[1], *tabs[2], gain_a)


def _value_ext_t(v_t):
    row = lax.broadcasted_iota(jnp.int32, (VEXT_ROWS - HEAD_DIM, v_t.shape[1]), 0)
    return jnp.concatenate([v_t, jnp.where(row == 0, 1.0, 0.0)], axis=0).astype(BF16)


def _attend_exact(k_ref, q_ts, vx_refs):
    outs = []
    for q_t, vx in zip(q_ts, vx_refs):
        s_t = _dot(k_ref[...], q_t)
        p_t = jnp.exp2(s_t - jnp.max(s_t, axis=0, keepdims=True)).astype(BF16)
        o = _dot(vx[...], p_t)
        outs.append(o[:HEAD_DIM] / o[HEAD_DIM:HEAD_DIM + 1])
    return outs


def _attend_shifted(k_ref, q_ts, shifts, vx_refs):
    t = k_ref.shape[0]
    ck = min(t, KEY_CHUNK)
    steps = [(j, slice(c * ck, (c + 1) * ck)) for j in range(len(q_ts)) for c in range(t // ck)]
    acc = [None] * len(q_ts)
    s_prev = p_prev = None
    for i in range(len(steps) + 2):
        s_new = p_new = None
        if i < len(steps):
            j, keys = steps[i]
            s_new = _dot(k_ref[keys, :], q_ts[j])
        if 1 <= i <= len(steps):
            p_new = jnp.exp2(s_prev - shifts[steps[i - 1][0]]).astype(BF16)
        if i >= 2:
            j, keys = steps[i - 2]
            o = _dot(vx_refs[j][:, keys], p_prev)
            acc[j] = o if acc[j] is None else acc[j] + o
        s_prev, p_prev = s_new, p_new
    return [a[:HEAD_DIM] / a[HEAD_DIM:HEAD_DIM + 1] for a in acc]


def _attend_t(k_ref, kn_ref, q_t, head_rows, head_keys, vx_refs, finish):
    c = k_ref.shape[1]
    tq = q_t.shape[1]
    qsq = q_t * q_t
    q_ts, shifts = [], []
    for rows, keys in zip(head_rows, head_keys):
        pieces = [jnp.zeros((keys.start, tq), F32), q_t[rows], jnp.zeros((c - keys.stop, tq), F32)]
        q_ts.append(jnp.concatenate([x for x in pieces if x.shape[0]], axis=0).astype(BF16))
        qn = jnp.sqrt(jnp.sum(qsq[rows], axis=0, keepdims=True))
        shifts.append(qn * kn_ref[0:1, keys.start:keys.start + 1])
    worst = jnp.max(jnp.concatenate(shifts, axis=0))

    @pl.when(2.0 * worst < SHIFT_LIMIT)
    def _():
        finish(_attend_shifted(k_ref, q_ts, shifts, vx_refs))

    @pl.when(jnp.logical_not(2.0 * worst < SHIFT_LIMIT))
    def _():
        finish(_attend_exact(k_ref, q_ts, vx_refs))


def _gqa_kernel(q_ref, k_ref, v_ref, o_ref, kn_s, vx_s):
    @pl.when(pl.program_id(1) == 0)
    def _():
        k = k_ref[0].astype(F32)
        kn_s[...] = jnp.sqrt(jnp.max(_head_sum(k * k, HEAD_DIM), axis=0, keepdims=True))
        v_t = v_ref[0].astype(F32).T
        for g in range(A_KV_HEADS):
            vx_s[g] = _value_ext_t(v_t[g * HEAD_DIM:(g + 1) * HEAD_DIM])

    q_t = q_ref[0].astype(F32).T
    k_s = k_ref.at[0]
    group = A_HEADS // A_KV_HEADS
    head_rows = [slice(h * HEAD_DIM, (h + 1) * HEAD_DIM) for h in range(A_HEADS)]
    head_keys = [slice((h // group) * HEAD_DIM, (h // group + 1) * HEAD_DIM) for h in range(A_HEADS)]

    def finish(outs):
        o_ref[0] = jnp.concatenate(outs, axis=0).T.astype(BF16)

    _attend_t(k_s, kn_s, q_t, head_rows, head_keys, [vx_s.at[h // group] for h in range(A_HEADS)], finish)


def _mixer_gqa(zabc, tq):
    b, t, _ = zabc.shape
    return pl.pallas_call(
        _gqa_kernel,
        grid=(b, t // tq),
        in_specs=[
            pl.BlockSpec((1, tq, 256), lambda bi, i: (bi, i, 0)),
            pl.BlockSpec((1, t, 128), lambda bi, i: (bi, 0, 2)),
            pl.BlockSpec((1, t, 128), lambda bi, i: (bi, 0, 3)),
        ],
        out_specs=pl.BlockSpec((1, tq, MIX_W), lambda bi, i: (bi, i, 0)),
        out_shape=jax.ShapeDtypeStruct((b, t, MIX_W), BF16),
        scratch_shapes=[
            pltpu.VMEM((1, A_KV_HEADS * HEAD_DIM), F32),
            pltpu.VMEM((A_KV_HEADS, VEXT_ROWS, t), BF16),
        ],
        compiler_params=_params("parallel", "arbitrary"),
        name="mixer_gqa",
    )(zabc, zabc, zabc)


def _diff_kernel(lam_init, q_ref, k_ref, v_ref, lam_ref, gain_ref, o_ref, kn_s, vx_s):
    @pl.when(pl.program_id(1) == 0)
    def _():
        k = k_ref[0].astype(F32)
        kn_s[...] = jnp.sqrt(jnp.max(_head_sum(k * k, B_SUB), axis=0, keepdims=True))
        v_t = v_ref[0].astype(F32).T
        for h in range(B_HEADS):
            vx_s[h] = _value_ext_t(v_t[h * HEAD_DIM:(h + 1) * HEAD_DIM])

    lp = lam_ref[...]
    lam = (jnp.exp(jnp.sum(lp[0:1] * lp[1:2], axis=1, keepdims=True))
           - jnp.exp(jnp.sum(lp[2:3] * lp[3:4], axis=1, keepdims=True)) + lam_init)
    q_t = q_ref[0].astype(F32).T
    k_s = k_ref.at[0]
    sub_rows = [slice(j * B_SUB, (j + 1) * B_SUB) for j in range(2 * B_HEADS)]

    def finish(parts):
        outs = []
        for h in range(B_HEADS):
            o = parts[2 * h] - lam * parts[2 * h + 1]
            o = o * lax.rsqrt(jnp.mean(o * o, axis=0, keepdims=True) + NORM_EPS) * gain_ref[...]
            outs.append(o * (1.0 - lam_init))
        o_ref[0] = jnp.concatenate(outs, axis=0).T.astype(BF16)

    _attend_t(k_s, kn_s, q_t, sub_rows, sub_rows, [vx_s.at[j // 2] for j in range(2 * B_HEADS)], finish)


def _mixer_diff(zabc, lam_params, gain, lam_init, tq):
    b, t, _ = zabc.shape
    return pl.pallas_call(
        functools.partial(_diff_kernel, lam_init),
        grid=(b, t // tq),
        in_specs=[
            pl.BlockSpec((1, tq, 256), lambda bi, i: (bi, i, 2)),
            pl.BlockSpec((1, t, 256), lambda bi, i: (bi, 0, 3)),
            pl.BlockSpec((1, t, 256), lambda bi, i: (bi, 0, 4)),
            pl.BlockSpec((4, B_SUB), lambda bi, i: (0, 0)),
            pl.BlockSpec((HEAD_DIM, 1), lambda bi, i: (0, 0)),
        ],
        out_specs=pl.BlockSpec((1, tq, MIX_W), lambda bi, i: (bi, i, 0)),
        out_shape=jax.ShapeDtypeStruct((b, t, MIX_W), BF16),
        scratch_shapes=[
            pltpu.VMEM((1, 2 * B_HEADS * B_SUB), F32),
            pltpu.VMEM((B_HEADS, VEXT_ROWS, t), BF16),
        ],
        compiler_params=_params("parallel", "arbitrary"),
        name="mixer_diff",
    )(zabc, zabc, zabc, lam_params, gain)


def _ret_log_gammas():
    lg = [math.log1p(-(2.0 ** (-5.0 - h))) for h in range(C_HEADS)]
    return lg, lg[::-1]


def _lane_consts(vals, shape):
    head = lax.broadcasted_iota(jnp.int32, shape, len(shape) - 1) // HEAD_DIM
    out = jnp.full(shape, vals[-1], F32)
    for h in range(len(vals) - 2, -1, -1):
        out = jnp.where(head == h, vals[h], out)
    return out


def _ret_kernel(chunk, q_ref, k_ref, v_ref, g_ref, gain_ref, o_ref, acc_s, sf_s, sb_s, dm_s):
    t = q_ref.shape[1]
    c = chunk
    n = t // c
    lgf, lgb = _ret_log_gammas()
    qr_s = q_ref.at[0]
    kr_s = k_ref.at[0]

    ti = lax.broadcasted_iota(jnp.int32, (c, c), 0)
    si = lax.broadcasted_iota(jnp.int32, (c, c), 1)
    dist = (ti - si).astype(F32)
    lane_head = lax.broadcasted_iota(jnp.int32, (1, MIX_W), 1) // HEAD_DIM
    row = lax.broadcasted_iota(jnp.int32, (c, MIX_W), 0).astype(F32)
    lgf_l = _lane_consts(lgf, (c, MIX_W))
    lgb_l = _lane_consts(lgb, (c, MIX_W))
    qdec_f = jnp.exp(lgf_l * (row + 1.0))
    kdec_f = jnp.exp(lgf_l * (c - 1.0 - row))
    qdec_b = jnp.exp(lgb_l * (c - row))
    kdec_b = jnp.exp(lgb_l * row)
    r2 = lax.broadcasted_iota(jnp.int32, (MIX_W, MIX_W), 0) // HEAD_DIM
    c2 = lax.broadcasted_iota(jnp.int32, (MIX_W, MIX_W), 1) // HEAD_DIM
    same_head = r2 == c2
    gf_blk = jnp.where(same_head, jnp.exp(_lane_consts(lgf, (MIX_W, MIX_W)) * c), 0.0)
    gb_blk = jnp.where(same_head, jnp.exp(_lane_consts(lgb, (MIX_W, MIX_W)) * c), 0.0)

    head_mask = [(lane_head == h).astype(BF16) for h in range(C_HEADS)]
    for h in range(C_HEADS):
        dm_s[h] = jnp.where(dist > 0, jnp.exp(lgf[h] * dist),
                            jnp.where(dist < 0, jnp.exp(-lgb[h] * dist), 2.0))

    def intra(qc, kc, vc):
        ps = [(_dot_nt(qc * head_mask[h], kc) * dm_s[h]).astype(BF16) for h in range(C_HEADS)]
        v_stack = jnp.concatenate([vc * head_mask[h] for h in range(C_HEADS)], axis=0)
        return _dot(jnp.concatenate(ps, axis=1), v_stack)

    sf_s[...] = jnp.zeros_like(sf_s)
    sb_s[...] = jnp.zeros_like(sb_s)

    def fwd(i, carry):
        rows = pl.ds(pl.multiple_of(i * c, c), c)
        qc = qr_s[rows, :]
        kc = kr_s[rows, :]
        vc = v_ref[0, rows, :]
        o = intra(qc, kc, vc) + _dot((qc.astype(F32) * qdec_f).astype(BF16), sf_s[...].astype(BF16))
        acc_s[rows, :] = o
        kd = (kc.astype(F32) * kdec_f).T.astype(BF16)
        sf_s[...] = gf_blk * sf_s[...] + jnp.where(same_head, _dot(kd, vc), 0.0)
        return carry

    lax.fori_loop(0, n, fwd, 0)

    def bwd(i, carry):
        rows = pl.ds(pl.multiple_of((n - 1 - i) * c, c), c)
        qc = qr_s[rows, :].astype(F32)
        kc = kr_s[rows, :].astype(F32)
        vc = v_ref[0, rows, :]
        acc_s[rows, :] = acc_s[rows, :] + _dot((qc * qdec_b).astype(BF16), sb_s[...].astype(BF16))
        kd = (kc * kdec_b).T.astype(BF16)
        sb_s[...] = gb_blk * sb_s[...] + jnp.where(same_head, _dot(kd, vc), 0.0)
        return carry

    lax.fori_loop(0, n, bwd, 0)

    o = acc_s[...]
    o = o * lax.rsqrt(_head_sum(o * o, HEAD_DIM) * (1.0 / HEAD_DIM) + NORM_EPS) * gain_ref[...]
    g = g_ref[0].astype(F32)
    o_ref[0] = (o * (g * _sigmoid(g))).astype(BF16)


def _mixer_ret(zabc, gain, chunk):
    b, t, _ = zabc.shape
    blk = lambda j: pl.BlockSpec((1, t, 256), lambda bi: (bi, 0, j))
    return pl.pallas_call(
        functools.partial(_ret_kernel, chunk),
        grid=(b,),
        in_specs=[blk(5), blk(6), blk(7), blk(8),
                  pl.BlockSpec((1, 256), lambda bi: (0, 0))],
        out_specs=pl.BlockSpec((1, t, MIX_W), lambda bi: (bi, 0, 0)),
        out_shape=jax.ShapeDtypeStruct((b, t, MIX_W), BF16),
        scratch_shapes=[
            pltpu.VMEM((t, MIX_W), F32),
            pltpu.VMEM((MIX_W, MIX_W), F32),
            pltpu.VMEM((MIX_W, MIX_W), F32),
            pltpu.VMEM((C_HEADS, chunk, chunk), F32),
        ],
        compiler_params=_params("parallel"),
        name="mixer_ret",
    )(zabc, zabc, zabc, zabc, gain)


def _wkv_prep_kernel(z_ref, zp_ref, zn_ref, mup_ref, mun_ref, w0_ref, wupf_ref, wupb_ref, a0_ref, aup_ref,
                     gup1_ref, gup2_ref, kk_ref, ka_ref, rk_ref,
                     r_o, k_o, v_o, kk_o, b_o, ldf_o, ldb_o, g_o, bonus_o):
    i = pl.program_id(1)
    last = pl.num_programs(1) - 1
    z = z_ref[0]
    tm = z.shape[0]
    row = lax.broadcasted_iota(jnp.int32, z.shape, 0)
    prev_row = zp_ref[0, 0, 7:8, :] * jnp.where(i > 0, 1.0, 0.0)
    next_row = zn_ref[0, 0, 0:1, :] * jnp.where(i < last, 1.0, 0.0)
    z_prev = jnp.where(row == 0, prev_row, pltpu.roll(z, 1, 0))
    z_next = jnp.where(row == tm - 1, next_row, pltpu.roll(z, tm - 1, 0))
    mu_p, mu_n = mup_ref[...], mun_ref[...]
    u = z * (1.0 - mu_p - mu_n) + z_prev * mu_p + z_next * mu_n
    r = u[:, 0:256]
    k = u[:, 256:512]
    v = u[:, 512:768]
    wd = jnp.tanh(u[:, 768:896]).astype(BF16)
    ag = u[:, 896:1024]
    g2 = u[:, 1024:1088]

    def log_decay(w0, wup):
        return _sigmoid(w0 + _dot(wd, wup)) * (-math.exp(-0.5))

    ldf_o[0] = log_decay(w0_ref[0:1, :], wupf_ref[...])
    ldb_o[0] = log_decay(w0_ref[1:2, :], wupb_ref[...])
    a = _sigmoid(a0_ref[...] + _dot(ag.astype(BF16), aup_ref[...]))
    g_o[0] = (_dot(_sigmoid(ag).astype(BF16), gup1_ref[...])
              + _dot(_sigmoid(g2).astype(BF16), gup2_ref[...])).astype(BF16)
    kk = k * kk_ref[...]
    kk = kk / jnp.maximum(jnp.sqrt(_head_sum(kk * kk, HEAD_DIM)), 1e-12)
    kh = k * (1.0 + (a - 1.0) * ka_ref[...])
    r_o[0] = r.astype(BF16)
    k_o[0] = kh.astype(BF16)
    v_o[0] = v.astype(BF16)
    kk_o[0] = kk.astype(BF16)
    b_o[0] = (kk * a).astype(BF16)
    bonus_o[0] = (_head_sum(r * kh * rk_ref[...], HEAD_DIM) * v).astype(BF16)


def _wkv_prep(zd, p, tm):
    b, t, _ = zd.shape
    zd8 = zd.reshape(b, t // 8, 8, D_COLS)
    r8 = tm // 8
    nb8 = t // 8
    row = lambda w: pl.BlockSpec((1, w), lambda bi, i: (0, 0))
    full = lambda a: pl.BlockSpec(a.shape, lambda bi, i: (0,) * a.ndim)
    out_spec = pl.BlockSpec((1, tm, MIX_W), lambda bi, i: (bi, i, 0))
    out_shape = jax.ShapeDtypeStruct((b, t, MIX_W), F32)
    out_bf16 = jax.ShapeDtypeStruct((b, t, MIX_W), BF16)
    return pl.pallas_call(
        _wkv_prep_kernel,
        grid=(b, t // tm),
        in_specs=[
            pl.BlockSpec((1, tm, D_COLS), lambda bi, i: (bi, i, 0)),
            pl.BlockSpec((1, 1, 8, D_COLS), lambda bi, i: (bi, jnp.maximum(i * r8 - 1, 0), 0, 0)),
            pl.BlockSpec((1, 1, 8, D_COLS), lambda bi, i: (bi, jnp.minimum((i + 1) * r8, nb8 - 1), 0, 0)),
            row(D_COLS), row(D_COLS),
            full(p["w0"]), full(p["wup_f"]), full(p["wup_b"]), row(256), full(p["aup"]),
            full(p["gup1"]), full(p["gup2"]), row(256), row(256), row(256),
        ],
        out_specs=[out_spec] * 9,
        out_shape=[out_bf16] * 5 + [out_shape] * 2 + [out_bf16] * 2,
        compiler_params=_params("parallel", "parallel"),
        name="wkv_prep",
    )(zd, zd8, zd8, p["mu_prev"], p["mu_next"], p["w0"], p["wup_f"], p["wup_b"], p["a0"], p["aup"],
      p["gup1"], p["gup2"], p["k_k"], p["k_a"], p["r_k"])


def _wkv_chunk_consts(chunk):
    c = chunk
    sc = D_HEADS * c
    ti = lax.broadcasted_iota(jnp.int32, (c, c), 0)
    si = lax.broadcasted_iota(jnp.int32, (c, c), 1)
    srow = lax.broadcasted_iota(jnp.int32, (sc, MIX_W), 0) // c
    slane = lax.broadcasted_iota(jnp.int32, (sc, MIX_W), 1) // HEAD_DIM
    gt = lax.broadcasted_iota(jnp.int32, (2 * sc, 2 * sc), 0)
    gs = lax.broadcasted_iota(jnp.int32, (2 * sc, 2 * sc), 1)
    t_in, s_in = gt % sc, gs % sc
    same = (t_in // c) == (s_in // c)
    upper = gt < sc
    st = lax.broadcasted_iota(jnp.int32, (sc, sc), 0)
    ss = lax.broadcasted_iota(jnp.int32, (sc, sc), 1)
    r2 = lax.broadcasted_iota(jnp.int32, (MIX_W, MIX_W), 0)
    c2 = lax.broadcasted_iota(jnp.int32, (MIX_W, MIX_W), 1)
    return {
        "tri": {False: jnp.where(si <= ti, 1.0, 0.0).astype(BF16), True: jnp.where(si >= ti, 1.0, 0.0).astype(BF16)},
        "hmask": srow == slane,
        "gmask": {False: same & ((s_in < t_in) | ((s_in == t_in) & ~upper)),
                  True: same & ((s_in > t_in) | ((s_in == t_in) & ~upper))},
        "eye": jnp.where(st == ss, 1.0, 0.0),
        "diag": r2 == c2,
    }


def _wkv_phase1(chunk, consts, probs):
    c = chunk
    sc = D_HEADS * c
    hmask = consts["hmask"]

    def stack(a):
        return jnp.where(hmask, jnp.concatenate([a] * D_HEADS, axis=0), 0.0).astype(BF16)

    pre = []
    for rev, r, kh, v, kk, bb, lw in probs:
        tri = consts["tri"][rev]
        l1, l2, l3 = _split3(lw)
        cl = _dot(tri, l1) + _dot(tri, l2) + _dot(tri, l3)
        tot = jnp.sum(lw, axis=0, keepdims=True)
        pre.append((cl, tot))
    yield
    ops = []
    for (rev, r, kh, v, kk, bb, lw), (cl, tot) in zip(probs, pre):
        w_inv = jnp.exp(-cl)
        w_end = jnp.exp(tot - cl)
        lhs = jnp.concatenate([stack(kk * jnp.exp(cl - lw)), stack(r * jnp.exp(cl))], axis=0)
        rhs = jnp.concatenate([stack(bb * w_inv), stack(kh * w_inv)], axis=0)
        kb = jnp.concatenate([stack(kh * w_end), stack(bb * w_end)], axis=0)
        wc = jnp.sum(jnp.where(consts["diag"], jnp.exp(tot), 0.0), axis=1, keepdims=True)
        wc = jnp.broadcast_to(wc, (MIX_W, MIX_W))
        ops.append((lhs, rhs, stack(v), kb, wc))
    yield
    gram = [jnp.where(consts["gmask"][p[0]], _dot_nt(o[0], o[1]), 0.0) for p, o in zip(probs, ops)]
    l_ab = [g[:sc, :sc] for g in gram]
    l_ak = [g[:sc, sc:].astype(BF16) for g in gram]
    m_r = [g[sc:, :].astype(BF16) for g in gram]
    yield

    steps = int(math.log2(c))
    pw = [(-l).astype(BF16) for l in l_ab]
    inv = [consts["eye"] - l for l in l_ab]
    w = [_dot(a, o[2]).astype(BF16) for a, o in zip(l_ak, ops)]
    for k in range(1, steps):
        if k == 1:
            pw = [_dot(p, p).astype(BF16) for p in pw]
            yield
        if k < steps - 1:
            nxt = [_dot(p, jnp.concatenate([p, i.astype(BF16)], axis=1)) for p, i in zip(pw, inv)]
            inv = [i + n[:, sc:] for i, n in zip(inv, nxt)]
            pw = [n[:, :sc].astype(BF16) for n in nxt]
        else:
            inv = [i + _dot(p, i.astype(BF16)) for p, i in zip(pw, inv)]
        yield
    inv = [i.astype(BF16) for i in inv]

    gu = [_dot(i, jnp.concatenate([o[0][:sc], wv], axis=1)) for i, o, wv in zip(inv, ops, w)]
    yield
    kb_t = [o[3].astype(F32).T.astype(BF16) for o in ops]
    mgu = [_dot(m[:, :sc], x.astype(BF16)) for m, x in zip(m_r, gu)]
    mv = [_dot(m[:, sc:], o[2]) for m, o in zip(m_r, ops)]
    yield
    return [{
        "qg": jnp.concatenate([(o[0][sc:].astype(F32) - mg[:, :MIX_W]).astype(BF16),
                               g[:, :MIX_W].astype(BF16)], axis=0),
        "y_hat": a - mg[:, MIX_W:],
        "u_hat": g[:, MIX_W:],
        "kb_t": kt,
        "v": o[2],
        "wc": o[4],
    } for o, g, mg, a, kt in zip(ops, gu, mgu, mv, kb_t)]


def _wkv_scan_kernel(chunk, group, rf, kf, vf, kkf, bf, lf, rb, kb, vb, kkb, bb, lb, yf_o, yb_o,
                     xf_s, xb_s, qg_s, yh_s, uh_s, kbt_s, v_s, wc_s):
    @pl.when(pl.program_id(1) == 0)
    def _():
        xf_s[...] = jnp.zeros_like(xf_s)
        xb_s[...] = jnp.zeros_like(xb_s)

    tm = rf.shape[1]
    n = tm // chunk
    sc = D_HEADS * chunk
    consts = _wkv_chunk_consts(chunk)
    sol_refs = {"qg": qg_s, "y_hat": yh_s, "u_hat": uh_s, "kb_t": kbt_s, "v": v_s, "wc": wc_s}

    def group_rows(g):
        rows = []
        for u in range(group):
            rows.append(pl.ds(pl.multiple_of((g * group + u) * chunk, chunk), chunk))
            rows.append(pl.ds(pl.multiple_of((n - 1 - g * group - u) * chunk, chunk), chunk))
        return rows

    def problems(g):
        probs = []
        for slot, rows in enumerate(group_rows(g)):
            if slot % 2:
                probs.append((True, rb[0, rows, :], kb[0, rows, :], vb[0, rows, :], kkb[0, rows, :],
                              bb[0, rows, :], lb[0, rows, :]))
            else:
                probs.append((False, rf[0, rows, :], kf[0, rows, :], vf[0, rows, :], kkf[0, rows, :],
                              bf[0, rows, :], lf[0, rows, :]))
        return probs

    def advance(slot, rows):
        x_s, y_o = (xb_s, yb_o) if slot % 2 else (xf_s, yf_o)
        x = x_s[...]
        res = _dot(qg_s[slot], x.astype(BF16))
        ys = res[:sc] + yh_s[slot]
        y = ys[0:chunk]
        for h in range(1, D_HEADS):
            y = y + ys[h * chunk:(h + 1) * chunk]
        y_o[0, rows, :] = y
        u = -(res[sc:] + uh_s[slot])
        x_s[...] = x * wc_s[slot] + _dot(kbt_s[slot], jnp.concatenate([v_s[slot], u.astype(BF16)], axis=0))

    def solve(g, pending):
        gen = _wkv_phase1(chunk, consts, problems(g))
        pending = list(pending)
        while True:
            try:
                next(gen)
            except StopIteration as done:
                sols = done.value
                break
            for step in pending[:2]:
                step()
            pending = pending[2:]
        for step in pending:
            step()
        for slot, sol in enumerate(sols):
            for name, ref in sol_refs.items():
                ref[slot] = sol[name]

    def steps_for(g):
        return [functools.partial(advance, slot, r) for slot, r in enumerate(group_rows(g))]

    n_groups = n // group
    solve(0, [])

    for g in range(1, n_groups):
        solve(g, steps_for(g - 1))
    for step in steps_for(n_groups - 1):
        step()


def _wkv_scan(r, kh, v, kk, bb, ldf, ldb, tm, chunk, group):
    b, t, _ = r.shape
    nt = t // tm
    sc = D_HEADS * chunk
    slots = 2 * group
    fspec = pl.BlockSpec((1, tm, MIX_W), lambda bi, i: (bi, i, 0))
    bspec = pl.BlockSpec((1, tm, MIX_W), lambda bi, i: (bi, nt - 1 - i, 0))
    shape = jax.ShapeDtypeStruct((b, t, MIX_W), F32)
    return pl.pallas_call(
        functools.partial(_wkv_scan_kernel, chunk, group),
        grid=(b, nt),
        in_specs=[fspec] * 6 + [bspec] * 6,
        out_specs=[fspec, bspec],
        out_shape=[shape, shape],
        scratch_shapes=[
            pltpu.VMEM((MIX_W, MIX_W), F32), pltpu.VMEM((MIX_W, MIX_W), F32),
            pltpu.VMEM((slots, 2 * sc, MIX_W), BF16),
            pltpu.VMEM((slots, sc, MIX_W), F32),
            pltpu.VMEM((slots, sc, MIX_W), F32),
            pltpu.VMEM((slots, MIX_W, 2 * sc), BF16),
            pltpu.VMEM((slots, sc, MIX_W), BF16),
            pltpu.VMEM((slots, MIX_W, MIX_W), F32),
        ],
        compiler_params=_params("parallel", "arbitrary"),
        name="wkv_scan",
    )(r, kh, v, kk, bb, ldf, r, kh, v, kk, bb, ldb)


def _wkv_out(yf, yb, bonus, g, gn_w, gn_b):
    y = yf + yb
    mean = _head_sum(y, HEAD_DIM) * (1.0 / HEAD_DIM)
    yc = y - mean
    var = _head_sum(yc * yc, HEAD_DIM) * (1.0 / HEAD_DIM)
    yn = yc * lax.rsqrt(var + WKV_GN_EPS) * gn_w + gn_b
    return ((yn + bonus) * g).astype(BF16)


def _post_kernel(x_ref, oa_ref, ob_ref, oc_ref, yf_ref, yb_ref, bonus_ref, g_ref, gnw_ref, gnb_ref,
                 wo_ref, gpost_ref, gpre_ref, wg_ref, wu_ref, wd_ref, gfpost_ref, y_ref):
    tm = x_ref.shape[0]
    groups = [slice(s, s + tm // ROW_GROUPS) for s in range(0, tm, tm // ROW_GROUPS)]
    o_d = [_wkv_out(yf_ref[r], yb_ref[r], bonus_ref[r], g_ref[r], gnw_ref[...], gnb_ref[...]) for r in groups]
    mix = [_dot(oa_ref[r], wo_ref[0]) + _dot(ob_ref[r], wo_ref[1]) + _dot(oc_ref[r], wo_ref[2])
           + _dot(od, wo_ref[3]) for r, od in zip(groups, o_d)]
    x = [x_ref[r] + _rms_rows(m, gpost_ref[...]) for r, m in zip(groups, mix)]
    h = [_rms_rows(xg, gpre_ref[...]).astype(BF16) for xg in x]
    gate = [_dot(hg, wg_ref[...]) for hg in h]
    up = [_dot(hg, wu_ref[...]) for hg in h]
    act = [(g * _sigmoid(g) * u).astype(BF16) for g, u in zip(gate, up)]
    f = [_dot(a, wd_ref[...]) for a in act]
    for r, xg, fg in zip(groups, x, f):
        y_ref[r] = xg + _rms_rows(fg, gfpost_ref[...])


def _post(x2, oa, ob, oc, wkv_parts, gn_w, gn_b, w_out4, g_post, g_pre, wg, wu, wd, gf_post, tm):
    n, d = x2.shape
    dff = wg.shape[1]
    tok = lambda w: pl.BlockSpec((tm, w), lambda i: (i, 0))
    row = pl.BlockSpec((1, d), lambda i: (0, 0))
    mrow = pl.BlockSpec((1, MIX_W), lambda i: (0, 0))
    return pl.pallas_call(
        _post_kernel,
        grid=(n // tm,),
        in_specs=[tok(d)] + [tok(MIX_W)] * 7 + [mrow, mrow,
                  pl.BlockSpec((4, MIX_W, d), lambda i: (0, 0, 0)), row, row,
                  pl.BlockSpec((d, dff), lambda i: (0, 0)),
                  pl.BlockSpec((d, dff), lambda i: (0, 0)),
                  pl.BlockSpec((dff, d), lambda i: (0, 0)), row],
        out_specs=tok(d),
        out_shape=jax.ShapeDtypeStruct((n, d), F32),
        compiler_params=_params("parallel"),
        name="out_proj_ffn",
    )(x2, oa, ob, oc, *wkv_parts, gn_w, gn_b, w_out4, g_post, g_pre, wg, wu, wd, gf_post)


def _angles(pos, rot_dim, theta):
    inv = theta ** (-jnp.arange(0, rot_dim, 2, dtype=F32) / rot_dim)
    return pos.astype(F32)[:, None] * inv[None, :]


def _rope_tables(t):
    rows = t // GRID_W
    row_idx = jnp.repeat(jnp.arange(rows), GRID_W)
    col_idx = jnp.tile(jnp.arange(GRID_W), rows)
    pos = jnp.arange(t)
    ar = _angles(row_idx, HEAD_DIM // 2, A_THETA)
    ac = _angles(col_idx, HEAD_DIM // 2, A_THETA)
    cos_a = jnp.concatenate([jnp.cos(ar), jnp.cos(ar), jnp.cos(ac), jnp.cos(ac)], axis=1)
    sin_a = jnp.concatenate([-jnp.sin(ar), jnp.sin(ar), -jnp.sin(ac), jnp.sin(ac)], axis=1)
    tabs_a = (jnp.tile(cos_a, (1, 4)), jnp.tile(sin_a, (1, 4)))
    ab = _angles(pos, B_ROT, B_THETA)
    pad1 = jnp.ones((t, B_SUB - B_ROT), F32)
    pad0 = jnp.zeros((t, B_SUB - B_ROT), F32)
    cos_b = jnp.concatenate([jnp.cos(ab), jnp.cos(ab), pad1], axis=1)
    sin_b = jnp.concatenate([-jnp.sin(ab), jnp.sin(ab), pad0], axis=1)
    tabs_b = (jnp.tile(cos_b, (1, 8)), jnp.tile(sin_b, (1, 8)))
    ang_c = _angles(pos, HEAD_DIM, C_THETA)
    cos_c = jnp.concatenate([jnp.cos(ang_c), jnp.cos(ang_c)], axis=1)
    sin_c = jnp.concatenate([-jnp.sin(ang_c), jnp.sin(ang_c)], axis=1)
    tabs_c = (jnp.tile(cos_c, (1, 4)), jnp.tile(sin_c, (1, 4)))
    return tabs_a, tabs_b, tabs_c


def _tile(n, pref):
    while n % pref:
        pref //= 2
    return pref


def _layer_weights(l, w):
    d_model = w["w_in"].shape[1]
    row = lambda a: a.reshape(1, -1).astype(F32)
    zeros = lambda r: jnp.zeros((r, MIX_W), F32)
    w_lora = w["d_w_up"].shape[2]
    return {
        "g_mix_pre": row(w["norm_mix_pre"][l]), "g_mix_post": row(w["norm_mix_post"][l]),
        "g_ffn_pre": row(w["norm_ffn_pre"][l]), "g_ffn_post": row(w["norm_ffn_post"][l]),
        "w_abc": w["w_in"][l][:, :ABC_COLS].astype(BF16), "w_d": w["w_in"][l][:, ABC_COLS:].astype(BF16),
        "w_out4": w["w_out"][l].reshape(4, MIX_W, d_model).astype(BF16),
        "a_gain": jnp.concatenate([jnp.tile(row(w["a_q_gain"][l]), (1, A_HEADS)),
                                   jnp.tile(row(w["a_k_gain"][l]), (1, A_KV_HEADS))], axis=1),
        "b_lambda": w["b_lambda"][l].astype(F32), "b_gain": w["b_subln_gain"][l].reshape(-1, 1).astype(F32),
        "c_gain": row(w["c_gn_gain"][l]),
        "wkv": {
            "mu_prev": row(w["d_mu_prev"][l]), "mu_next": row(w["d_mu_next"][l]),
            "w0": w["d_w0"][l].astype(F32),
            "wup_f": jnp.concatenate([w["d_w_up"][l, 0], zeros(w_lora)], axis=0).astype(BF16),
            "wup_b": jnp.concatenate([zeros(w_lora), w["d_w_up"][l, 1]], axis=0).astype(BF16),
            "a0": row(w["d_a0"][l]),
            "aup": jnp.concatenate([w["d_a_up"][l], zeros(64)], axis=0).astype(BF16),
            "gup1": jnp.concatenate([zeros(64), w["d_g_up"][l][:64]], axis=0).astype(BF16),
            "gup2": w["d_g_up"][l][64:].astype(BF16),
            "k_k": row(w["d_k_k"][l]), "k_a": row(w["d_k_a"][l]), "r_k": row(w["d_r_k"][l]),
        },
        "gn_w": row(w["d_gn_w"][l]), "gn_b": row(w["d_gn_b"][l]),
        "wg": w["ffn_w_gate"][l].astype(BF16), "wu": w["ffn_w_up"][l].astype(BF16),
        "wd": w["ffn_w_down"][l].astype(BF16),
    }


def _trunk(x, layers):
    b, t, d = x.shape
    n = b * t
    tabs = _rope_tables(t)
    tm_proj = _tile(t, 512)
    tm_post = _tile(n, 512)
    tq = _tile(t, 256)
    tm_wkv = _tile(t, 1024)
    tm_scan = _tile(t, 1024)
    chunk = 32
    x2 = x.reshape(n, d)
    for l, p in enumerate(layers):
        lam_init = 0.8 - 0.6 * math.exp(-0.3 * l)
        zabc, zd = _in_proj(x2, p["g_mix_pre"], p["w_abc"], p["w_d"], tabs, p["a_gain"], t, tm_proj)
        zabc = zabc.reshape(b, t, ABC_COLS)
        zd = zd.reshape(b, t, D_COLS)
        o_a = _mixer_gqa(zabc, tq)
        o_b = _mixer_diff(zabc, p["b_lambda"], p["b_gain"], lam_init, tq)
        o_c = _mixer_ret(zabc, p["c_gain"], _tile(t, 128))
        r, kh, v, kk, bb, ldf, ldb, g, bonus = _wkv_prep(zd, p["wkv"], tm_wkv)
        yf, yb = _wkv_scan(r, kh, v, kk, bb, ldf, ldb, tm_scan, chunk, min(4, tm_scan // chunk))
        wkv_parts = [a.reshape(n, MIX_W) for a in (yf, yb, bonus, g)]
        x2 = _post(x2, o_a.reshape(n, MIX_W), o_b.reshape(n, MIX_W), o_c.reshape(n, MIX_W), wkv_parts,
                   p["gn_w"], p["gn_b"], p["w_out4"], p["g_mix_post"], p["g_ffn_pre"], p["wg"], p["wu"], p["wd"],
                   p["g_ffn_post"], tm_post)
    return x2.reshape(b, t, d)


def kernel(x_prompt, x_sample, norm_mix_pre, norm_mix_post, norm_ffn_pre, norm_ffn_post, w_in, w_out,
           a_q_gain, a_k_gain, b_lambda, b_subln_gain, c_gn_gain, d_mu_prev, d_mu_next, d_w0, d_w_up,
           d_a0, d_a_up, d_g_up, d_k_k, d_k_a, d_r_k, d_gn_w, d_gn_b, ffn_w_gate, ffn_w_up, ffn_w_down):
    w = {
        "norm_mix_pre": norm_mix_pre, "norm_mix_post": norm_mix_post,
        "norm_ffn_pre": norm_ffn_pre, "norm_ffn_post": norm_ffn_post,
        "w_in": w_in, "w_out": w_out, "a_q_gain": a_q_gain, "a_k_gain": a_k_gain,
        "b_lambda": b_lambda, "b_subln_gain": b_subln_gain, "c_gn_gain": c_gn_gain,
        "d_mu_prev": d_mu_prev, "d_mu_next": d_mu_next, "d_w0": d_w0, "d_w_up": d_w_up,
        "d_a0": d_a0, "d_a_up": d_a_up, "d_g_up": d_g_up, "d_k_k": d_k_k, "d_k_a": d_k_a,
        "d_r_k": d_r_k, "d_gn_w": d_gn_w, "d_gn_b": d_gn_b,
        "ffn_w_gate": ffn_w_gate, "ffn_w_up": ffn_w_up, "ffn_w_down": ffn_w_down,
    }
    layers = [_layer_weights(l, w) for l in range(w_in.shape[0])]
    return (_trunk(x_prompt, layers), _trunk(x_sample, layers))
```

```python
import functools
import math

import jax
import jax.numpy as jnp
from jax import lax
from jax.experimental import pallas as pl
from jax.experimental.pallas import tpu as pltpu

F32 = jnp.float32
BF16 = jnp.bfloat16

HEAD_DIM = 64
GRID_W = 64
NORM_EPS = 1e-6
A_HEADS, A_KV_HEADS, A_THETA = 4, 2, 10000.0
B_HEADS, B_SUB, B_ROT, B_THETA = 4, 32, 8, 500000.0
C_HEADS, C_THETA = 4, 10000.0
D_HEADS = 4
WKV_GN_EPS = 64e-5
MIX_W = 256
ABC_COLS = 2304
A_Q, A_K, A_V = slice(0, 256), slice(256, 384), slice(384, 512)
B_Q, B_K, B_V = slice(512, 768), slice(768, 1024), slice(1024, 1280)
C_Q, C_K, C_VG = slice(1280, 1536), slice(1536, 1792), slice(1792, 2304)
D_COLS = 1088
BF16_SUBLANES = 16
VEXT_ROWS = HEAD_DIM + BF16_SUBLANES
LOG2E = math.log2(math.e)
KEY_CHUNK = 256
ROW_GROUPS = 2
SHIFT_LIMIT = 100.0
VMEM_LIMIT = 56 * 1024 * 1024


def _params(*sem):
    return pltpu.CompilerParams(dimension_semantics=sem, vmem_limit_bytes=VMEM_LIMIT)


def _rms_rows(x, gain):
    return x * lax.rsqrt(jnp.mean(x * x, axis=-1, keepdims=True) + NORM_EPS) * gain


def _split2(x):
    hi = x.astype(BF16)
    lo = (x - hi.astype(F32)).astype(BF16)
    return hi, lo


def _split3(x):
    h1 = x.astype(BF16)
    r1 = x - h1.astype(F32)
    h2 = r1.astype(BF16)
    h3 = (r1 - h2.astype(F32)).astype(BF16)
    return h1, h2, h3


def _head_sum(x, seg):
    w = x.shape[-1]
    r = lax.broadcasted_iota(jnp.int32, (w, w), 0) // seg
    c = lax.broadcasted_iota(jnp.int32, (w, w), 1) // seg
    bd = jnp.where(r == c, 1.0, 0.0).astype(BF16)
    hi, lo = _split2(x)
    return (jnp.dot(hi, bd, preferred_element_type=F32)
            + jnp.dot(lo, bd, preferred_element_type=F32))


def _rope(x, cos, sin, half):
    w = x.shape[-1]
    lane = lax.broadcasted_iota(jnp.int32, x.shape, 1)
    nxt = pltpu.roll(x, w - half, 1)
    prv = pltpu.roll(x, half, 1)
    return x * cos + jnp.where((lane % (2 * half)) < half, nxt, prv) * sin


def _sigmoid(x):
    return 0.5 * jnp.tanh(0.5 * x) + 0.5


def _dot(a, b):
    return jnp.dot(a, b, preferred_element_type=F32)


def _dot_nt(a, b):
    return lax.dot_general(a, b, (((1,), (1,)), ((), ())), preferred_element_type=F32)


def _store_qk_prepared(z, r, tabs, gain_a, out_ref):
    (cos_a, sin_a), (cos_b, sin_b), (cos_c, sin_c) = tabs

    def head_norm(x, gain):
        return x * lax.rsqrt(_head_sum(x * x, HEAD_DIM) * (1.0 / HEAD_DIM) + NORM_EPS) * gain

    a_qk = head_norm(z[:, A_Q.start:A_K.stop], gain_a)
    kw = A_K.stop - A_K.start
    pieces = {
        A_Q: _rope(a_qk[:, :A_Q.stop], cos_a, sin_a, HEAD_DIM // 4) * (HEAD_DIM ** -0.5 * LOG2E),
        A_K: _rope(a_qk[:, A_Q.stop:], cos_a[:, :kw], sin_a[:, :kw], HEAD_DIM // 4),
        B_Q: _rope(z[:, B_Q], cos_b, sin_b, B_ROT // 2) * (B_SUB ** -0.5 * LOG2E),
        B_K: _rope(z[:, B_K], cos_b, sin_b, B_ROT // 2),
        C_Q: _rope(z[:, C_Q], cos_c, sin_c, HEAD_DIM // 2),
        C_K: _rope(z[:, C_K], cos_c, sin_c, HEAD_DIM // 2) * (HEAD_DIM ** -0.5),
    }
    for cols in (A_V, B_V, C_VG):
        pieces[cols] = z[:, cols]
    for cols, val in pieces.items():
        out_ref[r, cols] = val.astype(BF16)


def _in_proj_kernel(x_ref, g_ref, wabc_ref, wd_ref, ca_ref, sa_ref, cb_ref, sb_ref, cc_ref, sc_ref, ga_ref,
                    zabc_ref, zd_ref):
    tm = x_ref.shape[0]
    groups = [slice(s, s + tm // ROW_GROUPS) for s in range(0, tm, tm // ROW_GROUPS)]
    h = [_rms_rows(x_ref[r], g_ref[...]).astype(BF16) for r in groups]
    z = [_dot(hg, wabc_ref[...]) for hg in h]
    for r, hg, zg in zip(groups, h, z):
        tabs = ((ca_ref[r], sa_ref[r]), (cb_ref[r], sb_ref[r]), (cc_ref[r], sc_ref[r]))
        _store_qk_prepared(zg, r, tabs, ga_ref[...], zabc_ref)
        zd_ref[r] = _dot(hg, wd_ref[...])


def _in_proj(x2, gain, w_abc, w_d, tabs, gain_a, t, tm):
    n, d = x2.shape
    per_seq = t // tm
    tab = pl.BlockSpec((tm, MIX_W), lambda i: (i % per_seq, 0))
    return pl.pallas_call(
        _in_proj_kernel,
        grid=(n // tm,),
        in_specs=[
            pl.BlockSpec((tm, d), lambda i: (i, 0)),
            pl.BlockSpec((1, d), lambda i: (0, 0)),
            pl.BlockSpec((d, ABC_COLS), lambda i: (0, 0)),
            pl.BlockSpec((d, D_COLS), lambda i: (0, 0)),
            tab, tab, tab, tab, tab, tab,
            pl.BlockSpec((1, A_K.stop), lambda i: (0, 0)),
        ],
        out_specs=[
            pl.BlockSpec((tm, ABC_COLS), lambda i: (i, 0)),
            pl.BlockSpec((tm, D_COLS), lambda i: (i, 0)),
        ],
        out_shape=[
            jax.ShapeDtypeStruct((n, ABC_COLS), BF16),
            jax.ShapeDtypeStruct((n, D_COLS), F32),
        ],
        compiler_params=_params("parallel"),
        name="in_proj",
    )(x2, gain, w_abc, w_d, *tabs[0], *tabs[1], *tabs[2], gain_a)


def _value_ext_t(v_t):
    row = lax.broadcasted_iota(jnp.int32, (VEXT_ROWS - HEAD_DIM, v_t.shape[1]), 0)
    return jnp.concatenate([v_t, jnp.where(row == 0, 1.0, 0.0)], axis=0).astype(BF16)


def _attend_exact(k_ref, q_ts, vx_refs):
    outs = []
    for q_t, vx in zip(q_ts, vx_refs):
        s_t = _dot(k_ref[...], q_t)
        p_t = jnp.exp2(s_t - jnp.max(s_t, axis=0, keepdims=True)).astype(BF16)
        o = _dot(vx[...], p_t)
        outs.append(o[:HEAD_DIM] / o[HEAD_DIM:HEAD_DIM + 1])
    return outs


def _attend_shifted(k_ref, q_ts, shifts, vx_refs):
    t = k_ref.shape[0]
    ck = min(t, KEY_CHUNK)
    steps = [(j, slice(c * ck, (c + 1) * ck)) for j in range(len(q_ts)) for c in range(t // ck)]
    acc = [None] * len(q_ts)
    s_prev = p_prev = None
    for i in range(len(steps) + 2):
        s_new = p_new = None
        if i < len(steps):
            j, keys = steps[i]
            s_new = _dot(k_ref[keys, :], q_ts[j])
        if 1 <= i <= len(steps):
            p_new = jnp.exp2(s_prev - shifts[steps[i - 1][0]]).astype(BF16)
        if i >= 2:
            j, keys = steps[i - 2]
            o = _dot(vx_refs[j][:, keys], p_prev)
            acc[j] = o if acc[j] is None else acc[j] + o
        s_prev, p_prev = s_new, p_new
    return [a[:HEAD_DIM] / a[HEAD_DIM:HEAD_DIM + 1] for a in acc]


def _attend_t(k_ref, kn_ref, q_t, head_rows, head_keys, vx_refs, finish):
    c = k_ref.shape[1]
    tq = q_t.shape[1]
    qsq = q_t * q_t
    q_ts, shifts = [], []
    for rows, keys in zip(head_rows, head_keys):
        pieces = [jnp.zeros((keys.start, tq), F32), q_t[rows], jnp.zeros((c - keys.stop, tq), F32)]
        q_ts.append(jnp.concatenate([x for x in pieces if x.shape[0]], axis=0).astype(BF16))
        qn = jnp.sqrt(jnp.sum(qsq[rows], axis=0, keepdims=True))
        shifts.append(qn * kn_ref[0:1, keys.start:keys.start + 1])
    worst = jnp.max(jnp.concatenate(shifts, axis=0))

    @pl.when(2.0 * worst < SHIFT_LIMIT)
    def _():
        finish(_attend_shifted(k_ref, q_ts, shifts, vx_refs))

    @pl.when(jnp.logical_not(2.0 * worst < SHIFT_LIMIT))
    def _():
        finish(_attend_exact(k_ref, q_ts, vx_refs))


def _gqa_kernel(q_ref, k_ref, v_ref, o_ref, kn_s, vx_s):
    @pl.when(pl.program_id(1) == 0)
    def _():
        k = k_ref[0].astype(F32)
        kn_s[...] = jnp.sqrt(jnp.max(_head_sum(k * k, HEAD_DIM), axis=0, keepdims=True))
        v_t = v_ref[0].astype(F32).T
        for g in range(A_KV_HEADS):
            vx_s[g] = _value_ext_t(v_t[g * HEAD_DIM:(g + 1) * HEAD_DIM])

    q_t = q_ref[0].astype(F32).T
    k_s = k_ref.at[0]
    group = A_HEADS // A_KV_HEADS
    head_rows = [slice(h * HEAD_DIM, (h + 1) * HEAD_DIM) for h in range(A_HEADS)]
    head_keys = [slice((h // group) * HEAD_DIM, (h // group + 1) * HEAD_DIM) for h in range(A_HEADS)]

    def finish(outs):
        o_ref[0] = jnp.concatenate(outs, axis=0).T.astype(BF16)

    _attend_t(k_s, kn_s, q_t, head_rows, head_keys, [vx_s.at[h // group] for h in range(A_HEADS)], finish)


def _mixer_gqa(zabc, tq):
    b, t, _ = zabc.shape
    return pl.pallas_call(
        _gqa_kernel,
        grid=(b, t // tq),
        in_specs=[
            pl.BlockSpec((1, tq, 256), lambda bi, i: (bi, i, 0)),
            pl.BlockSpec((1, t, 128), lambda bi, i: (bi, 0, 2)),
            pl.BlockSpec((1, t, 128), lambda bi, i: (bi, 0, 3)),
        ],
        out_specs=pl.BlockSpec((1, tq, MIX_W), lambda bi, i: (bi, i, 0)),
        out_shape=jax.ShapeDtypeStruct((b, t, MIX_W), BF16),
        scratch_shapes=[
            pltpu.VMEM((1, A_KV_HEADS * HEAD_DIM), F32),
            pltpu.VMEM((A_KV_HEADS, VEXT_ROWS, t), BF16),
        ],
        compiler_params=_params("parallel", "arbitrary"),
        name="mixer_gqa",
    )(zabc, zabc, zabc)


def _diff_kernel(lam_init, q_ref, k_ref, v_ref, lam_ref, gain_ref, o_ref, kn_s, vx_s):
    @pl.when(pl.program_id(1) == 0)
    def _():
        k = k_ref[0].astype(F32)
        kn_s[...] = jnp.sqrt(jnp.max(_head_sum(k * k, B_SUB), axis=0, keepdims=True))
        v_t = v_ref[0].astype(F32).T
        for h in range(B_HEADS):
            vx_s[h] = _value_ext_t(v_t[h * HEAD_DIM:(h + 1) * HEAD_DIM])

    lp = lam_ref[...]
    lam = (jnp.exp(jnp.sum(lp[0:1] * lp[1:2], axis=1, keepdims=True))
           - jnp.exp(jnp.sum(lp[2:3] * lp[3:4], axis=1, keepdims=True)) + lam_init)
    q_t = q_ref[0].astype(F32).T
    k_s = k_ref.at[0]
    sub_rows = [slice(j * B_SUB, (j + 1) * B_SUB) for j in range(2 * B_HEADS)]

    def finish(parts):
        outs = []
        for h in range(B_HEADS):
            o = parts[2 * h] - lam * parts[2 * h + 1]
            o = o * lax.rsqrt(jnp.mean(o * o, axis=0, keepdims=True) + NORM_EPS) * gain_ref[...]
            outs.append(o * (1.0 - lam_init))
        o_ref[0] = jnp.concatenate(outs, axis=0).T.astype(BF16)

    _attend_t(k_s, kn_s, q_t, sub_rows, sub_rows, [vx_s.at[j // 2] for j in range(2 * B_HEADS)], finish)


def _mixer_diff(zabc, lam_params, gain, lam_init, tq):
    b, t, _ = zabc.shape
    return pl.pallas_call(
        functools.partial(_diff_kernel, lam_init),
        grid=(b, t // tq),
        in_specs=[
            pl.BlockSpec((1, tq, 256), lambda bi, i: (bi, i, 2)),
            pl.BlockSpec((1, t, 256), lambda bi, i: (bi, 0, 3)),
            pl.BlockSpec((1, t, 256), lambda bi, i: (bi, 0, 4)),
            pl.BlockSpec((4, B_SUB), lambda bi, i: (0, 0)),
            pl.BlockSpec((HEAD_DIM, 1), lambda bi, i: (0, 0)),
        ],
        out_specs=pl.BlockSpec((1, tq, MIX_W), lambda bi, i: (bi, i, 0)),
        out_shape=jax.ShapeDtypeStruct((b, t, MIX_W), BF16),
        scratch_shapes=[
            pltpu.VMEM((1, 2 * B_HEADS * B_SUB), F32),
            pltpu.VMEM((B_HEADS, VEXT_ROWS, t), BF16),
        ],
        compiler_params=_params("parallel", "arbitrary"),
        name="mixer_diff",
    )(zabc, zabc, zabc, lam_params, gain)


def _ret_log_gammas():
    lg = [math.log1p(-(2.0 ** (-5.0 - h))) for h in range(C_HEADS)]
    return lg, lg[::-1]


def _lane_consts(vals, shape):
    head = lax.broadcasted_iota(jnp.int32, shape, len(shape) - 1) // HEAD_DIM
    out = jnp.full(shape, vals[-1], F32)
    for h in range(len(vals) - 2, -1, -1):
        out = jnp.where(head == h, vals[h], out)
    return out


def _ret_kernel(chunk, q_ref, k_ref, v_ref, g_ref, gain_ref, o_ref, acc_s, sf_s, sb_s, dm_s):
    t = q_ref.shape[1]
    c = chunk
    n = t // c
    lgf, lgb = _ret_log_gammas()
    qr_s = q_ref.at[0]
    kr_s = k_ref.at[0]

    ti = lax.broadcasted_iota(jnp.int32, (c, c), 0)
    si = lax.broadcasted_iota(jnp.int32, (c, c), 1)
    dist = (ti - si).astype(F32)
    lane_head = lax.broadcasted_iota(jnp.int32, (1, MIX_W), 1) // HEAD_DIM
    row = lax.broadcasted_iota(jnp.int32, (c, MIX_W), 0).astype(F32)
    lgf_l = _lane_consts(lgf, (c, MIX_W))
    lgb_l = _lane_consts(lgb, (c, MIX_W))
    qdec_f = jnp.exp(lgf_l * (row + 1.0))
    kdec_f = jnp.exp(lgf_l * (c - 1.0 - row))
    qdec_b = jnp.exp(lgb_l * (c - row))
    kdec_b = jnp.exp(lgb_l * row)
    r2 = lax.broadcasted_iota(jnp.int32, (MIX_W, MIX_W), 0) // HEAD_DIM
    c2 = lax.broadcasted_iota(jnp.int32, (MIX_W, MIX_W), 1) // HEAD_DIM
    same_head = r2 == c2
    gf_blk = jnp.where(same_head, jnp.exp(_lane_consts(lgf, (MIX_W, MIX_W)) * c), 0.0)
    gb_blk = jnp.where(same_head, jnp.exp(_lane_consts(lgb, (MIX_W, MIX_W)) * c), 0.0)

    head_mask = [(lane_head == h).astype(BF16) for h in range(C_HEADS)]
    for h in range(C_HEADS):
        dm_s[h] = jnp.where(dist > 0, jnp.exp(lgf[h] * dist),
                            jnp.where(dist < 0, jnp.exp(-lgb[h] * dist), 2.0))

    def intra(qc, kc, vc):
        ps = [(_dot_nt(qc * head_mask[h], kc) * dm_s[h]).astype(BF16) for h in range(C_HEADS)]
        v_stack = jnp.concatenate([vc * head_mask[h] for h in range(C_HEADS)], axis=0)
        return _dot(jnp.concatenate(ps, axis=1), v_stack)

    sf_s[...] = jnp.zeros_like(sf_s)
    sb_s[...] = jnp.zeros_like(sb_s)

    def fwd(i, carry):
        rows = pl.ds(pl.multiple_of(i * c, c), c)
        qc = qr_s[rows, :]
        kc = kr_s[rows, :]
        vc = v_ref[0, rows, :]
        o = intra(qc, kc, vc) + _dot((qc.astype(F32) * qdec_f).astype(BF16), sf_s[...].astype(BF16))
        acc_s[rows, :] = o
        kd = (kc.astype(F32) * kdec_f).T.astype(BF16)
        sf_s[...] = gf_blk * sf_s[...] + jnp.where(same_head, _dot(kd, vc), 0.0)
        return carry

    lax.fori_loop(0, n, fwd, 0)

    def bwd(i, carry):
        rows = pl.ds(pl.multiple_of((n - 1 - i) * c, c), c)
        qc = qr_s[rows, :].astype(F32)
        kc = kr_s[rows, :].astype(F32)
        vc = v_ref[0, rows, :]
        acc_s[rows, :] = acc_s[rows, :] + _dot((qc * qdec_b).astype(BF16), sb_s[...].astype(BF16))
        kd = (kc * kdec_b).T.astype(BF16)
        sb_s[...] = gb_blk * sb_s[...] + jnp.where(same_head, _dot(kd, vc), 0.0)
        return carry

    lax.fori_loop(0, n, bwd, 0)

    o = acc_s[...]
    o = o * lax.rsqrt(_head_sum(o * o, HEAD_DIM) * (1.0 / HEAD_DIM) + NORM_EPS) * gain_ref[...]
    g = g_ref[0].astype(F32)
    o_ref[0] = (o * (g * _sigmoid(g))).astype(BF16)


def _mixer_ret(zabc, gain, chunk):
    b, t, _ = zabc.shape
    blk = lambda j: pl.BlockSpec((1, t, 256), lambda bi: (bi, 0, j))
    return pl.pallas_call(
        functools.partial(_ret_kernel, chunk),
        grid=(b,),
        in_specs=[blk(5), blk(6), blk(7), blk(8),
                  pl.BlockSpec((1, 256), lambda bi: (0, 0))],
        out_specs=pl.BlockSpec((1, t, MIX_W), lambda bi: (bi, 0, 0)),
        out_shape=jax.ShapeDtypeStruct((b, t, MIX_W), BF16),
        scratch_shapes=[
            pltpu.VMEM((t, MIX_W), F32),
            pltpu.VMEM((MIX_W, MIX_W), F32),
            pltpu.VMEM((MIX_W, MIX_W), F32),
            pltpu.VMEM((C_HEADS, chunk, chunk), F32),
        ],
        compiler_params=_params("parallel"),
        name="mixer_ret",
    )(zabc, zabc, zabc, zabc, gain)


def _wkv_prep_kernel(z_ref, zp_ref, zn_ref, mup_ref, mun_ref, w0_ref, wupf_ref, wupb_ref, a0_ref, aup_ref,
                     gup1_ref, gup2_ref, kk_ref, ka_ref, rk_ref,
                     r_o, k_o, v_o, kk_o, b_o, ldf_o, ldb_o, g_o, bonus_o):
    i = pl.program_id(1)
    last = pl.num_programs(1) - 1
    z = z_ref[0]
    tm = z.shape[0]
    row = lax.broadcasted_iota(jnp.int32, z.shape, 0)
    prev_row = zp_ref[0, 0, 7:8, :] * jnp.where(i > 0, 1.0, 0.0)
    next_row = zn_ref[0, 0, 0:1, :] * jnp.where(i < last, 1.0, 0.0)
    z_prev = jnp.where(row == 0, prev_row, pltpu.roll(z, 1, 0))
    z_next = jnp.where(row == tm - 1, next_row, pltpu.roll(z, tm - 1, 0))
    mu_p, mu_n = mup_ref[...], mun_ref[...]
    u = z * (1.0 - mu_p - mu_n) + z_prev * mu_p + z_next * mu_n
    r = u[:, 0:256]
    k = u[:, 256:512]
    v = u[:, 512:768]
    wd = jnp.tanh(u[:, 768:896]).astype(BF16)
    ag = u[:, 896:1024]
    g2 = u[:, 1024:1088]

    def log_decay(w0, wup):
        return _sigmoid(w0 + _dot(wd, wup)) * (-math.exp(-0.5))

    ldf_o[0] = log_decay(w0_ref[0:1, :], wupf_ref[...])
    ldb_o[0] = log_decay(w0_ref[1:2, :], wupb_ref[...])
    a = _sigmoid(a0_ref[...] + _dot(ag.astype(BF16), aup_ref[...]))
    g_o[0] = (_dot(_sigmoid(ag).astype(BF16), gup1_ref[...])
              + _dot(_sigmoid(g2).astype(BF16), gup2_ref[...])).astype(BF16)
    kk = k * kk_ref[...]
    kk = kk / jnp.maximum(jnp.sqrt(_head_sum(kk * kk, HEAD_DIM)), 1e-12)
    kh = k * (1.0 + (a - 1.0) * ka_ref[...])
    r_o[0] = r.astype(BF16)
    k_o[0] = kh.astype(BF16)
    v_o[0] = v.astype(BF16)
    kk_o[0] = kk.astype(BF16)
    b_o[0] = (kk * a).astype(BF16)
    bonus_o[0] = (_head_sum(r * kh * rk_ref[...], HEAD_DIM) * v).astype(BF16)


def _wkv_prep(zd, p, tm):
    b, t, _ = zd.shape
    zd8 = zd.reshape(b, t // 8, 8, D_COLS)
    r8 = tm // 8
    nb8 = t // 8
    row = lambda w: pl.BlockSpec((1, w), lambda bi, i: (0, 0))
    full = lambda a: pl.BlockSpec(a.shape, lambda bi, i: (0,) * a.ndim)
    out_spec = pl.BlockSpec((1, tm, MIX_W), lambda bi, i: (bi, i, 0))
    out_shape = jax.ShapeDtypeStruct((b, t, MIX_W), F32)
    out_bf16 = jax.ShapeDtypeStruct((b, t, MIX_W), BF16)
    return pl.pallas_call(
        _wkv_prep_kernel,
        grid=(b, t // tm),
        in_specs=[
            pl.BlockSpec((1, tm, D_COLS), lambda bi, i: (bi, i, 0)),
            pl.BlockSpec((1, 1, 8, D_COLS), lambda bi, i: (bi, jnp.maximum(i * r8 - 1, 0), 0, 0)),
            pl.BlockSpec((1, 1, 8, D_COLS), lambda bi, i: (bi, jnp.minimum((i + 1) * r8, nb8 - 1), 0, 0)),
            row(D_COLS), row(D_COLS),
            full(p["w0"]), full(p["wup_f"]), full(p["wup_b"]), row(256), full(p["aup"]),
            full(p["gup1"]), full(p["gup2"]), row(256), row(256), row(256),
        ],
        out_specs=[out_spec] * 9,
        out_shape=[out_bf16] * 5 + [out_shape] * 2 + [out_bf16] * 2,
        compiler_params=_params("parallel", "parallel"),
        name="wkv_prep",
    )(zd, zd8, zd8, p["mu_prev"], p["mu_next"], p["w0"], p["wup_f"], p["wup_b"], p["a0"], p["aup"],
      p["gup1"], p["gup2"], p["k_k"], p["k_a"], p["r_k"])


def _wkv_chunk_consts(chunk):
    c = chunk
    sc = D_HEADS * c
    ti = lax.broadcasted_iota(jnp.int32, (c, c), 0)
    si = lax.broadcasted_iota(jnp.int32, (c, c), 1)
    srow = lax.broadcasted_iota(jnp.int32, (sc, MIX_W), 0) // c
    slane = lax.broadcasted_iota(jnp.int32, (sc, MIX_W), 1) // HEAD_DIM
    gt = lax.broadcasted_iota(jnp.int32, (2 * sc, 2 * sc), 0)
    gs = lax.broadcasted_iota(jnp.int32, (2 * sc, 2 * sc), 1)
    t_in, s_in = gt % sc, gs % sc
    same = (t_in // c) == (s_in // c)
    upper = gt < sc
    st = lax.broadcasted_iota(jnp.int32, (sc, sc), 0)
    ss = lax.broadcasted_iota(jnp.int32, (sc, sc), 1)
    r2 = lax.broadcasted_iota(jnp.int32, (MIX_W, MIX_W), 0)
    c2 = lax.broadcasted_iota(jnp.int32, (MIX_W, MIX_W), 1)
    return {
        "tri": {False: jnp.where(si <= ti, 1.0, 0.0).astype(BF16), True: jnp.where(si >= ti, 1.0, 0.0).astype(BF16)},
        "hmask": srow == slane,
        "gmask": {False: same & ((s_in < t_in) | ((s_in == t_in) & ~upper)),
                  True: same & ((s_in > t_in) | ((s_in == t_in) & ~upper))},
        "eye": jnp.where(st == ss, 1.0, 0.0),
        "diag": r2 == c2,
    }


def _wkv_phase1(chunk, consts, probs):
    c = chunk
    sc = D_HEADS * c
    hmask = consts["hmask"]

    def stack(a):
        return jnp.where(hmask, jnp.concatenate([a] * D_HEADS, axis=0), 0.0).astype(BF16)

    pre = []
    for rev, r, kh, v, kk, bb, lw in probs:
        tri = consts["tri"][rev]
        l1, l2, l3 = _split3(lw)
        cl = _dot(tri, l1) + _dot(tri, l2) + _dot(tri, l3)
        tot = jnp.sum(lw, axis=0, keepdims=True)
        pre.append((cl, tot))
    yield
    ops = []
    for (rev, r, kh, v, kk, bb, lw), (cl, tot) in zip(probs, pre):
        w_inv = jnp.exp(-cl)
        w_end = jnp.exp(tot - cl)
        lhs = jnp.concatenate([stack(kk * jnp.exp(cl - lw)), stack(r * jnp.exp(cl))], axis=0)
        rhs = jnp.concatenate([stack(bb * w_inv), stack(kh * w_inv)], axis=0)
        kb = jnp.concatenate([stack(kh * w_end), stack(bb * w_end)], axis=0)
        wc = jnp.sum(jnp.where(consts["diag"], jnp.exp(tot), 0.0), axis=1, keepdims=True)
        wc = jnp.broadcast_to(wc, (MIX_W, MIX_W))
        ops.append((lhs, rhs, stack(v), kb, wc))
    yield
    gram = [jnp.where(consts["gmask"][p[0]], _dot_nt(o[0], o[1]), 0.0) for p, o in zip(probs, ops)]
    l_ab = [g[:sc, :sc] for g in gram]
    l_ak = [g[:sc, sc:].astype(BF16) for g in gram]
    m_r = [g[sc:, :].astype(BF16) for g in gram]
    yield

    steps = int(math.log2(c))
    pw = [(-l).astype(BF16) for l in l_ab]
    inv = [consts["eye"] - l for l in l_ab]
    w = [_dot(a, o[2]).astype(BF16) for a, o in zip(l_ak, ops)]
    for k in range(1, steps):
        if k == 1:
            pw = [_dot(p, p).astype(BF16) for p in pw]
            yield
        if k < steps - 1:
            nxt = [_dot(p, jnp.concatenate([p, i.astype(BF16)], axis=1)) for p, i in zip(pw, inv)]
            inv = [i + n[:, sc:] for i, n in zip(inv, nxt)]
            pw = [n[:, :sc].astype(BF16) for n in nxt]
        else:
            inv = [i + _dot(p, i.astype(BF16)) for p, i in zip(pw, inv)]
        yield
    inv = [i.astype(BF16) for i in inv]

    gu = [_dot(i, jnp.concatenate([o[0][:sc], wv], axis=1)) for i, o, wv in zip(inv, ops, w)]
    yield
    kb_t = [o[3].astype(F32).T.astype(BF16) for o in ops]
    mgu = [_dot(m[:, :sc], x.astype(BF16)) for m, x in zip(m_r, gu)]
    mv = [_dot(m[:, sc:], o[2]) for m, o in zip(m_r, ops)]
    yield
    return [{
        "qg": jnp.concatenate([(o[0][sc:].astype(F32) - mg[:, :MIX_W]).astype(BF16),
                               g[:, :MIX_W].astype(BF16)], axis=0),
        "y_hat": a - mg[:, MIX_W:],
        "u_hat": g[:, MIX_W:],
        "kb_t": kt,
        "v": o[2],
        "wc": o[4],
    } for o, g, mg, a, kt in zip(ops, gu, mgu, mv, kb_t)]


def _wkv_scan_kernel(chunk, group, rf, kf, vf, kkf, bf, lf, rb, kb, vb, kkb, bb, lb, yf_o, yb_o,
                     xf_s, xb_s, qg_s, yh_s, uh_s, kbt_s, v_s, wc_s):
    @pl.when(pl.program_id(1) == 0)
    def _():
        xf_s[...] = jnp.zeros_like(xf_s)
        xb_s[...] = jnp.zeros_like(xb_s)

    tm = rf.shape[1]
    n = tm // chunk
    sc = D_HEADS * chunk
    consts = _wkv_chunk_consts(chunk)
    sol_refs = {"qg": qg_s, "y_hat": yh_s, "u_hat": uh_s, "kb_t": kbt_s, "v": v_s, "wc": wc_s}

    def group_rows(g):
        rows = []
        for u in range(group):
            rows.append(pl.ds(pl.multiple_of((g * group + u) * chunk, chunk), chunk))
            rows.append(pl.ds(pl.multiple_of((n - 1 - g * group - u) * chunk, chunk), chunk))
        return rows

    def problems(g):
        probs = []
        for slot, rows in enumerate(group_rows(g)):
            if slot % 2:
                probs.append((True, rb[0, rows, :], kb[0, rows, :], vb[0, rows, :], kkb[0, rows, :],
                              bb[0, rows, :], lb[0, rows, :]))
            else:
                probs.append((False, rf[0, rows, :], kf[0, rows, :], vf[0, rows, :], kkf[0, rows, :],
                              bf[0, rows, :], lf[0, rows, :]))
        return probs

    def advance(slot, rows):
        x_s, y_o = (xb_s, yb_o) if slot % 2 else (xf_s, yf_o)
        x = x_s[...]
        res = _dot(qg_s[slot], x.astype(BF16))
        ys = res[:sc] + yh_s[slot]
        y = ys[0:chunk]
        for h in range(1, D_HEADS):
            y = y + ys[h * chunk:(h + 1) * chunk]
        y_o[0, rows, :] = y
        u = -(res[sc:] + uh_s[slot])
        x_s[...] = x * wc_s[slot] + _dot(kbt_s[slot], jnp.concatenate([v_s[slot], u.astype(BF16)], axis=0))

    def solve(g, pending):
        gen = _wkv_phase1(chunk, consts, problems(g))
        pending = list(pending)
        while True:
            try:
                next(gen)
            except StopIteration as done:
                sols = done.value
                break
            for step in pending[:2]:
                step()
            pending = pending[2:]
        for step in pending:
            step()
        for slot, sol in enumerate(sols):
            for name, ref in sol_refs.items():
                ref[slot] = sol[name]

    def steps_for(g):
        return [functools.partial(advance, slot, r) for slot, r in enumerate(group_rows(g))]

    n_groups = n // group
    solve(0, [])

    for g in range(1, n_groups):
        solve(g, steps_for(g - 1))
    for step in steps_for(n_groups - 1):
        step()


def _wkv_scan(r, kh, v, kk, bb, ldf, ldb, tm, chunk, group):
    b, t, _ = r.shape
    nt = t // tm
    sc = D_HEADS * chunk
    slots = 2 * group
    fspec = pl.BlockSpec((1, tm, MIX_W), lambda bi, i: (bi, i, 0))
    bspec = pl.BlockSpec((1, tm, MIX_W), lambda bi, i: (bi, nt - 1 - i, 0))
    shape = jax.ShapeDtypeStruct((b, t, MIX_W), F32)
    return pl.pallas_call(
        functools.partial(_wkv_scan_kernel, chunk, group),
        grid=(b, nt),
        in_specs=[fspec] * 6 + [bspec] * 6,
        out_specs=[fspec, bspec],
        out_shape=[shape, shape],
        scratch_shapes=[
            pltpu.VMEM((MIX_W, MIX_W), F32), pltpu.VMEM((MIX_W, MIX_W), F32),
            pltpu.VMEM((slots, 2 * sc, MIX_W), BF16),
            pltpu.VMEM((slots, sc, MIX_W), F32),
            pltpu.VMEM((slots, sc, MIX_W), F32),
            pltpu.VMEM((slots, MIX_W, 2 * sc), BF16),
            pltpu.VMEM((slots, sc, MIX_W), BF16),
            pltpu.VMEM((slots, MIX_W, MIX_W), F32),
        ],
        compiler_params=_params("parallel", "arbitrary"),
        name="wkv_scan",
    )(r, kh, v, kk, bb, ldf, r, kh, v, kk, bb, ldb)


def _wkv_out(yf, yb, bonus, g, gn_w, gn_b):
    y = yf + yb
    mean = _head_sum(y, HEAD_DIM) * (1.0 / HEAD_DIM)
    yc = y - mean
    var = _head_sum(yc * yc, HEAD_DIM) * (1.0 / HEAD_DIM)
    yn = yc * lax.rsqrt(var + WKV_GN_EPS) * gn_w + gn_b
    return ((yn + bonus) * g).astype(BF16)


def _post_kernel(x_ref, oa_ref, ob_ref, oc_ref, yf_ref, yb_ref, bonus_ref, g_ref, gnw_ref, gnb_ref,
                 wo_ref, gpost_ref, gpre_ref, wg_ref, wu_ref, wd_ref, gfpost_ref, y_ref):
    tm = x_ref.shape[0]
    groups = [slice(s, s + tm // ROW_GROUPS) for s in range(0, tm, tm // ROW_GROUPS)]
    o_d = [_wkv_out(yf_ref[r], yb_ref[r], bonus_ref[r], g_ref[r], gnw_ref[...], gnb_ref[...]) for r in groups]
    mix = [_dot(oa_ref[r], wo_ref[0]) + _dot(ob_ref[r], wo_ref[1]) + _dot(oc_ref[r], wo_ref[2])
           + _dot(od, wo_ref[3]) for r, od in zip(groups, o_d)]
    x = [x_ref[r] + _rms_rows(m, gpost_ref[...]) for r, m in zip(groups, mix)]
    h = [_rms_rows(xg, gpre_ref[...]).astype(BF16) for xg in x]
    gate = [_dot(hg, wg_ref[...]) for hg in h]
    up = [_dot(hg, wu_ref[...]) for hg in h]
    act = [(g * _sigmoid(g) * u).astype(BF16) for g, u in zip(gate, up)]
    f = [_dot(a, wd_ref[...]) for a in act]
    for r, xg, fg in zip(groups, x, f):
        y_ref[r] = xg + _rms_rows(fg, gfpost_ref[...])


def _post(x2, oa, ob, oc, wkv_parts, gn_w, gn_b, w_out4, g_post, g_pre, wg, wu, wd, gf_post, tm):
    n, d = x2.shape
    dff = wg.shape[1]
    tok = lambda w: pl.BlockSpec((tm, w), lambda i: (i, 0))
    row = pl.BlockSpec((1, d), lambda i: (0, 0))
    mrow = pl.BlockSpec((1, MIX_W), lambda i: (0, 0))
    return pl.pallas_call(
        _post_kernel,
        grid=(n // tm,),
        in_specs=[tok(d)] + [tok(MIX_W)] * 7 + [mrow, mrow,
                  pl.BlockSpec((4, MIX_W, d), lambda i: (0, 0, 0)), row, row,
                  pl.BlockSpec((d, dff), lambda i: (0, 0)),
                  pl.BlockSpec((d, dff), lambda i: (0, 0)),
                  pl.BlockSpec((dff, d), lambda i: (0, 0)), row],
        out_specs=tok(d),
        out_shape=jax.ShapeDtypeStruct((n, d), F32),
        compiler_params=_params("parallel"),
        name="out_proj_ffn",
    )(x2, oa, ob, oc, *wkv_parts, gn_w, gn_b, w_out4, g_post, g_pre, wg, wu, wd, gf_post)


def _angles(pos, rot_dim, theta):
    inv = theta ** (-jnp.arange(0, rot_dim, 2, dtype=F32) / rot_dim)
    return pos.astype(F32)[:, None] * inv[None, :]


def _rope_tables(t):
    rows = t // GRID_W
    row_idx = jnp.repeat(jnp.arange(rows), GRID_W)
    col_idx = jnp.tile(jnp.arange(GRID_W), rows)
    pos = jnp.arange(t)
    ar = _angles(row_idx, HEAD_DIM // 2, A_THETA)
    ac = _angles(col_idx, HEAD_DIM // 2, A_THETA)
    cos_a = jnp.concatenate([jnp.cos(ar), jnp.cos(ar), jnp.cos(ac), jnp.cos(ac)], axis=1)
    sin_a = jnp.concatenate([-jnp.sin(ar), jnp.sin(ar), -jnp.sin(ac), jnp.sin(ac)], axis=1)
    tabs_a = (jnp.tile(cos_a, (1, 4)), jnp.tile(sin_a, (1, 4)))
    ab = _angles(pos, B_ROT, B_THETA)
    pad1 = jnp.ones((t, B_SUB - B_ROT), F32)
    pad0 = jnp.zeros((t, B_SUB - B_ROT), F32)
    cos_b = jnp.concatenate([jnp.cos(ab), jnp.cos(ab), pad1], axis=1)
    sin_b = jnp.concatenate([-jnp.sin(ab), jnp.sin(ab), pad0], axis=1)
    tabs_b = (jnp.tile(cos_b, (1, 8)), jnp.tile(sin_b, (1, 8)))
    ang_c = _angles(pos, HEAD_DIM, C_THETA)
    cos_c = jnp.concatenate([jnp.cos(ang_c), jnp.cos(ang_c)], axis=1)
    sin_c = jnp.concatenate([-jnp.sin(ang_c), jnp.sin(ang_c)], axis=1)
    tabs_c = (jnp.tile(cos_c, (1, 4)), jnp.tile(sin_c, (1, 4)))
    return tabs_a, tabs_b, tabs_c


def _tile(n, pref):
    while n % pref:
        pref //= 2
    return pref


def _layer_weights(l, w):
    d_model = w["w_in"].shape[1]
    row = lambda a: a.reshape(1, -1).astype(F32)
    zeros = lambda r: jnp.zeros((r, MIX_W), F32)
    w_lora = w["d_w_up"].shape[2]
    return {
        "g_mix_pre": row(w["norm_mix_pre"][l]), "g_mix_post": row(w["norm_mix_post"][l]),
        "g_ffn_pre": row(w["norm_ffn_pre"][l]), "g_ffn_post": row(w["norm_ffn_post"][l]),
        "w_abc": w["w_in"][l][:, :ABC_COLS].astype(BF16), "w_d": w["w_in"][l][:, ABC_COLS:].astype(BF16),
        "w_out4": w["w_out"][l].reshape(4, MIX_W, d_model).astype(BF16),
        "a_gain": jnp.concatenate([jnp.tile(row(w["a_q_gain"][l]), (1, A_HEADS)),
                                   jnp.tile(row(w["a_k_gain"][l]), (1, A_KV_HEADS))], axis=1),
        "b_lambda": w["b_lambda"][l].astype(F32), "b_gain": w["b_subln_gain"][l].reshape(-1, 1).astype(F32),
        "c_gain": row(w["c_gn_gain"][l]),
        "wkv": {
            "mu_prev": row(w["d_mu_prev"][l]), "mu_next": row(w["d_mu_next"][l]),
            "w0": w["d_w0"][l].astype(F32),
            "wup_f": jnp.concatenate([w["d_w_up"][l, 0], zeros(w_lora)], axis=0).astype(BF16),
            "wup_b": jnp.concatenate([zeros(w_lora), w["d_w_up"][l, 1]], axis=0).astype(BF16),
            "a0": row(w["d_a0"][l]),
            "aup": jnp.concatenate([w["d_a_up"][l], zeros(64)], axis=0).astype(BF16),
            "gup1": jnp.concatenate([zeros(64), w["d_g_up"][l][:64]], axis=0).astype(BF16),
            "gup2": w["d_g_up"][l][64:].astype(BF16),
            "k_k": row(w["d_k_k"][l]), "k_a": row(w["d_k_a"][l]), "r_k": row(w["d_r_k"][l]),
        },
        "gn_w": row(w["d_gn_w"][l]), "gn_b": row(w["d_gn_b"][l]),
        "wg": w["ffn_w_gate"][l].astype(BF16), "wu": w["ffn_w_up"][l].astype(BF16),
        "wd": w["ffn_w_down"][l].astype(BF16),
    }


def _trunk(x, layers):
    b, t, d = x.shape
    n = b * t
    tabs = _rope_tables(t)
    tm_proj = _tile(t, 512)
    tm_post = _tile(n, 512)
    tq = _tile(t, 512)
    tm_wkv = _tile(t, 1024)
    tm_scan = _tile(t, 1024)
    chunk = 32
    x2 = x.reshape(n, d)
    for l, p in enumerate(layers):
        lam_init = 0.8 - 0.6 * math.exp(-0.3 * l)
        zabc, zd = _in_proj(x2, p["g_mix_pre"], p["w_abc"], p["w_d"], tabs, p["a_gain"], t, tm_proj)
        zabc = zabc.reshape(b, t, ABC_COLS)
        zd = zd.reshape(b, t, D_COLS)
        o_a = _mixer_gqa(zabc, tq)
        o_b = _mixer_diff(zabc, p["b_lambda"], p["b_gain"], lam_init, tq)
        o_c = _mixer_ret(zabc, p["c_gain"], _tile(t, 256))
        r, kh, v, kk, bb, ldf, ldb, g, bonus = _wkv_prep(zd, p["wkv"], tm_wkv)
        yf, yb = _wkv_scan(r, kh, v, kk, bb, ldf, ldb, tm_scan, chunk, min(4, tm_scan // chunk))
        wkv_parts = [a.reshape(n, MIX_W) for a in (yf, yb, bonus, g)]
        x2 = _post(x2, o_a.reshape(n, MIX_W), o_b.reshape(n, MIX_W), o_c.reshape(n, MIX_W), wkv_parts,
                   p["gn_w"], p["gn_b"], p["w_out4"], p["g_mix_post"], p["g_ffn_pre"], p["wg"], p["wu"], p["wd"],
                   p["g_ffn_post"], tm_post)
    return x2.reshape(b, t, d)


def kernel(x_prompt, x_sample, norm_mix_pre, norm_mix_post, norm_ffn_pre, norm_ffn_post, w_in, w_out,
           a_q_gain, a_k_gain, b_lambda, b_subln_gain, c_gn_gain, d_mu_prev, d_mu_next, d_w0, d_w_up,
           d_a0, d_a_up, d_g_up, d_k_k, d_k_a, d_r_k, d_gn_w, d_gn_b, ffn_w_gate, ffn_w_up, ffn_w_down):
    w = {
        "norm_mix_pre": norm_mix_pre, "norm_mix_post": norm_mix_post,
        "norm_ffn_pre": norm_ffn_pre, "norm_ffn_post": norm_ffn_post,
        "w_in": w_in, "w_out": w_out, "a_q_gain": a_q_gain, "a_k_gain": a_k_gain,
        "b_lambda": b_lambda, "b_subln_gain": b_subln_gain, "c_gn_gain": c_gn_gain,
        "d_mu_prev": d_mu_prev, "d_mu_next": d_mu_next, "d_w0": d_w0, "d_w_up": d_w_up,
        "d_a0": d_a0, "d_a_up": d_a_up, "d_g_up": d_g_up, "d_k_k": d_k_k, "d_k_a": d_k_a,
        "d_r_k": d_r_k, "d_gn_w": d_gn_w, "d_gn_b": d_gn_b,
        "ffn_w_gate": ffn_w_gate, "ffn_w_up": ffn_w_up, "ffn_w_down": ffn_w_down,
    }
    layers = [_layer_weights(l, w) for l in range(w_in.shape[0])]
    return (_trunk(x_prompt, layers), _trunk(x_sample, layers))
```

```python
import functools
import math

import jax
import jax.numpy as jnp
from jax import lax
from jax.experimental import pallas as pl
from jax.experimental.pallas import tpu as pltpu

F32 = jnp.float32
BF16 = jnp.bfloat16

HEAD_DIM = 64
GRID_W = 64
NORM_EPS = 1e-6
A_HEADS, A_KV_HEADS, A_THETA = 4, 2, 10000.0
B_HEADS, B_SUB, B_ROT, B_THETA = 4, 32, 8, 500000.0
C_HEADS, C_THETA = 4, 10000.0
D_HEADS = 4
WKV_GN_EPS = 64e-5
MIX_W = 256
ABC_COLS = 2304
A_Q, A_K, A_V = slice(0, 256), slice(256, 384), slice(384, 512)
B_Q, B_K, B_V = slice(512, 768), slice(768, 1024), slice(1024, 1280)
C_Q, C_K, C_VG = slice(1280, 1536), slice(1536, 1792), slice(1792, 2304)
D_COLS = 1088
BF16_SUBLANES = 16
VEXT_ROWS = HEAD_DIM + BF16_SUBLANES
LOG2E = math.log2(math.e)
KEY_CHUNK = 512
ROW_GROUPS = 2
SHIFT_LIMIT = 100.0
VMEM_LIMIT = 56 * 1024 * 1024


def _params(*sem):
    return pltpu.CompilerParams(dimension_semantics=sem, vmem_limit_bytes=VMEM_LIMIT)


def _rms_rows(x, gain):
    return x * lax.rsqrt(jnp.mean(x * x, axis=-1, keepdims=True) + NORM_EPS) * gain


def _split2(x):
    hi = x.astype(BF16)
    lo = (x - hi.astype(F32)).astype(BF16)
    return hi, lo


def _split3(x):
    h1 = x.astype(BF16)
    r1 = x - h1.astype(F32)
    h2 = r1.astype(BF16)
    h3 = (r1 - h2.astype(F32)).astype(BF16)
    return h1, h2, h3


def _head_sum(x, seg):
    w = x.shape[-1]
    r = lax.broadcasted_iota(jnp.int32, (w, w), 0) // seg
    c = lax.broadcasted_iota(jnp.int32, (w, w), 1) // seg
    bd = jnp.where(r == c, 1.0, 0.0).astype(BF16)
    hi, lo = _split2(x)
    return (jnp.dot(hi, bd, preferred_element_type=F32)
            + jnp.dot(lo, bd, preferred_element_type=F32))


def _rope(x, cos, sin, half):
    w = x.shape[-1]
    lane = lax.broadcasted_iota(jnp.int32, x.shape, 1)
    nxt = pltpu.roll(x, w - half, 1)
    prv = pltpu.roll(x, half, 1)
    return x * cos + jnp.where((lane % (2 * half)) < half, nxt, prv) * sin


def _sigmoid(x):
    return 0.5 * jnp.tanh(0.5 * x) + 0.5


def _dot(a, b):
    return jnp.dot(a, b, preferred_element_type=F32)


def _dot_nt(a, b):
    return lax.dot_general(a, b, (((1,), (1,)), ((), ())), preferred_element_type=F32)


def _store_qk_prepared(z, r, tabs, gain_a, out_ref):
    (cos_a, sin_a), (cos_b, sin_b), (cos_c, sin_c) = tabs

    def head_norm(x, gain):
        return x * lax.rsqrt(_head_sum(x * x, HEAD_DIM) * (1.0 / HEAD_DIM) + NORM_EPS) * gain

    a_qk = head_norm(z[:, A_Q.start:A_K.stop], gain_a)
    kw = A_K.stop - A_K.start
    pieces = {
        A_Q: _rope(a_qk[:, :A_Q.stop], cos_a, sin_a, HEAD_DIM // 4) * (HEAD_DIM ** -0.5 * LOG2E),
        A_K: _rope(a_qk[:, A_Q.stop:], cos_a[:, :kw], sin_a[:, :kw], HEAD_DIM // 4),
        B_Q: _rope(z[:, B_Q], cos_b, sin_b, B_ROT // 2) * (B_SUB ** -0.5 * LOG2E),
        B_K: _rope(z[:, B_K], cos_b, sin_b, B_ROT // 2),
        C_Q: _rope(z[:, C_Q], cos_c, sin_c, HEAD_DIM // 2),
        C_K: _rope(z[:, C_K], cos_c, sin_c, HEAD_DIM // 2) * (HEAD_DIM ** -0.5),
    }
    for cols in (A_V, B_V, C_VG):
        pieces[cols] = z[:, cols]
    for cols, val in pieces.items():
        out_ref[r, cols] = val.astype(BF16)


def _in_proj_kernel(x_ref, g_ref, wabc_ref, wd_ref, ca_ref, sa_ref, cb_ref, sb_ref, cc_ref, sc_ref, ga_ref,
                    zabc_ref, zd_ref):
    tm = x_ref.shape[0]
    groups = [slice(s, s + tm // ROW_GROUPS) for s in range(0, tm, tm // ROW_GROUPS)]
    h = [_rms_rows(x_ref[r], g_ref[...]).astype(BF16) for r in groups]
    z = [_dot(hg, wabc_ref[...]) for hg in h]
    for r, hg, zg in zip(groups, h, z):
        tabs = ((ca_ref[r], sa_ref[r]), (cb_ref[r], sb_ref[r]), (cc_ref[r], sc_ref[r]))
        _store_qk_prepared(zg, r, tabs, ga_ref[...], zabc_ref)
        zd_ref[r] = _dot(hg, wd_ref[...])


def _in_proj(x2, gain, w_abc, w_d, tabs, gain_a, t, tm):
    n, d = x2.shape
    per_seq = t // tm
    tab = pl.BlockSpec((tm, MIX_W), lambda i: (i % per_seq, 0))
    return pl.pallas_call(
        _in_proj_kernel,
        grid=(n // tm,),
        in_specs=[
            pl.BlockSpec((tm, d), lambda i: (i, 0)),
            pl.BlockSpec((1, d), lambda i: (0, 0)),
            pl.BlockSpec((d, ABC_COLS), lambda i: (0, 0)),
            pl.BlockSpec((d, D_COLS), lambda i: (0, 0)),
            tab, tab, tab, tab, tab, tab,
            pl.BlockSpec((1, A_K.stop), lambda i: (0, 0)),
        ],
        out_specs=[
            pl.BlockSpec((tm, ABC_COLS), lambda i: (i, 0)),
            pl.BlockSpec((tm, D_COLS), lambda i: (i, 0)),
        ],
        out_shape=[
            jax.ShapeDtypeStruct((n, ABC_COLS), BF16),
            jax.ShapeDtypeStruct((n, D_COLS), F32),
        ],
        compiler_params=_params("parallel"),
        name="in_proj",
    )(x2, gain, w_abc, w_d, *tabs[0], *tabs[1], *tabs[2], gain_a)


def _value_ext_t(v_t):
    row = lax.broadcasted_iota(jnp.int32, (VEXT_ROWS - HEAD_DIM, v_t.shape[1]), 0)
    return jnp.concatenate([v_t, jnp.where(row == 0, 1.0, 0.0)], axis=0).astype(BF16)


def _attend_exact(k_ref, q_ts, vx_refs):
    outs = []
    for q_t, vx in zip(q_ts, vx_refs):
        s_t = _dot(k_ref[...], q_t)
        p_t = jnp.exp2(s_t - jnp.max(s_t, axis=0, keepdims=True)).astype(BF16)
        o = _dot(vx[...], p_t)
        outs.append(o[:HEAD_DIM] / o[HEAD_DIM:HEAD_DIM + 1])
    return outs


def _attend_shifted(k_ref, q_ts, shifts, vx_refs):
    t = k_ref.shape[0]
    ck = min(t, KEY_CHUNK)
    steps = [(j, slice(c * ck, (c + 1) * ck)) for j in range(len(q_ts)) for c in range(t // ck)]
    acc = [None] * len(q_ts)
    s_prev = p_prev = None
    for i in range(len(steps) + 2):
        s_new = p_new = None
        if i < len(steps):
            j, keys = steps[i]
            s_new = _dot(k_ref[keys, :], q_ts[j])
        if 1 <= i <= len(steps):
            p_new = jnp.exp2(s_prev - shifts[steps[i - 1][0]]).astype(BF16)
        if i >= 2:
            j, keys = steps[i - 2]
            o = _dot(vx_refs[j][:, keys], p_prev)
            acc[j] = o if acc[j] is None else acc[j] + o
        s_prev, p_prev = s_new, p_new
    return [a[:HEAD_DIM] / a[HEAD_DIM:HEAD_DIM + 1] for a in acc]


def _attend_t(k_ref, kn_ref, q_t, head_rows, head_keys, vx_refs, finish):
    c = k_ref.shape[1]
    tq = q_t.shape[1]
    qsq = q_t * q_t
    q_ts, shifts = [], []
    for rows, keys in zip(head_rows, head_keys):
        pieces = [jnp.zeros((keys.start, tq), F32), q_t[rows], jnp.zeros((c - keys.stop, tq), F32)]
        q_ts.append(jnp.concatenate([x for x in pieces if x.shape[0]], axis=0).astype(BF16))
        qn = jnp.sqrt(jnp.sum(qsq[rows], axis=0, keepdims=True))
        shifts.append(qn * kn_ref[0:1, keys.start:keys.start + 1])
    worst = jnp.max(jnp.concatenate(shifts, axis=0))

    @pl.when(2.0 * worst < SHIFT_LIMIT)
    def _():
        finish(_attend_shifted(k_ref, q_ts, shifts, vx_refs))

    @pl.when(jnp.logical_not(2.0 * worst < SHIFT_LIMIT))
    def _():
        finish(_attend_exact(k_ref, q_ts, vx_refs))


def _gqa_kernel(q_ref, k_ref, v_ref, o_ref, kn_s, vx_s):
    @pl.when(pl.program_id(1) == 0)
    def _():
        k = k_ref[0].astype(F32)
        kn_s[...] = jnp.sqrt(jnp.max(_head_sum(k * k, HEAD_DIM), axis=0, keepdims=True))
        v_t = v_ref[0].astype(F32).T
        for g in range(A_KV_HEADS):
            vx_s[g] = _value_ext_t(v_t[g * HEAD_DIM:(g + 1) * HEAD_DIM])

    q_t = q_ref[0].astype(F32).T
    k_s = k_ref.at[0]
    group = A_HEADS // A_KV_HEADS
    head_rows = [slice(h * HEAD_DIM, (h + 1) * HEAD_DIM) for h in range(A_HEADS)]
    head_keys = [slice((h // group) * HEAD_DIM, (h // group + 1) * HEAD_DIM) for h in range(A_HEADS)]

    def finish(outs):
        o_ref[0] = jnp.concatenate(outs, axis=0).T.astype(BF16)

    _attend_t(k_s, kn_s, q_t, head_rows, head_keys, [vx_s.at[h // group] for h in range(A_HEADS)], finish)


def _mixer_gqa(zabc, tq):
    b, t, _ = zabc.shape
    return pl.pallas_call(
        _gqa_kernel,
        grid=(b, t // tq),
        in_specs=[
            pl.BlockSpec((1, tq, 256), lambda bi, i: (bi, i, 0)),
            pl.BlockSpec((1, t, 128), lambda bi, i: (bi, 0, 2)),
            pl.BlockSpec((1, t, 128), lambda bi, i: (bi, 0, 3)),
        ],
        out_specs=pl.BlockSpec((1, tq, MIX_W), lambda bi, i: (bi, i, 0)),
        out_shape=jax.ShapeDtypeStruct((b, t, MIX_W), BF16),
        scratch_shapes=[
            pltpu.VMEM((1, A_KV_HEADS * HEAD_DIM), F32),
            pltpu.VMEM((A_KV_HEADS, VEXT_ROWS, t), BF16),
        ],
        compiler_params=_params("parallel", "arbitrary"),
        name="mixer_gqa",
    )(zabc, zabc, zabc)


def _diff_kernel(lam_init, q_ref, k_ref, v_ref, lam_ref, gain_ref, o_ref, kn_s, vx_s):
    @pl.when(pl.program_id(1) == 0)
    def _():
        k = k_ref[0].astype(F32)
        kn_s[...] = jnp.sqrt(jnp.max(_head_sum(k * k, B_SUB), axis=0, keepdims=True))
        v_t = v_ref[0].astype(F32).T
        for h in range(B_HEADS):
            vx_s[h] = _value_ext_t(v_t[h * HEAD_DIM:(h + 1) * HEAD_DIM])

    lp = lam_ref[...]
    lam = (jnp.exp(jnp.sum(lp[0:1] * lp[1:2], axis=1, keepdims=True))
           - jnp.exp(jnp.sum(lp[2:3] * lp[3:4], axis=1, keepdims=True)) + lam_init)
    q_t = q_ref[0].astype(F32).T
    k_s = k_ref.at[0]
    sub_rows = [slice(j * B_SUB, (j + 1) * B_SUB) for j in range(2 * B_HEADS)]

    def finish(parts):
        outs = []
        for h in range(B_HEADS):
            o = parts[2 * h] - lam * parts[2 * h + 1]
            o = o * lax.rsqrt(jnp.mean(o * o, axis=0, keepdims=True) + NORM_EPS) * gain_ref[...]
            outs.append(o * (1.0 - lam_init))
        o_ref[0] = jnp.concatenate(outs, axis=0).T.astype(BF16)

    _attend_t(k_s, kn_s, q_t, sub_rows, sub_rows, [vx_s.at[j // 2] for j in range(2 * B_HEADS)], finish)


def _mixer_diff(zabc, lam_params, gain, lam_init, tq):
    b, t, _ = zabc.shape
    return pl.pallas_call(
        functools.partial(_diff_kernel, lam_init),
        grid=(b, t // tq),
        in_specs=[
            pl.BlockSpec((1, tq, 256), lambda bi, i: (bi, i, 2)),
            pl.BlockSpec((1, t, 256), lambda bi, i: (bi, 0, 3)),
            pl.BlockSpec((1, t, 256), lambda bi, i: (bi, 0, 4)),
            pl.BlockSpec((4, B_SUB), lambda bi, i: (0, 0)),
            pl.BlockSpec((HEAD_DIM, 1), lambda bi, i: (0, 0)),
        ],
        out_specs=pl.BlockSpec((1, tq, MIX_W), lambda bi, i: (bi, i, 0)),
        out_shape=jax.ShapeDtypeStruct((b, t, MIX_W), BF16),
        scratch_shapes=[
            pltpu.VMEM((1, 2 * B_HEADS * B_SUB), F32),
            pltpu.VMEM((B_HEADS, VEXT_ROWS, t), BF16),
        ],
        compiler_params=_params("parallel", "arbitrary"),
        name="mixer_diff",
    )(zabc, zabc, zabc, lam_params, gain)


def _ret_log_gammas():
    lg = [math.log1p(-(2.0 ** (-5.0 - h))) for h in range(C_HEADS)]
    return lg, lg[::-1]


def _lane_consts(vals, shape):
    head = lax.broadcasted_iota(jnp.int32, shape, len(shape) - 1) // HEAD_DIM
    out = jnp.full(shape, vals[-1], F32)
    for h in range(len(vals) - 2, -1, -1):
        out = jnp.where(head == h, vals[h], out)
    return out


def _ret_kernel(chunk, q_ref, k_ref, v_ref, g_ref, gain_ref, o_ref, acc_s, sf_s, sb_s, dm_s):
    t = q_ref.shape[1]
    c = chunk
    n = t // c
    lgf, lgb = _ret_log_gammas()
    qr_s = q_ref.at[0]
    kr_s = k_ref.at[0]

    ti = lax.broadcasted_iota(jnp.int32, (c, c), 0)
    si = lax.broadcasted_iota(jnp.int32, (c, c), 1)
    dist = (ti - si).astype(F32)
    lane_head = lax.broadcasted_iota(jnp.int32, (1, MIX_W), 1) // HEAD_DIM
    row = lax.broadcasted_iota(jnp.int32, (c, MIX_W), 0).astype(F32)
    lgf_l = _lane_consts(lgf, (c, MIX_W))
    lgb_l = _lane_consts(lgb, (c, MIX_W))
    qdec_f = jnp.exp(lgf_l * (row + 1.0))
    kdec_f = jnp.exp(lgf_l * (c - 1.0 - row))
    qdec_b = jnp.exp(lgb_l * (c - row))
    kdec_b = jnp.exp(lgb_l * row)
    r2 = lax.broadcasted_iota(jnp.int32, (MIX_W, MIX_W), 0) // HEAD_DIM
    c2 = lax.broadcasted_iota(jnp.int32, (MIX_W, MIX_W), 1) // HEAD_DIM
    same_head = r2 == c2
    gf_blk = jnp.where(same_head, jnp.exp(_lane_consts(lgf, (MIX_W, MIX_W)) * c), 0.0)
    gb_blk = jnp.where(same_head, jnp.exp(_lane_consts(lgb, (MIX_W, MIX_W)) * c), 0.0)

    head_mask = [(lane_head == h).astype(BF16) for h in range(C_HEADS)]
    for h in range(C_HEADS):
        dm_s[h] = jnp.where(dist > 0, jnp.exp(lgf[h] * dist),
                            jnp.where(dist < 0, jnp.exp(-lgb[h] * dist), 2.0))

    def intra(qc, kc, vc):
        ps = [(_dot_nt(qc * head_mask[h], kc) * dm_s[h]).astype(BF16) for h in range(C_HEADS)]
        v_stack = jnp.concatenate([vc * head_mask[h] for h in range(C_HEADS)], axis=0)
        return _dot(jnp.concatenate(ps, axis=1), v_stack)

    sf_s[...] = jnp.zeros_like(sf_s)
    sb_s[...] = jnp.zeros_like(sb_s)

    def fwd(i, carry):
        rows = pl.ds(pl.multiple_of(i * c, c), c)
        qc = qr_s[rows, :]
        kc = kr_s[rows, :]
        vc = v_ref[0, rows, :]
        o = intra(qc, kc, vc) + _dot((qc.astype(F32) * qdec_f).astype(BF16), sf_s[...].astype(BF16))
        acc_s[rows, :] = o
        kd = (kc.astype(F32) * kdec_f).T.astype(BF16)
        sf_s[...] = gf_blk * sf_s[...] + jnp.where(same_head, _dot(kd, vc), 0.0)
        return carry

    lax.fori_loop(0, n, fwd, 0)

    def bwd(i, carry):
        rows = pl.ds(pl.multiple_of((n - 1 - i) * c, c), c)
        qc = qr_s[rows, :].astype(F32)
        kc = kr_s[rows, :].astype(F32)
        vc = v_ref[0, rows, :]
        acc_s[rows, :] = acc_s[rows, :] + _dot((qc * qdec_b).astype(BF16), sb_s[...].astype(BF16))
        kd = (kc * kdec_b).T.astype(BF16)
        sb_s[...] = gb_blk * sb_s[...] + jnp.where(same_head, _dot(kd, vc), 0.0)
        return carry

    lax.fori_loop(0, n, bwd, 0)

    o = acc_s[...]
    o = o * lax.rsqrt(_head_sum(o * o, HEAD_DIM) * (1.0 / HEAD_DIM) + NORM_EPS) * gain_ref[...]
    g = g_ref[0].astype(F32)
    o_ref[0] = (o * (g * _sigmoid(g))).astype(BF16)


def _mixer_ret(zabc, gain, chunk):
    b, t, _ = zabc.shape
    blk = lambda j: pl.BlockSpec((1, t, 256), lambda bi: (bi, 0, j))
    return pl.pallas_call(
        functools.partial(_ret_kernel, chunk),
        grid=(b,),
        in_specs=[blk(5), blk(6), blk(7), blk(8),
                  pl.BlockSpec((1, 256), lambda bi: (0, 0))],
        out_specs=pl.BlockSpec((1, t, MIX_W), lambda bi: (bi, 0, 0)),
        out_shape=jax.ShapeDtypeStruct((b, t, MIX_W), BF16),
        scratch_shapes=[
            pltpu.VMEM((t, MIX_W), F32),
            pltpu.VMEM((MIX_W, MIX_W), F32),
            pltpu.VMEM((MIX_W, MIX_W), F32),
            pltpu.VMEM((C_HEADS, chunk, chunk), F32),
        ],
        compiler_params=_params("parallel"),
        name="mixer_ret",
    )(zabc, zabc, zabc, zabc, gain)


def _wkv_prep_kernel(z_ref, zp_ref, zn_ref, mup_ref, mun_ref, w0_ref, wupf_ref, wupb_ref, a0_ref, aup_ref,
                     gup1_ref, gup2_ref, kk_ref, ka_ref, rk_ref,
                     r_o, k_o, v_o, kk_o, b_o, ldf_o, ldb_o, g_o, bonus_o):
    i = pl.program_id(1)
    last = pl.num_programs(1) - 1
    z = z_ref[0]
    tm = z.shape[0]
    row = lax.broadcasted_iota(jnp.int32, z.shape, 0)
    prev_row = zp_ref[0, 0, 7:8, :] * jnp.where(i > 0, 1.0, 0.0)
    next_row = zn_ref[0, 0, 0:1, :] * jnp.where(i < last, 1.0, 0.0)
    z_prev = jnp.where(row == 0, prev_row, pltpu.roll(z, 1, 0))
    z_next = jnp.where(row == tm - 1, next_row, pltpu.roll(z, tm - 1, 0))
    mu_p, mu_n = mup_ref[...], mun_ref[...]
    u = z * (1.0 - mu_p - mu_n) + z_prev * mu_p + z_next * mu_n
    r = u[:, 0:256]
    k = u[:, 256:512]
    v = u[:, 512:768]
    wd = jnp.tanh(u[:, 768:896]).astype(BF16)
    ag = u[:, 896:1024]
    g2 = u[:, 1024:1088]

    def log_decay(w0, wup):
        return _sigmoid(w0 + _dot(wd, wup)) * (-math.exp(-0.5))

    ldf_o[0] = log_decay(w0_ref[0:1, :], wupf_ref[...])
    ldb_o[0] = log_decay(w0_ref[1:2, :], wupb_ref[...])
    a = _sigmoid(a0_ref[...] + _dot(ag.astype(BF16), aup_ref[...]))
    g_o[0] = (_dot(_sigmoid(ag).astype(BF16), gup1_ref[...])
              + _dot(_sigmoid(g2).astype(BF16), gup2_ref[...])).astype(BF16)
    kk = k * kk_ref[...]
    kk = kk / jnp.maximum(jnp.sqrt(_head_sum(kk * kk, HEAD_DIM)), 1e-12)
    kh = k * (1.0 + (a - 1.0) * ka_ref[...])
    r_o[0] = r.astype(BF16)
    k_o[0] = kh.astype(BF16)
    v_o[0] = v.astype(BF16)
    kk_o[0] = kk.astype(BF16)
    b_o[0] = (kk * a).astype(BF16)
    bonus_o[0] = (_head_sum(r * kh * rk_ref[...], HEAD_DIM) * v).astype(BF16)


def _wkv_prep(zd, p, tm):
    b, t, _ = zd.shape
    zd8 = zd.reshape(b, t // 8, 8, D_COLS)
    r8 = tm // 8
    nb8 = t // 8
    row = lambda w: pl.BlockSpec((1, w), lambda bi, i: (0, 0))
    full = lambda a: pl.BlockSpec(a.shape, lambda bi, i: (0,) * a.ndim)
    out_spec = pl.BlockSpec((1, tm, MIX_W), lambda bi, i: (bi, i, 0))
    out_shape = jax.ShapeDtypeStruct((b, t, MIX_W), F32)
    out_bf16 = jax.ShapeDtypeStruct((b, t, MIX_W), BF16)
    return pl.pallas_call(
        _wkv_prep_kernel,
        grid=(b, t // tm),
        in_specs=[
            pl.BlockSpec((1, tm, D_COLS), lambda bi, i: (bi, i, 0)),
            pl.BlockSpec((1, 1, 8, D_COLS), lambda bi, i: (bi, jnp.maximum(i * r8 - 1, 0), 0, 0)),
            pl.BlockSpec((1, 1, 8, D_COLS), lambda bi, i: (bi, jnp.minimum((i + 1) * r8, nb8 - 1), 0, 0)),
            row(D_COLS), row(D_COLS),
            full(p["w0"]), full(p["wup_f"]), full(p["wup_b"]), row(256), full(p["aup"]),
            full(p["gup1"]), full(p["gup2"]), row(256), row(256), row(256),
        ],
        out_specs=[out_spec] * 9,
        out_shape=[out_bf16] * 5 + [out_shape] * 2 + [out_bf16] * 2,
        compiler_params=_params("parallel", "parallel"),
        name="wkv_prep",
    )(zd, zd8, zd8, p["mu_prev"], p["mu_next"], p["w0"], p["wup_f"], p["wup_b"], p["a0"], p["aup"],
      p["gup1"], p["gup2"], p["k_k"], p["k_a"], p["r_k"])


def _wkv_chunk_consts(chunk):
    c = chunk
    sc = D_HEADS * c
    ti = lax.broadcasted_iota(jnp.int32, (c, c), 0)
    si = lax.broadcasted_iota(jnp.int32, (c, c), 1)
    srow = lax.broadcasted_iota(jnp.int32, (sc, MIX_W), 0) // c
    slane = lax.broadcasted_iota(jnp.int32, (sc, MIX_W), 1) // HEAD_DIM
    gt = lax.broadcasted_iota(jnp.int32, (2 * sc, 2 * sc), 0)
    gs = lax.broadcasted_iota(jnp.int32, (2 * sc, 2 * sc), 1)
    t_in, s_in = gt % sc, gs % sc
    same = (t_in // c) == (s_in // c)
    upper = gt < sc
    st = lax.broadcasted_iota(jnp.int32, (sc, sc), 0)
    ss = lax.broadcasted_iota(jnp.int32, (sc, sc), 1)
    r2 = lax.broadcasted_iota(jnp.int32, (MIX_W, MIX_W), 0)
    c2 = lax.broadcasted_iota(jnp.int32, (MIX_W, MIX_W), 1)
    return {
        "tri": {False: jnp.where(si <= ti, 1.0, 0.0).astype(BF16), True: jnp.where(si >= ti, 1.0, 0.0).astype(BF16)},
        "hmask": srow == slane,
        "gmask": {False: same & ((s_in < t_in) | ((s_in == t_in) & ~upper)),
                  True: same & ((s_in > t_in) | ((s_in == t_in) & ~upper))},
        "eye": jnp.where(st == ss, 1.0, 0.0),
        "diag": r2 == c2,
    }


def _wkv_phase1(chunk, consts, probs):
    c = chunk
    sc = D_HEADS * c
    hmask = consts["hmask"]

    def stack(a):
        return jnp.where(hmask, jnp.concatenate([a] * D_HEADS, axis=0), 0.0).astype(BF16)

    pre = []
    for rev, r, kh, v, kk, bb, lw in probs:
        tri = consts["tri"][rev]
        l1, l2, l3 = _split3(lw)
        cl = _dot(tri, l1) + _dot(tri, l2) + _dot(tri, l3)
        tot = jnp.sum(lw, axis=0, keepdims=True)
        pre.append((cl, tot))
    yield
    ops = []
    for (rev, r, kh, v, kk, bb, lw), (cl, tot) in zip(probs, pre):
        w_inv = jnp.exp(-cl)
        w_end = jnp.exp(tot - cl)
        lhs = jnp.concatenate([stack(kk * jnp.exp(cl - lw)), stack(r * jnp.exp(cl))], axis=0)
        rhs = jnp.concatenate([stack(bb * w_inv), stack(kh * w_inv)], axis=0)
        kb = jnp.concatenate([stack(kh * w_end), stack(bb * w_end)], axis=0)
        wc = jnp.sum(jnp.where(consts["diag"], jnp.exp(tot), 0.0), axis=1, keepdims=True)
        wc = jnp.broadcast_to(wc, (MIX_W, MIX_W))
        ops.append((lhs, rhs, stack(v), kb, wc))
    yield
    gram = [jnp.where(consts["gmask"][p[0]], _dot_nt(o[0], o[1]), 0.0) for p, o in zip(probs, ops)]
    l_ab = [g[:sc, :sc] for g in gram]
    l_ak = [g[:sc, sc:].astype(BF16) for g in gram]
    m_r = [g[sc:, :].astype(BF16) for g in gram]
    yield

    steps = int(math.log2(c))
    pw = [(-l).astype(BF16) for l in l_ab]
    inv = [consts["eye"] - l for l in l_ab]
    w = [_dot(a, o[2]).astype(BF16) for a, o in zip(l_ak, ops)]
    for k in range(1, steps):
        if k == 1:
            pw = [_dot(p, p).astype(BF16) for p in pw]
            yield
        if k < steps - 1:
            nxt = [_dot(p, jnp.concatenate([p, i.astype(BF16)], axis=1)) for p, i in zip(pw, inv)]
            inv = [i + n[:, sc:] for i, n in zip(inv, nxt)]
            pw = [n[:, :sc].astype(BF16) for n in nxt]
        else:
            inv = [i + _dot(p, i.astype(BF16)) for p, i in zip(pw, inv)]
        yield
    inv = [i.astype(BF16) for i in inv]

    gu = [_dot(i, jnp.concatenate([o[0][:sc], wv], axis=1)) for i, o, wv in zip(inv, ops, w)]
    yield
    kb_t = [o[3].astype(F32).T.astype(BF16) for o in ops]
    mgu = [_dot(m[:, :sc], x.astype(BF16)) for m, x in zip(m_r, gu)]
    mv = [_dot(m[:, sc:], o[2]) for m, o in zip(m_r, ops)]
    yield
    return [{
        "qg": jnp.concatenate([(o[0][sc:].astype(F32) - mg[:, :MIX_W]).astype(BF16),
                               g[:, :MIX_W].astype(BF16)], axis=0),
        "y_hat": a - mg[:, MIX_W:],
        "u_hat": g[:, MIX_W:],
        "kb_t": kt,
        "v": o[2],
        "wc": o[4],
    } for o, g, mg, a, kt in zip(ops, gu, mgu, mv, kb_t)]


def _wkv_scan_kernel(chunk, group, rf, kf, vf, kkf, bf, lf, rb, kb, vb, kkb, bb, lb, yf_o, yb_o,
                     xf_s, xb_s, qg_s, yh_s, uh_s, kbt_s, v_s, wc_s):
    @pl.when(pl.program_id(1) == 0)
    def _():
        xf_s[...] = jnp.zeros_like(xf_s)
        xb_s[...] = jnp.zeros_like(xb_s)

    tm = rf.shape[1]
    n = tm // chunk
    sc = D_HEADS * chunk
    consts = _wkv_chunk_consts(chunk)
    sol_refs = {"qg": qg_s, "y_hat": yh_s, "u_hat": uh_s, "kb_t": kbt_s, "v": v_s, "wc": wc_s}

    def group_rows(g):
        rows = []
        for u in range(group):
            rows.append(pl.ds(pl.multiple_of((g * group + u) * chunk, chunk), chunk))
            rows.append(pl.ds(pl.multiple_of((n - 1 - g * group - u) * chunk, chunk), chunk))
        return rows

    def problems(g):
        probs = []
        for slot, rows in enumerate(group_rows(g)):
            if slot % 2:
                probs.append((True, rb[0, rows, :], kb[0, rows, :], vb[0, rows, :], kkb[0, rows, :],
                              bb[0, rows, :], lb[0, rows, :]))
            else:
                probs.append((False, rf[0, rows, :], kf[0, rows, :], vf[0, rows, :], kkf[0, rows, :],
                              bf[0, rows, :], lf[0, rows, :]))
        return probs

    def advance(slot, rows):
        x_s, y_o = (xb_s, yb_o) if slot % 2 else (xf_s, yf_o)
        x = x_s[...]
        res = _dot(qg_s[slot], x.astype(BF16))
        ys = res[:sc] + yh_s[slot]
        y = ys[0:chunk]
        for h in range(1, D_HEADS):
            y = y + ys[h * chunk:(h + 1) * chunk]
        y_o[0, rows, :] = y
        u = -(res[sc:] + uh_s[slot])
        x_s[...] = x * wc_s[slot] + _dot(kbt_s[slot], jnp.concatenate([v_s[slot], u.astype(BF16)], axis=0))

    def solve(g, pending):
        gen = _wkv_phase1(chunk, consts, problems(g))
        pending = list(pending)
        while True:
            try:
                next(gen)
            except StopIteration as done:
                sols = done.value
                break
            for step in pending[:2]:
                step()
            pending = pending[2:]
        for step in pending:
            step()
        for slot, sol in enumerate(sols):
            for name, ref in sol_refs.items():
                ref[slot] = sol[name]

    def steps_for(g):
        return [functools.partial(advance, slot, r) for slot, r in enumerate(group_rows(g))]

    n_groups = n // group
    solve(0, [])

    for g in range(1, n_groups):
        solve(g, steps_for(g - 1))
    for step in steps_for(n_groups - 1):
        step()


def _wkv_scan(r, kh, v, kk, bb, ldf, ldb, tm, chunk, group):
    b, t, _ = r.shape
    nt = t // tm
    sc = D_HEADS * chunk
    slots = 2 * group
    fspec = pl.BlockSpec((1, tm, MIX_W), lambda bi, i: (bi, i, 0))
    bspec = pl.BlockSpec((1, tm, MIX_W), lambda bi, i: (bi, nt - 1 - i, 0))
    shape = jax.ShapeDtypeStruct((b, t, MIX_W), F32)
    return pl.pallas_call(
        functools.partial(_wkv_scan_kernel, chunk, group),
        grid=(b, nt),
        in_specs=[fspec] * 6 + [bspec] * 6,
        out_specs=[fspec, bspec],
        out_shape=[shape, shape],
        scratch_shapes=[
            pltpu.VMEM((MIX_W, MIX_W), F32), pltpu.VMEM((MIX_W, MIX_W), F32),
            pltpu.VMEM((slots, 2 * sc, MIX_W), BF16),
            pltpu.VMEM((slots, sc, MIX_W), F32),
            pltpu.VMEM((slots, sc, MIX_W), F32),
            pltpu.VMEM((slots, MIX_W, 2 * sc), BF16),
            pltpu.VMEM((slots, sc, MIX_W), BF16),
            pltpu.VMEM((slots, MIX_W, MIX_W), F32),
        ],
        compiler_params=_params("parallel", "arbitrary"),
        name="wkv_scan",
    )(r, kh, v, kk, bb, ldf, r, kh, v, kk, bb, ldb)


def _wkv_out(yf, yb, bonus, g, gn_w, gn_b):
    y = yf + yb
    mean = _head_sum(y, HEAD_DIM) * (1.0 / HEAD_DIM)
    yc = y - mean
    var = _head_sum(yc * yc, HEAD_DIM) * (1.0 / HEAD_DIM)
    yn = yc * lax.rsqrt(var + WKV_GN_EPS) * gn_w + gn_b
    return ((yn + bonus) * g).astype(BF16)


def _post_kernel(x_ref, oa_ref, ob_ref, oc_ref, yf_ref, yb_ref, bonus_ref, g_ref, gnw_ref, gnb_ref,
                 wo_ref, gpost_ref, gpre_ref, wg_ref, wu_ref, wd_ref, gfpost_ref, y_ref):
    tm = x_ref.shape[0]
    groups = [slice(s, s + tm // ROW_GROUPS) for s in range(0, tm, tm // ROW_GROUPS)]
    o_d = [_wkv_out(yf_ref[r], yb_ref[r], bonus_ref[r], g_ref[r], gnw_ref[...], gnb_ref[...]) for r in groups]
    mix = [_dot(oa_ref[r], wo_ref[0]) + _dot(ob_ref[r], wo_ref[1]) + _dot(oc_ref[r], wo_ref[2])
           + _dot(od, wo_ref[3]) for r, od in zip(groups, o_d)]
    x = [x_ref[r] + _rms_rows(m, gpost_ref[...]) for r, m in zip(groups, mix)]
    h = [_rms_rows(xg, gpre_ref[...]).astype(BF16) for xg in x]
    gate = [_dot(hg, wg_ref[...]) for hg in h]
    up = [_dot(hg, wu_ref[...]) for hg in h]
    act = [(g * _sigmoid(g) * u).astype(BF16) for g, u in zip(gate, up)]
    f = [_dot(a, wd_ref[...]) for a in act]
    for r, xg, fg in zip(groups, x, f):
        y_ref[r] = xg + _rms_rows(fg, gfpost_ref[...])


def _post(x2, oa, ob, oc, wkv_parts, gn_w, gn_b, w_out4, g_post, g_pre, wg, wu, wd, gf_post, tm):
    n, d = x2.shape
    dff = wg.shape[1]
    tok = lambda w: pl.BlockSpec((tm, w), lambda i: (i, 0))
    row = pl.BlockSpec((1, d), lambda i: (0, 0))
    mrow = pl.BlockSpec((1, MIX_W), lambda i: (0, 0))
    return pl.pallas_call(
        _post_kernel,
        grid=(n // tm,),
        in_specs=[tok(d)] + [tok(MIX_W)] * 7 + [mrow, mrow,
                  pl.BlockSpec((4, MIX_W, d), lambda i: (0, 0, 0)), row, row,
                  pl.BlockSpec((d, dff), lambda i: (0, 0)),
                  pl.BlockSpec((d, dff), lambda i: (0, 0)),
                  pl.BlockSpec((dff, d), lambda i: (0, 0)), row],
        out_specs=tok(d),
        out_shape=jax.ShapeDtypeStruct((n, d), F32),
        compiler_params=_params("parallel"),
        name="out_proj_ffn",
    )(x2, oa, ob, oc, *wkv_parts, gn_w, gn_b, w_out4, g_post, g_pre, wg, wu, wd, gf_post)


def _angles(pos, rot_dim, theta):
    inv = theta ** (-jnp.arange(0, rot_dim, 2, dtype=F32) / rot_dim)
    return pos.astype(F32)[:, None] * inv[None, :]


def _rope_tables(t):
    rows = t // GRID_W
    row_idx = jnp.repeat(jnp.arange(rows), GRID_W)
    col_idx = jnp.tile(jnp.arange(GRID_W), rows)
    pos = jnp.arange(t)
    ar = _angles(row_idx, HEAD_DIM // 2, A_THETA)
    ac = _angles(col_idx, HEAD_DIM // 2, A_THETA)
    cos_a = jnp.concatenate([jnp.cos(ar), jnp.cos(ar), jnp.cos(ac), jnp.cos(ac)], axis=1)
    sin_a = jnp.concatenate([-jnp.sin(ar), jnp.sin(ar), -jnp.sin(ac), jnp.sin(ac)], axis=1)
    tabs_a = (jnp.tile(cos_a, (1, 4)), jnp.tile(sin_a, (1, 4)))
    ab = _angles(pos, B_ROT, B_THETA)
    pad1 = jnp.ones((t, B_SUB - B_ROT), F32)
    pad0 = jnp.zeros((t, B_SUB - B_ROT), F32)
    cos_b = jnp.concatenate([jnp.cos(ab), jnp.cos(ab), pad1], axis=1)
    sin_b = jnp.concatenate([-jnp.sin(ab), jnp.sin(ab), pad0], axis=1)
    tabs_b = (jnp.tile(cos_b, (1, 8)), jnp.tile(sin_b, (1, 8)))
    ang_c = _angles(pos, HEAD_DIM, C_THETA)
    cos_c = jnp.concatenate([jnp.cos(ang_c), jnp.cos(ang_c)], axis=1)
    sin_c = jnp.concatenate([-jnp.sin(ang_c), jnp.sin(ang_c)], axis=1)
    tabs_c = (jnp.tile(cos_c, (1, 4)), jnp.tile(sin_c, (1, 4)))
    return tabs_a, tabs_b, tabs_c


def _tile(n, pref):
    while n % pref:
        pref //= 2
    return pref


def _layer_weights(l, w):
    d_model = w["w_in"].shape[1]
    row = lambda a: a.reshape(1, -1).astype(F32)
    zeros = lambda r: jnp.zeros((r, MIX_W), F32)
    w_lora = w["d_w_up"].shape[2]
    return {
        "g_mix_pre": row(w["norm_mix_pre"][l]), "g_mix_post": row(w["norm_mix_post"][l]),
        "g_ffn_pre": row(w["norm_ffn_pre"][l]), "g_ffn_post": row(w["norm_ffn_post"][l]),
        "w_abc": w["w_in"][l][:, :ABC_COLS].astype(BF16), "w_d": w["w_in"][l][:, ABC_COLS:].astype(BF16),
        "w_out4": w["w_out"][l].reshape(4, MIX_W, d_model).astype(BF16),
        "a_gain": jnp.concatenate([jnp.tile(row(w["a_q_gain"][l]), (1, A_HEADS)),
                                   jnp.tile(row(w["a_k_gain"][l]), (1, A_KV_HEADS))], axis=1),
        "b_lambda": w["b_lambda"][l].astype(F32), "b_gain": w["b_subln_gain"][l].reshape(-1, 1).astype(F32),
        "c_gain": row(w["c_gn_gain"][l]),
        "wkv": {
            "mu_prev": row(w["d_mu_prev"][l]), "mu_next": row(w["d_mu_next"][l]),
            "w0": w["d_w0"][l].astype(F32),
            "wup_f": jnp.concatenate([w["d_w_up"][l, 0], zeros(w_lora)], axis=0).astype(BF16),
            "wup_b": jnp.concatenate([zeros(w_lora), w["d_w_up"][l, 1]], axis=0).astype(BF16),
            "a0": row(w["d_a0"][l]),
            "aup": jnp.concatenate([w["d_a_up"][l], zeros(64)], axis=0).astype(BF16),
            "gup1": jnp.concatenate([zeros(64), w["d_g_up"][l][:64]], axis=0).astype(BF16),
            "gup2": w["d_g_up"][l][64:].astype(BF16),
            "k_k": row(w["d_k_k"][l]), "k_a": row(w["d_k_a"][l]), "r_k": row(w["d_r_k"][l]),
        },
        "gn_w": row(w["d_gn_w"][l]), "gn_b": row(w["d_gn_b"][l]),
        "wg": w["ffn_w_gate"][l].astype(BF16), "wu": w["ffn_w_up"][l].astype(BF16),
        "wd": w["ffn_w_down"][l].astype(BF16),
    }


def _trunk(x, layers):
    b, t, d = x.shape
    n = b * t
    tabs = _rope_tables(t)
    tm_proj = _tile(t, 512)
    tm_post = _tile(n, 512)
    tq = _tile(t, 256)
    tm_wkv = _tile(t, 1024)
    tm_scan = _tile(t, 1024)
    chunk = 32
    x2 = x.reshape(n, d)
    for l, p in enumerate(layers):
        lam_init = 0.8 - 0.6 * math.exp(-0.3 * l)
        zabc, zd = _in_proj(x2, p["g_mix_pre"], p["w_abc"], p["w_d"], tabs, p["a_gain"], t, tm_proj)
        zabc = zabc.reshape(b, t, ABC_COLS)
        zd = zd.reshape(b, t, D_COLS)
        o_a = _mixer_gqa(zabc, tq)
        o_b = _mixer_diff(zabc, p["b_lambda"], p["b_gain"], lam_init, tq)
        o_c = _mixer_ret(zabc, p["c_gain"], _tile(t, 256))
        r, kh, v, kk, bb, ldf, ldb, g, bonus = _wkv_prep(zd, p["wkv"], tm_wkv)
        yf, yb = _wkv_scan(r, kh, v, kk, bb, ldf, ldb, tm_scan, chunk, min(4, tm_scan // chunk))
        wkv_parts = [a.reshape(n, MIX_W) for a in (yf, yb, bonus, g)]
        x2 = _post(x2, o_a.reshape(n, MIX_W), o_b.reshape(n, MIX_W), o_c.reshape(n, MIX_W), wkv_parts,
                   p["gn_w"], p["gn_b"], p["w_out4"], p["g_mix_post"], p["g_ffn_pre"], p["wg"], p["wu"], p["wd"],
                   p["g_ffn_post"], tm_post)
    return x2.reshape(b, t, d)


def kernel(x_prompt, x_sample, norm_mix_pre, norm_mix_post, norm_ffn_pre, norm_ffn_post, w_in, w_out,
           a_q_gain, a_k_gain, b_lambda, b_subln_gain, c_gn_gain, d_mu_prev, d_mu_next, d_w0, d_w_up,
           d_a0, d_a_up, d_g_up, d_k_k, d_k_a, d_r_k, d_gn_w, d_gn_b, ffn_w_gate, ffn_w_up, ffn_w_down):
    w = {
        "norm_mix_pre": norm_mix_pre, "norm_mix_post": norm_mix_post,
        "norm_ffn_pre": norm_ffn_pre, "norm_ffn_post": norm_ffn_post,
        "w_in": w_in, "w_out": w_out, "a_q_gain": a_q_gain, "a_k_gain": a_k_gain,
        "b_lambda": b_lambda, "b_subln_gain": b_subln_gain, "c_gn_gain": c_gn_gain,
        "d_mu_prev": d_mu_prev, "d_mu_next": d_mu_next, "d_w0": d_w0, "d_w_up": d_w_up,
        "d_a0": d_a0, "d_a_up": d_a_up, "d_g_up": d_g_up, "d_k_k": d_k_k, "d_k_a": d_k_a,
        "d_r_k": d_r_k, "d_gn_w": d_gn_w, "d_gn_b": d_gn_b,
        "ffn_w_gate": ffn_w_gate, "ffn_w_up": ffn_w_up, "ffn_w_down": ffn_w_down,
    }
    layers = [_layer_weights(l, w) for l in range(w_in.shape[0])]
    return (_trunk(x_prompt, layers), _trunk(x_sample, layers))
```
